```python
import math
import numpy as np
import jax
import jax.numpy as jnp
from jax import lax

D_MODEL = 1024
BATCH = 1
SEQ = 16384
DEPTH = 1
DEC_BATCH = 128
DEC_SEQ = 8
PAST_LEN = 8192
PAGE_SIZE = 128

MIX_WIDTH = D_MODEL
A_WIDTH = MIX_WIDTH // 2
A_HEAD_DIM = 64
A_HEADS = A_WIDTH // A_HEAD_DIM
DILATIONS = ((128, 1), (512, 4), (2048, 16))
WIN_MAX = 2048
BAND = 128
B_WIDTH = MIX_WIDTH - A_WIDTH
B_HEAD_DIM = 128
B_HEADS = B_WIDTH // B_HEAD_DIM
CONV_WIDTH = 4
CONV_CH = 3 * B_WIDTH
GDN_CHUNK = 64
IN_COLS = 3 * A_WIDTH + 4 * B_WIDTH + 2 * B_HEADS
N_MEM = 256
X_HEADS = 4
X_HEAD_DIM = D_MODEL // 8
X_WIDTH = X_HEADS * X_HEAD_DIM
N_GROUPS = 4
GROUP_EXPERTS = 8
N_EXPERTS = N_GROUPS * GROUP_EXPERTS
EXPERT_FF = D_MODEL // 2
TOP_K = 2
MOE_BLOCK = 128
EPS = 1e-6
F32 = jnp.float32

kernel_name = 'hymba_dilated_gdn_hmoe_step'


def rmsnorm(x, g):
    xf = x.astype(F32)
    y = xf * lax.rsqrt(jnp.mean(xf * xf, axis=-1, keepdims=True) + EPS)
    return (y * g.astype(F32)).astype(x.dtype)


def l2norm(x):
    return x * lax.rsqrt(jnp.sum(x * x, axis=-1, keepdims=True) + EPS)


def split_in(z):
    sizes = [A_WIDTH] * 3 + [B_WIDTH] * 4 + [B_HEADS] * 2
    return jnp.split(z, np.cumsum(sizes)[:-1].tolist(), axis=-1)


def attn_heads(qa, ka, va, g_qa, g_ka):
    shape = qa.shape[:-1] + (A_HEADS, A_HEAD_DIM)
    q = rmsnorm(qa.reshape(shape), g_qa) * (A_HEAD_DIM ** -0.5)
    k = rmsnorm(ka.reshape(shape), g_ka)
    return q, k, va.reshape(shape)


def dilated_prompt(q, k, v, window, dil):
    n, s, h, dh = q.shape
    sub_win = window // dil
    length = s // dil
    nb = -(-length // BAND)
    lp = nb * BAND

    def to_sub(t):
        t = t.reshape(n, length, dil, h, dh).transpose(0, 2, 1, 3, 4)
        t = jnp.pad(t, ((0, 0), (0, 0), (0, lp - length), (0, 0), (0, 0)))
        return t.reshape(n, dil, nb, BAND, h, dh)

    def with_prev(t):
        prev = jnp.pad(t[:, :, :-1], ((0, 0), (0, 0), (1, 0), (0, 0), (0, 0), (0, 0)))
        return jnp.concatenate([prev, t], axis=3)

    qb = to_sub(q)
    kc = with_prev(to_sub(k))
    vc = with_prev(to_sub(v))
    s_ = jnp.einsum('brnqhd,brnkhd->brnhqk', qb, kc, preferred_element_type=F32)
    qi = jnp.arange(BAND)[:, None]
    kj = jnp.arange(2 * BAND)[None, :]
    dist = qi + BAND - kj
    key_abs = jnp.arange(nb)[:, None, None] * BAND - BAND + kj[None]
    mask = (dist >= 0) & (dist <= sub_win) & (key_abs >= 0)
    s_ = jnp.where(mask[None, None, :, None], s_, -jnp.inf)
    m = jnp.max(s_, axis=-1, keepdims=True)
    p = jnp.exp(s_ - m)
    den = jnp.sum(p, axis=-1, keepdims=True)
    o = jnp.einsum('brnhqk,brnkhd->brnhqd', p, vc.astype(F32)) / den
    lse = (m + jnp.log(den))[..., 0]
    o = o.transpose(0, 1, 2, 4, 3, 5).reshape(n, dil, lp, h, dh)[:, :, :length]
    o = o.transpose(0, 2, 1, 3, 4).reshape(n, s, h, dh)
    lse = lse.transpose(0, 1, 2, 4, 3).reshape(n, dil, lp, h)[:, :, :length]
    lse = lse.transpose(0, 2, 1, 3).reshape(n, s, h)
    return o, lse


def dilated_step(q, kcat, vcat, window, dil, w_buf):
    t_new = q.shape[1]
    nk = window // dil + 1
    idx = w_buf + jnp.arange(t_new)[:, None] - dil * jnp.arange(nk)[None, :]
    valid = idx >= 0
    idx = jnp.maximum(idx, 0)
    kg = jnp.take(kcat, idx, axis=1)
    vg = jnp.take(vcat, idx, axis=1)
    s_ = jnp.einsum('bthd,btjhd->bthj', q, kg, preferred_element_type=F32)
    s_ = jnp.where(valid[None, :, None, :], s_, -jnp.inf)
    m = jnp.max(s_, axis=-1, keepdims=True)
    p = jnp.exp(s_ - m)
    den = jnp.sum(p, axis=-1, keepdims=True)
    o = jnp.einsum('bthj,btjhd->bthd', p, vg.astype(F32)) / den
    return o, (m + jnp.log(den))[..., 0]


def combine_dilations(results):
    outs = jnp.stack([r[0] for r in results])
    wts = jax.nn.softmax(jnp.stack([r[1] for r in results]), axis=0)
    return jnp.sum(wts[..., None] * outs, axis=0)


def causal_conv_silu(u, w, prev):
    full = jnp.concatenate([prev.astype(u.dtype), u], axis=1)
    n_new = u.shape[1]
    y = sum(full[:, i:i + n_new] * w[i] for i in range(CONV_WIDTH))
    return jax.nn.silu(y), full[:, n_new:]


def gdn_inputs(u, a, b_, a_log, dt_bias):
    n, length, _ = u.shape
    u = u.astype(F32).reshape(n, length, 3, B_HEADS, B_HEAD_DIM)
    q = l2norm(u[:, :, 0]) * (B_HEAD_DIM ** -0.5)
    k = l2norm(u[:, :, 1])
    v = u[:, :, 2]
    g = -jnp.exp(a_log.astype(F32)) * jax.nn.softplus(a.astype(F32) + dt_bias.astype(F32))
    beta = jax.nn.sigmoid(b_.astype(F32))
    return q, k, v, g, beta


def gdn_chunked(q, k, v, g, beta, s0):
    n, length, h, dk = q.shape
    c = GDN_CHUNK
    nc = length // c

    def chunks(t):
        t = t.reshape((n, nc, c, h) + t.shape[3:])
        return jnp.moveaxis(jnp.moveaxis(t, 1, 0), 2, 3)

    qc, kc, vc, gc, bc = (chunks(t) for t in (q, k, v, g, beta))
    gcum = jnp.cumsum(gc, axis=-1)
    diff = gcum[..., :, None] - gcum[..., None, :]
    ii = jnp.arange(c)
    strict = ii[:, None] > ii[None, :]
    incl = ii[:, None] >= ii[None, :]
    dec_strict = jnp.exp(jnp.where(strict, diff, -jnp.inf))
    dec_incl = jnp.exp(jnp.where(incl, diff, -jnp.inf))
    kk = jnp.einsum('xbhid,xbhjd->xbhij', kc, kc)
    tmat = jnp.eye(c, dtype=F32) + bc[..., :, None] * dec_strict * kk
    gamma = jnp.exp(gcum)
    w = lax.linalg.triangular_solve(tmat, (bc * gamma)[..., None] * kc, left_side=True, lower=True, unit_diagonal=True)
    uv = lax.linalg.triangular_solve(tmat, bc[..., None] * vc, left_side=True, lower=True, unit_diagonal=True)
    qk = jnp.einsum('xbhid,xbhjd->xbhij', qc, kc) * dec_incl
    tail = jnp.exp(gcum[..., -1:] - gcum)
    last = jnp.exp(gcum[..., -1])

    def step(s, xs):
        w_, uv_, q_, k_, qk_, gam_, tail_, last_ = xs
        u = uv_ - jnp.einsum('bhcd,bhde->bhce', w_, s)
        o = gam_[..., None] * jnp.einsum('bhcd,bhde->bhce', q_, s) + jnp.einsum('bhij,bhje->bhie', qk_, u)
        s = last_[..., None, None] * s + jnp.einsum('bhcd,bhce->bhde', k_ * tail_[..., None], u)
        return s, o

    s_fin, o = lax.scan(step, s0, (w, uv, qc, kc, qk, gamma, tail, last))
    o = jnp.swapaxes(jnp.moveaxis(o, 0, 1), 2, 3).reshape(n, length, h, -1)
    return o, s_fin


def gdn_recurrent(q, k, v, g, beta, s0):
    def step(s, xs):
        q_, k_, v_, g_, b_ = xs
        s = jnp.exp(g_)[..., None, None] * s
        u = b_[..., None] * (v_ - jnp.einsum('bhde,bhd->bhe', s, k_))
        s = s + jnp.einsum('bhd,bhe->bhde', k_, u)
        return s, jnp.einsum('bhde,bhd->bhe', s, q_)

    xs = tuple(jnp.moveaxis(t, 1, 0) for t in (q, k, v, g, beta))
    s_fin, o = lax.scan(step, s0, xs)
    return jnp.moveaxis(o, 0, 1), s_fin


def gdn_output(o, z, g_out):
    n, length = o.shape[:2]
    gate = jax.nn.silu(z.astype(F32)).reshape(n, length, B_HEADS, B_HEAD_DIM)
    return (rmsnorm(o, g_out) * gate).reshape(n, length, B_WIDTH).astype(z.dtype)


def mem_kv(mem, g_mem, w_xk, w_xv, g_xk):
    n, m, _ = mem.shape
    mn = rmsnorm(mem, g_mem)
    k = rmsnorm((mn @ w_xk).reshape(n, m, X_HEADS, X_HEAD_DIM), g_xk)
    v = (mn @ w_xv).reshape(n, m, X_HEADS, X_HEAD_DIM)
    return k, v


def mem_attend(hn, mem_k, mem_v, w_xq, g_xq, w_xo):
    n, length, _ = hn.shape
    q = rmsnorm((hn @ w_xq).reshape(n, length, X_HEADS, X_HEAD_DIM), g_xq)
    s = jnp.einsum('blhd,bmhd->bhlm', q, mem_k, preferred_element_type=F32) * (X_HEAD_DIM ** -0.5)
    p = jax.nn.softmax(s, axis=-1)
    o = jnp.einsum('bhlm,bmhd->blhd', p, mem_v.astype(F32))
    return o.reshape(n, length, X_WIDTH).astype(hn.dtype) @ w_xo


def route(hn, w_rg, b_rg, w_re, b_re):
    hf = hn.astype(F32)
    pg = jax.nn.softmax(hf @ w_rg.astype(F32) + b_rg.astype(F32), axis=-1)
    pg_top, g_idx = lax.top_k(pg, 1)
    le = (hf @ w_re.astype(F32) + b_re.astype(F32)).reshape(-1, N_GROUPS, GROUP_EXPERTS)
    le = jnp.take_along_axis(le, g_idx[:, :, None], axis=1)[:, 0]
    pe_top, e_idx = lax.top_k(jax.nn.softmax(le, axis=-1), TOP_K)
    wts = pg_top * pe_top / jnp.sum(pe_top, axis=-1, keepdims=True)
    return g_idx * GROUP_EXPERTS + e_idx, wts


def moe_ffn(xf, ids, wts, w_gate, w_up, w_down):
    n = xf.shape[0]
    nk = n * TOP_K
    flat_e = ids.reshape(-1)
    flat_t = jnp.repeat(jnp.arange(n, dtype=jnp.int32), TOP_K)
    flat_w = wts.reshape(-1)
    order = jnp.argsort(flat_e)
    se = flat_e[order]
    counts = jnp.bincount(flat_e, length=N_EXPERTS)
    start = jnp.cumsum(counts) - counts
    padded = (counts + MOE_BLOCK - 1) // MOE_BLOCK * MOE_BLOCK
    pend = jnp.cumsum(padded)
    dest = pend[se] - padded[se] + jnp.arange(nk) - start[se]
    n_blocks = -(-nk // MOE_BLOCK) + N_EXPERTS
    rows = n_blocks * MOE_BLOCK
    row_tok = jnp.zeros((rows,), jnp.int32).at[dest].set(flat_t[order])
    row_w = jnp.zeros((rows,), F32).at[dest].set(flat_w[order])
    blk_e = jnp.minimum(jnp.searchsorted(pend, jnp.arange(n_blocks) * MOE_BLOCK, side='right'), N_EXPERTS - 1)
    xb = xf[row_tok].reshape(n_blocks, MOE_BLOCK, -1)

    def expert_block(args):
        xblk, e = args
        return (jax.nn.silu(xblk @ w_gate[e]) * (xblk @ w_up[e])) @ w_down[e]

    yb = lax.map(expert_block, (xb, blk_e)).reshape(rows, -1).astype(F32) * row_w[:, None]
    return jnp.zeros(xf.shape, F32).at[row_tok].add(yb).astype(xf.dtype)


def channel_mixer(h, p):
    n, length, d = h.shape
    hn = rmsnorm(h, p['g_ffn']).reshape(n * length, d)
    ids, wts = route(hn, p['w_rg'], p['b_rg'], p['w_re'], p['b_re'])
    return moe_ffn(hn, ids, wts, p['w_gate'], p['w_up'], p['w_down']).reshape(n, length, d)


def layer_forward(h, win_k, win_v, conv_prev, s0, mem_k, mem_v, p):
    n, length, _ = h.shape
    qa, ka, va, qb, kb, vb, zb, ab, bb = split_in(rmsnorm(h, p['g_mix']) @ p['w_in'])
    q, k, v = attn_heads(qa, ka, va, p['g_qa'], p['g_ka'])
    if win_k is None:
        res = [dilated_prompt(q, k, v, w, d) for w, d in DILATIONS]
        keep = min(WIN_MAX, length)
        new_k, new_v = k[:, length - keep:], v[:, length - keep:]
    else:
        w_buf = win_k.shape[1]
        kcat = jnp.concatenate([win_k.astype(k.dtype), k], axis=1)
        vcat = jnp.concatenate([win_v.astype(v.dtype), v], axis=1)
        res = [dilated_step(q, kcat, vcat, w, d, w_buf) for w, d in DILATIONS]
        new_k, new_v = kcat[:, length:], vcat[:, length:]
    o_a = combine_dilations(res).reshape(n, length, A_WIDTH).astype(h.dtype)
    u, conv_new = causal_conv_silu(jnp.concatenate([qb, kb, vb], axis=-1), p['conv_w'], conv_prev)
    gq, gk, gv, gg, gbeta = gdn_inputs(u, ab, bb, p['a_log'], p['dt_bias'])
    gdn = gdn_chunked if win_k is None else gdn_recurrent
    o, s_fin = gdn(gq, gk, gv, gg, gbeta, s0)
    o_b = gdn_output(o, zb, p['g_gdn'])
    h = h + jnp.concatenate([o_a, o_b], axis=-1) @ p['w_out']
    h = h + mem_attend(rmsnorm(h, p['g_xattn']), mem_k, mem_v, p['w_xq'], p['g_xq'], p['w_xo'])
    h = h + channel_mixer(h, p)
    return h, new_k, new_v, conv_new, s_fin


def setup_inputs(seed: int = 0) -> dict:
    key = jax.random.key(seed)
    keys = iter(jax.random.split(key, 48))

    def nrm(shape, scale):
        return scale * jax.random.normal(next(keys), shape, F32)

    def gain(shape):
        return 1.0 + 0.02 * jax.random.normal(next(keys), shape, F32)

    L = DEPTH
    w_buf = min(WIN_MAX, PAST_LEN)
    a_log = jnp.log(jax.random.uniform(next(keys), (L, B_HEADS), F32, 1.0, 16.0))
    dt = jnp.exp(jax.random.uniform(next(keys), (L, B_HEADS), F32, math.log(1e-3), math.log(1e-1)))
    dt_bias = dt + jnp.log(-jnp.expm1(-dt))
    return {
        'x_prompt': nrm((BATCH, SEQ, D_MODEL), 1.0),
        'x_sample': nrm((DEC_BATCH, DEC_SEQ, D_MODEL), 1.0),
        'mem_prompt': nrm((BATCH, N_MEM, D_MODEL), 1.0),
        'cache_win_k': nrm((L, DEC_BATCH, w_buf, A_HEADS, A_HEAD_DIM), 1.0),
        'cache_win_v': nrm((L, DEC_BATCH, w_buf, A_HEADS, A_HEAD_DIM), 1.0),
        'state_conv': nrm((L, DEC_BATCH, CONV_WIDTH - 1, CONV_CH), 1.0),
        'state_delta': nrm((L, DEC_BATCH, B_HEADS, B_HEAD_DIM, B_HEAD_DIM), 0.1),
        'cache_mem_k': nrm((L, DEC_BATCH, N_MEM, X_HEADS, X_HEAD_DIM), 1.0),
        'cache_mem_v': nrm((L, DEC_BATCH, N_MEM, X_HEADS, X_HEAD_DIM), 1.0),
        'g_mix': gain((L, D_MODEL)),
        'w_in': nrm((L, D_MODEL, IN_COLS), D_MODEL ** -0.5),
        'g_qa': gain((L, A_HEAD_DIM)),
        'g_ka': gain((L, A_HEAD_DIM)),
        'conv_w': nrm((L, CONV_WIDTH, CONV_CH), 0.5),
        'a_log': a_log,
        'dt_bias': dt_bias,
        'g_gdn': gain((L, B_HEAD_DIM)),
        'w_out': nrm((L, MIX_WIDTH, D_MODEL), MIX_WIDTH ** -0.5),
        'g_xattn': gain((L, D_MODEL)),
        'g_mem': gain((L, D_MODEL)),
        'w_xq': nrm((L, D_MODEL, X_WIDTH), D_MODEL ** -0.5),
        'w_xk': nrm((L, D_MODEL, X_WIDTH), D_MODEL ** -0.5),
        'w_xv': nrm((L, D_MODEL, X_WIDTH), D_MODEL ** -0.5),
        'g_xq': gain((L, X_HEAD_DIM)),
        'g_xk': gain((L, X_HEAD_DIM)),
        'w_xo': nrm((L, X_WIDTH, D_MODEL), X_WIDTH ** -0.5),
        'g_ffn': gain((L, D_MODEL)),
        'w_rg': nrm((L, D_MODEL, N_GROUPS), D_MODEL ** -0.5),
        'b_rg': nrm((L, N_GROUPS), 0.01),
        'w_re': nrm((L, D_MODEL, N_EXPERTS), D_MODEL ** -0.5),
        'b_re': nrm((L, N_EXPERTS), 0.01),
        'w_gate': nrm((L, N_EXPERTS, D_MODEL, EXPERT_FF), D_MODEL ** -0.5),
        'w_up': nrm((L, N_EXPERTS, D_MODEL, EXPERT_FF), D_MODEL ** -0.5),
        'w_down': nrm((L, N_EXPERTS, EXPERT_FF, D_MODEL), EXPERT_FF ** -0.5),
    }


def reference(x_prompt, x_sample, mem_prompt, cache_win_k, cache_win_v, state_conv, state_delta,
              cache_mem_k, cache_mem_v, g_mix, w_in, g_qa, g_ka, conv_w, a_log, dt_bias, g_gdn, w_out,
              g_xattn, g_mem, w_xq, w_xk, w_xv, g_xq, g_xk, w_xo,
              g_ffn, w_rg, b_rg, w_re, b_re, w_gate, w_up, w_down):
    hp, hs = x_prompt, x_sample
    nb = x_prompt.shape[0]
    wkp, wvp, cvp, sdp, mkp, mvp, wks, wvs, cvs, sds = ([] for _ in range(10))
    for l in range(DEPTH):
        p = dict(g_mix=g_mix[l], w_in=w_in[l], g_qa=g_qa[l], g_ka=g_ka[l], conv_w=conv_w[l],
                 a_log=a_log[l], dt_bias=dt_bias[l], g_gdn=g_gdn[l], w_out=w_out[l],
                 g_xattn=g_xattn[l], w_xq=w_xq[l], g_xq=g_xq[l], w_xo=w_xo[l],
                 g_ffn=g_ffn[l], w_rg=w_rg[l], b_rg=b_rg[l], w_re=w_re[l], b_re=b_re[l],
                 w_gate=w_gate[l], w_up=w_up[l], w_down=w_down[l])
        mk, mv = mem_kv(mem_prompt, g_mem[l], w_xk[l], w_xv[l], g_xk[l])
        conv0 = jnp.zeros((nb, CONV_WIDTH - 1, CONV_CH), x_prompt.dtype)
        s0 = jnp.zeros((nb, B_HEADS, B_HEAD_DIM, B_HEAD_DIM), F32)
        hp, k_p, v_p, c_p, s_p = layer_forward(hp, None, None, conv0, s0, mk, mv, p)
        wkp.append(k_p)
        wvp.append(v_p)
        cvp.append(c_p)
        sdp.append(s_p)
        mkp.append(mk)
        mvp.append(mv)
        hs, k_s, v_s, c_s, s_s = layer_forward(hs, cache_win_k[l], cache_win_v[l], state_conv[l],
                                               state_delta[l].astype(F32), cache_mem_k[l], cache_mem_v[l], p)
        wks.append(k_s)
        wvs.append(v_s)
        cvs.append(c_s)
        sds.append(s_s)
    return (hp, hs,
            jnp.stack(wkp), jnp.stack(wvp), jnp.stack(cvp), jnp.stack(sdp).astype(state_delta.dtype),
            jnp.stack(mkp), jnp.stack(mvp),
            jnp.stack(wks), jnp.stack(wvs), jnp.stack(cvs), jnp.stack(sds).astype(state_delta.dtype))
```

```python
import functools

import jax
import jax.numpy as jnp
from jax import lax
from jax.experimental import pallas as pl
from jax.experimental.pallas import tpu as pltpu

F32 = jnp.float32
BF16 = jnp.bfloat16
EPS = 1e-6
NEG_INF = float("-inf")

A_HEAD_DIM = 64
B_HEAD_DIM = 128
B_HEADS = 4
BAND = 128
DILATIONS = ((128, 1), (512, 4), (2048, 16))
GDN_CHUNK = 64
CONV_WIDTH = 4
PREV_ROWS = 8
N_GROUPS = 4
GROUP_EXPERTS = 8
ROUTE_LANES = 128
MOE_BLOCK = 128
MOE_CHUNK = 1024
VMEM_LIMIT = 56 * 1024 * 1024


def _cparams(*sem):
    return pltpu.CompilerParams(dimension_semantics=sem, vmem_limit_bytes=VMEM_LIMIT)


def _bdot(a, b):
    return jnp.dot(a.astype(BF16), b.astype(BF16), preferred_element_type=F32)


def _bdot_nt(a, b):
    return lax.dot_general(a.astype(BF16), b.astype(BF16), (((1,), (1,)), ((), ())),
                           preferred_element_type=F32)


def _split3(x):
    hi = x.astype(BF16)
    r1 = x - hi.astype(F32)
    mid = r1.astype(BF16)
    lo = (r1 - mid.astype(F32)).astype(BF16)
    return hi, mid, lo


def _dot_exact_lhs(a01, x):
    a = a01.astype(BF16)
    hi, mid, lo = _split3(x)
    d = lambda p: jnp.dot(a, p, preferred_element_type=F32)
    return d(hi) + d(mid) + d(lo)


def _group_mean_sq(x, gm):
    sq = x * x
    hi = sq.astype(BF16)
    lo = (sq - hi.astype(F32)).astype(BF16)
    return (jnp.dot(hi, gm, preferred_element_type=F32) + jnp.dot(lo, gm, preferred_element_type=F32))


def _rms(x, g):
    return x * lax.rsqrt(jnp.mean(x * x, axis=-1, keepdims=True) + EPS) * g


def _sigmoid(x):
    return 1.0 / (1.0 + jnp.exp(-x))


def _group_mean_matrix(width, group):
    i = jnp.arange(width)
    return jnp.where((i[:, None] // group) == (i[None, :] // group), 1.0 / group, 0.0).astype(BF16)


def _inproj_kernel(x_ref, g_ref, w_ref, gm_ref, gq_ref, gk_ref,
                   q_ref, k_ref, v_ref, u_ref, z_ref, ab_ref, *, aw, bw):
    nb = _rms(x_ref[...], g_ref[...]).astype(BF16)

    def proj(lo, hi):
        return jnp.dot(nb, w_ref[:, lo:hi], preferred_element_type=F32)

    gm = gm_ref[...]
    qa = proj(0, aw)
    q_ref[...] = qa * lax.rsqrt(_group_mean_sq(qa, gm) + EPS) * gq_ref[...]
    ka = proj(aw, 2 * aw)
    k_ref[...] = ka * lax.rsqrt(_group_mean_sq(ka, gm) + EPS) * gk_ref[...]
    v_ref[...] = proj(2 * aw, 3 * aw)
    u_ref[...] = proj(3 * aw, 3 * aw + 3 * bw)
    z_ref[...] = proj(3 * aw + 3 * bw, 3 * aw + 4 * bw)
    ab_ref[...] = proj(3 * aw + 4 * bw, 3 * aw + 4 * bw + ROUTE_LANES)


def _inproj(x, g_mix, w_in_b, gm64, gq_t, gk_t, *, aw, bw, tm):
    n, d = x.shape
    cols = w_in_b.shape[1]
    row = lambda w: pl.BlockSpec((tm, w), lambda i: (i, 0))
    full = lambda a: pl.BlockSpec(a.shape, lambda i: (0, 0))
    out_w = (aw, aw, aw, 3 * bw, bw, ROUTE_LANES)
    return pl.pallas_call(
        functools.partial(_inproj_kernel, aw=aw, bw=bw),
        grid=(n // tm,),
        in_specs=[row(d), full(g_mix), pl.BlockSpec((d, cols), lambda i: (0, 0)),
                  full(gm64), full(gq_t), full(gk_t)],
        out_specs=[row(w) for w in out_w],
        out_shape=[jax.ShapeDtypeStruct((n, w), F32) for w in out_w],
        compiler_params=_cparams("parallel"),
        name="inproj",
    )(x, g_mix, w_in_b, gm64, gq_t, gk_t)


def _dil_attn_kernel(q_ref, kp_ref, kc_ref, vp_ref, vc_ref, o_ref, lse_ref, *, sub_win):
    n = pl.program_id(1)
    qi = lax.broadcasted_iota(jnp.int32, (BAND, 2 * BAND), 0)
    kj = lax.broadcasted_iota(jnp.int32, (BAND, 2 * BAND), 1)
    dist = qi + BAND - kj
    first = jnp.where(n > 0, 0, BAND)
    valid = (dist >= 0) & (dist <= sub_win) & (kj >= first)
    bias = jnp.where(valid, 0.0, NEG_INF)
    lane = lax.broadcasted_iota(jnp.int32, (BAND, 2 * A_HEAD_DIM), 1)
    lo_half = lane < A_HEAD_DIM

    k_all = jnp.concatenate([kp_ref[...], kc_ref[...]], axis=0).astype(BF16)
    v_all = jnp.concatenate([vp_ref[...], vc_ref[...]], axis=0).astype(BF16)
    q = q_ref[...]
    pair = 2 * A_HEAD_DIM
    for p in range(q.shape[1] // pair):
        sl = slice(p * pair, (p + 1) * pair)
        kpair, vpair, qpair = k_all[:, sl], v_all[:, sl], q[:, sl]
        res = []
        for half in range(2):
            keep = lo_half if half == 0 else jnp.logical_not(lo_half)
            qm = jnp.where(keep, qpair, 0.0)
            s = _bdot_nt(qm, kpair) + bias
            m = jnp.max(s, axis=-1, keepdims=True)
            e = jnp.exp(s - m)
            den = jnp.sum(e, axis=-1, keepdims=True)
            o = jnp.dot(e.astype(BF16), vpair, preferred_element_type=F32) / den
            res.append((o, m + jnp.log(den)))
        o_ref[:, sl] = jnp.where(lo_half, res[0][0], res[1][0])
        lse_ref[:, sl] = jnp.where(lo_half, res[0][1], res[1][1])


def _dil_attn(q, k, v, window, dil):
    s, aw = q.shape
    length = s // dil
    assert s % dil == 0 and length % BAND == 0 and window // dil <= BAND
    nb = length // BAND
    view = lambda t: t.reshape(length, dil * aw)
    cur = pl.BlockSpec((BAND, aw), lambda r, n: (n, r))
    prev = pl.BlockSpec((BAND, aw), lambda r, n: (jnp.maximum(n - 1, 0), r))
    o, lse = pl.pallas_call(
        functools.partial(_dil_attn_kernel, sub_win=window // dil),
        grid=(dil, nb),
        in_specs=[cur, prev, cur, prev, cur],
        out_specs=[cur, cur],
        out_shape=[jax.ShapeDtypeStruct((length, dil * aw), F32)] * 2,
        compiler_params=_cparams("parallel", "arbitrary"),
        name=f"dil_attn_{dil}",
    )(view(q), view(k), view(k), view(v), view(v))
    return o.reshape(s, aw), lse.reshape(s, aw)


def _step_attn_kernel(q_ref, kn_ref, vn_ref, kc_ref, vc_ref, ok_ref, ov_ref, oa_ref,
                      kcat_ref, vcat_ref, *, w_buf, t_new, pad_rows):
    heads = q_ref.shape[2] // A_HEAD_DIM
    aw = q_ref.shape[2]
    kn, vn = kn_ref[0], vn_ref[0]
    ok_ref[0, 0:w_buf - t_new, :] = kc_ref[0, t_new:w_buf, :]
    ok_ref[0, w_buf - t_new:w_buf, :] = kn
    ov_ref[0, 0:w_buf - t_new, :] = vc_ref[0, t_new:w_buf, :]
    ov_ref[0, w_buf - t_new:w_buf, :] = vn
    zpad = jnp.zeros((pad_rows - t_new, aw), F32)
    kcat_ref[0:w_buf, :] = kc_ref[0].astype(BF16)
    kcat_ref[w_buf:w_buf + pad_rows, :] = jnp.concatenate([kn, zpad], axis=0).astype(BF16)
    vcat_ref[0:w_buf, :] = vc_ref[0].astype(BF16)
    vcat_ref[w_buf:w_buf + pad_rows, :] = jnp.concatenate([vn, zpad], axis=0).astype(BF16)

    rows = heads * t_new
    ri = lax.broadcasted_iota(jnp.int32, (rows, aw), 0)
    li = lax.broadcasted_iota(jnp.int32, (rows, aw), 1)
    q_rep = jnp.concatenate([q_ref[0]] * heads, axis=0)
    q_blk = jnp.where((ri // t_new) == (li // A_HEAD_DIM), q_rep, 0.0)
    ncol = w_buf + pad_rows
    s_all = _bdot_nt(q_blk, kcat_ref[...])

    def branch(window, dil, col0):
        s = s_all[:, col0:]
        t = lax.broadcasted_iota(jnp.int32, s.shape, 0) % t_new
        r = lax.broadcasted_iota(jnp.int32, s.shape, 1) + col0
        diff = w_buf + t - r
        valid = (diff >= 0) & (diff <= window) & ((diff & (dil - 1)) == 0)
        s = jnp.where(valid, s, NEG_INF)
        m = jnp.max(s, axis=-1, keepdims=True)
        e = jnp.exp(s - m)
        den = jnp.sum(e, axis=-1, keepdims=True)
        return e, den, m + jnp.log(den)

    parts = []
    for window, dil in DILATIONS:
        assert dil & (dil - 1) == 0
        col0 = max(0, (w_buf - window) // 128 * 128)
        parts.append((col0,) + branch(window, dil, col0))
    mm = functools.reduce(jnp.maximum, [p[3] for p in parts])
    wexp = [jnp.exp(p[3] - mm) for p in parts]
    wsum = functools.reduce(lambda a, b: a + b, wexp)
    p_tot = jnp.zeros((rows, ncol), F32)
    for (col0, e, den, _), we in zip(parts, wexp):
        pe = e * (we / (wsum * den))
        if col0:
            pe = jnp.concatenate([jnp.zeros((rows, col0), F32), pe], axis=1)
        p_tot = p_tot + pe
    o = jnp.dot(p_tot.astype(BF16), vcat_ref[...], preferred_element_type=F32)
    lh = lax.broadcasted_iota(jnp.int32, (t_new, aw), 1) // A_HEAD_DIM
    acc = jnp.zeros((t_new, aw), F32)
    for h in range(heads):
        acc = acc + jnp.where(lh == h, o[h * t_new:(h + 1) * t_new, :], 0.0)
    oa_ref[0] = acc


def _step_attn(q, kn, vn, cache_k, cache_v):
    nseq, w_buf, aw = cache_k.shape
    t_new = q.shape[1]
    pad_rows = 128
    assert w_buf % 128 == 0 and t_new % 8 == 0 and t_new <= pad_rows
    new = pl.BlockSpec((1, t_new, aw), lambda i: (i, 0, 0))
    win = pl.BlockSpec((1, w_buf, aw), lambda i: (i, 0, 0))
    return pl.pallas_call(
        functools.partial(_step_attn_kernel, w_buf=w_buf, t_new=t_new, pad_rows=pad_rows),
        grid=(nseq,),
        in_specs=[new, new, new, win, win],
        out_specs=[win, win, new],
        out_shape=[jax.ShapeDtypeStruct(cache_k.shape, F32), jax.ShapeDtypeStruct(cache_v.shape, F32),
                   jax.ShapeDtypeStruct(q.shape, F32)],
        scratch_shapes=[pltpu.VMEM((w_buf + pad_rows, aw), BF16), pltpu.VMEM((w_buf + pad_rows, aw), BF16)],
        compiler_params=_cparams("parallel"),
        name="step_attn",
    )(q, kn, vn, cache_k, cache_v)


def _gdn_prep_kernel(u_ref, prev_ref, ab_ref, cw_ref, alog_ref, dtb_ref,
                     w_ref, uv_ref, qg_ref, kt_ref, qk_ref, last_ref, full_ref, *, seg, chunks):
    c = GDN_CHUNK
    dk = B_HEAD_DIM
    bw = B_HEADS * dk
    nseg = c // seg
    ii = lax.broadcasted_iota(jnp.int32, (c, c), 0)
    jj = lax.broadcasted_iota(jnp.int32, (c, c), 1)
    same = (ii // seg) == (jj // seg)
    incl = same & (ii >= jj)
    strict = same & (ii > jj)
    eye = ii == jj
    tri01 = jnp.where(incl, 1.0, 0.0)
    same01 = jnp.where(same, 1.0, 0.0)
    ones_c = jnp.ones((c, c), BF16)
    seg_cols = jnp.where((lax.broadcasted_iota(jnp.int32, (c, dk), 0) % seg)
                         == lax.broadcasted_iota(jnp.int32, (c, dk), 1), 1.0, 0.0)
    cw = cw_ref[...]

    def chunk_body(ci, carry):
        r0 = pl.multiple_of(ci * c, c)
        ys = []
        for sgi in range(nseg):
            base = sgi * (seg + PREV_ROWS)
            full_ref[base:base + PREV_ROWS, :] = prev_ref[ci * nseg + sgi]
            full_ref[base + PREV_ROWS:base + PREV_ROWS + seg, :] = u_ref[pl.ds(r0 + sgi * seg, seg), :]
            y = jnp.zeros((seg, 3 * bw), F32)
            for j in range(CONV_WIDTH):
                off = base + PREV_ROWS - (CONV_WIDTH - 1) + j
                y = y + full_ref[off:off + seg, :] * cw[j:j + 1, :]
            ys.append(y)
        y = ys[0] if nseg == 1 else jnp.concatenate(ys, axis=0)
        u = y * _sigmoid(y)

        ab = ab_ref[pl.ds(r0, c), :]
        sp = ab + dtb_ref[...]
        sp = jnp.maximum(sp, 0.0) + jnp.log(1.0 + jnp.exp(-jnp.abs(sp)))
        g_all = -jnp.exp(alog_ref[...]) * sp
        gcum_all = _dot_exact_lhs(tri01, g_all)
        gtot_all = _dot_exact_lhs(same01, g_all)
        beta_all = _sigmoid(ab)
        lasts = []
        for h in range(B_HEADS):
            hs = slice(h * dk, (h + 1) * dk)
            qh = u[:, h * dk:(h + 1) * dk]
            kh = u[:, bw + h * dk:bw + (h + 1) * dk]
            vh = u[:, 2 * bw + h * dk:2 * bw + (h + 1) * dk]
            qh = qh * lax.rsqrt(jnp.sum(qh * qh, axis=-1, keepdims=True) + EPS) * (dk ** -0.5)
            kh = kh * lax.rsqrt(jnp.sum(kh * kh, axis=-1, keepdims=True) + EPS)
            gc = gcum_all[:, h:h + 1]
            gt = gtot_all[:, h:h + 1]
            beta = beta_all[:, B_HEADS + h:B_HEADS + h + 1]
            grow = _dot_exact_lhs(ones_c, jnp.where(eye, gc, 0.0))
            diff = gc - grow
            e_incl = jnp.exp(jnp.where(incl, diff, NEG_INF))
            e_strict = jnp.where(strict, e_incl, 0.0)
            kb = kh.astype(BF16)
            kk = lax.dot_general(kb, kb, (((1,), (1,)), ((), ())), preferred_element_type=F32)
            a = beta * e_strict * kk
            blk = 8
            d0 = jnp.where((ii // blk) == (jj // blk), a, 0.0)
            d2 = _bdot(d0, d0)
            d4 = _bdot(d2, d2)
            eye_f = jnp.where(eye, 1.0, 0.0)
            x = (eye_f - d0) + _bdot(eye_f - d0, d2)
            x = x + _bdot(x, d4)
            while blk < seg:
                off_blk = ((ii // (2 * blk)) == (jj // (2 * blk))) & ((ii // blk) != (jj // blk))
                e_blk = jnp.where(off_blk, a, 0.0)
                x = x - _bdot(_bdot(x, e_blk), x)
                blk *= 2
            gamma = jnp.exp(gc)
            w_ref[pl.ds(r0, c), hs] = _bdot(x, (beta * gamma) * kh)
            uv_ref[pl.ds(r0, c), hs] = _bdot(x, beta * vh)
            qg_ref[pl.ds(r0, c), hs] = gamma * qh
            kt_ref[pl.ds(r0, c), hs] = kh * jnp.exp(gt - gc)
            qk = lax.dot_general(qh.astype(BF16), kb, (((1,), (1,)), ((), ())),
                                 preferred_element_type=F32) * e_incl
            qk_ref[pl.ds(r0, c), hs] = _bdot(qk, seg_cols)
            lasts.append(jnp.broadcast_to(jnp.exp(gt), (c, dk)))
        last_ref[pl.ds(r0, c), :] = jnp.concatenate(lasts, axis=1)
        return carry

    lax.fori_loop(0, chunks, chunk_body, 0)


def _gdn_prep(u_pre, prev, ab, conv_w, alog_row, dtb_row, *, seg, chunks):
    n, cw = u_pre.shape
    bw = cw // 3
    c = GDN_CHUNK
    rows = chunks * c
    assert n % rows == 0 and c % seg == 0 and seg % 8 == 0
    nseg = c // seg
    row = lambda w: pl.BlockSpec((rows, w), lambda i: (i, 0))
    full = lambda a: pl.BlockSpec(a.shape, lambda i: (0, 0))
    return pl.pallas_call(
        functools.partial(_gdn_prep_kernel, seg=seg, chunks=chunks),
        grid=(n // rows,),
        in_specs=[row(cw), pl.BlockSpec((chunks * nseg, PREV_ROWS, cw), lambda i: (i, 0, 0)),
                  row(ROUTE_LANES), full(conv_w), full(alog_row), full(dtb_row)],
        out_specs=[row(bw)] * 6,
        out_shape=[jax.ShapeDtypeStruct((n, bw), F32)] * 6,
        scratch_shapes=[pltpu.VMEM((nseg * (seg + PREV_ROWS), cw), F32)],
        compiler_params=_cparams("parallel"),
        name=f"gdn_prep_{seg}",
    )(u_pre, prev, ab, conv_w, alog_row, dtb_row)


def _gdn_scan_kernel(w_ref, uv_ref, qg_ref, kt_ref, qk_ref, last_ref, s0_ref, o_ref, s_ref, *, seg):
    dk = B_HEAD_DIM

    @pl.when(pl.program_id(1) == 0)
    def _():
        s_ref[...] = s0_ref[...]

    pad = dk - seg
    padr = lambda t: jnp.concatenate([t, jnp.zeros((pad, dk), F32)], axis=0) if pad else t
    ii = lax.broadcasted_iota(jnp.int32, (dk, dk), 0)
    jj = lax.broadcasted_iota(jnp.int32, (dk, dk), 1)
    eye = jnp.where(ii == jj, 1.0, 0.0).astype(BF16)
    for h in range(B_HEADS):
        hs = slice(h * dk, (h + 1) * dk)
        s = s_ref[0, h]
        sb = s.astype(BF16)
        u = padr(uv_ref[:, hs]) - _bdot(padr(w_ref[:, hs]), sb)
        ub = u.astype(BF16)
        o = _bdot(padr(qg_ref[:, hs]), sb) + _bdot(padr(qk_ref[:, hs]), ub)
        o_ref[:, hs] = o[:seg, :]
        ktt = _bdot_nt(eye, padr(kt_ref[:, hs]))
        s_ref[0, h] = last_ref[0:1, hs] * s + _bdot(ktt, ub)


def _gdn_scan(w, uv, qg, kt, qk, last, s0, *, seg):
    n, bw = w.shape
    nseq = s0.shape[0]
    per_seq = n // (nseq * seg)
    row = pl.BlockSpec((seg, bw), lambda s, i: (s * per_seq + i, 0))
    st = pl.BlockSpec((1,) + s0.shape[1:], lambda s, i: (s, 0, 0, 0))
    return pl.pallas_call(
        functools.partial(_gdn_scan_kernel, seg=seg),
        grid=(nseq, per_seq),
        in_specs=[row] * 6 + [st],
        out_specs=[row, st],
        out_shape=[jax.ShapeDtypeStruct((n, bw), F32), jax.ShapeDtypeStruct(s0.shape, F32)],
        compiler_params=_cparams("parallel", "arbitrary"),
        name=f"gdn_scan_{seg}",
    )(w, uv, qg, kt, qk, last, s0)


def _mem_kv_kernel(mem_ref, g_ref, wk_ref, wv_ref, gm_ref, gk_ref, k_ref, v_ref):
    mn = _rms(mem_ref[...], g_ref[...]).astype(BF16)
    k = jnp.dot(mn, wk_ref[...], preferred_element_type=F32)
    k_ref[...] = k * lax.rsqrt(_group_mean_sq(k, gm_ref[...]) + EPS) * gk_ref[...]
    v_ref[...] = jnp.dot(mn, wv_ref[...], preferred_element_type=F32)


def _mem_kv(mem, g_mem, w_xk_b, w_xv_b, gm128, gk_t):
    m = mem.shape[0]
    xw = w_xk_b.shape[1]
    return pl.pallas_call(
        _mem_kv_kernel,
        out_shape=[jax.ShapeDtypeStruct((m, xw), F32)] * 2,
        compiler_params=pltpu.CompilerParams(vmem_limit_bytes=VMEM_LIMIT),
        name="mem_kv",
    )(mem, g_mem, w_xk_b, w_xv_b, gm128, gk_t)


def _route(logits):
    lane = lax.broadcasted_iota(jnp.int32, logits.shape, 1)
    lane_f = lane.astype(F32)
    big = float(ROUTE_LANES)
    lg = jnp.where(lane < N_GROUPS, logits, NEG_INF)
    mg = jnp.max(lg, axis=-1, keepdims=True)
    zg = jnp.sum(jnp.exp(lg - mg), axis=-1, keepdims=True)
    pg_top = 1.0 / zg
    gidx = jnp.min(jnp.where(lg == mg, lane_f, big), axis=-1, keepdims=True)
    e_lo = N_GROUPS + GROUP_EXPERTS * gidx
    emask = (lane_f >= e_lo) & (lane_f < e_lo + GROUP_EXPERTS)
    le = jnp.where(emask, logits, NEG_INF)
    me = jnp.max(le, axis=-1, keepdims=True)
    ee = jnp.exp(le - me)
    pe = ee / jnp.sum(ee, axis=-1, keepdims=True)
    pe = jnp.where(emask, pe, -1.0)
    p1 = jnp.max(pe, axis=-1, keepdims=True)
    i1 = jnp.min(jnp.where(pe == p1, lane_f, big), axis=-1, keepdims=True)
    pe2 = jnp.where(lane_f == i1, -1.0, pe)
    p2 = jnp.max(pe2, axis=-1, keepdims=True)
    i2 = jnp.min(jnp.where(pe2 == p2, lane_f, big), axis=-1, keepdims=True)
    den = p1 + p2
    w1 = pg_top * p1 / den
    w2 = pg_top * p2 / den
    out = jnp.where(lane == 0, i1 - N_GROUPS, 0.0)
    out = jnp.where(lane == 1, i2 - N_GROUPS, out)
    out = jnp.where(lane == 2, w1, out)
    return jnp.where(lane == 3, w2, out)


def _post_common(oa, og_ref, z_ref, h_ref, wout_ref, gm_ref, ggdn_ref, gx_ref, wxq_ref, gxq_ref):
    og = og_ref[...]
    z = z_ref[...]
    ob = og * lax.rsqrt(_group_mean_sq(og, gm_ref[...]) + EPS) * ggdn_ref[...] * (z * _sigmoid(z))
    cat = jnp.concatenate([oa, ob], axis=-1).astype(BF16)
    h1 = h_ref[...] + jnp.dot(cat, wout_ref[...], preferred_element_type=F32)
    hx = _rms(h1, gx_ref[...]).astype(BF16)
    q = jnp.dot(hx, wxq_ref[...], preferred_element_type=F32)
    qn = q * lax.rsqrt(_group_mean_sq(q, gm_ref[...]) + EPS) * gxq_ref[...]
    return h1, qn


def _mem_attend_rows(qn, mk, mv):
    outs = []
    for h in range(qn.shape[1] // B_HEAD_DIM):
        hs = slice(h * B_HEAD_DIM, (h + 1) * B_HEAD_DIM)
        s = _bdot_nt(qn[:, hs], mk[:, hs])
        m = jnp.max(s, axis=-1, keepdims=True)
        e = jnp.exp(s - m)
        p = e / jnp.sum(e, axis=-1, keepdims=True)
        outs.append(_bdot(p, mv[:, hs]))
    return jnp.concatenate(outs, axis=-1)


def _post_tail(h1, ox, wxo_ref, gffn_ref, wr_ref, br_ref, h2_ref, hn_ref, route_ref):
    h2 = h1 + jnp.dot(ox.astype(BF16), wxo_ref[...], preferred_element_type=F32)
    h2_ref[...] = h2
    hn = _rms(h2, gffn_ref[...])
    hn_ref[...] = hn
    logits = jnp.dot(hn, wr_ref[...], preferred_element_type=F32,
                     precision=lax.Precision.HIGHEST) + br_ref[...]
    route_ref[...] = _route(logits)


def _post_prompt_kernel(o1_ref, o2_ref, o3_ref, l1_ref, l2_ref, l3_ref, og_ref, z_ref, h_ref,
                        wout_ref, gm_ref, ggdn_ref, gx_ref, wxq_ref, gxq_ref, mk_ref, mv_ref,
                        wxo_ref, gffn_ref, wr_ref, br_ref, h2_ref, hn_ref, route_ref):
    l1, l2, l3 = l1_ref[...], l2_ref[...], l3_ref[...]
    mm = jnp.maximum(jnp.maximum(l1, l2), l3)
    e1, e2, e3 = jnp.exp(l1 - mm), jnp.exp(l2 - mm), jnp.exp(l3 - mm)
    oa = (e1 * o1_ref[...] + e2 * o2_ref[...] + e3 * o3_ref[...]) / (e1 + e2 + e3)
    h1, qn = _post_common(oa, og_ref, z_ref, h_ref, wout_ref, gm_ref, ggdn_ref, gx_ref, wxq_ref, gxq_ref)
    ox = _mem_attend_rows(qn, mk_ref[...], mv_ref[...])
    _post_tail(h1, ox, wxo_ref, gffn_ref, wr_ref, br_ref, h2_ref, hn_ref, route_ref)


def _post_sample_kernel(oa_ref, og_ref, z_ref, h_ref,
                        wout_ref, gm_ref, ggdn_ref, gx_ref, wxq_ref, gxq_ref, mk_ref, mv_ref,
                        wxo_ref, gffn_ref, wr_ref, br_ref, h2_ref, hn_ref, route_ref, *, t_new):
    h1, qn = _post_common(oa_ref[...], og_ref, z_ref, h_ref, wout_ref, gm_ref, ggdn_ref, gx_ref,
                          wxq_ref, gxq_ref)
    xw = qn.shape[1]
    heads = xw // B_HEAD_DIM
    rows = heads * t_new
    ri = lax.broadcasted_iota(jnp.int32, (rows, xw), 0) // t_new
    li = lax.broadcasted_iota(jnp.int32, (rows, xw), 1) // B_HEAD_DIM
    lh = lax.broadcasted_iota(jnp.int32, (t_new, xw), 1) // B_HEAD_DIM
    outs = []
    for sq in range(mk_ref.shape[0]):
        q_rep = jnp.concatenate([qn[sq * t_new:(sq + 1) * t_new, :]] * heads, axis=0)
        s = _bdot_nt(jnp.where(ri == li, q_rep, 0.0), mk_ref[sq])
        m = jnp.max(s, axis=-1, keepdims=True)
        e = jnp.exp(s - m)
        o = _bdot(e / jnp.sum(e, axis=-1, keepdims=True), mv_ref[sq])
        acc = jnp.zeros((t_new, xw), F32)
        for h in range(heads):
            acc = acc + jnp.where(lh == h, o[h * t_new:(h + 1) * t_new, :], 0.0)
        outs.append(acc)
    ox = jnp.concatenate(outs, axis=0)
    _post_tail(h1, ox, wxo_ref, gffn_ref, wr_ref, br_ref, h2_ref, hn_ref, route_ref)


def _post_weights_specs(weights):
    return [pl.BlockSpec(a.shape, lambda i: (0, 0)) for a in weights]


def _post_outs(n, d, tm):
    row = lambda w: pl.BlockSpec((tm, w), lambda i: (i, 0))
    specs = [row(d), row(d), row(ROUTE_LANES)]
    shapes = [jax.ShapeDtypeStruct((n, d), F32), jax.ShapeDtypeStruct((n, d), F32),
              jax.ShapeDtypeStruct((n, ROUTE_LANES), F32)]
    return specs, shapes


def _post_prompt(os_, ls_, og, z, h, pw, mk, mv, *, tm):
    n, d = h.shape
    aw = og.shape[1]
    row = lambda w: pl.BlockSpec((tm, w), lambda i: (i, 0))
    full = lambda a: pl.BlockSpec(a.shape, lambda i: (0, 0))
    w1 = [pw["w_out"], pw["gm128"], pw["g_gdn"], pw["g_xattn"], pw["w_xq"], pw["g_xq"]]
    w2 = [pw["w_xo"], pw["g_ffn"], pw["w_r"], pw["b_r"]]
    specs, shapes = _post_outs(n, d, tm)
    return pl.pallas_call(
        _post_prompt_kernel,
        grid=(n // tm,),
        in_specs=[row(aw)] * 8 + [row(d)] + _post_weights_specs(w1) + [full(mk), full(mv)]
                 + _post_weights_specs(w2),
        out_specs=specs, out_shape=shapes,
        compiler_params=_cparams("parallel"),
        name="post_prompt",
    )(*os_, *ls_, og, z, h, *w1, mk, mv, *w2)


def _post_sample(oa, og, z, h, pw, mk, mv, *, t_new, seqs):
    n, d = h.shape
    aw = og.shape[1]
    tm = t_new * seqs
    row = lambda w: pl.BlockSpec((tm, w), lambda i: (i, 0))
    mem = pl.BlockSpec((seqs,) + mk.shape[1:], lambda i: (i, 0, 0))
    w1 = [pw["w_out"], pw["gm128"], pw["g_gdn"], pw["g_xattn"], pw["w_xq"], pw["g_xq"]]
    w2 = [pw["w_xo"], pw["g_ffn"], pw["w_r"], pw["b_r"]]
    specs, shapes = _post_outs(n, d, tm)
    return pl.pallas_call(
        functools.partial(_post_sample_kernel, t_new=t_new),
        grid=(n // tm,),
        in_specs=[row(aw)] * 3 + [row(d)] + _post_weights_specs(w1) + [mem, mem]
                 + _post_weights_specs(w2),
        out_specs=specs, out_shape=shapes,
        compiler_params=_cparams("parallel"),
        name="post_sample",
    )(oa, og, z, h, *w1, mk, mv, *w2)


def _moe_kernel(blk_e_ref, nblk_ref, tok_ref, x_ref, res_ref, rw_ref, wg_ref, wu_ref, wd_ref,
                out_ref, xs_ref, yw_ref):
    c = pl.program_id(0)
    b = pl.program_id(1)

    @pl.when(b == 0)
    def _():
        out_ref[...] = res_ref[...]

    @pl.when(b < nblk_ref[c])
    def _():
        def gather(r, carry):
            t = tok_ref[0, 0, r]
            xs_ref[pl.ds(r, 1), :] = x_ref[pl.ds(t, 1), :]
            return carry

        lax.fori_loop(0, MOE_BLOCK, gather, 0, unroll=8)
        xb = xs_ref[...].astype(BF16)
        hg = jnp.dot(xb, wg_ref[0], preferred_element_type=F32)
        hu = jnp.dot(xb, wu_ref[0], preferred_element_type=F32)
        act = (hg * _sigmoid(hg) * hu).astype(BF16)
        y = jnp.dot(act, wd_ref[0], preferred_element_type=F32)
        yw_ref[...] = y * rw_ref[0]

        def scatter(r, carry):
            t = tok_ref[0, 0, r]
            out_ref[pl.ds(t, 1), :] = out_ref[pl.ds(t, 1), :] + yw_ref[pl.ds(r, 1), :]
            return carry

        lax.fori_loop(0, MOE_BLOCK, scatter, 0, unroll=8)


def _dispatch(route, chunk, n_experts):
    n = route.shape[0]
    nch = n // chunk
    rows = 2 * chunk
    nb = rows // MOE_BLOCK + n_experts
    e = route[:, :2].astype(jnp.int32).reshape(nch, rows)
    w = route[:, 2:4].reshape(nch, rows)
    t = jnp.broadcast_to((jnp.arange(rows, dtype=jnp.int32) // 2)[None], (nch, rows))
    onehot = (e[:, :, None] == jnp.arange(n_experts, dtype=jnp.int32)[None, None, :]).astype(jnp.int32)
    rank = jnp.take_along_axis(jnp.cumsum(onehot, axis=1), e[:, :, None], axis=2)[:, :, 0] - 1
    counts = jnp.sum(onehot, axis=1)
    padded = (counts + MOE_BLOCK - 1) // MOE_BLOCK * MOE_BLOCK
    pend = jnp.cumsum(padded, axis=1)
    dest = jnp.take_along_axis(pend - padded, e, axis=1) + rank
    ci = jnp.broadcast_to(jnp.arange(nch, dtype=jnp.int32)[:, None], (nch, rows))
    row_tok = jnp.zeros((nch, nb * MOE_BLOCK), jnp.int32).at[ci, dest].set(t)
    row_w = jnp.zeros((nch, nb * MOE_BLOCK), F32).at[ci, dest].set(w)
    nblk = (pend[:, -1] // MOE_BLOCK).astype(jnp.int32)
    starts = jnp.arange(nb, dtype=jnp.int32)[None, :] * MOE_BLOCK
    starts = jnp.minimum(starts, (pend[:, -1:] - MOE_BLOCK))
    blk_e = jnp.sum((pend[:, None, :] <= starts[:, :, None]).astype(jnp.int32), axis=2)
    blk_e = jnp.minimum(blk_e, n_experts - 1).astype(jnp.int32)
    return blk_e.reshape(-1), nblk, row_tok.reshape(nch * nb, 1, MOE_BLOCK), row_w.reshape(nch * nb, MOE_BLOCK, 1), nb


def _moe(hn, h2, route, wg_b, wu_b, wd_b, *, chunk):
    n, d = hn.shape
    n_experts, _, ff = wg_b.shape
    chunk = min(chunk, n)
    assert n % chunk == 0
    nch = n // chunk
    blk_e, nblk, row_tok, row_w, nb = _dispatch(route, chunk, n_experts)
    tok_spec = pl.BlockSpec((1, 1, MOE_BLOCK), lambda c, b, be, nk: (c * nb + b, 0, 0),
                            memory_space=pltpu.SMEM)
    big = pl.BlockSpec((chunk, d), lambda c, b, be, nk: (c, 0))
    grid_spec = pltpu.PrefetchScalarGridSpec(
        num_scalar_prefetch=2,
        grid=(nch, nb),
        in_specs=[tok_spec, big, big,
                  pl.BlockSpec((1, MOE_BLOCK, 1), lambda c, b, be, nk: (c * nb + b, 0, 0)),
                  pl.BlockSpec((1, d, ff), lambda c, b, be, nk: (be[c * nb + b], 0, 0)),
                  pl.BlockSpec((1, d, ff), lambda c, b, be, nk: (be[c * nb + b], 0, 0)),
                  pl.BlockSpec((1, ff, d), lambda c, b, be, nk: (be[c * nb + b], 0, 0))],
        out_specs=big,
        scratch_shapes=[pltpu.VMEM((MOE_BLOCK, d), F32), pltpu.VMEM((MOE_BLOCK, d), F32)],
    )
    return pl.pallas_call(
        _moe_kernel,
        grid_spec=grid_spec,
        out_shape=jax.ShapeDtypeStruct((n, d), F32),
        compiler_params=_cparams("parallel", "arbitrary"),
        name="moe",
    )(blk_e, nblk, row_tok, hn, h2, row_w, wg_b, wu_b, wd_b)


def _tile_row(g, reps, scale=1.0):
    return (jnp.tile(g.astype(F32), reps) * scale)[None, :]


def _layer_weights(p):
    d, in_cols = p["w_in"].shape
    aw = d // 2
    bw = d - aw
    pad = 3 * aw + 4 * bw + ROUTE_LANES - in_cols
    n_experts = p["w_re"].shape[1]
    w_r = jnp.concatenate([p["w_rg"], p["w_re"],
                           jnp.zeros((d, ROUTE_LANES - N_GROUPS - n_experts), F32)], axis=1)
    b_r = jnp.concatenate([p["b_rg"], p["b_re"], jnp.zeros((ROUTE_LANES - N_GROUPS - n_experts,), F32)])
    lane_pad = lambda v: jnp.concatenate([v.astype(F32), jnp.zeros((ROUTE_LANES - v.shape[0],), F32)])[None, :]
    return dict(
        aw=aw, bw=bw,
        g_mix=p["g_mix"][None, :],
        w_in=jnp.pad(p["w_in"], ((0, 0), (0, pad))).astype(BF16),
        gm64=_group_mean_matrix(aw, A_HEAD_DIM),
        gq=_tile_row(p["g_qa"], aw // A_HEAD_DIM, A_HEAD_DIM ** -0.5),
        gk=_tile_row(p["g_ka"], aw // A_HEAD_DIM),
        conv_w=p["conv_w"],
        alog=lane_pad(p["a_log"]),
        dtb=lane_pad(jnp.concatenate([p["dt_bias"], jnp.zeros_like(p["dt_bias"])])),
        w_out=p["w_out"].astype(BF16),
        gm128=_group_mean_matrix(bw, B_HEAD_DIM),
        g_gdn=_tile_row(p["g_gdn"], bw // B_HEAD_DIM),
        g_xattn=p["g_xattn"][None, :],
        w_xq=p["w_xq"].astype(BF16),
        g_xq=_tile_row(p["g_xq"], p["w_xq"].shape[1] // B_HEAD_DIM, B_HEAD_DIM ** -0.5),
        w_xo=p["w_xo"].astype(BF16),
        g_ffn=p["g_ffn"][None, :],
        w_r=w_r, b_r=b_r[None, :],
        w_gate=p["w_gate"].astype(BF16), w_up=p["w_up"].astype(BF16), w_down=p["w_down"].astype(BF16),
    )


def _gdn(u_pre, prev, ab, s0, pw, *, seg, chunks):
    w, uv, qg, kt, qk, last = _gdn_prep(u_pre, prev, ab, pw["conv_w"], pw["alog"], pw["dtb"],
                                        seg=seg, chunks=chunks)
    return _gdn_scan(w, uv, qg, kt, qk, last, s0, seg=seg)


def _prompt_layer(h, mem, pw, praw):
    nb_, s, d = h.shape
    assert nb_ == 1
    x = h.reshape(s, d)
    aw, bw = pw["aw"], pw["bw"]
    q, k, v, u_pre, z, ab = _inproj(x, pw["g_mix"], pw["w_in"], pw["gm64"], pw["gq"], pw["gk"],
                                    aw=aw, bw=bw, tm=256)
    res = [_dil_attn(q, k, v, w_, d_) for w_, d_ in DILATIONS]
    c = GDN_CHUNK
    tails = u_pre.reshape(s // c, c, 3 * bw)[:, c - PREV_ROWS:, :]
    prev = jnp.concatenate([jnp.zeros((1, PREV_ROWS, 3 * bw), F32), tails[:-1]], axis=0)
    s0 = jnp.zeros((1, B_HEADS, B_HEAD_DIM, B_HEAD_DIM), F32)
    og, s_fin = _gdn(u_pre, prev, ab, s0, pw, seg=c, chunks=8)
    mk, mv = _mem_kv(mem.reshape(mem.shape[1], d), praw["g_mem"][None, :], praw["w_xk"].astype(BF16),
                     praw["w_xv"].astype(BF16), pw["gm128"], _tile_row(praw["g_xk"], bw // B_HEAD_DIM))
    h2, hn, route = _post_prompt([r[0] for r in res], [r[1] for r in res], og, z, x, pw, mk, mv, tm=256)
    y = _moe(hn, h2, route, pw["w_gate"], pw["w_up"], pw["w_down"], chunk=MOE_CHUNK)
    keep = min(DILATIONS[-1][0], s)
    heads = aw // A_HEAD_DIM
    new_k = k[s - keep:].reshape(1, keep, heads, A_HEAD_DIM)
    new_v = v[s - keep:].reshape(1, keep, heads, A_HEAD_DIM)
    conv_new = u_pre[s - (CONV_WIDTH - 1):].reshape(1, CONV_WIDTH - 1, 3 * bw)
    xh = mk.shape[1] // B_HEAD_DIM
    return (y.reshape(1, s, d), new_k, new_v, conv_new, s_fin,
            mk.reshape(1, -1, xh, B_HEAD_DIM), mv.reshape(1, -1, xh, B_HEAD_DIM))


def _sample_layer(h, win_k, win_v, conv_prev, s0, mem_k, mem_v, pw):
    nseq, t_new, d = h.shape
    aw, bw = pw["aw"], pw["bw"]
    n = nseq * t_new
    assert t_new == 8 and GDN_CHUNK % t_new == 0
    x = h.reshape(n, d)
    q, k, v, u_pre, z, ab = _inproj(x, pw["g_mix"], pw["w_in"], pw["gm64"], pw["gq"], pw["gk"],
                                    aw=aw, bw=bw, tm=256)
    w_buf = win_k.shape[1]
    heads = aw // A_HEAD_DIM
    r3 = lambda t_: t_.reshape(nseq, t_new, aw)
    new_k, new_v, oa = _step_attn(r3(q), r3(k), r3(v), win_k.reshape(nseq, w_buf, aw),
                                  win_v.reshape(nseq, w_buf, aw))
    prev = jnp.concatenate([jnp.zeros((nseq, PREV_ROWS - (CONV_WIDTH - 1), 3 * bw), F32),
                            conv_prev.astype(F32)], axis=1)
    og, s_fin = _gdn(u_pre, prev, ab, s0, pw, seg=t_new, chunks=4)
    xw = mem_k.shape[2] * mem_k.shape[3]
    h2, hn, route = _post_sample(oa.reshape(n, aw), og, z, x, pw, mem_k.reshape(nseq, -1, xw),
                                 mem_v.reshape(nseq, -1, xw), t_new=t_new, seqs=8)
    y = _moe(hn, h2, route, pw["w_gate"], pw["w_up"], pw["w_down"], chunk=MOE_CHUNK)
    conv_new = u_pre.reshape(nseq, t_new, 3 * bw)[:, t_new - (CONV_WIDTH - 1):, :]
    return (y.reshape(nseq, t_new, d), new_k.reshape(nseq, w_buf, heads, A_HEAD_DIM),
            new_v.reshape(nseq, w_buf, heads, A_HEAD_DIM), conv_new, s_fin)


def kernel(x_prompt, x_sample, mem_prompt, cache_win_k, cache_win_v, state_conv, state_delta, cache_mem_k, cache_mem_v, g_mix, w_in, g_qa, g_ka, conv_w, a_log, dt_bias, g_gdn, w_out, g_xattn, g_mem, w_xq, w_xk, w_xv, g_xq, g_xk, w_xo, g_ffn, w_rg, b_rg, w_re, b_re, w_gate, w_up, w_down):
    depth = w_in.shape[0]
    hp, hs = x_prompt, x_sample
    outs = [[] for _ in range(10)]
    for l in range(depth):
        praw = dict(g_mix=g_mix[l], w_in=w_in[l], g_qa=g_qa[l], g_ka=g_ka[l], conv_w=conv_w[l],
                    a_log=a_log[l], dt_bias=dt_bias[l], g_gdn=g_gdn[l], w_out=w_out[l],
                    g_xattn=g_xattn[l], g_mem=g_mem[l], w_xq=w_xq[l], w_xk=w_xk[l], w_xv=w_xv[l],
                    g_xq=g_xq[l], g_xk=g_xk[l], w_xo=w_xo[l], g_ffn=g_ffn[l], w_rg=w_rg[l],
                    b_rg=b_rg[l], w_re=w_re[l], b_re=b_re[l], w_gate=w_gate[l], w_up=w_up[l],
                    w_down=w_down[l])
        pw = _layer_weights(praw)
        hp, k_p, v_p, c_p, s_p, mk, mv = _prompt_layer(hp, mem_prompt, pw, praw)
        hs, k_s, v_s, c_s, s_s = _sample_layer(hs, cache_win_k[l], cache_win_v[l], state_conv[l],
                                               state_delta[l].astype(F32), cache_mem_k[l], cache_mem_v[l], pw)
        for lst, val in zip(outs, (k_p, v_p, c_p, s_p, mk, mv, k_s, v_s, c_s, s_s)):
            lst.append(val)
    st = [jnp.stack(o) for o in outs]
    st[3] = st[3].astype(state_delta.dtype)
    st[9] = st[9].astype(state_delta.dtype)
    return (hp, hs, *st)
```

```python
import functools

import jax
import jax.numpy as jnp
from jax import lax
from jax.experimental import pallas as pl
from jax.experimental.pallas import tpu as pltpu

F32 = jnp.float32
BF16 = jnp.bfloat16
EPS = 1e-6
NEG_INF = float("-inf")

A_HEAD_DIM = 64
B_HEAD_DIM = 128
B_HEADS = 4
BAND = 128
DILATIONS = ((128, 1), (512, 4), (2048, 16))
GDN_CHUNK = 64
CONV_WIDTH = 4
PREV_ROWS = 8
N_GROUPS = 4
GROUP_EXPERTS = 8
ROUTE_LANES = 128
MOE_BLOCK = 128
MOE_CHUNK = 2048
VMEM_LIMIT = 56 * 1024 * 1024


def _cparams(*sem):
    return pltpu.CompilerParams(dimension_semantics=sem, vmem_limit_bytes=VMEM_LIMIT)


def _bdot(a, b):
    return jnp.dot(a.astype(BF16), b.astype(BF16), preferred_element_type=F32)


def _bdot_nt(a, b):
    return lax.dot_general(a.astype(BF16), b.astype(BF16), (((1,), (1,)), ((), ())),
                           preferred_element_type=F32)


def _split3(x):
    hi = x.astype(BF16)
    r1 = x - hi.astype(F32)
    mid = r1.astype(BF16)
    lo = (r1 - mid.astype(F32)).astype(BF16)
    return hi, mid, lo


def _dot_exact_lhs(a01, x):
    a = a01.astype(BF16)
    hi, mid, lo = _split3(x)
    d = lambda p: jnp.dot(a, p, preferred_element_type=F32)
    return d(hi) + d(mid) + d(lo)


def _group_mean_sq(x, gm):
    sq = x * x
    hi = sq.astype(BF16)
    lo = (sq - hi.astype(F32)).astype(BF16)
    return (jnp.dot(hi, gm, preferred_element_type=F32) + jnp.dot(lo, gm, preferred_element_type=F32))


def _rms(x, g):
    return x * lax.rsqrt(jnp.mean(x * x, axis=-1, keepdims=True) + EPS) * g


def _sigmoid(x):
    return 1.0 / (1.0 + jnp.exp(-x))


def _group_mean_matrix(width, group):
    i = jnp.arange(width)
    return jnp.where((i[:, None] // group) == (i[None, :] // group), 1.0 / group, 0.0).astype(BF16)


def _inproj_kernel(x_ref, g_ref, w_ref, gm_ref, gq_ref, gk_ref,
                   q_ref, k_ref, v_ref, u_ref, z_ref, ab_ref, *, aw, bw):
    nb = _rms(x_ref[...], g_ref[...]).astype(BF16)

    def proj(lo, hi):
        return jnp.dot(nb, w_ref[:, lo:hi], preferred_element_type=F32)

    gm = gm_ref[...]
    qa = proj(0, aw)
    q_ref[...] = qa * lax.rsqrt(_group_mean_sq(qa, gm) + EPS) * gq_ref[...]
    ka = proj(aw, 2 * aw)
    k_ref[...] = ka * lax.rsqrt(_group_mean_sq(ka, gm) + EPS) * gk_ref[...]
    v_ref[...] = proj(2 * aw, 3 * aw)
    u_ref[...] = proj(3 * aw, 3 * aw + 3 * bw)
    z_ref[...] = proj(3 * aw + 3 * bw, 3 * aw + 4 * bw)
    ab_ref[...] = proj(3 * aw + 4 * bw, 3 * aw + 4 * bw + ROUTE_LANES)


def _inproj(x, g_mix, w_in_b, gm64, gq_t, gk_t, *, aw, bw, tm):
    n, d = x.shape
    cols = w_in_b.shape[1]
    row = lambda w: pl.BlockSpec((tm, w), lambda i: (i, 0))
    full = lambda a: pl.BlockSpec(a.shape, lambda i: (0, 0))
    out_w = (aw, aw, aw, 3 * bw, bw, ROUTE_LANES)
    return pl.pallas_call(
        functools.partial(_inproj_kernel, aw=aw, bw=bw),
        grid=(n // tm,),
        in_specs=[row(d), full(g_mix), pl.BlockSpec((d, cols), lambda i: (0, 0)),
                  full(gm64), full(gq_t), full(gk_t)],
        out_specs=[row(w) for w in out_w],
        out_shape=[jax.ShapeDtypeStruct((n, w), F32) for w in out_w],
        compiler_params=_cparams("parallel"),
        name="inproj",
    )(x, g_mix, w_in_b, gm64, gq_t, gk_t)


def _dil_attn_kernel(q_ref, kp_ref, kc_ref, vp_ref, vc_ref, o_ref, lse_ref, *, sub_win):
    n = pl.program_id(1)
    qi = lax.broadcasted_iota(jnp.int32, (BAND, 2 * BAND), 0)
    kj = lax.broadcasted_iota(jnp.int32, (BAND, 2 * BAND), 1)
    dist = qi + BAND - kj
    first = jnp.where(n > 0, 0, BAND)
    valid = (dist >= 0) & (dist <= sub_win) & (kj >= first)
    bias = jnp.where(valid, 0.0, NEG_INF)
    lane = lax.broadcasted_iota(jnp.int32, (BAND, 2 * A_HEAD_DIM), 1)
    lo_half = lane < A_HEAD_DIM

    k_all = jnp.concatenate([kp_ref[...], kc_ref[...]], axis=0).astype(BF16)
    v_all = jnp.concatenate([vp_ref[...], vc_ref[...]], axis=0).astype(BF16)
    q = q_ref[...]
    pair = 2 * A_HEAD_DIM
    for p in range(q.shape[1] // pair):
        sl = slice(p * pair, (p + 1) * pair)
        kpair, vpair, qpair = k_all[:, sl], v_all[:, sl], q[:, sl]
        res = []
        for half in range(2):
            keep = lo_half if half == 0 else jnp.logical_not(lo_half)
            qm = jnp.where(keep, qpair, 0.0)
            s = _bdot_nt(qm, kpair) + bias
            m = jnp.max(s, axis=-1, keepdims=True)
            e = jnp.exp(s - m)
            den = jnp.sum(e, axis=-1, keepdims=True)
            o = jnp.dot(e.astype(BF16), vpair, preferred_element_type=F32) / den
            res.append((o, m + jnp.log(den)))
        o_ref[:, sl] = jnp.where(lo_half, res[0][0], res[1][0])
        lse_ref[:, sl] = jnp.where(lo_half, res[0][1], res[1][1])


def _dil_attn(q, k, v, window, dil):
    s, aw = q.shape
    length = s // dil
    assert s % dil == 0 and length % BAND == 0 and window // dil <= BAND
    nb = length // BAND
    view = lambda t: t.reshape(length, dil * aw)
    cur = pl.BlockSpec((BAND, aw), lambda r, n: (n, r))
    prev = pl.BlockSpec((BAND, aw), lambda r, n: (jnp.maximum(n - 1, 0), r))
    o, lse = pl.pallas_call(
        functools.partial(_dil_attn_kernel, sub_win=window // dil),
        grid=(dil, nb),
        in_specs=[cur, prev, cur, prev, cur],
        out_specs=[cur, cur],
        out_shape=[jax.ShapeDtypeStruct((length, dil * aw), F32)] * 2,
        compiler_params=_cparams("parallel", "arbitrary"),
        name=f"dil_attn_{dil}",
    )(view(q), view(k), view(k), view(v), view(v))
    return o.reshape(s, aw), lse.reshape(s, aw)


def _step_attn_kernel(q_ref, kn_ref, vn_ref, kc_ref, vc_ref, ok_ref, ov_ref, oa_ref,
                      kb_ref, vb_ref, *, w_buf, t_new):
    aw = q_ref.shape[2]
    heads = aw // A_HEAD_DIM
    lanes = kn_ref.shape[2]
    kt, vt = kc_ref[0], vc_ref[0]
    kn, vn = kn_ref[0], vn_ref[0]
    is_new = lax.broadcasted_iota(jnp.int32, (aw, lanes), 1) >= lanes - t_new
    for src, new, dst in ((kt, kn, ok_ref), (vt, vn, ov_ref)):
        rolled = pltpu.roll(src, w_buf - t_new, axis=1)
        dst[0, :, 0:w_buf - lanes] = rolled[:, 0:w_buf - lanes]
        dst[0, :, w_buf - lanes:w_buf] = jnp.where(is_new, new, rolled[:, w_buf - lanes:w_buf])
    kb_ref[...] = kt.astype(BF16)
    vb_ref[...] = vt.astype(BF16)

    rows = heads * t_new
    ri = lax.broadcasted_iota(jnp.int32, (rows, aw), 0)
    li = lax.broadcasted_iota(jnp.int32, (rows, aw), 1)
    q_rep = jnp.concatenate([q_ref[0]] * heads, axis=0)
    q_blk = jnp.where((ri // t_new) == (li // A_HEAD_DIM), q_rep, 0.0).astype(BF16)
    ncol = w_buf + lanes
    s_all = jnp.concatenate([jnp.dot(q_blk, kb_ref[...], preferred_element_type=F32),
                             jnp.dot(q_blk, kn.astype(BF16), preferred_element_type=F32)], axis=1)

    def branch(window, dil, col0):
        s = s_all[:, col0:]
        t = lax.broadcasted_iota(jnp.int32, s.shape, 0) % t_new
        cidx = lax.broadcasted_iota(jnp.int32, s.shape, 1) + col0
        r = jnp.where(cidx < w_buf, cidx, cidx - (lanes - t_new))
        diff = w_buf + t - r
        valid = (diff >= 0) & (diff <= window) & ((diff & (dil - 1)) == 0)
        valid = valid & ((cidx < w_buf) | (cidx >= ncol - t_new))
        s = jnp.where(valid, s, NEG_INF)
        m = jnp.max(s, axis=-1, keepdims=True)
        e = jnp.exp(s - m)
        den = jnp.sum(e, axis=-1, keepdims=True)
        return e, den, m + jnp.log(den)

    parts = []
    for window, dil in DILATIONS:
        assert dil & (dil - 1) == 0
        col0 = max(0, (w_buf - window) // 128 * 128)
        parts.append((col0,) + branch(window, dil, col0))
    mm = functools.reduce(jnp.maximum, [p[3] for p in parts])
    wexp = [jnp.exp(p[3] - mm) for p in parts]
    wsum = functools.reduce(lambda a, b: a + b, wexp)
    p_tot = jnp.zeros((rows, ncol), F32)
    for (col0, e, den, _), we in zip(parts, wexp):
        pe = e * (we / (wsum * den))
        if col0:
            pe = jnp.concatenate([jnp.zeros((rows, col0), F32), pe], axis=1)
        p_tot = p_tot + pe
    o = _bdot_nt(p_tot[:, :w_buf], vb_ref[...]) + _bdot_nt(p_tot[:, w_buf:], vn)
    lh = lax.broadcasted_iota(jnp.int32, (t_new, aw), 1) // A_HEAD_DIM
    acc = jnp.zeros((t_new, aw), F32)
    for h in range(heads):
        acc = acc + jnp.where(lh == h, o[h * t_new:(h + 1) * t_new, :], 0.0)
    oa_ref[0] = acc


def _step_attn(q, kn_t, vn_t, cache_kt, cache_vt):
    nseq, aw, w_buf = cache_kt.shape
    t_new = q.shape[1]
    lanes = kn_t.shape[2]
    assert w_buf % lanes == 0 and t_new % 8 == 0 and t_new <= lanes
    qs = pl.BlockSpec((1, t_new, aw), lambda i: (i, 0, 0))
    new = pl.BlockSpec((1, aw, lanes), lambda i: (i, 0, 0))
    win = pl.BlockSpec((1, aw, w_buf), lambda i: (i, 0, 0))
    return pl.pallas_call(
        functools.partial(_step_attn_kernel, w_buf=w_buf, t_new=t_new),
        grid=(nseq,),
        in_specs=[qs, new, new, win, win],
        out_specs=[win, win, qs],
        out_shape=[jax.ShapeDtypeStruct(cache_kt.shape, F32), jax.ShapeDtypeStruct(cache_vt.shape, F32),
                   jax.ShapeDtypeStruct(q.shape, F32)],
        scratch_shapes=[pltpu.VMEM((aw, w_buf), BF16), pltpu.VMEM((aw, w_buf), BF16)],
        compiler_params=_cparams("parallel"),
        name="step_attn",
    )(q, kn_t, vn_t, cache_kt, cache_vt)


def _gdn_prep_kernel(u_ref, prev_ref, ab_ref, cw_ref, alog_ref, dtb_ref,
                     w_ref, uv_ref, qg_ref, kt_ref, qk_ref, last_ref, full_ref, act_ref,
                     *, seg, conv_seg, chunks):
    c = GDN_CHUNK
    dk = B_HEAD_DIM
    bw = B_HEADS * dk
    ii = lax.broadcasted_iota(jnp.int32, (c, c), 0)
    jj = lax.broadcasted_iota(jnp.int32, (c, c), 1)
    same = (ii // seg) == (jj // seg)
    incl = same & (ii >= jj)
    strict = same & (ii > jj)
    eye = ii == jj
    ones_c = jnp.ones((c, c), BF16)
    seg_cols = jnp.where((lax.broadcasted_iota(jnp.int32, (c, dk), 0) % seg)
                         == lax.broadcasted_iota(jnp.int32, (c, dk), 1), 1.0, 0.0)
    cw = cw_ref[...]
    rows = chunks * c

    for sgi in range(rows // conv_seg):
        base = sgi * (conv_seg + PREV_ROWS)
        full_ref[base:base + PREV_ROWS, :] = prev_ref[sgi]
        full_ref[base + PREV_ROWS:base + PREV_ROWS + conv_seg, :] = u_ref[sgi * conv_seg:(sgi + 1) * conv_seg, :]
        y = jnp.zeros((conv_seg, 3 * bw), F32)
        for j in range(CONV_WIDTH):
            off = base + PREV_ROWS - (CONV_WIDTH - 1) + j
            y = y + full_ref[off:off + conv_seg, :] * cw[j:j + 1, :]
        act_ref[sgi * conv_seg:(sgi + 1) * conv_seg, :] = y * _sigmoid(y)

    ri = lax.broadcasted_iota(jnp.int32, (rows, rows), 0)
    rj = lax.broadcasted_iota(jnp.int32, (rows, rows), 1)
    rsame = (ri // seg) == (rj // seg)
    ab = ab_ref[...]
    sp = ab + dtb_ref[...]
    sp = jnp.maximum(sp, 0.0) + jnp.log(1.0 + jnp.exp(-jnp.abs(sp)))
    g_all = -jnp.exp(alog_ref[...]) * sp
    gcum_all = _dot_exact_lhs(jnp.where(rsame & (ri >= rj), 1.0, 0.0), g_all)
    gtot_all = _dot_exact_lhs(jnp.where(rsame, 1.0, 0.0), g_all)
    beta_all = _sigmoid(ab)
    eye_f = jnp.where(eye, 1.0, 0.0)

    grow = []
    for ci in range(chunks):
        rs = slice(ci * c, (ci + 1) * c)
        diag = jnp.concatenate([jnp.where(eye, gcum_all[rs, h:h + 1], 0.0) for h in range(B_HEADS)], axis=1)
        grow.append(_dot_exact_lhs(ones_c, diag))

    prob = [(ci, h) for ci in range(chunks) for h in range(B_HEADS)]
    col = lambda arr, ci, lane: arr[ci * c:(ci + 1) * c, lane:lane + 1]
    gc = [col(gcum_all, ci, h) for ci, h in prob]
    gt = [col(gtot_all, ci, h) for ci, h in prob]
    beta = [col(beta_all, ci, B_HEADS + h) for ci, h in prob]
    q, k, v = [], [], []
    for ci, h in prob:
        rs = slice(ci * c, (ci + 1) * c)
        qh = act_ref[rs, h * dk:(h + 1) * dk]
        kh = act_ref[rs, bw + h * dk:bw + (h + 1) * dk]
        q.append(qh * lax.rsqrt(jnp.sum(qh * qh, axis=-1, keepdims=True) + EPS) * (dk ** -0.5))
        k.append(kh * lax.rsqrt(jnp.sum(kh * kh, axis=-1, keepdims=True) + EPS))
        v.append(act_ref[rs, 2 * bw + h * dk:2 * bw + (h + 1) * dk])
    e_incl = [jnp.exp(jnp.where(incl, gc[i] - grow[ci][:, h * c:(h + 1) * c], NEG_INF))
              for i, (ci, h) in enumerate(prob)]
    qkk = [_bdot_nt(jnp.concatenate([q[i], k[i]], axis=0), k[i]) for i in range(len(prob))]
    a = [beta[i] * jnp.where(strict, e_incl[i], 0.0) * qkk[i][c:, :] for i in range(len(prob))]
    blk = 8
    inblk = (ii // blk) == (jj // blk)
    d0 = [jnp.where(inblk, t, 0.0) for t in a]
    d2 = [_bdot(t, t) for t in d0]
    d4 = [_bdot(t, t) for t in d2]
    x = [(eye_f - t) + _bdot(eye_f - t, t2) for t, t2 in zip(d0, d2)]
    x = [t + _bdot(t, t4) for t, t4 in zip(x, d4)]
    while blk < seg:
        off_blk = ((ii // (2 * blk)) == (jj // (2 * blk))) & ((ii // blk) != (jj // blk))
        xe = [_bdot(t, jnp.where(off_blk, ta, 0.0)) for t, ta in zip(x, a)]
        x = [t - _bdot(te, t) for t, te in zip(x, xe)]
        blk *= 2
    gamma = [jnp.exp(t) for t in gc]
    wuv = [_bdot(x[i], jnp.concatenate([(beta[i] * gamma[i]) * k[i], beta[i] * v[i]], axis=1))
           for i in range(len(prob))]
    for i, (ci, h) in enumerate(prob):
        rs = slice(ci * c, (ci + 1) * c)
        hs = slice(h * dk, (h + 1) * dk)
        w_ref[rs, hs] = wuv[i][:, :dk]
        uv_ref[rs, hs] = wuv[i][:, dk:]
        qg_ref[rs, hs] = gamma[i] * q[i]
        kt_ref[rs, hs] = k[i] * jnp.exp(gt[i] - gc[i])
        qk = qkk[i][:c, :] * e_incl[i]
        qk_ref[rs, hs] = (jnp.concatenate([qk, jnp.zeros((c, dk - c), F32)], axis=1) if seg == c
                          else _bdot(qk, seg_cols))
        last_ref[rs, hs] = jnp.broadcast_to(jnp.exp(gt[i]), (c, dk))


def _gdn_prep(u_pre, prev, ab, conv_w, alog_row, dtb_row, *, seg, conv_seg, chunks):
    n, cw = u_pre.shape
    bw = cw // 3
    c = GDN_CHUNK
    rows = chunks * c
    assert n % rows == 0 and c % seg == 0 and seg % 8 == 0 and rows % conv_seg == 0 and conv_seg % seg == 0
    ncs = rows // conv_seg
    row = lambda w: pl.BlockSpec((rows, w), lambda i: (i, 0))
    full = lambda a: pl.BlockSpec(a.shape, lambda i: (0, 0))
    return pl.pallas_call(
        functools.partial(_gdn_prep_kernel, seg=seg, conv_seg=conv_seg, chunks=chunks),
        grid=(n // rows,),
        in_specs=[row(cw), pl.BlockSpec((ncs, PREV_ROWS, cw), lambda i: (i, 0, 0)),
                  row(ROUTE_LANES), full(conv_w), full(alog_row), full(dtb_row)],
        out_specs=[row(bw)] * 6,
        out_shape=[jax.ShapeDtypeStruct((n, bw), F32)] * 6,
        scratch_shapes=[pltpu.VMEM((ncs * (conv_seg + PREV_ROWS), cw), F32), pltpu.VMEM((rows, cw), F32)],
        compiler_params=_cparams("parallel"),
        name=f"gdn_prep_{seg}",
    )(u_pre, prev, ab, conv_w, alog_row, dtb_row)


def _gdn_scan_kernel(w_ref, uv_ref, qg_ref, kt_ref, qk_ref, last_ref, s0_ref, o_ref, s_ref, *, seg):
    dk = B_HEAD_DIM

    @pl.when(pl.program_id(1) == 0)
    def _():
        s_ref[...] = s0_ref[...]

    pad = dk - seg
    padr = lambda t: jnp.concatenate([t, jnp.zeros((pad, dk), F32)], axis=0) if pad else t
    ii = lax.broadcasted_iota(jnp.int32, (dk, dk), 0)
    jj = lax.broadcasted_iota(jnp.int32, (dk, dk), 1)
    eye = jnp.where(ii == jj, 1.0, 0.0).astype(BF16)
    for h in range(B_HEADS):
        hs = slice(h * dk, (h + 1) * dk)
        s = s_ref[0, h]
        sb = s.astype(BF16)
        u = padr(uv_ref[:, hs]) - _bdot(padr(w_ref[:, hs]), sb)
        ub = u.astype(BF16)
        o = _bdot(padr(qg_ref[:, hs]), sb) + _bdot(padr(qk_ref[:, hs]), ub)
        o_ref[:, hs] = o[:seg, :]
        ktt = _bdot_nt(eye, padr(kt_ref[:, hs]))
        s_ref[0, h] = last_ref[0:1, hs] * s + _bdot(ktt, ub)


def _gdn_scan(w, uv, qg, kt, qk, last, s0, *, seg):
    n, bw = w.shape
    nseq = s0.shape[0]
    per_seq = n // (nseq * seg)
    row = pl.BlockSpec((seg, bw), lambda s, i: (s * per_seq + i, 0))
    st = pl.BlockSpec((1,) + s0.shape[1:], lambda s, i: (s, 0, 0, 0))
    return pl.pallas_call(
        functools.partial(_gdn_scan_kernel, seg=seg),
        grid=(nseq, per_seq),
        in_specs=[row] * 6 + [st],
        out_specs=[row, st],
        out_shape=[jax.ShapeDtypeStruct((n, bw), F32), jax.ShapeDtypeStruct(s0.shape, F32)],
        compiler_params=_cparams("parallel", "arbitrary"),
        name=f"gdn_scan_{seg}",
    )(w, uv, qg, kt, qk, last, s0)


def _mem_kv_kernel(mem_ref, g_ref, wk_ref, wv_ref, gm_ref, gk_ref, k_ref, v_ref):
    mn = _rms(mem_ref[...], g_ref[...]).astype(BF16)
    k = jnp.dot(mn, wk_ref[...], preferred_element_type=F32)
    k_ref[...] = k * lax.rsqrt(_group_mean_sq(k, gm_ref[...]) + EPS) * gk_ref[...]
    v_ref[...] = jnp.dot(mn, wv_ref[...], preferred_element_type=F32)


def _mem_kv(mem, g_mem, w_xk_b, w_xv_b, gm128, gk_t):
    m = mem.shape[0]
    xw = w_xk_b.shape[1]
    return pl.pallas_call(
        _mem_kv_kernel,
        out_shape=[jax.ShapeDtypeStruct((m, xw), F32)] * 2,
        compiler_params=pltpu.CompilerParams(vmem_limit_bytes=VMEM_LIMIT),
        name="mem_kv",
    )(mem, g_mem, w_xk_b, w_xv_b, gm128, gk_t)


def _route(logits):
    lane = lax.broadcasted_iota(jnp.int32, logits.shape, 1)
    lane_f = lane.astype(F32)
    big = float(ROUTE_LANES)
    lg = jnp.where(lane < N_GROUPS, logits, NEG_INF)
    mg = jnp.max(lg, axis=-1, keepdims=True)
    zg = jnp.sum(jnp.exp(lg - mg), axis=-1, keepdims=True)
    pg_top = 1.0 / zg
    gidx = jnp.min(jnp.where(lg == mg, lane_f, big), axis=-1, keepdims=True)
    e_lo = N_GROUPS + GROUP_EXPERTS * gidx
    emask = (lane_f >= e_lo) & (lane_f < e_lo + GROUP_EXPERTS)
    le = jnp.where(emask, logits, NEG_INF)
    me = jnp.max(le, axis=-1, keepdims=True)
    ee = jnp.exp(le - me)
    pe = ee / jnp.sum(ee, axis=-1, keepdims=True)
    pe = jnp.where(emask, pe, -1.0)
    p1 = jnp.max(pe, axis=-1, keepdims=True)
    i1 = jnp.min(jnp.where(pe == p1, lane_f, big), axis=-1, keepdims=True)
    pe2 = jnp.where(lane_f == i1, -1.0, pe)
    p2 = jnp.max(pe2, axis=-1, keepdims=True)
    i2 = jnp.min(jnp.where(pe2 == p2, lane_f, big), axis=-1, keepdims=True)
    den = p1 + p2
    w1 = pg_top * p1 / den
    w2 = pg_top * p2 / den
    out = jnp.where(lane == 0, i1 - N_GROUPS, 0.0)
    out = jnp.where(lane == 1, i2 - N_GROUPS, out)
    out = jnp.where(lane == 2, w1, out)
    return jnp.where(lane == 3, w2, out)


def _post_common(oa, og_ref, z_ref, h_ref, wout_ref, gm_ref, ggdn_ref, gx_ref, wxq_ref, gxq_ref):
    og = og_ref[...]
    z = z_ref[...]
    ob = og * lax.rsqrt(_group_mean_sq(og, gm_ref[...]) + EPS) * ggdn_ref[...] * (z * _sigmoid(z))
    cat = jnp.concatenate([oa, ob], axis=-1).astype(BF16)
    h1 = h_ref[...] + jnp.dot(cat, wout_ref[...], preferred_element_type=F32)
    hx = _rms(h1, gx_ref[...]).astype(BF16)
    q = jnp.dot(hx, wxq_ref[...], preferred_element_type=F32)
    qn = q * lax.rsqrt(_group_mean_sq(q, gm_ref[...]) + EPS) * gxq_ref[...]
    return h1, qn


def _mem_attend_rows(qn, mk, mv):
    outs = []
    for h in range(qn.shape[1] // B_HEAD_DIM):
        hs = slice(h * B_HEAD_DIM, (h + 1) * B_HEAD_DIM)
        s = _bdot_nt(qn[:, hs], mk[:, hs])
        m = jnp.max(s, axis=-1, keepdims=True)
        e = jnp.exp(s - m)
        p = e / jnp.sum(e, axis=-1, keepdims=True)
        outs.append(_bdot(p, mv[:, hs]))
    return jnp.concatenate(outs, axis=-1)


def _post_tail(h1, ox, wxo_ref, gffn_ref, wr_ref, br_ref, h2_ref, hn_ref, route_ref):
    h2 = h1 + jnp.dot(ox.astype(BF16), wxo_ref[...], preferred_element_type=F32)
    h2_ref[...] = h2
    hn = _rms(h2, gffn_ref[...])
    hn_ref[...] = hn
    logits = jnp.dot(hn, wr_ref[...], preferred_element_type=F32,
                     precision=lax.Precision.HIGHEST) + br_ref[...]
    route_ref[...] = _route(logits)


def _post_prompt_kernel(o1_ref, o2_ref, o3_ref, l1_ref, l2_ref, l3_ref, og_ref, z_ref, h_ref,
                        wout_ref, gm_ref, ggdn_ref, gx_ref, wxq_ref, gxq_ref, mk_ref, mv_ref,
                        wxo_ref, gffn_ref, wr_ref, br_ref, h2_ref, hn_ref, route_ref):
    l1, l2, l3 = l1_ref[...], l2_ref[...], l3_ref[...]
    mm = jnp.maximum(jnp.maximum(l1, l2), l3)
    e1, e2, e3 = jnp.exp(l1 - mm), jnp.exp(l2 - mm), jnp.exp(l3 - mm)
    oa = (e1 * o1_ref[...] + e2 * o2_ref[...] + e3 * o3_ref[...]) / (e1 + e2 + e3)
    h1, qn = _post_common(oa, og_ref, z_ref, h_ref, wout_ref, gm_ref, ggdn_ref, gx_ref, wxq_ref, gxq_ref)
    ox = _mem_attend_rows(qn, mk_ref[...], mv_ref[...])
    _post_tail(h1, ox, wxo_ref, gffn_ref, wr_ref, br_ref, h2_ref, hn_ref, route_ref)


def _post_sample_kernel(oa_ref, og_ref, z_ref, h_ref,
                        wout_ref, gm_ref, ggdn_ref, gx_ref, wxq_ref, gxq_ref, mk_ref, mv_ref,
                        wxo_ref, gffn_ref, wr_ref, br_ref, h2_ref, hn_ref, route_ref, *, t_new):
    h1, qn = _post_common(oa_ref[...], og_ref, z_ref, h_ref, wout_ref, gm_ref, ggdn_ref, gx_ref,
                          wxq_ref, gxq_ref)
    xw = qn.shape[1]
    heads = xw // B_HEAD_DIM
    rows = heads * t_new
    ri = lax.broadcasted_iota(jnp.int32, (rows, xw), 0) // t_new
    li = lax.broadcasted_iota(jnp.int32, (rows, xw), 1) // B_HEAD_DIM
    lh = lax.broadcasted_iota(jnp.int32, (t_new, xw), 1) // B_HEAD_DIM
    outs = []
    for sq in range(mk_ref.shape[0]):
        q_rep = jnp.concatenate([qn[sq * t_new:(sq + 1) * t_new, :]] * heads, axis=0)
        s = _bdot_nt(jnp.where(ri == li, q_rep, 0.0), mk_ref[sq])
        m = jnp.max(s, axis=-1, keepdims=True)
        e = jnp.exp(s - m)
        o = _bdot(e / jnp.sum(e, axis=-1, keepdims=True), mv_ref[sq])
        acc = jnp.zeros((t_new, xw), F32)
        for h in range(heads):
            acc = acc + jnp.where(lh == h, o[h * t_new:(h + 1) * t_new, :], 0.0)
        outs.append(acc)
    ox = jnp.concatenate(outs, axis=0)
    _post_tail(h1, ox, wxo_ref, gffn_ref, wr_ref, br_ref, h2_ref, hn_ref, route_ref)


def _post_weights_specs(weights):
    return [pl.BlockSpec(a.shape, lambda i: (0, 0)) for a in weights]


def _post_outs(n, d, tm):
    row = lambda w: pl.BlockSpec((tm, w), lambda i: (i, 0))
    specs = [row(d), row(d), row(ROUTE_LANES)]
    shapes = [jax.ShapeDtypeStruct((n, d), F32), jax.ShapeDtypeStruct((n, d), F32),
              jax.ShapeDtypeStruct((n, ROUTE_LANES), F32)]
    return specs, shapes


def _post_prompt(os_, ls_, og, z, h, pw, mk, mv, *, tm):
    n, d = h.shape
    aw = og.shape[1]
    row = lambda w: pl.BlockSpec((tm, w), lambda i: (i, 0))
    full = lambda a: pl.BlockSpec(a.shape, lambda i: (0, 0))
    w1 = [pw["w_out"], pw["gm128"], pw["g_gdn"], pw["g_xattn"], pw["w_xq"], pw["g_xq"]]
    w2 = [pw["w_xo"], pw["g_ffn"], pw["w_r"], pw["b_r"]]
    specs, shapes = _post_outs(n, d, tm)
    return pl.pallas_call(
        _post_prompt_kernel,
        grid=(n // tm,),
        in_specs=[row(aw)] * 8 + [row(d)] + _post_weights_specs(w1) + [full(mk), full(mv)]
                 + _post_weights_specs(w2),
        out_specs=specs, out_shape=shapes,
        compiler_params=_cparams("parallel"),
        name="post_prompt",
    )(*os_, *ls_, og, z, h, *w1, mk, mv, *w2)


def _post_sample(oa, og, z, h, pw, mk, mv, *, t_new, seqs):
    n, d = h.shape
    aw = og.shape[1]
    tm = t_new * seqs
    row = lambda w: pl.BlockSpec((tm, w), lambda i: (i, 0))
    mem = pl.BlockSpec((seqs,) + mk.shape[1:], lambda i: (i, 0, 0))
    w1 = [pw["w_out"], pw["gm128"], pw["g_gdn"], pw["g_xattn"], pw["w_xq"], pw["g_xq"]]
    w2 = [pw["w_xo"], pw["g_ffn"], pw["w_r"], pw["b_r"]]
    specs, shapes = _post_outs(n, d, tm)
    return pl.pallas_call(
        functools.partial(_post_sample_kernel, t_new=t_new),
        grid=(n // tm,),
        in_specs=[row(aw)] * 3 + [row(d)] + _post_weights_specs(w1) + [mem, mem]
                 + _post_weights_specs(w2),
        out_specs=specs, out_shape=shapes,
        compiler_params=_cparams("parallel"),
        name="post_sample",
    )(oa, og, z, h, *w1, mk, mv, *w2)


def _moe_kernel(blk_e_ref, blk_start_ref, blk_cnt_ref, tok_ref, tw_ref, x_ref, res_ref,
                wg_ref, wu_ref, wd_ref, out_ref, xs_ref, y_ref, *, nb, rows):
    c = pl.program_id(0)
    b = pl.program_id(1)

    @pl.when(b == 0)
    def _():
        out_ref[...] = res_ref[...]

    cnt = blk_cnt_ref[c * nb + b]

    @pl.when(cnt > 0)
    def _():
        base = blk_start_ref[c * nb + b]

        def gather(r, carry):
            t = tok_ref[0, 0, jnp.minimum(base + r, rows - 1)]
            xs_ref[pl.ds(r, 1), :] = x_ref[pl.ds(t, 1), :]
            return carry

        lax.fori_loop(0, MOE_BLOCK, gather, 0, unroll=8)
        xb = xs_ref[...].astype(BF16)
        hg = jnp.dot(xb, wg_ref[0], preferred_element_type=F32)
        hu = jnp.dot(xb, wu_ref[0], preferred_element_type=F32)
        act = (hg * _sigmoid(hg) * hu).astype(BF16)
        y_ref[...] = jnp.dot(act, wd_ref[0], preferred_element_type=F32)

        def scatter(r, carry):
            i = jnp.minimum(base + r, rows - 1)
            t = tok_ref[0, 0, i]
            w = jnp.where(r < cnt, tw_ref[0, 0, i], 0.0)
            out_ref[pl.ds(t, 1), :] = out_ref[pl.ds(t, 1), :] + w * y_ref[pl.ds(r, 1), :]
            return carry

        lax.fori_loop(0, MOE_BLOCK, scatter, 0, unroll=8)


def _dispatch(route, chunk, n_experts):
    n = route.shape[0]
    nch = n // chunk
    rows = 2 * chunk
    nb = rows // MOE_BLOCK + n_experts
    e = route[:, :2].astype(jnp.int32).reshape(nch, rows)
    w = route[:, 2:4].reshape(nch, rows)
    order = jnp.argsort(e, axis=1, stable=True).astype(jnp.int32)
    tok = order // 2
    tw = jnp.take_along_axis(w, order, axis=1)
    ex = jnp.arange(n_experts, dtype=jnp.int32)
    counts = jnp.sum((e[:, :, None] == ex).astype(jnp.int32), axis=1)
    start = jnp.cumsum(counts, axis=1) - counts
    nblk_e = (counts + MOE_BLOCK - 1) // MOE_BLOCK
    bend = jnp.cumsum(nblk_e, axis=1)
    bstart = bend - nblk_e
    total = bend[:, -1:]
    b = jnp.arange(nb, dtype=jnp.int32)[None, :]
    bb = jnp.minimum(b, total - 1)
    eb = jnp.sum((bend[:, None, :] <= bb[:, :, None]).astype(jnp.int32), axis=2)
    sel = eb[:, :, None] == ex
    pick = lambda tbl: jnp.sum(jnp.where(sel, tbl[:, None, :], 0), axis=2)
    j = bb - pick(bstart)
    blk_start = pick(start) + j * MOE_BLOCK
    blk_cnt = jnp.where(b < total, jnp.clip(pick(counts) - j * MOE_BLOCK, 0, MOE_BLOCK), 0)
    flat = lambda t_: t_.astype(jnp.int32).reshape(-1)
    return flat(eb), flat(blk_start), flat(blk_cnt), tok.reshape(nch, 1, rows), tw.reshape(nch, 1, rows), nb


def _moe(hn, h2, route, wg_b, wu_b, wd_b, *, chunk):
    n, d = hn.shape
    n_experts, _, ff = wg_b.shape
    chunk = min(chunk, n)
    assert n % chunk == 0
    nch = n // chunk
    rows = 2 * chunk
    blk_e, blk_start, blk_cnt, tok, tw, nb = _dispatch(route, chunk, n_experts)
    tab = pl.BlockSpec((1, 1, rows), lambda c, b, *_: (c, 0, 0), memory_space=pltpu.SMEM)
    once = pl.BlockSpec((chunk, d), lambda c, b, *_: (c, 0), pipeline_mode=pl.Buffered(1))
    wspec = lambda shp: pl.BlockSpec((1,) + shp, lambda c, b, be, *_: (be[c * nb + b], 0, 0))
    grid_spec = pltpu.PrefetchScalarGridSpec(
        num_scalar_prefetch=3,
        grid=(nch, nb),
        in_specs=[tab, tab, once, once, wspec((d, ff)), wspec((d, ff)), wspec((ff, d))],
        out_specs=pl.BlockSpec((chunk, d), lambda c, b, *_: (c, 0)),
        scratch_shapes=[pltpu.VMEM((MOE_BLOCK, d), F32), pltpu.VMEM((MOE_BLOCK, d), F32)],
    )
    return pl.pallas_call(
        functools.partial(_moe_kernel, nb=nb, rows=rows),
        grid_spec=grid_spec,
        out_shape=jax.ShapeDtypeStruct((n, d), F32),
        compiler_params=_cparams("parallel", "arbitrary"),
        name="moe",
    )(blk_e, blk_start, blk_cnt, tok, tw, hn, h2, wg_b, wu_b, wd_b)


def _tile_row(g, reps, scale=1.0):
    return (jnp.tile(g.astype(F32), reps) * scale)[None, :]


def _layer_weights(p):
    d, in_cols = p["w_in"].shape
    aw = d // 2
    bw = d - aw
    pad = 3 * aw + 4 * bw + ROUTE_LANES - in_cols
    n_experts = p["w_re"].shape[1]
    w_r = jnp.concatenate([p["w_rg"], p["w_re"],
                           jnp.zeros((d, ROUTE_LANES - N_GROUPS - n_experts), F32)], axis=1)
    b_r = jnp.concatenate([p["b_rg"], p["b_re"], jnp.zeros((ROUTE_LANES - N_GROUPS - n_experts,), F32)])
    lane_pad = lambda v: jnp.concatenate([v.astype(F32), jnp.zeros((ROUTE_LANES - v.shape[0],), F32)])[None, :]
    return dict(
        aw=aw, bw=bw,
        g_mix=p["g_mix"][None, :],
        w_in=jnp.pad(p["w_in"], ((0, 0), (0, pad))).astype(BF16),
        gm64=_group_mean_matrix(aw, A_HEAD_DIM),
        gq=_tile_row(p["g_qa"], aw // A_HEAD_DIM, A_HEAD_DIM ** -0.5),
        gk=_tile_row(p["g_ka"], aw // A_HEAD_DIM),
        conv_w=p["conv_w"],
        alog=lane_pad(p["a_log"]),
        dtb=lane_pad(jnp.concatenate([p["dt_bias"], jnp.zeros_like(p["dt_bias"])])),
        w_out=p["w_out"].astype(BF16),
        gm128=_group_mean_matrix(bw, B_HEAD_DIM),
        g_gdn=_tile_row(p["g_gdn"], bw // B_HEAD_DIM),
        g_xattn=p["g_xattn"][None, :],
        w_xq=p["w_xq"].astype(BF16),
        g_xq=_tile_row(p["g_xq"], p["w_xq"].shape[1] // B_HEAD_DIM, B_HEAD_DIM ** -0.5),
        w_xo=p["w_xo"].astype(BF16),
        g_ffn=p["g_ffn"][None, :],
        w_r=w_r, b_r=b_r[None, :],
        w_gate=p["w_gate"].astype(BF16), w_up=p["w_up"].astype(BF16), w_down=p["w_down"].astype(BF16),
    )


def _gdn(u_pre, prev, ab, s0, pw, *, seg, conv_seg, chunks):
    w, uv, qg, kt, qk, last = _gdn_prep(u_pre, prev, ab, pw["conv_w"], pw["alog"], pw["dtb"],
                                        seg=seg, conv_seg=conv_seg, chunks=chunks)
    return _gdn_scan(w, uv, qg, kt, qk, last, s0, seg=seg)


def _prompt_layer(h, mem, pw, praw):
    nb_, s, d = h.shape
    assert nb_ == 1
    x = h.reshape(s, d)
    aw, bw = pw["aw"], pw["bw"]
    q, k, v, u_pre, z, ab = _inproj(x, pw["g_mix"], pw["w_in"], pw["gm64"], pw["gq"], pw["gk"],
                                    aw=aw, bw=bw, tm=256)
    res = [_dil_attn(q, k, v, w_, d_) for w_, d_ in DILATIONS]
    gdn_chunks = 4
    blk_rows = gdn_chunks * GDN_CHUNK
    tails = u_pre.reshape(s // blk_rows, blk_rows, 3 * bw)[:, blk_rows - PREV_ROWS:, :]
    prev = jnp.concatenate([jnp.zeros((1, PREV_ROWS, 3 * bw), F32), tails[:-1]], axis=0)
    s0 = jnp.zeros((1, B_HEADS, B_HEAD_DIM, B_HEAD_DIM), F32)
    og, s_fin = _gdn(u_pre, prev, ab, s0, pw, seg=GDN_CHUNK, conv_seg=blk_rows, chunks=gdn_chunks)
    mk, mv = _mem_kv(mem.reshape(mem.shape[1], d), praw["g_mem"][None, :], praw["w_xk"].astype(BF16),
                     praw["w_xv"].astype(BF16), pw["gm128"], _tile_row(praw["g_xk"], bw // B_HEAD_DIM))
    h2, hn, route = _post_prompt([r[0] for r in res], [r[1] for r in res], og, z, x, pw, mk, mv, tm=256)
    y = _moe(hn, h2, route, pw["w_gate"], pw["w_up"], pw["w_down"], chunk=MOE_CHUNK)
    keep = min(DILATIONS[-1][0], s)
    heads = aw // A_HEAD_DIM
    new_k = k[s - keep:].reshape(1, keep, heads, A_HEAD_DIM)
    new_v = v[s - keep:].reshape(1, keep, heads, A_HEAD_DIM)
    conv_new = u_pre[s - (CONV_WIDTH - 1):].reshape(1, CONV_WIDTH - 1, 3 * bw)
    xh = mk.shape[1] // B_HEAD_DIM
    return (y.reshape(1, s, d), new_k, new_v, conv_new, s_fin,
            mk.reshape(1, -1, xh, B_HEAD_DIM), mv.reshape(1, -1, xh, B_HEAD_DIM))


def _sample_layer(h, win_k, win_v, conv_prev, s0, mem_k, mem_v, pw):
    nseq, t_new, d = h.shape
    aw, bw = pw["aw"], pw["bw"]
    n = nseq * t_new
    assert t_new == 8 and GDN_CHUNK % t_new == 0
    x = h.reshape(n, d)
    q, k, v, u_pre, z, ab = _inproj(x, pw["g_mix"], pw["w_in"], pw["gm64"], pw["gq"], pw["gk"],
                                    aw=aw, bw=bw, tm=256)
    w_buf = win_k.shape[1]
    heads = aw // A_HEAD_DIM
    win_t = lambda c_: jnp.transpose(c_, (0, 2, 3, 1)).reshape(nseq, aw, w_buf)
    new_t = lambda t_: jnp.pad(jnp.transpose(t_.reshape(nseq, t_new, aw), (0, 2, 1)),
                               ((0, 0), (0, 0), (128 - t_new, 0)))
    win_back = lambda c_: jnp.transpose(c_.reshape(nseq, heads, A_HEAD_DIM, w_buf), (0, 3, 1, 2))
    new_kt, new_vt, oa = _step_attn(q.reshape(nseq, t_new, aw), new_t(k), new_t(v), win_t(win_k), win_t(win_v))
    new_k, new_v = win_back(new_kt), win_back(new_vt)
    prev = jnp.concatenate([jnp.zeros((nseq, PREV_ROWS - (CONV_WIDTH - 1), 3 * bw), F32),
                            conv_prev.astype(F32)], axis=1)
    og, s_fin = _gdn(u_pre, prev, ab, s0, pw, seg=t_new, conv_seg=t_new, chunks=4)
    xw = mem_k.shape[2] * mem_k.shape[3]
    h2, hn, route = _post_sample(oa.reshape(n, aw), og, z, x, pw, mem_k.reshape(nseq, -1, xw),
                                 mem_v.reshape(nseq, -1, xw), t_new=t_new, seqs=8)
    y = _moe(hn, h2, route, pw["w_gate"], pw["w_up"], pw["w_down"], chunk=MOE_CHUNK)
    conv_new = u_pre.reshape(nseq, t_new, 3 * bw)[:, t_new - (CONV_WIDTH - 1):, :]
    return (y.reshape(nseq, t_new, d), new_k, new_v, conv_new, s_fin)


def kernel(x_prompt, x_sample, mem_prompt, cache_win_k, cache_win_v, state_conv, state_delta, cache_mem_k, cache_mem_v, g_mix, w_in, g_qa, g_ka, conv_w, a_log, dt_bias, g_gdn, w_out, g_xattn, g_mem, w_xq, w_xk, w_xv, g_xq, g_xk, w_xo, g_ffn, w_rg, b_rg, w_re, b_re, w_gate, w_up, w_down):
    depth = w_in.shape[0]
    hp, hs = x_prompt, x_sample
    outs = [[] for _ in range(10)]
    for l in range(depth):
        praw = dict(g_mix=g_mix[l], w_in=w_in[l], g_qa=g_qa[l], g_ka=g_ka[l], conv_w=conv_w[l],
                    a_log=a_log[l], dt_bias=dt_bias[l], g_gdn=g_gdn[l], w_out=w_out[l],
                    g_xattn=g_xattn[l], g_mem=g_mem[l], w_xq=w_xq[l], w_xk=w_xk[l], w_xv=w_xv[l],
                    g_xq=g_xq[l], g_xk=g_xk[l], w_xo=w_xo[l], g_ffn=g_ffn[l], w_rg=w_rg[l],
                    b_rg=b_rg[l], w_re=w_re[l], b_re=b_re[l], w_gate=w_gate[l], w_up=w_up[l],
                    w_down=w_down[l])
        pw = _layer_weights(praw)
        hp, k_p, v_p, c_p, s_p, mk, mv = _prompt_layer(hp, mem_prompt, pw, praw)
        hs, k_s, v_s, c_s, s_s = _sample_layer(hs, cache_win_k[l], cache_win_v[l], state_conv[l],
                                               state_delta[l].astype(F32), cache_mem_k[l], cache_mem_v[l], pw)
        for lst, val in zip(outs, (k_p, v_p, c_p, s_p, mk, mv, k_s, v_s, c_s, s_s)):
            lst.append(val)
    st = [jnp.stack(o) for o in outs]
    st[3] = st[3].astype(state_delta.dtype)
    st[9] = st[9].astype(state_delta.dtype)
    return (hp, hs, *st)
```

```python
import functools

import jax
import jax.numpy as jnp
from jax import lax
from jax.experimental import pallas as pl
from jax.experimental.pallas import tpu as pltpu

F32 = jnp.float32
BF16 = jnp.bfloat16
EPS = 1e-6
NEG_INF = float("-inf")

A_HEAD_DIM = 64
B_HEAD_DIM = 128
B_HEADS = 4
BAND = 128
DILATIONS = ((128, 1), (512, 4), (2048, 16))
GDN_CHUNK = 64
CONV_WIDTH = 4
PREV_ROWS = 8
N_GROUPS = 4
GROUP_EXPERTS = 8
SUB, LANE = 8, 128
ROUTE_LANES = LANE
MOE_BLOCK = 128
MOE_CHUNK = 2048
VMEM_LIMIT = 56 * 1024 * 1024


def _cparams(*sem):
    return pltpu.CompilerParams(dimension_semantics=sem, vmem_limit_bytes=VMEM_LIMIT)


def _bdot(a, b):
    return jnp.dot(a.astype(BF16), b.astype(BF16), preferred_element_type=F32)


def _bdot_nt(a, b):
    return lax.dot_general(a.astype(BF16), b.astype(BF16), (((1,), (1,)), ((), ())),
                           preferred_element_type=F32)


def _split3(x):
    hi = x.astype(BF16)
    r1 = x - hi.astype(F32)
    mid = r1.astype(BF16)
    lo = (r1 - mid.astype(F32)).astype(BF16)
    return hi, mid, lo


def _dot_exact_lhs(a01, x):
    a = a01.astype(BF16)
    hi, mid, lo = _split3(x)
    d = lambda p: jnp.dot(a, p, preferred_element_type=F32)
    return d(hi) + d(mid) + d(lo)


def _group_mean_sq(x, gm):
    sq = x * x
    hi = sq.astype(BF16)
    lo = (sq - hi.astype(F32)).astype(BF16)
    return (jnp.dot(hi, gm, preferred_element_type=F32) + jnp.dot(lo, gm, preferred_element_type=F32))


def _rms(x, g):
    return x * lax.rsqrt(jnp.mean(x * x, axis=-1, keepdims=True) + EPS) * g


def _sigmoid(x):
    return 1.0 / (1.0 + jnp.exp(-x))


def _group_mean_matrix(width, group):
    i = jnp.arange(width)
    return jnp.where((i[:, None] // group) == (i[None, :] // group), 1.0 / group, 0.0).astype(BF16)


def _inproj_kernel(x_ref, g_ref, w_ref, gm_ref, gq_ref, gk_ref,
                   q_ref, k_ref, v_ref, u_ref, z_ref, ab_ref, *, aw, bw):
    nb = _rms(x_ref[...], g_ref[...]).astype(BF16)

    def proj(lo, hi):
        return jnp.dot(nb, w_ref[:, lo:hi], preferred_element_type=F32)

    gm = gm_ref[...]
    qa = proj(0, aw)
    q_ref[...] = qa * lax.rsqrt(_group_mean_sq(qa, gm) + EPS) * gq_ref[...]
    ka = proj(aw, 2 * aw)
    k_ref[...] = ka * lax.rsqrt(_group_mean_sq(ka, gm) + EPS) * gk_ref[...]
    v_ref[...] = proj(2 * aw, 3 * aw)
    u_ref[...] = proj(3 * aw, 3 * aw + 3 * bw)
    z_ref[...] = proj(3 * aw + 3 * bw, 3 * aw + 4 * bw)
    ab_ref[...] = proj(3 * aw + 4 * bw, 3 * aw + 4 * bw + ROUTE_LANES)


def _inproj(x, g_mix, w_in_b, gm64, gq_t, gk_t, *, aw, bw, tm):
    n, d = x.shape
    cols = w_in_b.shape[1]
    row = lambda w: pl.BlockSpec((tm, w), lambda i: (i, 0))
    full = lambda a: pl.BlockSpec(a.shape, lambda i: (0, 0))
    out_w = (aw, aw, aw, 3 * bw, bw, ROUTE_LANES)
    return pl.pallas_call(
        functools.partial(_inproj_kernel, aw=aw, bw=bw),
        grid=(n // tm,),
        in_specs=[row(d), full(g_mix), pl.BlockSpec((d, cols), lambda i: (0, 0)),
                  full(gm64), full(gq_t), full(gk_t)],
        out_specs=[row(w) for w in out_w],
        out_shape=[jax.ShapeDtypeStruct((n, w), F32) for w in out_w],
        compiler_params=_cparams("parallel"),
        name="inproj",
    )(x, g_mix, w_in_b, gm64, gq_t, gk_t)


def _dil_attn_kernel(q_ref, kp_ref, kc_ref, vp_ref, vc_ref, o_ref, lse_ref, *, sub_win):
    n = pl.program_id(1)
    qi = lax.broadcasted_iota(jnp.int32, (BAND, 2 * BAND), 0)
    kj = lax.broadcasted_iota(jnp.int32, (BAND, 2 * BAND), 1)
    dist = qi + BAND - kj
    first = jnp.where(n > 0, 0, BAND)
    valid = (dist >= 0) & (dist <= sub_win) & (kj >= first)
    bias = jnp.where(valid, 0.0, NEG_INF)
    lane = lax.broadcasted_iota(jnp.int32, (BAND, 2 * A_HEAD_DIM), 1)
    lo_half = lane < A_HEAD_DIM

    k_all = jnp.concatenate([kp_ref[...], kc_ref[...]], axis=0).astype(BF16)
    v_all = jnp.concatenate([vp_ref[...], vc_ref[...]], axis=0).astype(BF16)
    q = q_ref[...]
    pair = 2 * A_HEAD_DIM
    for p in range(q.shape[1] // pair):
        sl = slice(p * pair, (p + 1) * pair)
        kpair, vpair, qpair = k_all[:, sl], v_all[:, sl], q[:, sl]
        res = []
        for half in range(2):
            keep = lo_half if half == 0 else jnp.logical_not(lo_half)
            qm = jnp.where(keep, qpair, 0.0)
            s = _bdot_nt(qm, kpair) + bias
            m = jnp.max(s, axis=-1, keepdims=True)
            e = jnp.exp(s - m)
            den = jnp.sum(e, axis=-1, keepdims=True)
            o = jnp.dot(e.astype(BF16), vpair, preferred_element_type=F32) / den
            res.append((o, m + jnp.log(den)))
        o_ref[:, sl] = jnp.where(lo_half, res[0][0], res[1][0])
        lse_ref[:, sl] = jnp.where(lo_half, res[0][1], res[1][1])


def _dil_attn(q, k, v, window, dil):
    s, aw = q.shape
    length = s // dil
    assert s % dil == 0 and length % BAND == 0 and window // dil <= BAND
    nb = length // BAND
    view = lambda t: t.reshape(length, dil * aw)
    cur = pl.BlockSpec((BAND, aw), lambda r, n: (n, r))
    prev = pl.BlockSpec((BAND, aw), lambda r, n: (jnp.maximum(n - 1, 0), r))
    o, lse = pl.pallas_call(
        functools.partial(_dil_attn_kernel, sub_win=window // dil),
        grid=(dil, nb),
        in_specs=[cur, prev, cur, prev, cur],
        out_specs=[cur, cur],
        out_shape=[jax.ShapeDtypeStruct((length, dil * aw), F32)] * 2,
        compiler_params=_cparams("parallel", "arbitrary"),
        name=f"dil_attn_{dil}",
    )(view(q), view(k), view(k), view(v), view(v))
    return o.reshape(s, aw), lse.reshape(s, aw)


def _step_attn_kernel(q_ref, kn_ref, vn_ref, kc_ref, vc_ref, ok_ref, ov_ref, oa_ref,
                      kb_ref, vb_ref, *, w_buf, t_new):
    aw = q_ref.shape[2]
    heads = aw // A_HEAD_DIM
    lanes = kn_ref.shape[2]
    kt, vt = kc_ref[0], vc_ref[0]
    kn, vn = kn_ref[0], vn_ref[0]
    is_new = lax.broadcasted_iota(jnp.int32, (aw, lanes), 1) >= lanes - t_new
    for src, new, dst in ((kt, kn, ok_ref), (vt, vn, ov_ref)):
        rolled = pltpu.roll(src, w_buf - t_new, axis=1)
        dst[0, :, 0:w_buf - lanes] = rolled[:, 0:w_buf - lanes]
        dst[0, :, w_buf - lanes:w_buf] = jnp.where(is_new, new, rolled[:, w_buf - lanes:w_buf])
    kb_ref[...] = kt.astype(BF16)
    vb_ref[...] = vt.astype(BF16)

    rows = heads * t_new
    ri = lax.broadcasted_iota(jnp.int32, (rows, aw), 0)
    li = lax.broadcasted_iota(jnp.int32, (rows, aw), 1)
    q_rep = jnp.concatenate([q_ref[0]] * heads, axis=0)
    q_blk = jnp.where((ri // t_new) == (li // A_HEAD_DIM), q_rep, 0.0).astype(BF16)
    ncol = w_buf + lanes
    s_all = jnp.concatenate([jnp.dot(q_blk, kb_ref[...], preferred_element_type=F32),
                             jnp.dot(q_blk, kn.astype(BF16), preferred_element_type=F32)], axis=1)

    def branch(window, dil, col0):
        s = s_all[:, col0:]
        t = lax.broadcasted_iota(jnp.int32, s.shape, 0) % t_new
        cidx = lax.broadcasted_iota(jnp.int32, s.shape, 1) + col0
        r = jnp.where(cidx < w_buf, cidx, cidx - (lanes - t_new))
        diff = w_buf + t - r
        valid = (diff >= 0) & (diff <= window) & ((diff & (dil - 1)) == 0)
        valid = valid & ((cidx < w_buf) | (cidx >= ncol - t_new))
        s = jnp.where(valid, s, NEG_INF)
        m = jnp.max(s, axis=-1, keepdims=True)
        e = jnp.exp(s - m)
        den = jnp.sum(e, axis=-1, keepdims=True)
        return e, den, m + jnp.log(den)

    parts = []
    for window, dil in DILATIONS:
        assert dil & (dil - 1) == 0
        col0 = max(0, (w_buf - window) // 128 * 128)
        parts.append((col0,) + branch(window, dil, col0))
    mm = functools.reduce(jnp.maximum, [p[3] for p in parts])
    wexp = [jnp.exp(p[3] - mm) for p in parts]
    wsum = functools.reduce(lambda a, b: a + b, wexp)
    p_tot = jnp.zeros((rows, ncol), F32)
    for (col0, e, den, _), we in zip(parts, wexp):
        pe = e * (we / (wsum * den))
        if col0:
            pe = jnp.concatenate([jnp.zeros((rows, col0), F32), pe], axis=1)
        p_tot = p_tot + pe
    o = _bdot_nt(p_tot[:, :w_buf], vb_ref[...]) + _bdot_nt(p_tot[:, w_buf:], vn)
    lh = lax.broadcasted_iota(jnp.int32, (t_new, aw), 1) // A_HEAD_DIM
    acc = jnp.zeros((t_new, aw), F32)
    for h in range(heads):
        acc = acc + jnp.where(lh == h, o[h * t_new:(h + 1) * t_new, :], 0.0)
    oa_ref[0] = acc


def _step_attn(q, kn_t, vn_t, cache_kt, cache_vt):
    nseq, aw, w_buf = cache_kt.shape
    t_new = q.shape[1]
    lanes = kn_t.shape[2]
    assert w_buf % lanes == 0 and t_new % 8 == 0 and t_new <= lanes
    qs = pl.BlockSpec((1, t_new, aw), lambda i: (i, 0, 0))
    new = pl.BlockSpec((1, aw, lanes), lambda i: (i, 0, 0))
    win = pl.BlockSpec((1, aw, w_buf), lambda i: (i, 0, 0))
    return pl.pallas_call(
        functools.partial(_step_attn_kernel, w_buf=w_buf, t_new=t_new),
        grid=(nseq,),
        in_specs=[qs, new, new, win, win],
        out_specs=[win, win, qs],
        out_shape=[jax.ShapeDtypeStruct(cache_kt.shape, F32), jax.ShapeDtypeStruct(cache_vt.shape, F32),
                   jax.ShapeDtypeStruct(q.shape, F32)],
        scratch_shapes=[pltpu.VMEM((aw, w_buf), BF16), pltpu.VMEM((aw, w_buf), BF16)],
        compiler_params=_cparams("parallel"),
        name="step_attn",
    )(q, kn_t, vn_t, cache_kt, cache_vt)


def _gdn_prep_kernel(u_ref, prev_ref, ab_ref, cw_ref, alog_ref, dtb_ref,
                     w_ref, uv_ref, qg_ref, kt_ref, qk_ref, last_ref, full_ref, act_ref,
                     *, seg, conv_seg, chunks):
    c = GDN_CHUNK
    dk = B_HEAD_DIM
    bw = B_HEADS * dk
    ii = lax.broadcasted_iota(jnp.int32, (c, c), 0)
    jj = lax.broadcasted_iota(jnp.int32, (c, c), 1)
    same = (ii // seg) == (jj // seg)
    incl = same & (ii >= jj)
    strict = same & (ii > jj)
    eye = ii == jj
    ones_c = jnp.ones((c, c), BF16)
    seg_cols = jnp.where((lax.broadcasted_iota(jnp.int32, (c, dk), 0) % seg)
                         == lax.broadcasted_iota(jnp.int32, (c, dk), 1), 1.0, 0.0)
    cw = cw_ref[...]
    rows = chunks * c

    for sgi in range(rows // conv_seg):
        base = sgi * (conv_seg + PREV_ROWS)
        full_ref[base:base + PREV_ROWS, :] = prev_ref[sgi]
        full_ref[base + PREV_ROWS:base + PREV_ROWS + conv_seg, :] = u_ref[sgi * conv_seg:(sgi + 1) * conv_seg, :]
        y = jnp.zeros((conv_seg, 3 * bw), F32)
        for j in range(CONV_WIDTH):
            off = base + PREV_ROWS - (CONV_WIDTH - 1) + j
            y = y + full_ref[off:off + conv_seg, :] * cw[j:j + 1, :]
        act_ref[sgi * conv_seg:(sgi + 1) * conv_seg, :] = y * _sigmoid(y)

    ri = lax.broadcasted_iota(jnp.int32, (rows, rows), 0)
    rj = lax.broadcasted_iota(jnp.int32, (rows, rows), 1)
    rsame = (ri // seg) == (rj // seg)
    ab = ab_ref[...]
    sp = ab + dtb_ref[...]
    sp = jnp.maximum(sp, 0.0) + jnp.log(1.0 + jnp.exp(-jnp.abs(sp)))
    g_all = -jnp.exp(alog_ref[...]) * sp
    gcum_all = _dot_exact_lhs(jnp.where(rsame & (ri >= rj), 1.0, 0.0), g_all)
    gtot_all = _dot_exact_lhs(jnp.where(rsame, 1.0, 0.0), g_all)
    beta_all = _sigmoid(ab)
    eye_f = jnp.where(eye, 1.0, 0.0)

    grow = []
    for ci in range(chunks):
        rs = slice(ci * c, (ci + 1) * c)
        diag = jnp.concatenate([jnp.where(eye, gcum_all[rs, h:h + 1], 0.0) for h in range(B_HEADS)], axis=1)
        grow.append(_dot_exact_lhs(ones_c, diag))

    prob = [(ci, h) for ci in range(chunks) for h in range(B_HEADS)]
    col = lambda arr, ci, lane: arr[ci * c:(ci + 1) * c, lane:lane + 1]
    gc = [col(gcum_all, ci, h) for ci, h in prob]
    gt = [col(gtot_all, ci, h) for ci, h in prob]
    beta = [col(beta_all, ci, B_HEADS + h) for ci, h in prob]
    q, k, v = [], [], []
    for ci, h in prob:
        rs = slice(ci * c, (ci + 1) * c)
        qh = act_ref[rs, h * dk:(h + 1) * dk]
        kh = act_ref[rs, bw + h * dk:bw + (h + 1) * dk]
        q.append(qh * lax.rsqrt(jnp.sum(qh * qh, axis=-1, keepdims=True) + EPS) * (dk ** -0.5))
        k.append(kh * lax.rsqrt(jnp.sum(kh * kh, axis=-1, keepdims=True) + EPS))
        v.append(act_ref[rs, 2 * bw + h * dk:2 * bw + (h + 1) * dk])
    e_incl = [jnp.exp(jnp.where(incl, gc[i] - grow[ci][:, h * c:(h + 1) * c], NEG_INF))
              for i, (ci, h) in enumerate(prob)]
    qkk = [_bdot_nt(jnp.concatenate([q[i], k[i]], axis=0), k[i]) for i in range(len(prob))]
    a = [beta[i] * jnp.where(strict, e_incl[i], 0.0) * qkk[i][c:, :] for i in range(len(prob))]
    blk = 8
    inblk = (ii // blk) == (jj // blk)
    d0 = [jnp.where(inblk, t, 0.0) for t in a]
    d2 = [_bdot(t, t) for t in d0]
    d4 = [_bdot(t, t) for t in d2]
    x = [(eye_f - t) + _bdot(eye_f - t, t2) for t, t2 in zip(d0, d2)]
    x = [t + _bdot(t, t4) for t, t4 in zip(x, d4)]
    while blk < seg:
        off_blk = ((ii // (2 * blk)) == (jj // (2 * blk))) & ((ii // blk) != (jj // blk))
        xe = [_bdot(t, jnp.where(off_blk, ta, 0.0)) for t, ta in zip(x, a)]
        x = [t - _bdot(te, t) for t, te in zip(x, xe)]
        blk *= 2
    gamma = [jnp.exp(t) for t in gc]
    wuv = [_bdot(x[i], jnp.concatenate([(beta[i] * gamma[i]) * k[i], beta[i] * v[i]], axis=1))
           for i in range(len(prob))]
    for i, (ci, h) in enumerate(prob):
        rs = slice(ci * c, (ci + 1) * c)
        hs = slice(h * dk, (h + 1) * dk)
        w_ref[rs, hs] = wuv[i][:, :dk]
        uv_ref[rs, hs] = wuv[i][:, dk:]
        qg_ref[rs, hs] = gamma[i] * q[i]
        kt_ref[rs, hs] = k[i] * jnp.exp(gt[i] - gc[i])
        qk = qkk[i][:c, :] * e_incl[i]
        qk_ref[rs, hs] = (jnp.concatenate([qk, jnp.zeros((c, dk - c), F32)], axis=1) if seg == c
                          else _bdot(qk, seg_cols))
        last_ref[rs, hs] = jnp.broadcast_to(jnp.exp(gt[i]), (c, dk))


def _gdn_prep(u_pre, prev, ab, conv_w, alog_row, dtb_row, *, seg, conv_seg, chunks):
    n, cw = u_pre.shape
    bw = cw // 3
    c = GDN_CHUNK
    rows = chunks * c
    assert n % rows == 0 and c % seg == 0 and seg % 8 == 0 and rows % conv_seg == 0 and conv_seg % seg == 0
    ncs = rows // conv_seg
    row = lambda w: pl.BlockSpec((rows, w), lambda i: (i, 0))
    full = lambda a: pl.BlockSpec(a.shape, lambda i: (0, 0))
    return pl.pallas_call(
        functools.partial(_gdn_prep_kernel, seg=seg, conv_seg=conv_seg, chunks=chunks),
        grid=(n // rows,),
        in_specs=[row(cw), pl.BlockSpec((ncs, PREV_ROWS, cw), lambda i: (i, 0, 0)),
                  row(ROUTE_LANES), full(conv_w), full(alog_row), full(dtb_row)],
        out_specs=[row(bw)] * 6,
        out_shape=[jax.ShapeDtypeStruct((n, bw), F32)] * 6,
        scratch_shapes=[pltpu.VMEM((ncs * (conv_seg + PREV_ROWS), cw), F32), pltpu.VMEM((rows, cw), F32)],
        compiler_params=_cparams("parallel"),
        name=f"gdn_prep_{seg}",
    )(u_pre, prev, ab, conv_w, alog_row, dtb_row)


def _gdn_scan_kernel(w_ref, uv_ref, qg_ref, kt_ref, qk_ref, last_ref, s0_ref, o_ref, s_ref, *, seg):
    dk = B_HEAD_DIM

    @pl.when(pl.program_id(1) == 0)
    def _():
        s_ref[...] = s0_ref[...]

    pad = dk - seg
    padr = lambda t: jnp.concatenate([t, jnp.zeros((pad, dk), F32)], axis=0) if pad else t
    ii = lax.broadcasted_iota(jnp.int32, (dk, dk), 0)
    jj = lax.broadcasted_iota(jnp.int32, (dk, dk), 1)
    eye = jnp.where(ii == jj, 1.0, 0.0).astype(BF16)
    for h in range(B_HEADS):
        hs = slice(h * dk, (h + 1) * dk)
        s = s_ref[0, h]
        sb = s.astype(BF16)
        u = padr(uv_ref[:, hs]) - _bdot(padr(w_ref[:, hs]), sb)
        ub = u.astype(BF16)
        o = _bdot(padr(qg_ref[:, hs]), sb) + _bdot(padr(qk_ref[:, hs]), ub)
        o_ref[:, hs] = o[:seg, :]
        ktt = _bdot_nt(eye, padr(kt_ref[:, hs]))
        s_ref[0, h] = last_ref[0:1, hs] * s + _bdot(ktt, ub)


def _gdn_scan(w, uv, qg, kt, qk, last, s0, *, seg):
    n, bw = w.shape
    nseq = s0.shape[0]
    per_seq = n // (nseq * seg)
    row = pl.BlockSpec((seg, bw), lambda s, i: (s * per_seq + i, 0))
    st = pl.BlockSpec((1,) + s0.shape[1:], lambda s, i: (s, 0, 0, 0))
    return pl.pallas_call(
        functools.partial(_gdn_scan_kernel, seg=seg),
        grid=(nseq, per_seq),
        in_specs=[row] * 6 + [st],
        out_specs=[row, st],
        out_shape=[jax.ShapeDtypeStruct((n, bw), F32), jax.ShapeDtypeStruct(s0.shape, F32)],
        compiler_params=_cparams("parallel", "arbitrary"),
        name=f"gdn_scan_{seg}",
    )(w, uv, qg, kt, qk, last, s0)


def _mem_kv_kernel(mem_ref, g_ref, wk_ref, wv_ref, gm_ref, gk_ref, k_ref, v_ref):
    mn = _rms(mem_ref[...], g_ref[...]).astype(BF16)
    k = jnp.dot(mn, wk_ref[...], preferred_element_type=F32)
    k_ref[...] = k * lax.rsqrt(_group_mean_sq(k, gm_ref[...]) + EPS) * gk_ref[...]
    v_ref[...] = jnp.dot(mn, wv_ref[...], preferred_element_type=F32)


def _mem_kv(mem, g_mem, w_xk_b, w_xv_b, gm128, gk_t):
    m = mem.shape[0]
    xw = w_xk_b.shape[1]
    return pl.pallas_call(
        _mem_kv_kernel,
        out_shape=[jax.ShapeDtypeStruct((m, xw), F32)] * 2,
        compiler_params=pltpu.CompilerParams(vmem_limit_bytes=VMEM_LIMIT),
        name="mem_kv",
    )(mem, g_mem, w_xk_b, w_xv_b, gm128, gk_t)


def _route(logits):
    lane = lax.broadcasted_iota(jnp.int32, logits.shape, 1)
    lane_f = lane.astype(F32)
    big = float(ROUTE_LANES)
    lg = jnp.where(lane < N_GROUPS, logits, NEG_INF)
    mg = jnp.max(lg, axis=-1, keepdims=True)
    zg = jnp.sum(jnp.exp(lg - mg), axis=-1, keepdims=True)
    pg_top = 1.0 / zg
    gidx = jnp.min(jnp.where(lg == mg, lane_f, big), axis=-1, keepdims=True)
    e_lo = N_GROUPS + GROUP_EXPERTS * gidx
    emask = (lane_f >= e_lo) & (lane_f < e_lo + GROUP_EXPERTS)
    le = jnp.where(emask, logits, NEG_INF)
    me = jnp.max(le, axis=-1, keepdims=True)
    ee = jnp.exp(le - me)
    pe = ee / jnp.sum(ee, axis=-1, keepdims=True)
    pe = jnp.where(emask, pe, -1.0)
    p1 = jnp.max(pe, axis=-1, keepdims=True)
    i1 = jnp.min(jnp.where(pe == p1, lane_f, big), axis=-1, keepdims=True)
    pe2 = jnp.where(lane_f == i1, -1.0, pe)
    p2 = jnp.max(pe2, axis=-1, keepdims=True)
    i2 = jnp.min(jnp.where(pe2 == p2, lane_f, big), axis=-1, keepdims=True)
    den = p1 + p2
    w1 = pg_top * p1 / den
    w2 = pg_top * p2 / den
    out = jnp.where(lane == 0, i1 - N_GROUPS, 0.0)
    out = jnp.where(lane == 1, i2 - N_GROUPS, out)
    out = jnp.where(lane == 2, w1, out)
    return jnp.where(lane == 3, w2, out)


def _post_common(oa, og_ref, z_ref, h_ref, wout_ref, gm_ref, ggdn_ref, gx_ref, wxq_ref, gxq_ref):
    og = og_ref[...]
    z = z_ref[...]
    ob = og * lax.rsqrt(_group_mean_sq(og, gm_ref[...]) + EPS) * ggdn_ref[...] * (z * _sigmoid(z))
    cat = jnp.concatenate([oa, ob], axis=-1).astype(BF16)
    h1 = h_ref[...] + jnp.dot(cat, wout_ref[...], preferred_element_type=F32)
    hx = _rms(h1, gx_ref[...]).astype(BF16)
    q = jnp.dot(hx, wxq_ref[...], preferred_element_type=F32)
    qn = q * lax.rsqrt(_group_mean_sq(q, gm_ref[...]) + EPS) * gxq_ref[...]
    return h1, qn


def _mem_attend_rows(qn, mk, mv):
    outs = []
    for h in range(qn.shape[1] // B_HEAD_DIM):
        hs = slice(h * B_HEAD_DIM, (h + 1) * B_HEAD_DIM)
        s = _bdot_nt(qn[:, hs], mk[:, hs])
        m = jnp.max(s, axis=-1, keepdims=True)
        e = jnp.exp(s - m)
        p = e / jnp.sum(e, axis=-1, keepdims=True)
        outs.append(_bdot(p, mv[:, hs]))
    return jnp.concatenate(outs, axis=-1)


def _post_tail(h1, ox, wxo_ref, gffn_ref, wr_ref, br_ref, h2_ref, hn_ref, route_ref):
    h2 = h1 + jnp.dot(ox.astype(BF16), wxo_ref[...], preferred_element_type=F32)
    h2_ref[...] = h2
    hn = _rms(h2, gffn_ref[...])
    tm = hn.shape[0]
    for j in range(hn.shape[1] // LANE):
        hn_ref[pl.ds(j, tm, stride=SUB), :] = hn[:, j * LANE:(j + 1) * LANE]
    logits = jnp.dot(hn, wr_ref[...], preferred_element_type=F32,
                     precision=lax.Precision.HIGHEST) + br_ref[...]
    route_ref[...] = _route(logits)


def _post_prompt_kernel(o1_ref, o2_ref, o3_ref, l1_ref, l2_ref, l3_ref, og_ref, z_ref, h_ref,
                        wout_ref, gm_ref, ggdn_ref, gx_ref, wxq_ref, gxq_ref, mk_ref, mv_ref,
                        wxo_ref, gffn_ref, wr_ref, br_ref, h2_ref, hn_ref, route_ref):
    l1, l2, l3 = l1_ref[...], l2_ref[...], l3_ref[...]
    mm = jnp.maximum(jnp.maximum(l1, l2), l3)
    e1, e2, e3 = jnp.exp(l1 - mm), jnp.exp(l2 - mm), jnp.exp(l3 - mm)
    oa = (e1 * o1_ref[...] + e2 * o2_ref[...] + e3 * o3_ref[...]) / (e1 + e2 + e3)
    h1, qn = _post_common(oa, og_ref, z_ref, h_ref, wout_ref, gm_ref, ggdn_ref, gx_ref, wxq_ref, gxq_ref)
    ox = _mem_attend_rows(qn, mk_ref[...], mv_ref[...])
    _post_tail(h1, ox, wxo_ref, gffn_ref, wr_ref, br_ref, h2_ref, hn_ref, route_ref)


def _post_sample_kernel(oa_ref, og_ref, z_ref, h_ref,
                        wout_ref, gm_ref, ggdn_ref, gx_ref, wxq_ref, gxq_ref, mk_ref, mv_ref,
                        wxo_ref, gffn_ref, wr_ref, br_ref, h2_ref, hn_ref, route_ref, *, t_new):
    h1, qn = _post_common(oa_ref[...], og_ref, z_ref, h_ref, wout_ref, gm_ref, ggdn_ref, gx_ref,
                          wxq_ref, gxq_ref)
    dh = B_HEAD_DIM
    heads = qn.shape[1] // dh
    rows = heads * t_new
    ncol = mk_ref.shape[1]
    own = ((lax.broadcasted_iota(jnp.int32, (rows, ncol), 0) // t_new)
           == (lax.broadcasted_iota(jnp.int32, (rows, ncol), 1) % heads))
    outs = []
    for sq in range(mk_ref.shape[0]):
        qs = qn[sq * t_new:(sq + 1) * t_new, :]
        q_rows = jnp.concatenate([qs[:, h * dh:(h + 1) * dh] for h in range(heads)], axis=0)
        s = jnp.where(own, _bdot_nt(q_rows, mk_ref[sq]), NEG_INF)
        m = jnp.max(s, axis=-1, keepdims=True)
        e = jnp.exp(s - m)
        o = _bdot(e / jnp.sum(e, axis=-1, keepdims=True), mv_ref[sq])
        outs.append(jnp.concatenate([o[h * t_new:(h + 1) * t_new, :] for h in range(heads)], axis=1))
    ox = jnp.concatenate(outs, axis=0)
    _post_tail(h1, ox, wxo_ref, gffn_ref, wr_ref, br_ref, h2_ref, hn_ref, route_ref)


def _post_weights_specs(weights):
    return [pl.BlockSpec(a.shape, lambda i: (0, 0)) for a in weights]


def _post_outs(n, d, tm):
    assert d == SUB * LANE
    row = lambda w: pl.BlockSpec((tm, w), lambda i: (i, 0))
    specs = [row(d), pl.BlockSpec((tm * SUB, LANE), lambda i: (i, 0)), row(ROUTE_LANES)]
    shapes = [jax.ShapeDtypeStruct((n, d), F32), jax.ShapeDtypeStruct((n * SUB, LANE), F32),
              jax.ShapeDtypeStruct((n, ROUTE_LANES), F32)]
    return specs, shapes


def _post_prompt(os_, ls_, og, z, h, pw, mk, mv, *, tm):
    n, d = h.shape
    aw = og.shape[1]
    row = lambda w: pl.BlockSpec((tm, w), lambda i: (i, 0))
    full = lambda a: pl.BlockSpec(a.shape, lambda i: (0, 0))
    w1 = [pw["w_out"], pw["gm128"], pw["g_gdn"], pw["g_xattn"], pw["w_xq"], pw["g_xq"]]
    w2 = [pw["w_xo"], pw["g_ffn"], pw["w_r"], pw["b_r"]]
    specs, shapes = _post_outs(n, d, tm)
    return pl.pallas_call(
        _post_prompt_kernel,
        grid=(n // tm,),
        in_specs=[row(aw)] * 8 + [row(d)] + _post_weights_specs(w1) + [full(mk), full(mv)]
                 + _post_weights_specs(w2),
        out_specs=specs, out_shape=shapes,
        compiler_params=_cparams("parallel"),
        name="post_prompt",
    )(*os_, *ls_, og, z, h, *w1, mk, mv, *w2)


def _post_sample(oa, og, z, h, pw, mk, mv, *, t_new, seqs):
    n, d = h.shape
    aw = og.shape[1]
    tm = t_new * seqs
    row = lambda w: pl.BlockSpec((tm, w), lambda i: (i, 0))
    mem = pl.BlockSpec((seqs,) + mk.shape[1:], lambda i: (i, 0, 0))
    w1 = [pw["w_out"], pw["gm128"], pw["g_gdn"], pw["g_xattn"], pw["w_xq"], pw["g_xq"]]
    w2 = [pw["w_xo"], pw["g_ffn"], pw["w_r"], pw["b_r"]]
    specs, shapes = _post_outs(n, d, tm)
    return pl.pallas_call(
        functools.partial(_post_sample_kernel, t_new=t_new),
        grid=(n // tm,),
        in_specs=[row(aw)] * 3 + [row(d)] + _post_weights_specs(w1) + [mem, mem]
                 + _post_weights_specs(w2),
        out_specs=specs, out_shape=shapes,
        compiler_params=_cparams("parallel"),
        name="post_sample",
    )(oa, og, z, h, *w1, mk, mv, *w2)


def _moe_kernel(blk_e_ref, blk_start_ref, blk_cnt_ref, tok_ref, tw_ref, x_ref, res_ref,
                wg_ref, wu_ref, wd_ref, out_ref, acc_ref, xs_ref, ys_ref, *, nb):
    c = pl.program_id(0)
    b = pl.program_id(1)
    chunk, d = res_ref.shape
    nl = d // LANE
    cnt = blk_cnt_ref[c * nb + b]

    @pl.when(cnt > 0)
    def _():
        base = blk_start_ref[c * nb + b]
        for r in range(MOE_BLOCK):
            t = tok_ref[0, 0, base + r] >> 1
            xs_ref[r * SUB:(r + 1) * SUB, :] = x_ref[pl.ds(pl.multiple_of(t * SUB, SUB), SUB), :]
        xb = jnp.concatenate([xs_ref[pl.ds(j, MOE_BLOCK, stride=SUB), :] for j in range(nl)],
                             axis=1).astype(BF16)
        hg = jnp.dot(xb, wg_ref[0], preferred_element_type=F32)
        hu = jnp.dot(xb, wu_ref[0], preferred_element_type=F32)
        act = (hg * _sigmoid(hg) * hu).astype(BF16)
        y = jnp.dot(act, wd_ref[0], preferred_element_type=F32)
        for j in range(nl):
            ys_ref[pl.ds(j, MOE_BLOCK, stride=SUB), :] = y[:, j * LANE:(j + 1) * LANE]
        for r in range(MOE_BLOCK):
            e = tok_ref[0, 0, base + r]
            slot = jnp.where(r < cnt, (e & 1) * chunk + (e >> 1), 2 * chunk)
            off = pl.multiple_of(slot * SUB, SUB)
            acc_ref[pl.ds(off, SUB), :] = tw_ref[0, 0, base + r] * ys_ref[r * SUB:(r + 1) * SUB, :]

    @pl.when(b == nb - 1)
    def _():
        step = 256
        for r0 in range(0, chunk, step):
            for j in range(nl):
                out_ref[r0:r0 + step, j * LANE:(j + 1) * LANE] = (
                    res_ref[r0:r0 + step, j * LANE:(j + 1) * LANE]
                    + acc_ref[pl.ds(r0 * SUB + j, step, stride=SUB), :]
                    + acc_ref[pl.ds((chunk + r0) * SUB + j, step, stride=SUB), :])


def _dispatch(route, chunk, n_experts):
    n = route.shape[0]
    nch = n // chunk
    rows = 2 * chunk
    nb = rows // MOE_BLOCK + n_experts
    e = route[:, :2].astype(jnp.int32).reshape(nch, rows)
    w = route[:, 2:4].reshape(nch, rows)
    tok = jnp.argsort(e, axis=1, stable=True).astype(jnp.int32)
    tw = jnp.take_along_axis(w, tok, axis=1)
    ex = jnp.arange(n_experts, dtype=jnp.int32)
    counts = jnp.sum((e[:, :, None] == ex).astype(jnp.int32), axis=1)
    start = jnp.cumsum(counts, axis=1) - counts
    nblk_e = (counts + MOE_BLOCK - 1) // MOE_BLOCK
    bend = jnp.cumsum(nblk_e, axis=1)
    bstart = bend - nblk_e
    total = bend[:, -1:]
    b = jnp.arange(nb, dtype=jnp.int32)[None, :]
    bb = jnp.minimum(b, total - 1)
    eb = jnp.sum((bend[:, None, :] <= bb[:, :, None]).astype(jnp.int32), axis=2)
    sel = eb[:, :, None] == ex
    pick = lambda tbl: jnp.sum(jnp.where(sel, tbl[:, None, :], 0), axis=2)
    j = bb - pick(bstart)
    blk_start = pick(start) + j * MOE_BLOCK
    blk_cnt = jnp.where(b < total, jnp.clip(pick(counts) - j * MOE_BLOCK, 0, MOE_BLOCK), 0)
    flat = lambda t_: t_.astype(jnp.int32).reshape(-1)
    pad = lambda t_: jnp.pad(t_, ((0, 0), (0, MOE_BLOCK))).reshape(nch, 1, rows + MOE_BLOCK)
    return flat(eb), flat(blk_start), flat(blk_cnt), pad(tok), pad(tw), nb


def _moe(hn_t, h2, route, wg_b, wu_b, wd_b, *, chunk):
    n, d = h2.shape
    n_experts, _, ff = wg_b.shape
    chunk = min(chunk, n)
    assert n % chunk == 0 and d == SUB * LANE
    nch = n // chunk
    blk_e, blk_start, blk_cnt, tok, tw, nb = _dispatch(route, chunk, n_experts)
    tab = pl.BlockSpec((1, 1, tok.shape[2]), lambda c, b, *_: (c, 0, 0), memory_space=pltpu.SMEM)
    once = lambda shp: pl.BlockSpec(shp, lambda c, b, *_: (c, 0), pipeline_mode=pl.Buffered(1))
    wspec = lambda shp: pl.BlockSpec((1,) + shp, lambda c, b, be, *_: (be[c * nb + b], 0, 0))
    grid_spec = pltpu.PrefetchScalarGridSpec(
        num_scalar_prefetch=3,
        grid=(nch, nb),
        in_specs=[tab, tab, once((chunk * SUB, LANE)), once((chunk, d)),
                  wspec((d, ff)), wspec((d, ff)), wspec((ff, d))],
        out_specs=once((chunk, d)),
        scratch_shapes=[pltpu.VMEM(((2 * chunk + 1) * SUB, LANE), F32), pltpu.VMEM((MOE_BLOCK * SUB, LANE), F32),
                        pltpu.VMEM((MOE_BLOCK * SUB, LANE), F32)],
    )
    return pl.pallas_call(
        functools.partial(_moe_kernel, nb=nb),
        grid_spec=grid_spec,
        out_shape=jax.ShapeDtypeStruct((n, d), F32),
        compiler_params=_cparams("parallel", "arbitrary"),
        name="moe",
    )(blk_e, blk_start, blk_cnt, tok, tw, hn_t, h2, wg_b, wu_b, wd_b)


def _tile_row(g, reps, scale=1.0):
    return (jnp.tile(g.astype(F32), reps) * scale)[None, :]


def _layer_weights(p):
    d, in_cols = p["w_in"].shape
    aw = d // 2
    bw = d - aw
    pad = 3 * aw + 4 * bw + ROUTE_LANES - in_cols
    n_experts = p["w_re"].shape[1]
    w_r = jnp.concatenate([p["w_rg"], p["w_re"],
                           jnp.zeros((d, ROUTE_LANES - N_GROUPS - n_experts), F32)], axis=1)
    b_r = jnp.concatenate([p["b_rg"], p["b_re"], jnp.zeros((ROUTE_LANES - N_GROUPS - n_experts,), F32)])
    lane_pad = lambda v: jnp.concatenate([v.astype(F32), jnp.zeros((ROUTE_LANES - v.shape[0],), F32)])[None, :]
    return dict(
        aw=aw, bw=bw,
        g_mix=p["g_mix"][None, :],
        w_in=jnp.pad(p["w_in"], ((0, 0), (0, pad))).astype(BF16),
        gm64=_group_mean_matrix(aw, A_HEAD_DIM),
        gq=_tile_row(p["g_qa"], aw // A_HEAD_DIM, A_HEAD_DIM ** -0.5),
        gk=_tile_row(p["g_ka"], aw // A_HEAD_DIM),
        conv_w=p["conv_w"],
        alog=lane_pad(p["a_log"]),
        dtb=lane_pad(jnp.concatenate([p["dt_bias"], jnp.zeros_like(p["dt_bias"])])),
        w_out=p["w_out"].astype(BF16),
        gm128=_group_mean_matrix(bw, B_HEAD_DIM),
        g_gdn=_tile_row(p["g_gdn"], bw // B_HEAD_DIM),
        g_xattn=p["g_xattn"][None, :],
        w_xq=p["w_xq"].astype(BF16),
        g_xq=_tile_row(p["g_xq"], p["w_xq"].shape[1] // B_HEAD_DIM, B_HEAD_DIM ** -0.5),
        w_xo=p["w_xo"].astype(BF16),
        g_ffn=p["g_ffn"][None, :],
        w_r=w_r, b_r=b_r[None, :],
        w_gate=p["w_gate"].astype(BF16), w_up=p["w_up"].astype(BF16), w_down=p["w_down"].astype(BF16),
    )


def _gdn(u_pre, prev, ab, s0, pw, *, seg, conv_seg, chunks):
    w, uv, qg, kt, qk, last = _gdn_prep(u_pre, prev, ab, pw["conv_w"], pw["alog"], pw["dtb"],
                                        seg=seg, conv_seg=conv_seg, chunks=chunks)
    return _gdn_scan(w, uv, qg, kt, qk, last, s0, seg=seg)


def _prompt_layer(h, mem, pw, praw):
    nb_, s, d = h.shape
    assert nb_ == 1
    x = h.reshape(s, d)
    aw, bw = pw["aw"], pw["bw"]
    q, k, v, u_pre, z, ab = _inproj(x, pw["g_mix"], pw["w_in"], pw["gm64"], pw["gq"], pw["gk"],
                                    aw=aw, bw=bw, tm=256)
    res = [_dil_attn(q, k, v, w_, d_) for w_, d_ in DILATIONS]
    gdn_chunks = 4
    blk_rows = gdn_chunks * GDN_CHUNK
    tails = u_pre.reshape(s // blk_rows, blk_rows, 3 * bw)[:, blk_rows - PREV_ROWS:, :]
    prev = jnp.concatenate([jnp.zeros((1, PREV_ROWS, 3 * bw), F32), tails[:-1]], axis=0)
    s0 = jnp.zeros((1, B_HEADS, B_HEAD_DIM, B_HEAD_DIM), F32)
    og, s_fin = _gdn(u_pre, prev, ab, s0, pw, seg=GDN_CHUNK, conv_seg=blk_rows, chunks=gdn_chunks)
    mk, mv = _mem_kv(mem.reshape(mem.shape[1], d), praw["g_mem"][None, :], praw["w_xk"].astype(BF16),
                     praw["w_xv"].astype(BF16), pw["gm128"], _tile_row(praw["g_xk"], bw // B_HEAD_DIM))
    h2, hn, route = _post_prompt([r[0] for r in res], [r[1] for r in res], og, z, x, pw, mk, mv, tm=256)
    y = _moe(hn, h2, route, pw["w_gate"], pw["w_up"], pw["w_down"], chunk=MOE_CHUNK)
    keep = min(DILATIONS[-1][0], s)
    heads = aw // A_HEAD_DIM
    new_k = k[s - keep:].reshape(1, keep, heads, A_HEAD_DIM)
    new_v = v[s - keep:].reshape(1, keep, heads, A_HEAD_DIM)
    conv_new = u_pre[s - (CONV_WIDTH - 1):].reshape(1, CONV_WIDTH - 1, 3 * bw)
    xh = mk.shape[1] // B_HEAD_DIM
    return (y.reshape(1, s, d), new_k, new_v, conv_new, s_fin,
            mk.reshape(1, -1, xh, B_HEAD_DIM), mv.reshape(1, -1, xh, B_HEAD_DIM))


def _sample_layer(h, win_k, win_v, conv_prev, s0, mem_k, mem_v, pw):
    nseq, t_new, d = h.shape
    aw, bw = pw["aw"], pw["bw"]
    n = nseq * t_new
    assert t_new == 8 and GDN_CHUNK % t_new == 0
    x = h.reshape(n, d)
    q, k, v, u_pre, z, ab = _inproj(x, pw["g_mix"], pw["w_in"], pw["gm64"], pw["gq"], pw["gk"],
                                    aw=aw, bw=bw, tm=256)
    w_buf = win_k.shape[1]
    heads = aw // A_HEAD_DIM
    win_t = lambda c_: jnp.transpose(c_, (0, 2, 3, 1)).reshape(nseq, aw, w_buf)
    new_t = lambda t_: jnp.pad(jnp.transpose(t_.reshape(nseq, t_new, aw), (0, 2, 1)),
                               ((0, 0), (0, 0), (128 - t_new, 0)))
    win_back = lambda c_: jnp.transpose(c_.reshape(nseq, heads, A_HEAD_DIM, w_buf), (0, 3, 1, 2))
    new_kt, new_vt, oa = _step_attn(q.reshape(nseq, t_new, aw), new_t(k), new_t(v), win_t(win_k), win_t(win_v))
    new_k, new_v = win_back(new_kt), win_back(new_vt)
    prev = jnp.concatenate([jnp.zeros((nseq, PREV_ROWS - (CONV_WIDTH - 1), 3 * bw), F32),
                            conv_prev.astype(F32)], axis=1)
    og, s_fin = _gdn(u_pre, prev, ab, s0, pw, seg=t_new, conv_seg=t_new, chunks=4)
    mem_rows = lambda m_: m_.reshape(nseq, -1, m_.shape[3])
    h2, hn, route = _post_sample(oa.reshape(n, aw), og, z, x, pw, mem_rows(mem_k), mem_rows(mem_v),
                                 t_new=t_new, seqs=8)
    y = _moe(hn, h2, route, pw["w_gate"], pw["w_up"], pw["w_down"], chunk=MOE_CHUNK)
    conv_new = u_pre.reshape(nseq, t_new, 3 * bw)[:, t_new - (CONV_WIDTH - 1):, :]
    return (y.reshape(nseq, t_new, d), new_k, new_v, conv_new, s_fin)


def kernel(x_prompt, x_sample, mem_prompt, cache_win_k, cache_win_v, state_conv, state_delta, cache_mem_k, cache_mem_v, g_mix, w_in, g_qa, g_ka, conv_w, a_log, dt_bias, g_gdn, w_out, g_xattn, g_mem, w_xq, w_xk, w_xv, g_xq, g_xk, w_xo, g_ffn, w_rg, b_rg, w_re, b_re, w_gate, w_up, w_down):
    depth = w_in.shape[0]
    hp, hs = x_prompt, x_sample
    outs = [[] for _ in range(10)]
    for l in range(depth):
        praw = dict(g_mix=g_mix[l], w_in=w_in[l], g_qa=g_qa[l], g_ka=g_ka[l], conv_w=conv_w[l],
                    a_log=a_log[l], dt_bias=dt_bias[l], g_gdn=g_gdn[l], w_out=w_out[l],
                    g_xattn=g_xattn[l], g_mem=g_mem[l], w_xq=w_xq[l], w_xk=w_xk[l], w_xv=w_xv[l],
                    g_xq=g_xq[l], g_xk=g_xk[l], w_xo=w_xo[l], g_ffn=g_ffn[l], w_rg=w_rg[l],
                    b_rg=b_rg[l], w_re=w_re[l], b_re=b_re[l], w_gate=w_gate[l], w_up=w_up[l],
                    w_down=w_down[l])
        pw = _layer_weights(praw)
        hp, k_p, v_p, c_p, s_p, mk, mv = _prompt_layer(hp, mem_prompt, pw, praw)
        hs, k_s, v_s, c_s, s_s = _sample_layer(hs, cache_win_k[l], cache_win_v[l], state_conv[l],
                                               state_delta[l].astype(F32), cache_mem_k[l], cache_mem_v[l], pw)
        for lst, val in zip(outs, (k_p, v_p, c_p, s_p, mk, mv, k_s, v_s, c_s, s_s)):
            lst.append(val)
    st = [jnp.stack(o) for o in outs]
    st[3] = st[3].astype(state_delta.dtype)
    st[9] = st[9].astype(state_delta.dtype)
    return (hp, hs, *st)
```

```python
import functools

import jax
import jax.numpy as jnp
from jax import lax
from jax.experimental import pallas as pl
from jax.experimental.pallas import tpu as pltpu

F32 = jnp.float32
BF16 = jnp.bfloat16
EPS = 1e-6
NEG_INF = float("-inf")

A_HEAD_DIM = 64
B_HEAD_DIM = 128
B_HEADS = 4
BAND = 128
DILATIONS = ((128, 1), (512, 4), (2048, 16))
GDN_CHUNK = 64
CONV_WIDTH = 4
PREV_ROWS = 8
N_GROUPS = 4
GROUP_EXPERTS = 8
SUB, LANE = 8, 128
ROUTE_LANES = LANE
MOE_BLOCK = 160
MOE_CHUNK = 2048
VMEM_LIMIT = 56 * 1024 * 1024


def _cparams(*sem):
    return pltpu.CompilerParams(dimension_semantics=sem, vmem_limit_bytes=VMEM_LIMIT)


def _bdot(a, b):
    return jnp.dot(a.astype(BF16), b.astype(BF16), preferred_element_type=F32)


def _bdot_nt(a, b):
    return lax.dot_general(a.astype(BF16), b.astype(BF16), (((1,), (1,)), ((), ())),
                           preferred_element_type=F32)


def _split3(x):
    hi = x.astype(BF16)
    r1 = x - hi.astype(F32)
    mid = r1.astype(BF16)
    lo = (r1 - mid.astype(F32)).astype(BF16)
    return hi, mid, lo


def _dot_exact_lhs(a01, x):
    a = a01.astype(BF16)
    hi, mid, lo = _split3(x)
    d = lambda p: jnp.dot(a, p, preferred_element_type=F32)
    return d(hi) + d(mid) + d(lo)


def _group_mean_sq(x, gm):
    sq = x * x
    hi = sq.astype(BF16)
    lo = (sq - hi.astype(F32)).astype(BF16)
    return (jnp.dot(hi, gm, preferred_element_type=F32) + jnp.dot(lo, gm, preferred_element_type=F32))


def _rms(x, g):
    return x * lax.rsqrt(jnp.mean(x * x, axis=-1, keepdims=True) + EPS) * g


def _sigmoid(x):
    return 1.0 / (1.0 + jnp.exp(-x))


def _group_mean_matrix(width, group):
    i = jnp.arange(width)
    return jnp.where((i[:, None] // group) == (i[None, :] // group), 1.0 / group, 0.0).astype(BF16)


def _inproj_kernel(x_ref, g_ref, w_ref, gm_ref, gq_ref, gk_ref,
                   q_ref, k_ref, v_ref, u_ref, z_ref, ab_ref, *, aw, bw):
    nb = _rms(x_ref[...], g_ref[...]).astype(BF16)

    def proj(lo, hi):
        return jnp.dot(nb, w_ref[:, lo:hi], preferred_element_type=F32)

    def put_slabs(ref, val):
        for p in range(aw // LANE):
            ref[p] = val[:, p * LANE:(p + 1) * LANE]

    gm = gm_ref[...]
    qa = proj(0, aw)
    put_slabs(q_ref, qa * lax.rsqrt(_group_mean_sq(qa, gm) + EPS) * gq_ref[...])
    ka = proj(aw, 2 * aw)
    put_slabs(k_ref, ka * lax.rsqrt(_group_mean_sq(ka, gm) + EPS) * gk_ref[...])
    put_slabs(v_ref, proj(2 * aw, 3 * aw))
    u_ref[...] = proj(3 * aw, 3 * aw + 3 * bw)
    z_ref[...] = proj(3 * aw + 3 * bw, 3 * aw + 4 * bw)
    ab_ref[...] = proj(3 * aw + 4 * bw, 3 * aw + 4 * bw + ROUTE_LANES)


def _inproj(x, g_mix, w_in_b, gm64, gq_t, gk_t, *, aw, bw, tm):
    n, d = x.shape
    cols = w_in_b.shape[1]
    row = lambda w: pl.BlockSpec((tm, w), lambda i: (i, 0))
    full = lambda a: pl.BlockSpec(a.shape, lambda i: (0, 0))
    out_w = (3 * bw, bw, ROUTE_LANES)
    slab = pl.BlockSpec((aw // LANE, tm, LANE), lambda i: (0, i, 0))
    return pl.pallas_call(
        functools.partial(_inproj_kernel, aw=aw, bw=bw),
        grid=(n // tm,),
        in_specs=[row(d), full(g_mix), pl.BlockSpec((d, cols), lambda i: (0, 0)),
                  full(gm64), full(gq_t), full(gk_t)],
        out_specs=[slab] * 3 + [row(w) for w in out_w],
        out_shape=[jax.ShapeDtypeStruct((aw // LANE, n, LANE), F32)] * 3
                  + [jax.ShapeDtypeStruct((n, w), F32) for w in out_w],
        compiler_params=_cparams("parallel"),
        name="inproj",
    )(x, g_mix, w_in_b, gm64, gq_t, gk_t)


def _dil_attn_kernel(q_ref, kp_ref, kc_ref, vp_ref, vc_ref, o_ref, od_ref, ld_ref):
    n = pl.program_id(1)
    rows = q_ref.shape[1]
    qi = lax.broadcasted_iota(jnp.int32, (BAND, 2 * BAND), 0)
    kj = lax.broadcasted_iota(jnp.int32, (BAND, 2 * BAND), 1)
    band = (kj >= qi) & (kj <= qi + BAND)
    bias = jnp.where(band, 0.0, NEG_INF)
    bias_first = jnp.where(band & ((kj >= BAND) | (n > 0)), 0.0, NEG_INF)
    lo_half = lax.broadcasted_iota(jnp.int32, (BAND, LANE), 1) < A_HEAD_DIM

    for di, (window, dil) in enumerate(DILATIONS):
        assert window // dil == BAND and rows % (BAND * dil) == 0
        span = BAND * dil
        for r in range(dil):
            for j in range(rows // span):
                start = r + j * span
                take = lambda ref, st: ref[0, pl.ds(st, BAND, stride=dil), :]
                q = take(q_ref, start)
                if j:
                    k_prev, v_prev = take(kc_ref, start - span), take(vc_ref, start - span)
                else:
                    k_prev, v_prev = take(kp_ref, r + rows - span), take(vp_ref, r + rows - span)
                kcat = jnp.concatenate([k_prev, take(kc_ref, start)], axis=0).astype(BF16)
                vcat = jnp.concatenate([v_prev, take(vc_ref, start)], axis=0).astype(BF16)
                res = []
                for half in range(2):
                    keep = lo_half if half == 0 else jnp.logical_not(lo_half)
                    s = _bdot_nt(jnp.where(keep, q, 0.0), kcat) + (bias if j else bias_first)
                    m = jnp.max(s, axis=-1, keepdims=True)
                    e = jnp.exp(s - m)
                    den = jnp.sum(e, axis=-1, keepdims=True)
                    o = jnp.dot(e.astype(BF16), vcat, preferred_element_type=F32) / den
                    res.append((o, m + jnp.log(den)))
                od_ref[di, pl.ds(start, BAND, stride=dil), :] = jnp.where(lo_half, res[0][0], res[1][0])
                ld_ref[di, pl.ds(start, BAND, stride=dil), :] = jnp.where(lo_half, res[0][1], res[1][1])

    step = 256
    for r0 in range(0, rows, step):
        ls = [ld_ref[di, r0:r0 + step, :] for di in range(len(DILATIONS))]
        mm = functools.reduce(jnp.maximum, ls)
        es = [jnp.exp(l - mm) for l in ls]
        num = functools.reduce(lambda a, b: a + b, [e * od_ref[di, r0:r0 + step, :] for di, e in enumerate(es)])
        o_ref[0, r0:r0 + step, :] = num / functools.reduce(lambda a, b: a + b, es)


def _dil_attn(q, k, v, *, rows):
    pairs, s, lanes = q.shape
    assert s % rows == 0 and lanes == LANE
    cur = pl.BlockSpec((1, rows, LANE), lambda p, n: (p, n, 0))
    prev = pl.BlockSpec((1, rows, LANE), lambda p, n: (p, jnp.maximum(n - 1, 0), 0))
    return pl.pallas_call(
        _dil_attn_kernel,
        grid=(pairs, s // rows),
        in_specs=[cur, prev, cur, prev, cur],
        out_specs=cur,
        out_shape=jax.ShapeDtypeStruct(q.shape, F32),
        scratch_shapes=[pltpu.VMEM((len(DILATIONS), rows, LANE), F32)] * 2,
        compiler_params=_cparams("parallel", "arbitrary"),
        name="dil_attn",
    )(q, k, k, v, v)


def _step_attn_kernel(q_ref, kn_ref, vn_ref, kc_ref, vc_ref, ok_ref, ov_ref, oa_ref,
                      kb_ref, vb_ref, *, w_buf, t_new):
    aw = q_ref.shape[2]
    heads = aw // A_HEAD_DIM
    lanes = kn_ref.shape[2]
    kt, vt = kc_ref[0], vc_ref[0]
    kn, vn = kn_ref[0], vn_ref[0]
    is_new = lax.broadcasted_iota(jnp.int32, (aw, lanes), 1) >= lanes - t_new
    for src, new, dst in ((kt, kn, ok_ref), (vt, vn, ov_ref)):
        rolled = pltpu.roll(src, w_buf - t_new, axis=1)
        dst[0, :, 0:w_buf - lanes] = rolled[:, 0:w_buf - lanes]
        dst[0, :, w_buf - lanes:w_buf] = jnp.where(is_new, new, rolled[:, w_buf - lanes:w_buf])
    kb_ref[...] = kt.astype(BF16)
    vb_ref[...] = vt.astype(BF16)

    rows = heads * t_new
    ri = lax.broadcasted_iota(jnp.int32, (rows, aw), 0)
    li = lax.broadcasted_iota(jnp.int32, (rows, aw), 1)
    q_rep = jnp.concatenate([q_ref[0]] * heads, axis=0)
    q_blk = jnp.where((ri // t_new) == (li // A_HEAD_DIM), q_rep, 0.0).astype(BF16)
    ncol = w_buf + lanes
    s_all = jnp.concatenate([jnp.dot(q_blk, kb_ref[...], preferred_element_type=F32),
                             jnp.dot(q_blk, kn.astype(BF16), preferred_element_type=F32)], axis=1)

    def branch(window, dil, col0):
        s = s_all[:, col0:]
        t = lax.broadcasted_iota(jnp.int32, s.shape, 0) % t_new
        cidx = lax.broadcasted_iota(jnp.int32, s.shape, 1) + col0
        r = jnp.where(cidx < w_buf, cidx, cidx - (lanes - t_new))
        diff = w_buf + t - r
        valid = (diff >= 0) & (diff <= window) & ((diff & (dil - 1)) == 0)
        valid = valid & ((cidx < w_buf) | (cidx >= ncol - t_new))
        s = jnp.where(valid, s, NEG_INF)
        m = jnp.max(s, axis=-1, keepdims=True)
        e = jnp.exp(s - m)
        den = jnp.sum(e, axis=-1, keepdims=True)
        return e, den, m + jnp.log(den)

    parts = []
    for window, dil in DILATIONS:
        assert dil & (dil - 1) == 0
        col0 = max(0, (w_buf - window) // 128 * 128)
        parts.append((col0,) + branch(window, dil, col0))
    mm = functools.reduce(jnp.maximum, [p[3] for p in parts])
    wexp = [jnp.exp(p[3] - mm) for p in parts]
    wsum = functools.reduce(lambda a, b: a + b, wexp)
    p_tot = jnp.zeros((rows, ncol), F32)
    for (col0, e, den, _), we in zip(parts, wexp):
        pe = e * (we / (wsum * den))
        if col0:
            pe = jnp.concatenate([jnp.zeros((rows, col0), F32), pe], axis=1)
        p_tot = p_tot + pe
    o = _bdot_nt(p_tot[:, :w_buf], vb_ref[...]) + _bdot_nt(p_tot[:, w_buf:], vn)
    lh = lax.broadcasted_iota(jnp.int32, (t_new, aw), 1) // A_HEAD_DIM
    acc = jnp.zeros((t_new, aw), F32)
    for h in range(heads):
        acc = acc + jnp.where(lh == h, o[h * t_new:(h + 1) * t_new, :], 0.0)
    oa_ref[0] = acc


def _step_attn(q, kn_t, vn_t, cache_kt, cache_vt):
    nseq, aw, w_buf = cache_kt.shape
    t_new = q.shape[1]
    lanes = kn_t.shape[2]
    assert w_buf % lanes == 0 and t_new % 8 == 0 and t_new <= lanes
    qs = pl.BlockSpec((1, t_new, aw), lambda i: (i, 0, 0))
    new = pl.BlockSpec((1, aw, lanes), lambda i: (i, 0, 0))
    win = pl.BlockSpec((1, aw, w_buf), lambda i: (i, 0, 0))
    return pl.pallas_call(
        functools.partial(_step_attn_kernel, w_buf=w_buf, t_new=t_new),
        grid=(nseq,),
        in_specs=[qs, new, new, win, win],
        out_specs=[win, win, qs],
        out_shape=[jax.ShapeDtypeStruct(cache_kt.shape, F32), jax.ShapeDtypeStruct(cache_vt.shape, F32),
                   jax.ShapeDtypeStruct(q.shape, F32)],
        scratch_shapes=[pltpu.VMEM((aw, w_buf), BF16), pltpu.VMEM((aw, w_buf), BF16)],
        compiler_params=_cparams("parallel"),
        name="step_attn",
    )(q, kn_t, vn_t, cache_kt, cache_vt)


def _gdn_prep_kernel(u_ref, prev_ref, ab_ref, cw_ref, alog_ref, dtb_ref,
                     w_ref, uv_ref, qg_ref, kt_ref, qk_ref, last_ref, full_ref, act_ref,
                     *, seg, conv_seg, chunks):
    c = GDN_CHUNK
    dk = B_HEAD_DIM
    bw = B_HEADS * dk
    ii = lax.broadcasted_iota(jnp.int32, (c, c), 0)
    jj = lax.broadcasted_iota(jnp.int32, (c, c), 1)
    same = (ii // seg) == (jj // seg)
    incl = same & (ii >= jj)
    strict = same & (ii > jj)
    eye = ii == jj
    ones_c = jnp.ones((c, c), BF16)
    seg_cols = jnp.where((lax.broadcasted_iota(jnp.int32, (c, dk), 0) % seg)
                         == lax.broadcasted_iota(jnp.int32, (c, dk), 1), 1.0, 0.0)
    cw = cw_ref[...]
    rows = chunks * c

    for sgi in range(rows // conv_seg):
        base = sgi * (conv_seg + PREV_ROWS)
        full_ref[base:base + PREV_ROWS, :] = prev_ref[sgi]
        full_ref[base + PREV_ROWS:base + PREV_ROWS + conv_seg, :] = u_ref[sgi * conv_seg:(sgi + 1) * conv_seg, :]
        y = jnp.zeros((conv_seg, 3 * bw), F32)
        for j in range(CONV_WIDTH):
            off = base + PREV_ROWS - (CONV_WIDTH - 1) + j
            y = y + full_ref[off:off + conv_seg, :] * cw[j:j + 1, :]
        act_ref[sgi * conv_seg:(sgi + 1) * conv_seg, :] = y * _sigmoid(y)

    ri = lax.broadcasted_iota(jnp.int32, (rows, rows), 0)
    rj = lax.broadcasted_iota(jnp.int32, (rows, rows), 1)
    rsame = (ri // seg) == (rj // seg)
    ab = ab_ref[...]
    sp = ab + dtb_ref[...]
    sp = jnp.maximum(sp, 0.0) + jnp.log(1.0 + jnp.exp(-jnp.abs(sp)))
    g_all = -jnp.exp(alog_ref[...]) * sp
    gcum_all = _dot_exact_lhs(jnp.where(rsame & (ri >= rj), 1.0, 0.0), g_all)
    gtot_all = _dot_exact_lhs(jnp.where(rsame, 1.0, 0.0), g_all)
    beta_all = _sigmoid(ab)
    eye_f = jnp.where(eye, 1.0, 0.0)

    grow = []
    for ci in range(chunks):
        rs = slice(ci * c, (ci + 1) * c)
        diag = jnp.concatenate([jnp.where(eye, gcum_all[rs, h:h + 1], 0.0) for h in range(B_HEADS)], axis=1)
        grow.append(_dot_exact_lhs(ones_c, diag))

    prob = [(ci, h) for ci in range(chunks) for h in range(B_HEADS)]
    col = lambda arr, ci, lane: arr[ci * c:(ci + 1) * c, lane:lane + 1]
    gc = [col(gcum_all, ci, h) for ci, h in prob]
    gt = [col(gtot_all, ci, h) for ci, h in prob]
    beta = [col(beta_all, ci, B_HEADS + h) for ci, h in prob]
    q, k, v = [], [], []
    for ci, h in prob:
        rs = slice(ci * c, (ci + 1) * c)
        qh = act_ref[rs, h * dk:(h + 1) * dk]
        kh = act_ref[rs, bw + h * dk:bw + (h + 1) * dk]
        q.append(qh * lax.rsqrt(jnp.sum(qh * qh, axis=-1, keepdims=True) + EPS) * (dk ** -0.5))
        k.append(kh * lax.rsqrt(jnp.sum(kh * kh, axis=-1, keepdims=True) + EPS))
        v.append(act_ref[rs, 2 * bw + h * dk:2 * bw + (h + 1) * dk])
    e_incl = [jnp.exp(jnp.where(incl, gc[i] - grow[ci][:, h * c:(h + 1) * c], NEG_INF))
              for i, (ci, h) in enumerate(prob)]
    qkk = [_bdot_nt(jnp.concatenate([q[i], k[i]], axis=0), k[i]) for i in range(len(prob))]
    a = [beta[i] * jnp.where(strict, e_incl[i], 0.0) * qkk[i][c:, :] for i in range(len(prob))]
    blk = 8
    inblk = (ii // blk) == (jj // blk)
    d0 = [jnp.where(inblk, t, 0.0) for t in a]
    d2 = [_bdot(t, t) for t in d0]
    d4 = [_bdot(t, t) for t in d2]
    x = [(eye_f - t) + _bdot(eye_f - t, t2) for t, t2 in zip(d0, d2)]
    x = [t + _bdot(t, t4) for t, t4 in zip(x, d4)]
    while blk < seg:
        off_blk = ((ii // (2 * blk)) == (jj // (2 * blk))) & ((ii // blk) != (jj // blk))
        xe = [_bdot(t, jnp.where(off_blk, ta, 0.0)) for t, ta in zip(x, a)]
        x = [t - _bdot(te, t) for t, te in zip(x, xe)]
        blk *= 2
    gamma = [jnp.exp(t) for t in gc]
    wuv = [_bdot(x[i], jnp.concatenate([(beta[i] * gamma[i]) * k[i], beta[i] * v[i]], axis=1))
           for i in range(len(prob))]
    for i, (ci, h) in enumerate(prob):
        rs = slice(ci * c, (ci + 1) * c)
        hs = slice(h * dk, (h + 1) * dk)
        w_ref[rs, hs] = wuv[i][:, :dk]
        uv_ref[rs, hs] = wuv[i][:, dk:]
        qg_ref[rs, hs] = gamma[i] * q[i]
        kt_ref[rs, hs] = k[i] * jnp.exp(gt[i] - gc[i])
        qk = qkk[i][:c, :] * e_incl[i]
        qk_ref[rs, hs] = (jnp.concatenate([qk, jnp.zeros((c, dk - c), F32)], axis=1) if seg == c
                          else _bdot(qk, seg_cols))
        last_ref[rs, hs] = jnp.broadcast_to(jnp.exp(gt[i]), (c, dk))


def _gdn_prep(u_pre, prev, ab, conv_w, alog_row, dtb_row, *, seg, conv_seg, chunks):
    n, cw = u_pre.shape
    bw = cw // 3
    c = GDN_CHUNK
    rows = chunks * c
    assert n % rows == 0 and c % seg == 0 and seg % 8 == 0 and rows % conv_seg == 0 and conv_seg % seg == 0
    ncs = rows // conv_seg
    row = lambda w: pl.BlockSpec((rows, w), lambda i: (i, 0))
    full = lambda a: pl.BlockSpec(a.shape, lambda i: (0, 0))
    return pl.pallas_call(
        functools.partial(_gdn_prep_kernel, seg=seg, conv_seg=conv_seg, chunks=chunks),
        grid=(n // rows,),
        in_specs=[row(cw), pl.BlockSpec((ncs, PREV_ROWS, cw), lambda i: (i, 0, 0)),
                  row(ROUTE_LANES), full(conv_w), full(alog_row), full(dtb_row)],
        out_specs=[row(bw)] * 6,
        out_shape=[jax.ShapeDtypeStruct((n, bw), F32)] * 6,
        scratch_shapes=[pltpu.VMEM((ncs * (conv_seg + PREV_ROWS), cw), F32), pltpu.VMEM((rows, cw), F32)],
        compiler_params=_cparams("parallel"),
        name=f"gdn_prep_{seg}",
    )(u_pre, prev, ab, conv_w, alog_row, dtb_row)


def _gdn_scan_kernel(w_ref, uv_ref, qg_ref, kt_ref, qk_ref, last_ref, s0_ref, o_ref, s_ref, *, seg):
    dk = B_HEAD_DIM

    @pl.when(pl.program_id(1) == 0)
    def _():
        s_ref[...] = s0_ref[...]

    pad = dk - seg
    padr = lambda t: jnp.concatenate([t, jnp.zeros((pad, dk), F32)], axis=0) if pad else t
    ii = lax.broadcasted_iota(jnp.int32, (dk, dk), 0)
    jj = lax.broadcasted_iota(jnp.int32, (dk, dk), 1)
    eye = jnp.where(ii == jj, 1.0, 0.0).astype(BF16)
    for h in range(B_HEADS):
        hs = slice(h * dk, (h + 1) * dk)
        s = s_ref[0, h]
        sb = s.astype(BF16)
        u = padr(uv_ref[:, hs]) - _bdot(padr(w_ref[:, hs]), sb)
        ub = u.astype(BF16)
        o = _bdot(padr(qg_ref[:, hs]), sb) + _bdot(padr(qk_ref[:, hs]), ub)
        o_ref[:, hs] = o[:seg, :]
        ktt = _bdot_nt(eye, padr(kt_ref[:, hs]))
        s_ref[0, h] = last_ref[0:1, hs] * s + _bdot(ktt, ub)


def _gdn_scan(w, uv, qg, kt, qk, last, s0, *, seg):
    n, bw = w.shape
    nseq = s0.shape[0]
    per_seq = n // (nseq * seg)
    row = pl.BlockSpec((seg, bw), lambda s, i: (s * per_seq + i, 0))
    st = pl.BlockSpec((1,) + s0.shape[1:], lambda s, i: (s, 0, 0, 0))
    return pl.pallas_call(
        functools.partial(_gdn_scan_kernel, seg=seg),
        grid=(nseq, per_seq),
        in_specs=[row] * 6 + [st],
        out_specs=[row, st],
        out_shape=[jax.ShapeDtypeStruct((n, bw), F32), jax.ShapeDtypeStruct(s0.shape, F32)],
        compiler_params=_cparams("parallel", "arbitrary"),
        name=f"gdn_scan_{seg}",
    )(w, uv, qg, kt, qk, last, s0)


def _mem_kv_kernel(mem_ref, g_ref, wk_ref, wv_ref, gm_ref, gk_ref, k_ref, v_ref):
    mn = _rms(mem_ref[...], g_ref[...]).astype(BF16)
    k = jnp.dot(mn, wk_ref[...], preferred_element_type=F32)
    k_ref[...] = k * lax.rsqrt(_group_mean_sq(k, gm_ref[...]) + EPS) * gk_ref[...]
    v_ref[...] = jnp.dot(mn, wv_ref[...], preferred_element_type=F32)


def _mem_kv(mem, g_mem, w_xk_b, w_xv_b, gm128, gk_t):
    m = mem.shape[0]
    xw = w_xk_b.shape[1]
    return pl.pallas_call(
        _mem_kv_kernel,
        out_shape=[jax.ShapeDtypeStruct((m, xw), F32)] * 2,
        compiler_params=pltpu.CompilerParams(vmem_limit_bytes=VMEM_LIMIT),
        name="mem_kv",
    )(mem, g_mem, w_xk_b, w_xv_b, gm128, gk_t)


def _route(logits):
    lane = lax.broadcasted_iota(jnp.int32, logits.shape, 1)
    lane_f = lane.astype(F32)
    big = float(ROUTE_LANES)
    lg = jnp.where(lane < N_GROUPS, logits, NEG_INF)
    mg = jnp.max(lg, axis=-1, keepdims=True)
    zg = jnp.sum(jnp.exp(lg - mg), axis=-1, keepdims=True)
    pg_top = 1.0 / zg
    gidx = jnp.min(jnp.where(lg == mg, lane_f, big), axis=-1, keepdims=True)
    e_lo = N_GROUPS + GROUP_EXPERTS * gidx
    emask = (lane_f >= e_lo) & (lane_f < e_lo + GROUP_EXPERTS)
    le = jnp.where(emask, logits, NEG_INF)
    me = jnp.max(le, axis=-1, keepdims=True)
    ee = jnp.exp(le - me)
    pe = ee / jnp.sum(ee, axis=-1, keepdims=True)
    pe = jnp.where(emask, pe, -1.0)
    p1 = jnp.max(pe, axis=-1, keepdims=True)
    i1 = jnp.min(jnp.where(pe == p1, lane_f, big), axis=-1, keepdims=True)
    pe2 = jnp.where(lane_f == i1, -1.0, pe)
    p2 = jnp.max(pe2, axis=-1, keepdims=True)
    i2 = jnp.min(jnp.where(pe2 == p2, lane_f, big), axis=-1, keepdims=True)
    den = p1 + p2
    w1 = pg_top * p1 / den
    w2 = pg_top * p2 / den
    out = jnp.where(lane == 0, i1 - N_GROUPS, 0.0)
    out = jnp.where(lane == 1, i2 - N_GROUPS, out)
    out = jnp.where(lane == 2, w1, out)
    return jnp.where(lane == 3, w2, out)


def _post_common(oa, og_ref, z_ref, h_ref, wout_ref, gm_ref, ggdn_ref, gx_ref, wxq_ref, gxq_ref):
    og = og_ref[...]
    z = z_ref[...]
    ob = og * lax.rsqrt(_group_mean_sq(og, gm_ref[...]) + EPS) * ggdn_ref[...] * (z * _sigmoid(z))
    cat = jnp.concatenate([oa, ob], axis=-1).astype(BF16)
    h1 = h_ref[...] + jnp.dot(cat, wout_ref[...], preferred_element_type=F32)
    hx = _rms(h1, gx_ref[...]).astype(BF16)
    q = jnp.dot(hx, wxq_ref[...], preferred_element_type=F32)
    qn = q * lax.rsqrt(_group_mean_sq(q, gm_ref[...]) + EPS) * gxq_ref[...]
    return h1, qn


def _mem_attend_rows(qn, mk, mv):
    outs = []
    for h in range(qn.shape[1] // B_HEAD_DIM):
        hs = slice(h * B_HEAD_DIM, (h + 1) * B_HEAD_DIM)
        s = _bdot_nt(qn[:, hs], mk[:, hs])
        m = jnp.max(s, axis=-1, keepdims=True)
        e = jnp.exp(s - m)
        p = e / jnp.sum(e, axis=-1, keepdims=True)
        outs.append(_bdot(p, mv[:, hs]))
    return jnp.concatenate(outs, axis=-1)


def _post_tail(h1, ox, wxo_ref, gffn_ref, wr_ref, br_ref, h2_ref, hn_ref, route_ref):
    h2 = h1 + jnp.dot(ox.astype(BF16), wxo_ref[...], preferred_element_type=F32)
    h2_ref[...] = h2
    hn = _rms(h2, gffn_ref[...])
    tm = hn.shape[0]
    for j in range(hn.shape[1] // LANE):
        hn_ref[pl.ds(j, tm, stride=SUB), :] = hn[:, j * LANE:(j + 1) * LANE]
    logits = jnp.dot(hn, wr_ref[...], preferred_element_type=F32,
                     precision=lax.Precision.HIGHEST) + br_ref[...]
    route_ref[...] = _route(logits)


def _post_prompt_kernel(oa_ref, og_ref, z_ref, h_ref,
                        wout_ref, gm_ref, ggdn_ref, gx_ref, wxq_ref, gxq_ref, mk_ref, mv_ref,
                        wxo_ref, gffn_ref, wr_ref, br_ref, h2_ref, hn_ref, route_ref):
    oa = jnp.concatenate([oa_ref[p] for p in range(oa_ref.shape[0])], axis=-1)
    h1, qn = _post_common(oa, og_ref, z_ref, h_ref, wout_ref, gm_ref, ggdn_ref, gx_ref, wxq_ref, gxq_ref)
    ox = _mem_attend_rows(qn, mk_ref[...], mv_ref[...])
    _post_tail(h1, ox, wxo_ref, gffn_ref, wr_ref, br_ref, h2_ref, hn_ref, route_ref)


def _post_sample_kernel(oa_ref, og_ref, z_ref, h_ref,
                        wout_ref, gm_ref, ggdn_ref, gx_ref, wxq_ref, gxq_ref, mk_ref, mv_ref,
                        wxo_ref, gffn_ref, wr_ref, br_ref, h2_ref, hn_ref, route_ref, *, t_new):
    h1, qn = _post_common(oa_ref[...], og_ref, z_ref, h_ref, wout_ref, gm_ref, ggdn_ref, gx_ref,
                          wxq_ref, gxq_ref)
    dh = B_HEAD_DIM
    heads = qn.shape[1] // dh
    rows = heads * t_new
    ncol = mk_ref.shape[1]
    own = ((lax.broadcasted_iota(jnp.int32, (rows, ncol), 0) // t_new)
           == (lax.broadcasted_iota(jnp.int32, (rows, ncol), 1) % heads))
    outs = []
    for sq in range(mk_ref.shape[0]):
        qs = qn[sq * t_new:(sq + 1) * t_new, :]
        q_rows = jnp.concatenate([qs[:, h * dh:(h + 1) * dh] for h in range(heads)], axis=0)
        s = jnp.where(own, _bdot_nt(q_rows, mk_ref[sq]), NEG_INF)
        m = jnp.max(s, axis=-1, keepdims=True)
        e = jnp.exp(s - m)
        o = _bdot(e / jnp.sum(e, axis=-1, keepdims=True), mv_ref[sq])
        outs.append(jnp.concatenate([o[h * t_new:(h + 1) * t_new, :] for h in range(heads)], axis=1))
    ox = jnp.concatenate(outs, axis=0)
    _post_tail(h1, ox, wxo_ref, gffn_ref, wr_ref, br_ref, h2_ref, hn_ref, route_ref)


def _post_weights_specs(weights):
    return [pl.BlockSpec(a.shape, lambda i: (0, 0)) for a in weights]


def _post_outs(n, d, tm):
    assert d == SUB * LANE
    row = lambda w: pl.BlockSpec((tm, w), lambda i: (i, 0))
    specs = [row(d), pl.BlockSpec((tm * SUB, LANE), lambda i: (i, 0)), row(ROUTE_LANES)]
    shapes = [jax.ShapeDtypeStruct((n, d), F32), jax.ShapeDtypeStruct((n * SUB, LANE), F32),
              jax.ShapeDtypeStruct((n, ROUTE_LANES), F32)]
    return specs, shapes


def _post_prompt(oa_slabs, og, z, h, pw, mk, mv, *, tm):
    n, d = h.shape
    aw = og.shape[1]
    row = lambda w: pl.BlockSpec((tm, w), lambda i: (i, 0))
    full = lambda a: pl.BlockSpec(a.shape, lambda i: (0, 0))
    slab = pl.BlockSpec((oa_slabs.shape[0], tm, LANE), lambda i: (0, i, 0))
    w1 = [pw["w_out"], pw["gm128"], pw["g_gdn"], pw["g_xattn"], pw["w_xq"], pw["g_xq"]]
    w2 = [pw["w_xo"], pw["g_ffn"], pw["w_r"], pw["b_r"]]
    specs, shapes = _post_outs(n, d, tm)
    return pl.pallas_call(
        _post_prompt_kernel,
        grid=(n // tm,),
        in_specs=[slab, row(aw), row(aw), row(d)] + _post_weights_specs(w1) + [full(mk), full(mv)]
                 + _post_weights_specs(w2),
        out_specs=specs, out_shape=shapes,
        compiler_params=_cparams("parallel"),
        name="post_prompt",
    )(oa_slabs, og, z, h, *w1, mk, mv, *w2)


def _post_sample(oa, og, z, h, pw, mk, mv, *, t_new, seqs):
    n, d = h.shape
    aw = og.shape[1]
    tm = t_new * seqs
    row = lambda w: pl.BlockSpec((tm, w), lambda i: (i, 0))
    mem = pl.BlockSpec((seqs,) + mk.shape[1:], lambda i: (i, 0, 0))
    w1 = [pw["w_out"], pw["gm128"], pw["g_gdn"], pw["g_xattn"], pw["w_xq"], pw["g_xq"]]
    w2 = [pw["w_xo"], pw["g_ffn"], pw["w_r"], pw["b_r"]]
    specs, shapes = _post_outs(n, d, tm)
    return pl.pallas_call(
        functools.partial(_post_sample_kernel, t_new=t_new),
        grid=(n // tm,),
        in_specs=[row(aw)] * 3 + [row(d)] + _post_weights_specs(w1) + [mem, mem]
                 + _post_weights_specs(w2),
        out_specs=specs, out_shape=shapes,
        compiler_params=_cparams("parallel"),
        name="post_sample",
    )(oa, og, z, h, *w1, mk, mv, *w2)


def _moe_kernel(blk_e_ref, blk_start_ref, blk_cnt_ref, tok_ref, tw_ref, x_ref, res_ref,
                wg_ref, wu_ref, wd_ref, out_ref, acc_ref, xs_ref, ys_ref, *, nb):
    c = pl.program_id(0)
    b = pl.program_id(1)
    chunk, d = res_ref.shape
    nl = d // LANE
    cnt = blk_cnt_ref[c * nb + b]

    @pl.when(cnt > 0)
    def _():
        base = blk_start_ref[c * nb + b]
        for r in range(MOE_BLOCK):
            t = tok_ref[0, 0, base + r] >> 1
            xs_ref[r * SUB:(r + 1) * SUB, :] = x_ref[pl.ds(pl.multiple_of(t * SUB, SUB), SUB), :]
        xb = jnp.concatenate([xs_ref[pl.ds(j, MOE_BLOCK, stride=SUB), :] for j in range(nl)],
                             axis=1).astype(BF16)
        hg = jnp.dot(xb, wg_ref[0], preferred_element_type=F32)
        hu = jnp.dot(xb, wu_ref[0], preferred_element_type=F32)
        act = (hg * _sigmoid(hg) * hu).astype(BF16)
        y = jnp.dot(act, wd_ref[0], preferred_element_type=F32)
        for j in range(nl):
            ys_ref[pl.ds(j, MOE_BLOCK, stride=SUB), :] = y[:, j * LANE:(j + 1) * LANE]
        for r in range(MOE_BLOCK):
            e = tok_ref[0, 0, base + r]
            slot = jnp.where(r < cnt, (e & 1) * chunk + (e >> 1), 2 * chunk)
            off = pl.multiple_of(slot * SUB, SUB)
            acc_ref[pl.ds(off, SUB), :] = tw_ref[0, 0, base + r] * ys_ref[r * SUB:(r + 1) * SUB, :]

    @pl.when(b == nb - 1)
    def _():
        step = 256
        for r0 in range(0, chunk, step):
            for j in range(nl):
                out_ref[r0:r0 + step, j * LANE:(j + 1) * LANE] = (
                    res_ref[r0:r0 + step, j * LANE:(j + 1) * LANE]
                    + acc_ref[pl.ds(r0 * SUB + j, step, stride=SUB), :]
                    + acc_ref[pl.ds((chunk + r0) * SUB + j, step, stride=SUB), :])


def _dispatch(route, chunk, n_experts):
    n = route.shape[0]
    nch = n // chunk
    rows = 2 * chunk
    nb = rows // MOE_BLOCK + n_experts
    e = route[:, :2].astype(jnp.int32).reshape(nch, rows)
    w = route[:, 2:4].reshape(nch, rows)
    tok = jnp.argsort(e, axis=1, stable=True).astype(jnp.int32)
    tw = jnp.take_along_axis(w, tok, axis=1)
    ex = jnp.arange(n_experts, dtype=jnp.int32)
    counts = jnp.sum((e[:, :, None] == ex).astype(jnp.int32), axis=1)
    start = jnp.cumsum(counts, axis=1) - counts
    nblk_e = (counts + MOE_BLOCK - 1) // MOE_BLOCK
    bend = jnp.cumsum(nblk_e, axis=1)
    bstart = bend - nblk_e
    total = bend[:, -1:]
    b = jnp.arange(nb, dtype=jnp.int32)[None, :]
    bb = jnp.minimum(b, total - 1)
    eb = jnp.sum((bend[:, None, :] <= bb[:, :, None]).astype(jnp.int32), axis=2)
    sel = eb[:, :, None] == ex
    pick = lambda tbl: jnp.sum(jnp.where(sel, tbl[:, None, :], 0), axis=2)
    j = bb - pick(bstart)
    blk_start = pick(start) + j * MOE_BLOCK
    blk_cnt = jnp.where(b < total, jnp.clip(pick(counts) - j * MOE_BLOCK, 0, MOE_BLOCK), 0)
    flat = lambda t_: t_.astype(jnp.int32).reshape(-1)
    pad = lambda t_: jnp.pad(t_, ((0, 0), (0, MOE_BLOCK))).reshape(nch, 1, rows + MOE_BLOCK)
    return flat(eb), flat(blk_start), flat(blk_cnt), pad(tok), pad(tw), nb


def _moe(hn_t, h2, route, wg_b, wu_b, wd_b, *, chunk):
    n, d = h2.shape
    n_experts, _, ff = wg_b.shape
    chunk = min(chunk, n)
    assert n % chunk == 0 and d == SUB * LANE
    nch = n // chunk
    blk_e, blk_start, blk_cnt, tok, tw, nb = _dispatch(route, chunk, n_experts)
    tab = pl.BlockSpec((1, 1, tok.shape[2]), lambda c, b, *_: (c, 0, 0), memory_space=pltpu.SMEM)
    once = lambda shp: pl.BlockSpec(shp, lambda c, b, *_: (c, 0), pipeline_mode=pl.Buffered(1))
    wspec = lambda shp: pl.BlockSpec((1,) + shp, lambda c, b, be, *_: (be[c * nb + b], 0, 0))
    grid_spec = pltpu.PrefetchScalarGridSpec(
        num_scalar_prefetch=3,
        grid=(nch, nb),
        in_specs=[tab, tab, once((chunk * SUB, LANE)), once((chunk, d)),
                  wspec((d, ff)), wspec((d, ff)), wspec((ff, d))],
        out_specs=once((chunk, d)),
        scratch_shapes=[pltpu.VMEM(((2 * chunk + 1) * SUB, LANE), F32), pltpu.VMEM((MOE_BLOCK * SUB, LANE), F32),
                        pltpu.VMEM((MOE_BLOCK * SUB, LANE), F32)],
    )
    return pl.pallas_call(
        functools.partial(_moe_kernel, nb=nb),
        grid_spec=grid_spec,
        out_shape=jax.ShapeDtypeStruct((n, d), F32),
        compiler_params=_cparams("parallel", "arbitrary"),
        name="moe",
    )(blk_e, blk_start, blk_cnt, tok, tw, hn_t, h2, wg_b, wu_b, wd_b)


def _tile_row(g, reps, scale=1.0):
    return (jnp.tile(g.astype(F32), reps) * scale)[None, :]


def _layer_weights(p):
    d, in_cols = p["w_in"].shape
    aw = d // 2
    bw = d - aw
    pad = 3 * aw + 4 * bw + ROUTE_LANES - in_cols
    n_experts = p["w_re"].shape[1]
    w_r = jnp.concatenate([p["w_rg"], p["w_re"],
                           jnp.zeros((d, ROUTE_LANES - N_GROUPS - n_experts), F32)], axis=1)
    b_r = jnp.concatenate([p["b_rg"], p["b_re"], jnp.zeros((ROUTE_LANES - N_GROUPS - n_experts,), F32)])
    lane_pad = lambda v: jnp.concatenate([v.astype(F32), jnp.zeros((ROUTE_LANES - v.shape[0],), F32)])[None, :]
    return dict(
        aw=aw, bw=bw,
        g_mix=p["g_mix"][None, :],
        w_in=jnp.pad(p["w_in"], ((0, 0), (0, pad))).astype(BF16),
        gm64=_group_mean_matrix(aw, A_HEAD_DIM),
        gq=_tile_row(p["g_qa"], aw // A_HEAD_DIM, A_HEAD_DIM ** -0.5),
        gk=_tile_row(p["g_ka"], aw // A_HEAD_DIM),
        conv_w=p["conv_w"],
        alog=lane_pad(p["a_log"]),
        dtb=lane_pad(jnp.concatenate([p["dt_bias"], jnp.zeros_like(p["dt_bias"])])),
        w_out=p["w_out"].astype(BF16),
        gm128=_group_mean_matrix(bw, B_HEAD_DIM),
        g_gdn=_tile_row(p["g_gdn"], bw // B_HEAD_DIM),
        g_xattn=p["g_xattn"][None, :],
        w_xq=p["w_xq"].astype(BF16),
        g_xq=_tile_row(p["g_xq"], p["w_xq"].shape[1] // B_HEAD_DIM, B_HEAD_DIM ** -0.5),
        w_xo=p["w_xo"].astype(BF16),
        g_ffn=p["g_ffn"][None, :],
        w_r=w_r, b_r=b_r[None, :],
        w_gate=p["w_gate"].astype(BF16), w_up=p["w_up"].astype(BF16), w_down=p["w_down"].astype(BF16),
    )


def _gdn(u_pre, prev, ab, s0, pw, *, seg, conv_seg, chunks):
    w, uv, qg, kt, qk, last = _gdn_prep(u_pre, prev, ab, pw["conv_w"], pw["alog"], pw["dtb"],
                                        seg=seg, conv_seg=conv_seg, chunks=chunks)
    return _gdn_scan(w, uv, qg, kt, qk, last, s0, seg=seg)


def _prompt_layer(h, mem, pw, praw):
    nb_, s, d = h.shape
    assert nb_ == 1
    x = h.reshape(s, d)
    aw, bw = pw["aw"], pw["bw"]
    q, k, v, u_pre, z, ab = _inproj(x, pw["g_mix"], pw["w_in"], pw["gm64"], pw["gq"], pw["gk"],
                                    aw=aw, bw=bw, tm=512)
    oa = _dil_attn(q, k, v, rows=DILATIONS[-1][0])
    gdn_chunks = 4
    blk_rows = gdn_chunks * GDN_CHUNK
    tails = u_pre.reshape(s // blk_rows, blk_rows, 3 * bw)[:, blk_rows - PREV_ROWS:, :]
    prev = jnp.concatenate([jnp.zeros((1, PREV_ROWS, 3 * bw), F32), tails[:-1]], axis=0)
    s0 = jnp.zeros((1, B_HEADS, B_HEAD_DIM, B_HEAD_DIM), F32)
    og, s_fin = _gdn(u_pre, prev, ab, s0, pw, seg=GDN_CHUNK, conv_seg=blk_rows, chunks=gdn_chunks)
    mk, mv = _mem_kv(mem.reshape(mem.shape[1], d), praw["g_mem"][None, :], praw["w_xk"].astype(BF16),
                     praw["w_xv"].astype(BF16), pw["gm128"], _tile_row(praw["g_xk"], bw // B_HEAD_DIM))
    h2, hn, route = _post_prompt(oa, og, z, x, pw, mk, mv, tm=256)
    y = _moe(hn, h2, route, pw["w_gate"], pw["w_up"], pw["w_down"], chunk=MOE_CHUNK)
    keep = min(DILATIONS[-1][0], s)
    heads = aw // A_HEAD_DIM
    tail = lambda t_: jnp.transpose(t_[:, s - keep:, :], (1, 0, 2)).reshape(1, keep, heads, A_HEAD_DIM)
    new_k, new_v = tail(k), tail(v)
    conv_new = u_pre[s - (CONV_WIDTH - 1):].reshape(1, CONV_WIDTH - 1, 3 * bw)
    xh = mk.shape[1] // B_HEAD_DIM
    return (y.reshape(1, s, d), new_k, new_v, conv_new, s_fin,
            mk.reshape(1, -1, xh, B_HEAD_DIM), mv.reshape(1, -1, xh, B_HEAD_DIM))


def _sample_layer(h, win_k, win_v, conv_prev, s0, mem_k, mem_v, pw):
    nseq, t_new, d = h.shape
    aw, bw = pw["aw"], pw["bw"]
    n = nseq * t_new
    assert t_new == 8 and GDN_CHUNK % t_new == 0
    x = h.reshape(n, d)
    q, k, v, u_pre, z, ab = _inproj(x, pw["g_mix"], pw["w_in"], pw["gm64"], pw["gq"], pw["gk"],
                                    aw=aw, bw=bw, tm=512)
    w_buf = win_k.shape[1]
    heads = aw // A_HEAD_DIM
    win_t = lambda c_: jnp.transpose(c_, (0, 2, 3, 1)).reshape(nseq, aw, w_buf)
    new_t = lambda t_: jnp.pad(jnp.transpose(t_.reshape(aw // LANE, nseq, t_new, LANE), (1, 0, 3, 2))
                               .reshape(nseq, aw, t_new), ((0, 0), (0, 0), (LANE - t_new, 0)))
    win_back = lambda c_: jnp.transpose(c_.reshape(nseq, heads, A_HEAD_DIM, w_buf), (0, 3, 1, 2))
    q_rows = jnp.transpose(q.reshape(aw // LANE, nseq, t_new, LANE), (1, 2, 0, 3)).reshape(nseq, t_new, aw)
    new_kt, new_vt, oa = _step_attn(q_rows, new_t(k), new_t(v), win_t(win_k), win_t(win_v))
    new_k, new_v = win_back(new_kt), win_back(new_vt)
    prev = jnp.concatenate([jnp.zeros((nseq, PREV_ROWS - (CONV_WIDTH - 1), 3 * bw), F32),
                            conv_prev.astype(F32)], axis=1)
    og, s_fin = _gdn(u_pre, prev, ab, s0, pw, seg=t_new, conv_seg=t_new, chunks=4)
    mem_rows = lambda m_: m_.reshape(nseq, -1, m_.shape[3])
    h2, hn, route = _post_sample(oa.reshape(n, aw), og, z, x, pw, mem_rows(mem_k), mem_rows(mem_v),
                                 t_new=t_new, seqs=8)
    y = _moe(hn, h2, route, pw["w_gate"], pw["w_up"], pw["w_down"], chunk=MOE_CHUNK)
    conv_new = u_pre.reshape(nseq, t_new, 3 * bw)[:, t_new - (CONV_WIDTH - 1):, :]
    return (y.reshape(nseq, t_new, d), new_k, new_v, conv_new, s_fin)


def kernel(x_prompt, x_sample, mem_prompt, cache_win_k, cache_win_v, state_conv, state_delta, cache_mem_k, cache_mem_v, g_mix, w_in, g_qa, g_ka, conv_w, a_log, dt_bias, g_gdn, w_out, g_xattn, g_mem, w_xq, w_xk, w_xv, g_xq, g_xk, w_xo, g_ffn, w_rg, b_rg, w_re, b_re, w_gate, w_up, w_down):
    depth = w_in.shape[0]
    hp, hs = x_prompt, x_sample
    outs = [[] for _ in range(10)]
    for l in range(depth):
        praw = dict(g_mix=g_mix[l], w_in=w_in[l], g_qa=g_qa[l], g_ka=g_ka[l], conv_w=conv_w[l],
                    a_log=a_log[l], dt_bias=dt_bias[l], g_gdn=g_gdn[l], w_out=w_out[l],
                    g_xattn=g_xattn[l], g_mem=g_mem[l], w_xq=w_xq[l], w_xk=w_xk[l], w_xv=w_xv[l],
                    g_xq=g_xq[l], g_xk=g_xk[l], w_xo=w_xo[l], g_ffn=g_ffn[l], w_rg=w_rg[l],
                    b_rg=b_rg[l], w_re=w_re[l], b_re=b_re[l], w_gate=w_gate[l], w_up=w_up[l],
                    w_down=w_down[l])
        pw = _layer_weights(praw)
        hp, k_p, v_p, c_p, s_p, mk, mv = _prompt_layer(hp, mem_prompt, pw, praw)
        hs, k_s, v_s, c_s, s_s = _sample_layer(hs, cache_win_k[l], cache_win_v[l], state_conv[l],
                                               state_delta[l].astype(F32), cache_mem_k[l], cache_mem_v[l], pw)
        for lst, val in zip(outs, (k_p, v_p, c_p, s_p, mk, mv, k_s, v_s, c_s, s_s)):
            lst.append(val)
    st = [jnp.stack(o) for o in outs]
    st[3] = st[3].astype(state_delta.dtype)
    st[9] = st[9].astype(state_delta.dtype)
    return (hp, hs, *st)
```

```python
import functools

import jax
import jax.numpy as jnp
from jax import lax
from jax.experimental import pallas as pl
from jax.experimental.pallas import tpu as pltpu

F32 = jnp.float32
BF16 = jnp.bfloat16
EPS = 1e-6
NEG_INF = float("-inf")

A_HEAD_DIM = 64
B_HEAD_DIM = 128
B_HEADS = 4
BAND = 128
DILATIONS = ((128, 1), (512, 4), (2048, 16))
GDN_CHUNK = 64
CONV_WIDTH = 4
PREV_ROWS = 8
N_GROUPS = 4
GROUP_EXPERTS = 8
SUB, LANE = 8, 128
BF16_ROWS = 16
ROUTE_LANES = LANE
MOE_BLOCK = 160
MOE_CHUNK = 2048
VMEM_LIMIT = 56 * 1024 * 1024


def _cparams(*sem):
    return pltpu.CompilerParams(dimension_semantics=sem, vmem_limit_bytes=VMEM_LIMIT)


def _bdot(a, b):
    return jnp.dot(a.astype(BF16), b.astype(BF16), preferred_element_type=F32)


def _bdot_nt(a, b):
    return lax.dot_general(a.astype(BF16), b.astype(BF16), (((1,), (1,)), ((), ())),
                           preferred_element_type=F32)


def _split3(x):
    hi = x.astype(BF16)
    r1 = x - hi.astype(F32)
    mid = r1.astype(BF16)
    lo = (r1 - mid.astype(F32)).astype(BF16)
    return hi, mid, lo


def _dot_exact_lhs(a01, x):
    a = a01.astype(BF16)
    hi, mid, lo = _split3(x)
    d = lambda p: jnp.dot(a, p, preferred_element_type=F32)
    return d(hi) + d(mid) + d(lo)


def _group_mean_sq(x, gm):
    sq = x * x
    hi = sq.astype(BF16)
    lo = (sq - hi.astype(F32)).astype(BF16)
    return (jnp.dot(hi, gm, preferred_element_type=F32) + jnp.dot(lo, gm, preferred_element_type=F32))


def _lane_tile_rms(x):
    parts = []
    for c in range(x.shape[1] // LANE):
        xc = x[:, c * LANE:(c + 1) * LANE]
        parts.append(xc * lax.rsqrt(jnp.mean(xc * xc, axis=-1, keepdims=True) + EPS))
    return jnp.concatenate(parts, axis=1)


def _rms(x, g):
    return x * lax.rsqrt(jnp.mean(x * x, axis=-1, keepdims=True) + EPS) * g


def _sigmoid(x):
    return 1.0 / (1.0 + jnp.exp(-x))


def _group_mean_matrix(width, group):
    i = jnp.arange(width)
    return jnp.where((i[:, None] // group) == (i[None, :] // group), 1.0 / group, 0.0).astype(BF16)


def _inproj_kernel(x_ref, g_ref, w_ref, gm_ref, gq_ref, gk_ref,
                   q_ref, k_ref, v_ref, u_ref, z_ref, ab_ref, *, aw, bw):
    nb = _rms(x_ref[...], g_ref[...]).astype(BF16)

    def proj(lo, hi):
        return jnp.dot(nb, w_ref[:, lo:hi], preferred_element_type=F32)

    def put_slabs(ref, val):
        for p in range(aw // LANE):
            ref[p] = val[:, p * LANE:(p + 1) * LANE]

    gm = gm_ref[...]
    qa = proj(0, aw)
    put_slabs(q_ref, qa * lax.rsqrt(_group_mean_sq(qa, gm) + EPS) * gq_ref[...])
    ka = proj(aw, 2 * aw)
    put_slabs(k_ref, ka * lax.rsqrt(_group_mean_sq(ka, gm) + EPS) * gk_ref[...])
    put_slabs(v_ref, proj(2 * aw, 3 * aw))
    u_ref[...] = proj(3 * aw, 3 * aw + 3 * bw)
    z_ref[...] = proj(3 * aw + 3 * bw, 3 * aw + 4 * bw)
    ab_ref[...] = proj(3 * aw + 4 * bw, 3 * aw + 4 * bw + ROUTE_LANES)


def _inproj(x, g_mix, w_in_b, gm64, gq_t, gk_t, *, aw, bw, tm):
    n, d = x.shape
    cols = w_in_b.shape[1]
    row = lambda w: pl.BlockSpec((tm, w), lambda i: (i, 0))
    full = lambda a: pl.BlockSpec(a.shape, lambda i: (0, 0))
    out_w = (3 * bw, bw, ROUTE_LANES)
    slab = pl.BlockSpec((aw // LANE, tm, LANE), lambda i: (0, i, 0))
    return pl.pallas_call(
        functools.partial(_inproj_kernel, aw=aw, bw=bw),
        grid=(n // tm,),
        in_specs=[row(d), full(g_mix), pl.BlockSpec((d, cols), lambda i: (0, 0)),
                  full(gm64), full(gq_t), full(gk_t)],
        out_specs=[slab] * 3 + [row(w) for w in out_w],
        out_shape=[jax.ShapeDtypeStruct((aw // LANE, n, LANE), F32)] * 3
                  + [jax.ShapeDtypeStruct((n, w), F32) for w in out_w],
        compiler_params=_cparams("parallel"),
        name="inproj",
    )(x, g_mix, w_in_b, gm64, gq_t, gk_t)


def _dil_attn_kernel(q_ref, kp_ref, kc_ref, vp_ref, vc_ref, o_ref, od_ref, ld_ref):
    n = pl.program_id(1)
    rows = q_ref.shape[1]
    qi = lax.broadcasted_iota(jnp.int32, (BAND, 2 * BAND), 0)
    kj = lax.broadcasted_iota(jnp.int32, (BAND, 2 * BAND), 1)
    band = (kj >= qi) & (kj <= qi + BAND)
    bias = jnp.where(band, 0.0, NEG_INF)
    bias_first = jnp.where(band & ((kj >= BAND) | (n > 0)), 0.0, NEG_INF)
    lo_half = lax.broadcasted_iota(jnp.int32, (BAND, LANE), 1) < A_HEAD_DIM

    for di, (window, dil) in enumerate(DILATIONS):
        assert window // dil == BAND and rows % (BAND * dil) == 0
        span = BAND * dil
        take = lambda ref, st: ref[0, pl.ds(st, BAND, stride=dil), :]
        for r in range(dil):
            k_prev = take(kp_ref, r + rows - span).astype(BF16)
            v_prev = take(vp_ref, r + rows - span).astype(BF16)
            for j in range(rows // span):
                start = r + j * span
                q = take(q_ref, start)
                k_cur, v_cur = take(kc_ref, start).astype(BF16), take(vc_ref, start).astype(BF16)
                kcat = jnp.concatenate([k_prev, k_cur], axis=0)
                vcat = jnp.concatenate([v_prev, v_cur], axis=0)
                k_prev, v_prev = k_cur, v_cur
                res = []
                for half in range(2):
                    keep = lo_half if half == 0 else jnp.logical_not(lo_half)
                    s = _bdot_nt(jnp.where(keep, q, 0.0), kcat) + (bias if j else bias_first)
                    m = jnp.max(s, axis=-1, keepdims=True)
                    e = jnp.exp(s - m)
                    den = jnp.sum(e, axis=-1, keepdims=True)
                    o = jnp.dot(e.astype(BF16), vcat, preferred_element_type=F32) / den
                    res.append((o, m + jnp.log(den)))
                od_ref[di, pl.ds(start, BAND, stride=dil), :] = jnp.where(lo_half, res[0][0], res[1][0])
                ld_ref[di, pl.ds(start, BAND, stride=dil), :] = jnp.where(lo_half, res[0][1], res[1][1])

    step = 256
    for r0 in range(0, rows, step):
        ls = [ld_ref[di, r0:r0 + step, :] for di in range(len(DILATIONS))]
        mm = functools.reduce(jnp.maximum, ls)
        es = [jnp.exp(l - mm) for l in ls]
        num = functools.reduce(lambda a, b: a + b, [e * od_ref[di, r0:r0 + step, :] for di, e in enumerate(es)])
        o_ref[0, r0:r0 + step, :] = num / functools.reduce(lambda a, b: a + b, es)


def _dil_attn(q, k, v, *, rows):
    pairs, s, lanes = q.shape
    assert s % rows == 0 and lanes == LANE
    cur = pl.BlockSpec((1, rows, LANE), lambda p, n: (p, n, 0))
    prev = pl.BlockSpec((1, rows, LANE), lambda p, n: (p, jnp.maximum(n - 1, 0), 0))
    return pl.pallas_call(
        _dil_attn_kernel,
        grid=(pairs, s // rows),
        in_specs=[cur, prev, cur, prev, cur],
        out_specs=cur,
        out_shape=jax.ShapeDtypeStruct(q.shape, F32),
        scratch_shapes=[pltpu.VMEM((len(DILATIONS), rows, LANE), F32)] * 2,
        compiler_params=_cparams("parallel", "arbitrary"),
        name="dil_attn",
    )(q, k, k, v, v)


def _step_attn_kernel(q_ref, kn_ref, vn_ref, kc_ref, vc_ref, ok_ref, ov_ref, oa_ref,
                      kb_ref, vb_ref, *, w_buf, t_new):
    aw = q_ref.shape[2]
    heads = aw // A_HEAD_DIM
    lanes = LANE
    kt, vt = kc_ref[0], vc_ref[0]
    lead = jnp.zeros((aw, lanes - t_new), F32)
    kn = jnp.concatenate([lead, kn_ref[0]], axis=1)
    vn = jnp.concatenate([lead, vn_ref[0]], axis=1)
    is_new = lax.broadcasted_iota(jnp.int32, (aw, lanes), 1) >= lanes - t_new
    for src, new, dst in ((kt, kn, ok_ref), (vt, vn, ov_ref)):
        rolled = pltpu.roll(src, w_buf - t_new, axis=1)
        dst[0, :, 0:w_buf - lanes] = rolled[:, 0:w_buf - lanes]
        dst[0, :, w_buf - lanes:w_buf] = jnp.where(is_new, new, rolled[:, w_buf - lanes:w_buf])
    kb_ref[...] = kt.astype(BF16)
    vb_ref[...] = vt.astype(BF16)

    rows = heads * t_new
    ri = lax.broadcasted_iota(jnp.int32, (rows, aw), 0)
    li = lax.broadcasted_iota(jnp.int32, (rows, aw), 1)
    q_rep = jnp.concatenate([q_ref[0]] * heads, axis=0)
    q_blk = jnp.where((ri // t_new) == (li // A_HEAD_DIM), q_rep, 0.0).astype(BF16)
    ncol = w_buf + lanes
    s_all = jnp.concatenate([jnp.dot(q_blk, kb_ref[...], preferred_element_type=F32),
                             jnp.dot(q_blk, kn.astype(BF16), preferred_element_type=F32)], axis=1)

    def branch(window, dil, col0):
        s = s_all[:, col0:]
        t = lax.broadcasted_iota(jnp.int32, s.shape, 0) % t_new
        cidx = lax.broadcasted_iota(jnp.int32, s.shape, 1) + col0
        r = jnp.where(cidx < w_buf, cidx, cidx - (lanes - t_new))
        diff = w_buf + t - r
        valid = (diff >= 0) & (diff <= window) & ((diff & (dil - 1)) == 0)
        valid = valid & ((cidx < w_buf) | (cidx >= ncol - t_new))
        s = jnp.where(valid, s, NEG_INF)
        m = jnp.max(s, axis=-1, keepdims=True)
        e = jnp.exp(s - m)
        den = jnp.sum(e, axis=-1, keepdims=True)
        return e, den, m + jnp.log(den)

    parts = []
    for window, dil in DILATIONS:
        assert dil & (dil - 1) == 0
        col0 = max(0, (w_buf - window) // 128 * 128)
        parts.append((col0,) + branch(window, dil, col0))
    mm = functools.reduce(jnp.maximum, [p[3] for p in parts])
    wexp = [jnp.exp(p[3] - mm) for p in parts]
    wsum = functools.reduce(lambda a, b: a + b, wexp)
    p_tot = jnp.zeros((rows, ncol), F32)
    for (col0, e, den, _), we in zip(parts, wexp):
        pe = e * (we / (wsum * den))
        if col0:
            pe = jnp.concatenate([jnp.zeros((rows, col0), F32), pe], axis=1)
        p_tot = p_tot + pe
    o = _bdot_nt(p_tot[:, :w_buf], vb_ref[...]) + _bdot_nt(p_tot[:, w_buf:], vn)
    lh = lax.broadcasted_iota(jnp.int32, (t_new, aw), 1) // A_HEAD_DIM
    acc = jnp.zeros((t_new, aw), F32)
    for h in range(heads):
        acc = acc + jnp.where(lh == h, o[h * t_new:(h + 1) * t_new, :], 0.0)
    oa_ref[0] = acc


def _step_attn(q, kn_t, vn_t, cache_kt, cache_vt):
    nseq, aw, w_buf = cache_kt.shape
    t_new = q.shape[1]
    assert w_buf % LANE == 0 and t_new % SUB == 0 and t_new <= LANE and kn_t.shape[2] == t_new
    qs = pl.BlockSpec((1, t_new, aw), lambda i: (i, 0, 0))
    new = pl.BlockSpec((1, aw, t_new), lambda i: (i, 0, 0))
    win = pl.BlockSpec((1, aw, w_buf), lambda i: (i, 0, 0))
    return pl.pallas_call(
        functools.partial(_step_attn_kernel, w_buf=w_buf, t_new=t_new),
        grid=(nseq,),
        in_specs=[qs, new, new, win, win],
        out_specs=[win, win, qs],
        out_shape=[jax.ShapeDtypeStruct(cache_kt.shape, F32), jax.ShapeDtypeStruct(cache_vt.shape, F32),
                   jax.ShapeDtypeStruct(q.shape, F32)],
        scratch_shapes=[pltpu.VMEM((aw, w_buf), BF16), pltpu.VMEM((aw, w_buf), BF16)],
        compiler_params=_cparams("parallel"),
        name="step_attn",
    )(q, kn_t, vn_t, cache_kt, cache_vt)


def _gdn_prep_kernel(u_ref, prev_ref, ab_ref, cw_ref, alog_ref, dtb_ref,
                     w_ref, uv_ref, qg_ref, kt_ref, qk_ref, last_ref, full_ref, act_ref,
                     *, seg, conv_seg, chunks):
    c = GDN_CHUNK
    dk = B_HEAD_DIM
    bw = B_HEADS * dk
    ii = lax.broadcasted_iota(jnp.int32, (c, c), 0)
    jj = lax.broadcasted_iota(jnp.int32, (c, c), 1)
    same = (ii // seg) == (jj // seg)
    incl = same & (ii >= jj)
    strict = same & (ii > jj)
    eye = ii == jj
    ones_c = jnp.ones((c, c), BF16)
    seg_cols = jnp.where((lax.broadcasted_iota(jnp.int32, (c, dk), 0) % seg)
                         == lax.broadcasted_iota(jnp.int32, (c, dk), 1), 1.0, 0.0)
    cw = cw_ref[...]
    rows = chunks * c

    for sgi in range(rows // conv_seg):
        base = sgi * (conv_seg + PREV_ROWS)
        full_ref[base:base + PREV_ROWS, :] = prev_ref[sgi]
        full_ref[base + PREV_ROWS:base + PREV_ROWS + conv_seg, :] = u_ref[sgi * conv_seg:(sgi + 1) * conv_seg, :]
        y = jnp.zeros((conv_seg, 3 * bw), F32)
        for j in range(CONV_WIDTH):
            off = base + PREV_ROWS - (CONV_WIDTH - 1) + j
            y = y + full_ref[off:off + conv_seg, :] * cw[j:j + 1, :]
        act_ref[sgi * conv_seg:(sgi + 1) * conv_seg, :] = y * _sigmoid(y)

    ri = lax.broadcasted_iota(jnp.int32, (rows, rows), 0)
    rj = lax.broadcasted_iota(jnp.int32, (rows, rows), 1)
    rsame = (ri // seg) == (rj // seg)
    ab = ab_ref[...]
    sp = ab + dtb_ref[...]
    sp = jnp.maximum(sp, 0.0) + jnp.log(1.0 + jnp.exp(-jnp.abs(sp)))
    g_all = -jnp.exp(alog_ref[...]) * sp
    gcum_all = _dot_exact_lhs(jnp.where(rsame & (ri >= rj), 1.0, 0.0), g_all)
    gtot_all = _dot_exact_lhs(jnp.where(rsame, 1.0, 0.0), g_all)
    beta_all = _sigmoid(ab)
    eye_f = jnp.where(eye, 1.0, 0.0)

    grow = []
    for ci in range(chunks):
        rs = slice(ci * c, (ci + 1) * c)
        diag = jnp.concatenate([jnp.where(eye, gcum_all[rs, h:h + 1], 0.0) for h in range(B_HEADS)], axis=1)
        grow.append(_dot_exact_lhs(ones_c, diag))

    prob = [(ci, h) for ci in range(chunks) for h in range(B_HEADS)]
    col = lambda arr, ci, lane: arr[ci * c:(ci + 1) * c, lane:lane + 1]
    gc = [col(gcum_all, ci, h) for ci, h in prob]
    gt = [col(gtot_all, ci, h) for ci, h in prob]
    beta = [col(beta_all, ci, B_HEADS + h) for ci, h in prob]
    q, k, v = [], [], []
    for ci, h in prob:
        rs = slice(ci * c, (ci + 1) * c)
        qh = act_ref[rs, h * dk:(h + 1) * dk]
        kh = act_ref[rs, bw + h * dk:bw + (h + 1) * dk]
        q.append(qh * lax.rsqrt(jnp.sum(qh * qh, axis=-1, keepdims=True) + EPS) * (dk ** -0.5))
        k.append(kh * lax.rsqrt(jnp.sum(kh * kh, axis=-1, keepdims=True) + EPS))
        v.append(act_ref[rs, 2 * bw + h * dk:2 * bw + (h + 1) * dk])
    e_incl = [jnp.exp(jnp.where(incl, gc[i] - grow[ci][:, h * c:(h + 1) * c], NEG_INF))
              for i, (ci, h) in enumerate(prob)]
    qkk = [_bdot_nt(jnp.concatenate([q[i], k[i]], axis=0), k[i]) for i in range(len(prob))]
    a = [beta[i] * jnp.where(strict, e_incl[i], 0.0) * qkk[i][c:, :] for i in range(len(prob))]
    blk = 8
    inblk = (ii // blk) == (jj // blk)
    d0 = [jnp.where(inblk, t, 0.0) for t in a]
    d2 = [_bdot(t, t) for t in d0]
    d4 = [_bdot(t, t) for t in d2]
    x = [(eye_f - t) + _bdot(eye_f - t, t2) for t, t2 in zip(d0, d2)]
    x = [t + _bdot(t, t4) for t, t4 in zip(x, d4)]
    while blk < seg:
        off_blk = ((ii // (2 * blk)) == (jj // (2 * blk))) & ((ii // blk) != (jj // blk))
        xe = [_bdot(t, jnp.where(off_blk, ta, 0.0)) for t, ta in zip(x, a)]
        x = [t - _bdot(te, t) for t, te in zip(x, xe)]
        blk *= 2
    gamma = [jnp.exp(t) for t in gc]
    wuv = [_bdot(x[i], jnp.concatenate([(beta[i] * gamma[i]) * k[i], beta[i] * v[i]], axis=1))
           for i in range(len(prob))]
    for i, (ci, h) in enumerate(prob):
        rs = slice(ci * c, (ci + 1) * c)
        hs = slice(h * dk, (h + 1) * dk)
        w_ref[rs, hs] = wuv[i][:, :dk]
        uv_ref[rs, hs] = wuv[i][:, dk:]
        qg_ref[rs, hs] = gamma[i] * q[i]
        kt_ref[rs, hs] = k[i] * jnp.exp(gt[i] - gc[i])
        qk = qkk[i][:c, :] * e_incl[i]
        qk_ref[rs, hs] = (jnp.concatenate([qk, jnp.zeros((c, dk - c), F32)], axis=1) if seg == c
                          else _bdot(qk, seg_cols))
        last_ref[rs, hs] = jnp.broadcast_to(jnp.exp(gt[i]), (c, dk))


def _gdn_prep(u_pre, prev, ab, conv_w, alog_row, dtb_row, *, seg, conv_seg, chunks):
    n, cw = u_pre.shape
    bw = cw // 3
    c = GDN_CHUNK
    rows = chunks * c
    assert n % rows == 0 and c % seg == 0 and seg % 8 == 0 and rows % conv_seg == 0 and conv_seg % seg == 0
    ncs = rows // conv_seg
    row = lambda w: pl.BlockSpec((rows, w), lambda i: (i, 0))
    full = lambda a: pl.BlockSpec(a.shape, lambda i: (0, 0))
    return pl.pallas_call(
        functools.partial(_gdn_prep_kernel, seg=seg, conv_seg=conv_seg, chunks=chunks),
        grid=(n // rows,),
        in_specs=[row(cw), pl.BlockSpec((ncs, PREV_ROWS, cw), lambda i: (i, 0, 0)),
                  row(ROUTE_LANES), full(conv_w), full(alog_row), full(dtb_row)],
        out_specs=[row(bw)] * 6,
        out_shape=[jax.ShapeDtypeStruct((n, bw), F32)] * 6,
        scratch_shapes=[pltpu.VMEM((ncs * (conv_seg + PREV_ROWS), cw), F32), pltpu.VMEM((rows, cw), F32)],
        compiler_params=_cparams("parallel"),
        name=f"gdn_prep_{seg}",
    )(u_pre, prev, ab, conv_w, alog_row, dtb_row)


def _gdn_scan_kernel(w_ref, uv_ref, qg_ref, kt_ref, qk_ref, last_ref, s0_ref, o_ref, s_ref, *, seg):
    dk = B_HEAD_DIM

    @pl.when(pl.program_id(1) == 0)
    def _():
        s_ref[...] = s0_ref[...]

    rows = max(seg, BF16_ROWS)
    pad = rows - seg
    padr = lambda t: jnp.concatenate([t, jnp.zeros((pad, t.shape[1]), F32)], axis=0) if pad else t
    ii = lax.broadcasted_iota(jnp.int32, (dk, dk), 0)
    jj = lax.broadcasted_iota(jnp.int32, (dk, dk), 1)
    eye = jnp.where(ii == jj, 1.0, 0.0).astype(BF16)
    heads = range(B_HEADS)
    hs = [slice(h * dk, (h + 1) * dk) for h in heads]
    s = [s_ref[0, h] for h in heads]
    sb = [t.astype(BF16) for t in s]
    ws = [_bdot(padr(w_ref[:, hs[h]]), sb[h]) for h in heads]
    ub = [(padr(uv_ref[:, hs[h]]) - ws[h]).astype(BF16) for h in heads]
    ktt = [_bdot_nt(eye, padr(kt_ref[:, hs[h]])) for h in heads]
    for h in heads:
        s_ref[0, h] = last_ref[0:1, hs[h]] * s[h] + _bdot(ktt[h], ub[h])
    for h in heads:
        o = _bdot(padr(qg_ref[:, hs[h]]), sb[h]) + _bdot(padr(qk_ref[:, hs[h]])[:, :rows], ub[h])
        o_ref[:, hs[h]] = o[:seg, :]


def _gdn_scan(w, uv, qg, kt, qk, last, s0, *, seg):
    n, bw = w.shape
    nseq = s0.shape[0]
    per_seq = n // (nseq * seg)
    row = pl.BlockSpec((seg, bw), lambda s, i: (s * per_seq + i, 0))
    st = pl.BlockSpec((1,) + s0.shape[1:], lambda s, i: (s, 0, 0, 0))
    return pl.pallas_call(
        functools.partial(_gdn_scan_kernel, seg=seg),
        grid=(nseq, per_seq),
        in_specs=[row] * 6 + [st],
        out_specs=[row, st],
        out_shape=[jax.ShapeDtypeStruct((n, bw), F32), jax.ShapeDtypeStruct(s0.shape, F32)],
        compiler_params=_cparams("parallel", "arbitrary"),
        name=f"gdn_scan_{seg}",
    )(w, uv, qg, kt, qk, last, s0)


def _mem_kv_kernel(mem_ref, g_ref, wk_ref, wv_ref, gk_ref, k_ref, v_ref):
    mn = _rms(mem_ref[...], g_ref[...]).astype(BF16)
    k = jnp.dot(mn, wk_ref[...], preferred_element_type=F32)
    k_ref[...] = _lane_tile_rms(k) * gk_ref[...]
    v_ref[...] = jnp.dot(mn, wv_ref[...], preferred_element_type=F32)


def _mem_kv(mem, g_mem, w_xk_b, w_xv_b, gk_t):
    m = mem.shape[0]
    xw = w_xk_b.shape[1]
    return pl.pallas_call(
        _mem_kv_kernel,
        out_shape=[jax.ShapeDtypeStruct((m, xw), F32)] * 2,
        compiler_params=pltpu.CompilerParams(vmem_limit_bytes=VMEM_LIMIT),
        name="mem_kv",
    )(mem, g_mem, w_xk_b, w_xv_b, gk_t)


def _route(logits):
    lane = lax.broadcasted_iota(jnp.int32, logits.shape, 1)
    lane_f = lane.astype(F32)
    big = float(ROUTE_LANES)
    lg = jnp.where(lane < N_GROUPS, logits, NEG_INF)
    mg = jnp.max(lg, axis=-1, keepdims=True)
    zg = jnp.sum(jnp.exp(lg - mg), axis=-1, keepdims=True)
    pg_top = 1.0 / zg
    gidx = jnp.min(jnp.where(lg == mg, lane_f, big), axis=-1, keepdims=True)
    e_lo = N_GROUPS + GROUP_EXPERTS * gidx
    emask = (lane_f >= e_lo) & (lane_f < e_lo + GROUP_EXPERTS)
    le = jnp.where(emask, logits, NEG_INF)
    me = jnp.max(le, axis=-1, keepdims=True)
    ee = jnp.exp(le - me)
    pe = ee / jnp.sum(ee, axis=-1, keepdims=True)
    pe = jnp.where(emask, pe, -1.0)
    p1 = jnp.max(pe, axis=-1, keepdims=True)
    i1 = jnp.min(jnp.where(pe == p1, lane_f, big), axis=-1, keepdims=True)
    pe2 = jnp.where(lane_f == i1, -1.0, pe)
    p2 = jnp.max(pe2, axis=-1, keepdims=True)
    i2 = jnp.min(jnp.where(pe2 == p2, lane_f, big), axis=-1, keepdims=True)
    den = p1 + p2
    w1 = pg_top * p1 / den
    w2 = pg_top * p2 / den
    out = jnp.where(lane == 0, i1 - N_GROUPS, 0.0)
    out = jnp.where(lane == 1, i2 - N_GROUPS, out)
    out = jnp.where(lane == 2, w1, out)
    return jnp.where(lane == 3, w2, out)


def _post_common(oa, og_ref, z_ref, h_ref, wout_ref, ggdn_ref, gx_ref, wxq_ref, gxq_ref):
    og = og_ref[...]
    z = z_ref[...]
    ob = _lane_tile_rms(og) * ggdn_ref[...] * (z * _sigmoid(z))
    cat = jnp.concatenate([oa, ob], axis=-1).astype(BF16)
    h1 = h_ref[...] + jnp.dot(cat, wout_ref[...], preferred_element_type=F32)
    hx = _rms(h1, gx_ref[...]).astype(BF16)
    q = jnp.dot(hx, wxq_ref[...], preferred_element_type=F32)
    qn = _lane_tile_rms(q) * gxq_ref[...]
    return h1, qn


def _mem_attend_rows(qn, mk, mv):
    outs = []
    for h in range(qn.shape[1] // B_HEAD_DIM):
        hs = slice(h * B_HEAD_DIM, (h + 1) * B_HEAD_DIM)
        s = _bdot_nt(qn[:, hs], mk[:, hs])
        m = jnp.max(s, axis=-1, keepdims=True)
        e = jnp.exp(s - m)
        p = e / jnp.sum(e, axis=-1, keepdims=True)
        outs.append(_bdot(p, mv[:, hs]))
    return jnp.concatenate(outs, axis=-1)


def _post_tail(h1, ox, wxo_ref, gffn_ref, wr_ref, br_ref, h2_ref, hn_ref, route_ref):
    h2 = h1 + jnp.dot(ox.astype(BF16), wxo_ref[...], preferred_element_type=F32)
    h2_ref[...] = h2
    hn = _rms(h2, gffn_ref[...])
    tm = hn.shape[0]
    for j in range(hn.shape[1] // LANE):
        hn_ref[pl.ds(j, tm, stride=SUB), :] = hn[:, j * LANE:(j + 1) * LANE]
    hn_hi = hn.astype(BF16)
    hn_mid = (hn - hn_hi.astype(F32)).astype(BF16)
    d = lambda a, b: jnp.dot(a, b, preferred_element_type=F32)
    logits = d(hn_hi, wr_ref[0]) + d(hn_hi, wr_ref[1]) + d(hn_mid, wr_ref[0]) + br_ref[...]
    route_ref[...] = _route(logits)


def _post_prompt_kernel(oa_ref, og_ref, z_ref, h_ref,
                        wout_ref, ggdn_ref, gx_ref, wxq_ref, gxq_ref, mk_ref, mv_ref,
                        wxo_ref, gffn_ref, wr_ref, br_ref, h2_ref, hn_ref, route_ref):
    oa = jnp.concatenate([oa_ref[p] for p in range(oa_ref.shape[0])], axis=-1)
    h1, qn = _post_common(oa, og_ref, z_ref, h_ref, wout_ref, ggdn_ref, gx_ref, wxq_ref, gxq_ref)
    ox = _mem_attend_rows(qn, mk_ref[...], mv_ref[...])
    _post_tail(h1, ox, wxo_ref, gffn_ref, wr_ref, br_ref, h2_ref, hn_ref, route_ref)


def _post_sample_kernel(oa_ref, og_ref, z_ref, h_ref,
                        wout_ref, ggdn_ref, gx_ref, wxq_ref, gxq_ref, mk_ref, mv_ref,
                        wxo_ref, gffn_ref, wr_ref, br_ref, h2_ref, hn_ref, route_ref, *, t_new):
    h1, qn = _post_common(oa_ref[...], og_ref, z_ref, h_ref, wout_ref, ggdn_ref, gx_ref, wxq_ref, gxq_ref)
    dh = B_HEAD_DIM
    heads = qn.shape[1] // dh
    rows = heads * t_new
    ncol = mk_ref.shape[1]
    own = ((lax.broadcasted_iota(jnp.int32, (rows, ncol), 0) // t_new)
           == (lax.broadcasted_iota(jnp.int32, (rows, ncol), 1) % heads))
    outs = []
    for sq in range(mk_ref.shape[0]):
        qs = qn[sq * t_new:(sq + 1) * t_new, :]
        q_rows = jnp.concatenate([qs[:, h * dh:(h + 1) * dh] for h in range(heads)], axis=0)
        s = jnp.where(own, _bdot_nt(q_rows, mk_ref[sq]), NEG_INF)
        m = jnp.max(s, axis=-1, keepdims=True)
        e = jnp.exp(s - m)
        o = _bdot(e / jnp.sum(e, axis=-1, keepdims=True), mv_ref[sq])
        outs.append(jnp.concatenate([o[h * t_new:(h + 1) * t_new, :] for h in range(heads)], axis=1))
    ox = jnp.concatenate(outs, axis=0)
    _post_tail(h1, ox, wxo_ref, gffn_ref, wr_ref, br_ref, h2_ref, hn_ref, route_ref)


def _post_weights_specs(weights):
    return [pl.BlockSpec(a.shape, lambda i, nd=a.ndim: (0,) * nd) for a in weights]


def _post_outs(n, d, tm):
    assert d == SUB * LANE
    row = lambda w: pl.BlockSpec((tm, w), lambda i: (i, 0))
    specs = [row(d), pl.BlockSpec((tm * SUB, LANE), lambda i: (i, 0)), row(ROUTE_LANES)]
    shapes = [jax.ShapeDtypeStruct((n, d), F32), jax.ShapeDtypeStruct((n * SUB, LANE), F32),
              jax.ShapeDtypeStruct((n, ROUTE_LANES), F32)]
    return specs, shapes


def _post_prompt(oa_slabs, og, z, h, pw, mk, mv, *, tm):
    n, d = h.shape
    aw = og.shape[1]
    row = lambda w: pl.BlockSpec((tm, w), lambda i: (i, 0))
    full = lambda a: pl.BlockSpec(a.shape, lambda i: (0, 0))
    slab = pl.BlockSpec((oa_slabs.shape[0], tm, LANE), lambda i: (0, i, 0))
    w1 = [pw["w_out"], pw["g_gdn"], pw["g_xattn"], pw["w_xq"], pw["g_xq"]]
    w2 = [pw["w_xo"], pw["g_ffn"], pw["w_r"], pw["b_r"]]
    specs, shapes = _post_outs(n, d, tm)
    return pl.pallas_call(
        _post_prompt_kernel,
        grid=(n // tm,),
        in_specs=[slab, row(aw), row(aw), row(d)] + _post_weights_specs(w1) + [full(mk), full(mv)]
                 + _post_weights_specs(w2),
        out_specs=specs, out_shape=shapes,
        compiler_params=_cparams("parallel"),
        name="post_prompt",
    )(oa_slabs, og, z, h, *w1, mk, mv, *w2)


def _post_sample(oa, og, z, h, pw, mk, mv, *, t_new, seqs):
    n, d = h.shape
    aw = og.shape[1]
    tm = t_new * seqs
    row = lambda w: pl.BlockSpec((tm, w), lambda i: (i, 0))
    mem = pl.BlockSpec((seqs,) + mk.shape[1:], lambda i: (i, 0, 0))
    w1 = [pw["w_out"], pw["g_gdn"], pw["g_xattn"], pw["w_xq"], pw["g_xq"]]
    w2 = [pw["w_xo"], pw["g_ffn"], pw["w_r"], pw["b_r"]]
    specs, shapes = _post_outs(n, d, tm)
    return pl.pallas_call(
        functools.partial(_post_sample_kernel, t_new=t_new),
        grid=(n // tm,),
        in_specs=[row(aw)] * 3 + [row(d)] + _post_weights_specs(w1) + [mem, mem]
                 + _post_weights_specs(w2),
        out_specs=specs, out_shape=shapes,
        compiler_params=_cparams("parallel"),
        name="post_sample",
    )(oa, og, z, h, *w1, mk, mv, *w2)


def _moe_kernel(blk_e_ref, blk_start_ref, blk_cnt_ref, tok_ref, tw_ref, x_ref, res_ref,
                wg_ref, wu_ref, wd_ref, out_ref, acc_ref, xs_ref, ys_ref, *, nb):
    c = pl.program_id(0)
    b = pl.program_id(1)
    chunk, d = res_ref.shape
    nl = d // LANE
    cnt = blk_cnt_ref[c * nb + b]

    @pl.when(cnt > 0)
    def _():
        base = blk_start_ref[c * nb + b]
        for r in range(MOE_BLOCK):
            t = tok_ref[0, 0, base + r] >> 1
            xs_ref[r * SUB:(r + 1) * SUB, :] = x_ref[pl.ds(pl.multiple_of(t * SUB, SUB), SUB), :]
        xb = jnp.concatenate([xs_ref[pl.ds(j, MOE_BLOCK, stride=SUB), :] for j in range(nl)],
                             axis=1).astype(BF16)
        hg = jnp.dot(xb, wg_ref[0], preferred_element_type=F32)
        hu = jnp.dot(xb, wu_ref[0], preferred_element_type=F32)
        act = (hg * _sigmoid(hg) * hu).astype(BF16)
        y = jnp.dot(act, wd_ref[0], preferred_element_type=F32)
        for j in range(nl):
            ys_ref[pl.ds(j, MOE_BLOCK, stride=SUB), :] = y[:, j * LANE:(j + 1) * LANE]
        for r in range(MOE_BLOCK):
            e = tok_ref[0, 0, base + r]
            slot = jnp.where(r < cnt, (e & 1) * chunk + (e >> 1), 2 * chunk)
            off = pl.multiple_of(slot * SUB, SUB)
            acc_ref[pl.ds(off, SUB), :] = tw_ref[0, 0, base + r] * ys_ref[r * SUB:(r + 1) * SUB, :]

    @pl.when(b == nb - 1)
    def _():
        step = 256
        for r0 in range(0, chunk, step):
            for j in range(nl):
                out_ref[r0:r0 + step, j * LANE:(j + 1) * LANE] = (
                    res_ref[r0:r0 + step, j * LANE:(j + 1) * LANE]
                    + acc_ref[pl.ds(r0 * SUB + j, step, stride=SUB), :]
                    + acc_ref[pl.ds((chunk + r0) * SUB + j, step, stride=SUB), :])


def _dispatch(route, chunk, n_experts):
    n = route.shape[0]
    nch = n // chunk
    rows = 2 * chunk
    nb = rows // MOE_BLOCK + n_experts
    e = route[:, :2].astype(jnp.int32).reshape(nch, rows)
    w = route[:, 2:4].reshape(nch, rows)
    tok = jnp.argsort(e, axis=1, stable=True).astype(jnp.int32)
    tw = jnp.take_along_axis(w, tok, axis=1)
    ex = jnp.arange(n_experts, dtype=jnp.int32)
    counts = jnp.sum((e[:, :, None] == ex).astype(jnp.int32), axis=1)
    start = jnp.cumsum(counts, axis=1) - counts
    nblk_e = (counts + MOE_BLOCK - 1) // MOE_BLOCK
    bend = jnp.cumsum(nblk_e, axis=1)
    bstart = bend - nblk_e
    total = bend[:, -1:]
    b = jnp.arange(nb, dtype=jnp.int32)[None, :]
    bb = jnp.minimum(b, total - 1)
    eb = jnp.sum((bend[:, None, :] <= bb[:, :, None]).astype(jnp.int32), axis=2)
    sel = eb[:, :, None] == ex
    pick = lambda tbl: jnp.sum(jnp.where(sel, tbl[:, None, :], 0), axis=2)
    j = bb - pick(bstart)
    blk_start = pick(start) + j * MOE_BLOCK
    blk_cnt = jnp.where(b < total, jnp.clip(pick(counts) - j * MOE_BLOCK, 0, MOE_BLOCK), 0)
    flat = lambda t_: t_.astype(jnp.int32).reshape(-1)
    pad = lambda t_: jnp.pad(t_, ((0, 0), (0, MOE_BLOCK))).reshape(nch, 1, rows + MOE_BLOCK)
    return flat(eb), flat(blk_start), flat(blk_cnt), pad(tok), pad(tw), nb


def _moe(hn_t, h2, route, wg_b, wu_b, wd_b, *, chunk):
    n, d = h2.shape
    n_experts, _, ff = wg_b.shape
    chunk = min(chunk, n)
    assert n % chunk == 0 and d == SUB * LANE
    nch = n // chunk
    blk_e, blk_start, blk_cnt, tok, tw, nb = _dispatch(route, chunk, n_experts)
    tab = pl.BlockSpec((1, 1, tok.shape[2]), lambda c, b, *_: (c, 0, 0), memory_space=pltpu.SMEM)
    once = lambda shp: pl.BlockSpec(shp, lambda c, b, *_: (c, 0), pipeline_mode=pl.Buffered(1))
    wspec = lambda shp: pl.BlockSpec((1,) + shp, lambda c, b, be, *_: (be[c * nb + b], 0, 0))
    grid_spec = pltpu.PrefetchScalarGridSpec(
        num_scalar_prefetch=3,
        grid=(nch, nb),
        in_specs=[tab, tab, once((chunk * SUB, LANE)), once((chunk, d)),
                  wspec((d, ff)), wspec((d, ff)), wspec((ff, d))],
        out_specs=once((chunk, d)),
        scratch_shapes=[pltpu.VMEM(((2 * chunk + 1) * SUB, LANE), F32), pltpu.VMEM((MOE_BLOCK * SUB, LANE), F32),
                        pltpu.VMEM((MOE_BLOCK * SUB, LANE), F32)],
    )
    return pl.pallas_call(
        functools.partial(_moe_kernel, nb=nb),
        grid_spec=grid_spec,
        out_shape=jax.ShapeDtypeStruct((n, d), F32),
        compiler_params=_cparams("parallel", "arbitrary"),
        name="moe",
    )(blk_e, blk_start, blk_cnt, tok, tw, hn_t, h2, wg_b, wu_b, wd_b)


def _tile_row(g, reps, scale=1.0):
    return (jnp.tile(g.astype(F32), reps) * scale)[None, :]


def _layer_weights(p):
    d, in_cols = p["w_in"].shape
    aw = d // 2
    bw = d - aw
    pad = 3 * aw + 4 * bw + ROUTE_LANES - in_cols
    n_experts = p["w_re"].shape[1]
    w_r = jnp.concatenate([p["w_rg"], p["w_re"],
                           jnp.zeros((d, ROUTE_LANES - N_GROUPS - n_experts), F32)], axis=1)
    b_r = jnp.concatenate([p["b_rg"], p["b_re"], jnp.zeros((ROUTE_LANES - N_GROUPS - n_experts,), F32)])
    lane_pad = lambda v: jnp.concatenate([v.astype(F32), jnp.zeros((ROUTE_LANES - v.shape[0],), F32)])[None, :]
    return dict(
        aw=aw, bw=bw,
        g_mix=p["g_mix"][None, :],
        w_in=jnp.pad(p["w_in"], ((0, 0), (0, pad))).astype(BF16),
        gm64=_group_mean_matrix(aw, A_HEAD_DIM),
        gq=_tile_row(p["g_qa"], aw // A_HEAD_DIM, A_HEAD_DIM ** -0.5),
        gk=_tile_row(p["g_ka"], aw // A_HEAD_DIM),
        conv_w=p["conv_w"],
        alog=lane_pad(p["a_log"]),
        dtb=lane_pad(jnp.concatenate([p["dt_bias"], jnp.zeros_like(p["dt_bias"])])),
        w_out=p["w_out"].astype(BF16),
        g_gdn=_tile_row(p["g_gdn"], bw // B_HEAD_DIM),
        g_xattn=p["g_xattn"][None, :],
        w_xq=p["w_xq"].astype(BF16),
        g_xq=_tile_row(p["g_xq"], p["w_xq"].shape[1] // B_HEAD_DIM, B_HEAD_DIM ** -0.5),
        w_xo=p["w_xo"].astype(BF16),
        g_ffn=p["g_ffn"][None, :],
        w_r=jnp.stack([w_r.astype(BF16), (w_r - w_r.astype(BF16).astype(F32)).astype(BF16)]), b_r=b_r[None, :],
        w_gate=p["w_gate"].astype(BF16), w_up=p["w_up"].astype(BF16), w_down=p["w_down"].astype(BF16),
    )


def _gdn(u_pre, prev, ab, s0, pw, *, seg, conv_seg, chunks):
    w, uv, qg, kt, qk, last = _gdn_prep(u_pre, prev, ab, pw["conv_w"], pw["alog"], pw["dtb"],
                                        seg=seg, conv_seg=conv_seg, chunks=chunks)
    return _gdn_scan(w, uv, qg, kt, qk, last, s0, seg=seg)


def _prompt_layer(h, mem, pw, praw):
    nb_, s, d = h.shape
    assert nb_ == 1
    x = h.reshape(s, d)
    aw, bw = pw["aw"], pw["bw"]
    q, k, v, u_pre, z, ab = _inproj(x, pw["g_mix"], pw["w_in"], pw["gm64"], pw["gq"], pw["gk"],
                                    aw=aw, bw=bw, tm=512)
    oa = _dil_attn(q, k, v, rows=DILATIONS[-1][0])
    gdn_chunks = 4
    blk_rows = gdn_chunks * GDN_CHUNK
    tails = u_pre.reshape(s // blk_rows, blk_rows, 3 * bw)[:, blk_rows - PREV_ROWS:, :]
    prev = jnp.concatenate([jnp.zeros((1, PREV_ROWS, 3 * bw), F32), tails[:-1]], axis=0)
    s0 = jnp.zeros((1, B_HEADS, B_HEAD_DIM, B_HEAD_DIM), F32)
    og, s_fin = _gdn(u_pre, prev, ab, s0, pw, seg=GDN_CHUNK, conv_seg=blk_rows, chunks=gdn_chunks)
    mk, mv = _mem_kv(mem.reshape(mem.shape[1], d), praw["g_mem"][None, :], praw["w_xk"].astype(BF16),
                     praw["w_xv"].astype(BF16), _tile_row(praw["g_xk"], bw // B_HEAD_DIM))
    h2, hn, route = _post_prompt(oa, og, z, x, pw, mk, mv, tm=256)
    y = _moe(hn, h2, route, pw["w_gate"], pw["w_up"], pw["w_down"], chunk=MOE_CHUNK)
    keep = min(DILATIONS[-1][0], s)
    heads = aw // A_HEAD_DIM
    tail = lambda t_: jnp.transpose(t_[:, s - keep:, :], (1, 0, 2)).reshape(1, keep, heads, A_HEAD_DIM)
    new_k, new_v = tail(k), tail(v)
    conv_new = u_pre[s - (CONV_WIDTH - 1):].reshape(1, CONV_WIDTH - 1, 3 * bw)
    xh = mk.shape[1] // B_HEAD_DIM
    return (y.reshape(1, s, d), new_k, new_v, conv_new, s_fin,
            mk.reshape(1, -1, xh, B_HEAD_DIM), mv.reshape(1, -1, xh, B_HEAD_DIM))


def _sample_layer(h, win_k, win_v, conv_prev, s0, mem_k, mem_v, pw):
    nseq, t_new, d = h.shape
    aw, bw = pw["aw"], pw["bw"]
    n = nseq * t_new
    assert t_new == 8 and GDN_CHUNK % t_new == 0
    x = h.reshape(n, d)
    q, k, v, u_pre, z, ab = _inproj(x, pw["g_mix"], pw["w_in"], pw["gm64"], pw["gq"], pw["gk"],
                                    aw=aw, bw=bw, tm=512)
    w_buf = win_k.shape[1]
    heads = aw // A_HEAD_DIM
    win_t = lambda c_: jnp.transpose(c_, (0, 2, 3, 1)).reshape(nseq, aw, w_buf)
    new_t = lambda t_: jnp.transpose(t_.reshape(aw // LANE, nseq, t_new, LANE), (1, 0, 3, 2)).reshape(nseq, aw, t_new)
    win_back = lambda c_: jnp.transpose(c_.reshape(nseq, heads, A_HEAD_DIM, w_buf), (0, 3, 1, 2))
    q_rows = jnp.transpose(q.reshape(aw // LANE, nseq, t_new, LANE), (1, 2, 0, 3)).reshape(nseq, t_new, aw)
    new_kt, new_vt, oa = _step_attn(q_rows, new_t(k), new_t(v), win_t(win_k), win_t(win_v))
    new_k, new_v = win_back(new_kt), win_back(new_vt)
    prev = jnp.concatenate([jnp.zeros((nseq, PREV_ROWS - (CONV_WIDTH - 1), 3 * bw), F32),
                            conv_prev.astype(F32)], axis=1)
    og, s_fin = _gdn(u_pre, prev, ab, s0, pw, seg=t_new, conv_seg=t_new, chunks=4)
    mem_rows = lambda m_: m_.reshape(nseq, -1, m_.shape[3])
    h2, hn, route = _post_sample(oa.reshape(n, aw), og, z, x, pw, mem_rows(mem_k), mem_rows(mem_v),
                                 t_new=t_new, seqs=8)
    y = _moe(hn, h2, route, pw["w_gate"], pw["w_up"], pw["w_down"], chunk=MOE_CHUNK)
    conv_new = u_pre.reshape(nseq, t_new, 3 * bw)[:, t_new - (CONV_WIDTH - 1):, :]
    return (y.reshape(nseq, t_new, d), new_k, new_v, conv_new, s_fin)


def kernel(x_prompt, x_sample, mem_prompt, cache_win_k, cache_win_v, state_conv, state_delta, cache_mem_k, cache_mem_v, g_mix, w_in, g_qa, g_ka, conv_w, a_log, dt_bias, g_gdn, w_out, g_xattn, g_mem, w_xq, w_xk, w_xv, g_xq, g_xk, w_xo, g_ffn, w_rg, b_rg, w_re, b_re, w_gate, w_up, w_down):
    depth = w_in.shape[0]
    hp, hs = x_prompt, x_sample
    outs = [[] for _ in range(10)]
    for l in range(depth):
        praw = dict(g_mix=g_mix[l], w_in=w_in[l], g_qa=g_qa[l], g_ka=g_ka[l], conv_w=conv_w[l],
                    a_log=a_log[l], dt_bias=dt_bias[l], g_gdn=g_gdn[l], w_out=w_out[l],
                    g_xattn=g_xattn[l], g_mem=g_mem[l], w_xq=w_xq[l], w_xk=w_xk[l], w_xv=w_xv[l],
                    g_xq=g_xq[l], g_xk=g_xk[l], w_xo=w_xo[l], g_ffn=g_ffn[l], w_rg=w_rg[l],
                    b_rg=b_rg[l], w_re=w_re[l], b_re=b_re[l], w_gate=w_gate[l], w_up=w_up[l],
                    w_down=w_down[l])
        pw = _layer_weights(praw)
        hp, k_p, v_p, c_p, s_p, mk, mv = _prompt_layer(hp, mem_prompt, pw, praw)
        hs, k_s, v_s, c_s, s_s = _sample_layer(hs, cache_win_k[l], cache_win_v[l], state_conv[l],
                                               state_delta[l].astype(F32), cache_mem_k[l], cache_mem_v[l], pw)
        for lst, val in zip(outs, (k_p, v_p, c_p, s_p, mk, mv, k_s, v_s, c_s, s_s)):
            lst.append(val)
    st = [jnp.stack(o) for o in outs]
    st[3] = st[3].astype(state_delta.dtype)
    st[9] = st[9].astype(state_delta.dtype)
    return (hp, hs, *st)
```

```python
import functools

import jax
import jax.numpy as jnp
from jax import lax
from jax.experimental import pallas as pl
from jax.experimental.pallas import tpu as pltpu

F32 = jnp.float32
BF16 = jnp.bfloat16
EPS = 1e-6
NEG_INF = float("-inf")

A_HEAD_DIM = 64
B_HEAD_DIM = 128
B_HEADS = 4
BAND = 128
DILATIONS = ((128, 1), (512, 4), (2048, 16))
GDN_CHUNK = 64
CONV_WIDTH = 4
PREV_ROWS = 8
N_GROUPS = 4
GROUP_EXPERTS = 8
SUB, LANE = 8, 128
BF16_ROWS = 16
ROUTE_LANES = LANE
MOE_BLOCK = 160
MOE_CHUNK = 2048
VMEM_LIMIT = 56 * 1024 * 1024


def _cparams(*sem):
    return pltpu.CompilerParams(dimension_semantics=sem, vmem_limit_bytes=VMEM_LIMIT)


def _bdot(a, b):
    return jnp.dot(a.astype(BF16), b.astype(BF16), preferred_element_type=F32)


def _bdot_nt(a, b):
    return lax.dot_general(a.astype(BF16), b.astype(BF16), (((1,), (1,)), ((), ())),
                           preferred_element_type=F32)


def _split3(x):
    hi = x.astype(BF16)
    r1 = x - hi.astype(F32)
    mid = r1.astype(BF16)
    lo = (r1 - mid.astype(F32)).astype(BF16)
    return hi, mid, lo


def _dot_exact_lhs(a01, x):
    a = a01.astype(BF16)
    hi, mid, lo = _split3(x)
    d = lambda p: jnp.dot(a, p, preferred_element_type=F32)
    return d(hi) + d(mid) + d(lo)


def _lane_tile_rms(x):
    parts = []
    for c in range(x.shape[1] // LANE):
        xc = x[:, c * LANE:(c + 1) * LANE]
        parts.append(xc * lax.rsqrt(jnp.mean(xc * xc, axis=-1, keepdims=True) + EPS))
    return jnp.concatenate(parts, axis=1)


def _half_tile_rms(x):
    half = LANE // 2
    lo = lax.broadcasted_iota(jnp.int32, (x.shape[0], LANE), 1) < half
    parts = []
    for c in range(x.shape[1] // LANE):
        xc = x[:, c * LANE:(c + 1) * LANE]
        sq = xc * xc
        s_lo = jnp.sum(jnp.where(lo, sq, 0.0), axis=-1, keepdims=True)
        s_hi = jnp.sum(jnp.where(lo, 0.0, sq), axis=-1, keepdims=True)
        parts.append(xc * lax.rsqrt(jnp.where(lo, s_lo, s_hi) * (1.0 / half) + EPS))
    return jnp.concatenate(parts, axis=1)


def _rms(x, g):
    return x * lax.rsqrt(jnp.mean(x * x, axis=-1, keepdims=True) + EPS) * g


def _sigmoid(x):
    return 1.0 / (1.0 + jnp.exp(-x))


def _inproj_kernel(x_ref, g_ref, w_ref, gq_ref, gk_ref,
                   q_ref, k_ref, v_ref, u_ref, z_ref, ab_ref, *, aw, bw):
    nb = _rms(x_ref[...], g_ref[...]).astype(BF16)

    def proj(lo, hi):
        return jnp.dot(nb, w_ref[:, lo:hi], preferred_element_type=F32)

    def put_slabs(ref, val):
        for p in range(aw // LANE):
            ref[p] = val[:, p * LANE:(p + 1) * LANE]

    assert A_HEAD_DIM * 2 == LANE
    put_slabs(q_ref, _half_tile_rms(proj(0, aw)) * gq_ref[...])
    put_slabs(k_ref, _half_tile_rms(proj(aw, 2 * aw)) * gk_ref[...])
    put_slabs(v_ref, proj(2 * aw, 3 * aw))
    u_ref[...] = proj(3 * aw, 3 * aw + 3 * bw)
    z_ref[...] = proj(3 * aw + 3 * bw, 3 * aw + 4 * bw)
    ab_ref[...] = proj(3 * aw + 4 * bw, 3 * aw + 4 * bw + ROUTE_LANES)


def _inproj(x, g_mix, w_in_b, gq_t, gk_t, *, aw, bw, tm):
    n, d = x.shape
    cols = w_in_b.shape[1]
    row = lambda w: pl.BlockSpec((tm, w), lambda i: (i, 0))
    full = lambda a: pl.BlockSpec(a.shape, lambda i: (0, 0))
    out_w = (3 * bw, bw, ROUTE_LANES)
    slab = pl.BlockSpec((aw // LANE, tm, LANE), lambda i: (0, i, 0))
    return pl.pallas_call(
        functools.partial(_inproj_kernel, aw=aw, bw=bw),
        grid=(n // tm,),
        in_specs=[row(d), full(g_mix), pl.BlockSpec((d, cols), lambda i: (0, 0)),
                  full(gq_t), full(gk_t)],
        out_specs=[slab] * 3 + [row(w) for w in out_w],
        out_shape=[jax.ShapeDtypeStruct((aw // LANE, n, LANE), F32)] * 3
                  + [jax.ShapeDtypeStruct((n, w), F32) for w in out_w],
        compiler_params=_cparams("parallel"),
        name="inproj",
    )(x, g_mix, w_in_b, gq_t, gk_t)


def _dil_attn_kernel(q_ref, kp_ref, kc_ref, vp_ref, vc_ref, o_ref, od_ref, ld_ref):
    n = pl.program_id(1)
    rows = q_ref.shape[1]
    qi = lax.broadcasted_iota(jnp.int32, (BAND, 2 * BAND), 0)
    kj = lax.broadcasted_iota(jnp.int32, (BAND, 2 * BAND), 1)
    band = (kj >= qi) & (kj <= qi + BAND)
    bias = jnp.where(band, 0.0, NEG_INF)
    bias_first = jnp.where(band & ((kj >= BAND) | (n > 0)), 0.0, NEG_INF)
    lo_half = lax.broadcasted_iota(jnp.int32, (BAND, LANE), 1) < A_HEAD_DIM

    for di, (window, dil) in enumerate(DILATIONS):
        assert window // dil == BAND and rows % (BAND * dil) == 0
        span = BAND * dil
        take = lambda ref, st: ref[0, pl.ds(st, BAND, stride=dil), :]
        for r in range(dil):
            k_prev = take(kp_ref, r + rows - span).astype(BF16)
            v_prev = take(vp_ref, r + rows - span).astype(BF16)
            for j in range(rows // span):
                start = r + j * span
                q = take(q_ref, start)
                k_cur, v_cur = take(kc_ref, start).astype(BF16), take(vc_ref, start).astype(BF16)
                kcat = jnp.concatenate([k_prev, k_cur], axis=0)
                vcat = jnp.concatenate([v_prev, v_cur], axis=0)
                k_prev, v_prev = k_cur, v_cur
                res = []
                for half in range(2):
                    keep = lo_half if half == 0 else jnp.logical_not(lo_half)
                    s = _bdot_nt(jnp.where(keep, q, 0.0), kcat) + (bias if j else bias_first)
                    m = jnp.max(s, axis=-1, keepdims=True)
                    e = jnp.exp(s - m)
                    den = jnp.sum(e, axis=-1, keepdims=True)
                    o = jnp.dot(e.astype(BF16), vcat, preferred_element_type=F32) / den
                    res.append((o, m + jnp.log(den)))
                od_ref[di, pl.ds(start, BAND, stride=dil), :] = jnp.where(lo_half, res[0][0], res[1][0])
                ld_ref[di, pl.ds(start, BAND, stride=dil), :] = jnp.where(lo_half, res[0][1], res[1][1])

    step = 256
    for r0 in range(0, rows, step):
        ls = [ld_ref[di, r0:r0 + step, :] for di in range(len(DILATIONS))]
        mm = functools.reduce(jnp.maximum, ls)
        es = [jnp.exp(l - mm) for l in ls]
        num = functools.reduce(lambda a, b: a + b, [e * od_ref[di, r0:r0 + step, :] for di, e in enumerate(es)])
        o_ref[0, r0:r0 + step, :] = num / functools.reduce(lambda a, b: a + b, es)


def _dil_attn(q, k, v, *, rows):
    pairs, s, lanes = q.shape
    assert s % rows == 0 and lanes == LANE
    cur = pl.BlockSpec((1, rows, LANE), lambda p, n: (p, n, 0))
    prev = pl.BlockSpec((1, rows, LANE), lambda p, n: (p, jnp.maximum(n - 1, 0), 0))
    return pl.pallas_call(
        _dil_attn_kernel,
        grid=(pairs, s // rows),
        in_specs=[cur, prev, cur, prev, cur],
        out_specs=cur,
        out_shape=jax.ShapeDtypeStruct(q.shape, F32),
        scratch_shapes=[pltpu.VMEM((len(DILATIONS), rows, LANE), F32)] * 2,
        compiler_params=_cparams("parallel", "arbitrary"),
        name="dil_attn",
    )(q, k, k, v, v)


def _step_attn_kernel(q_ref, kn_ref, vn_ref, kc_ref, vc_ref, ok_ref, ov_ref, oa_ref,
                      kb_ref, vb_ref, *, w_buf, t_new):
    aw = q_ref.shape[2]
    heads = aw // A_HEAD_DIM
    lanes = LANE
    kt, vt = kc_ref[0], vc_ref[0]
    lead = jnp.zeros((aw, lanes - t_new), F32)
    kn = jnp.concatenate([lead, kn_ref[0]], axis=1)
    vn = jnp.concatenate([lead, vn_ref[0]], axis=1)
    is_new = lax.broadcasted_iota(jnp.int32, (aw, lanes), 1) >= lanes - t_new
    for src, new, dst in ((kt, kn, ok_ref), (vt, vn, ov_ref)):
        rolled = pltpu.roll(src, w_buf - t_new, axis=1)
        dst[0, :, 0:w_buf - lanes] = rolled[:, 0:w_buf - lanes]
        dst[0, :, w_buf - lanes:w_buf] = jnp.where(is_new, new, rolled[:, w_buf - lanes:w_buf])
    kb_ref[...] = kt.astype(BF16)
    vb_ref[...] = vt.astype(BF16)

    rows = heads * t_new
    ri = lax.broadcasted_iota(jnp.int32, (rows, aw), 0)
    li = lax.broadcasted_iota(jnp.int32, (rows, aw), 1)
    q_rep = jnp.concatenate([q_ref[0]] * heads, axis=0)
    q_blk = jnp.where((ri // t_new) == (li // A_HEAD_DIM), q_rep, 0.0).astype(BF16)
    ncol = w_buf + lanes
    s_all = jnp.concatenate([jnp.dot(q_blk, kb_ref[...], preferred_element_type=F32),
                             jnp.dot(q_blk, kn.astype(BF16), preferred_element_type=F32)], axis=1)

    def branch(window, dil, col0):
        s = s_all[:, col0:]
        t = lax.broadcasted_iota(jnp.int32, s.shape, 0) % t_new
        cidx = lax.broadcasted_iota(jnp.int32, s.shape, 1) + col0
        r = jnp.where(cidx < w_buf, cidx, cidx - (lanes - t_new))
        diff = w_buf + t - r
        valid = (diff >= 0) & (diff <= window) & ((diff & (dil - 1)) == 0)
        valid = valid & ((cidx < w_buf) | (cidx >= ncol - t_new))
        s = jnp.where(valid, s, NEG_INF)
        m = jnp.max(s, axis=-1, keepdims=True)
        e = jnp.exp(s - m)
        den = jnp.sum(e, axis=-1, keepdims=True)
        return e, den, m + jnp.log(den)

    parts = []
    for window, dil in DILATIONS:
        assert dil & (dil - 1) == 0
        col0 = max(0, (w_buf - window) // 128 * 128)
        parts.append((col0,) + branch(window, dil, col0))
    mm = functools.reduce(jnp.maximum, [p[3] for p in parts])
    wexp = [jnp.exp(p[3] - mm) for p in parts]
    wsum = functools.reduce(lambda a, b: a + b, wexp)
    p_tot = jnp.zeros((rows, ncol), F32)
    for (col0, e, den, _), we in zip(parts, wexp):
        pe = e * (we / (wsum * den))
        if col0:
            pe = jnp.concatenate([jnp.zeros((rows, col0), F32), pe], axis=1)
        p_tot = p_tot + pe
    o = _bdot_nt(p_tot[:, :w_buf], vb_ref[...]) + _bdot_nt(p_tot[:, w_buf:], vn)
    lh = lax.broadcasted_iota(jnp.int32, (t_new, aw), 1) // A_HEAD_DIM
    acc = jnp.zeros((t_new, aw), F32)
    for h in range(heads):
        acc = acc + jnp.where(lh == h, o[h * t_new:(h + 1) * t_new, :], 0.0)
    oa_ref[0] = acc


def _step_attn(q, kn_t, vn_t, cache_kt, cache_vt):
    nseq, aw, w_buf = cache_kt.shape
    t_new = q.shape[1]
    assert w_buf % LANE == 0 and t_new % SUB == 0 and t_new <= LANE and kn_t.shape[2] == t_new
    qs = pl.BlockSpec((1, t_new, aw), lambda i: (i, 0, 0))
    new = pl.BlockSpec((1, aw, t_new), lambda i: (i, 0, 0))
    win = pl.BlockSpec((1, aw, w_buf), lambda i: (i, 0, 0))
    return pl.pallas_call(
        functools.partial(_step_attn_kernel, w_buf=w_buf, t_new=t_new),
        grid=(nseq,),
        in_specs=[qs, new, new, win, win],
        out_specs=[win, win, qs],
        out_shape=[jax.ShapeDtypeStruct(cache_kt.shape, F32), jax.ShapeDtypeStruct(cache_vt.shape, F32),
                   jax.ShapeDtypeStruct(q.shape, F32)],
        scratch_shapes=[pltpu.VMEM((aw, w_buf), BF16), pltpu.VMEM((aw, w_buf), BF16)],
        compiler_params=_cparams("parallel"),
        name="step_attn",
    )(q, kn_t, vn_t, cache_kt, cache_vt)


def _gdn_prep_kernel(u_ref, prev_ref, ab_ref, cw_ref, alog_ref, dtb_ref,
                     w_ref, uv_ref, qg_ref, kt_ref, qk_ref, last_ref, full_ref, act_ref,
                     *, seg, conv_seg, chunks):
    c = GDN_CHUNK
    dk = B_HEAD_DIM
    bw = B_HEADS * dk
    ii = lax.broadcasted_iota(jnp.int32, (c, c), 0)
    jj = lax.broadcasted_iota(jnp.int32, (c, c), 1)
    same = (ii // seg) == (jj // seg)
    incl = same & (ii >= jj)
    strict = same & (ii > jj)
    eye = ii == jj
    ones_c = jnp.ones((c, c), BF16)
    seg_cols = jnp.where((lax.broadcasted_iota(jnp.int32, (c, dk), 0) % seg)
                         == lax.broadcasted_iota(jnp.int32, (c, dk), 1), 1.0, 0.0)
    cw = cw_ref[...]
    rows = chunks * c

    for sgi in range(rows // conv_seg):
        base = sgi * (conv_seg + PREV_ROWS)
        full_ref[base:base + PREV_ROWS, :] = prev_ref[sgi]
        full_ref[base + PREV_ROWS:base + PREV_ROWS + conv_seg, :] = u_ref[sgi * conv_seg:(sgi + 1) * conv_seg, :]
        y = jnp.zeros((conv_seg, 3 * bw), F32)
        for j in range(CONV_WIDTH):
            off = base + PREV_ROWS - (CONV_WIDTH - 1) + j
            y = y + full_ref[off:off + conv_seg, :] * cw[j:j + 1, :]
        act_ref[sgi * conv_seg:(sgi + 1) * conv_seg, :] = y * _sigmoid(y)

    ri = lax.broadcasted_iota(jnp.int32, (rows, rows), 0)
    rj = lax.broadcasted_iota(jnp.int32, (rows, rows), 1)
    rsame = (ri // seg) == (rj // seg)
    ab = ab_ref[...]
    sp = ab + dtb_ref[...]
    sp = jnp.maximum(sp, 0.0) + jnp.log(1.0 + jnp.exp(-jnp.abs(sp)))
    g_all = -jnp.exp(alog_ref[...]) * sp
    gcum_all = _dot_exact_lhs(jnp.where(rsame & (ri >= rj), 1.0, 0.0), g_all)
    gtot_all = _dot_exact_lhs(jnp.where(rsame, 1.0, 0.0), g_all)
    beta_all = _sigmoid(ab)
    eye_f = jnp.where(eye, 1.0, 0.0)

    grow = []
    for ci in range(chunks):
        rs = slice(ci * c, (ci + 1) * c)
        diag = jnp.concatenate([jnp.where(eye, gcum_all[rs, h:h + 1], 0.0) for h in range(B_HEADS)], axis=1)
        grow.append(_dot_exact_lhs(ones_c, diag))

    prob = [(ci, h) for ci in range(chunks) for h in range(B_HEADS)]
    col = lambda arr, ci, lane: arr[ci * c:(ci + 1) * c, lane:lane + 1]
    gc = [col(gcum_all, ci, h) for ci, h in prob]
    gt = [col(gtot_all, ci, h) for ci, h in prob]
    beta = [col(beta_all, ci, B_HEADS + h) for ci, h in prob]
    q, k, v = [], [], []
    for ci, h in prob:
        rs = slice(ci * c, (ci + 1) * c)
        qh = act_ref[rs, h * dk:(h + 1) * dk]
        kh = act_ref[rs, bw + h * dk:bw + (h + 1) * dk]
        q.append(qh * lax.rsqrt(jnp.sum(qh * qh, axis=-1, keepdims=True) + EPS) * (dk ** -0.5))
        k.append(kh * lax.rsqrt(jnp.sum(kh * kh, axis=-1, keepdims=True) + EPS))
        v.append(act_ref[rs, 2 * bw + h * dk:2 * bw + (h + 1) * dk])
    e_incl = [jnp.exp(jnp.where(incl, gc[i] - grow[ci][:, h * c:(h + 1) * c], NEG_INF))
              for i, (ci, h) in enumerate(prob)]
    qkk = [_bdot_nt(jnp.concatenate([q[i], k[i]], axis=0), k[i]) for i in range(len(prob))]
    a = [beta[i] * jnp.where(strict, e_incl[i], 0.0) * qkk[i][c:, :] for i in range(len(prob))]
    blk = 8
    inblk = (ii // blk) == (jj // blk)
    d0 = [jnp.where(inblk, t, 0.0) for t in a]
    d2 = [_bdot(t, t) for t in d0]
    d4 = [_bdot(t, t) for t in d2]
    x = [(eye_f - t) + _bdot(eye_f - t, t2) for t, t2 in zip(d0, d2)]
    x = [t + _bdot(t, t4) for t, t4 in zip(x, d4)]
    while blk < seg:
        off_blk = ((ii // (2 * blk)) == (jj // (2 * blk))) & ((ii // blk) != (jj // blk))
        xe = [_bdot(t, jnp.where(off_blk, ta, 0.0)) for t, ta in zip(x, a)]
        x = [t - _bdot(te, t) for t, te in zip(x, xe)]
        blk *= 2
    gamma = [jnp.exp(t) for t in gc]
    wuv = [_bdot(x[i], jnp.concatenate([(beta[i] * gamma[i]) * k[i], beta[i] * v[i]], axis=1))
           for i in range(len(prob))]
    for i, (ci, h) in enumerate(prob):
        rs = slice(ci * c, (ci + 1) * c)
        hs = slice(h * dk, (h + 1) * dk)
        w_ref[rs, hs] = wuv[i][:, :dk]
        uv_ref[rs, hs] = wuv[i][:, dk:]
        qg_ref[rs, hs] = gamma[i] * q[i]
        kt_ref[rs, hs] = k[i] * jnp.exp(gt[i] - gc[i])
        qk = qkk[i][:c, :] * e_incl[i]
        qk_ref[rs, hs] = (jnp.concatenate([qk, jnp.zeros((c, dk - c), F32)], axis=1) if seg == c
                          else _bdot(qk, seg_cols))
        last_ref[rs, hs] = jnp.broadcast_to(jnp.exp(gt[i]), (c, dk))


def _gdn_prep(u_pre, prev, ab, conv_w, alog_row, dtb_row, *, seg, conv_seg, chunks):
    n, cw = u_pre.shape
    bw = cw // 3
    c = GDN_CHUNK
    rows = chunks * c
    assert n % rows == 0 and c % seg == 0 and seg % 8 == 0 and rows % conv_seg == 0 and conv_seg % seg == 0
    ncs = rows // conv_seg
    row = lambda w: pl.BlockSpec((rows, w), lambda i: (i, 0))
    full = lambda a: pl.BlockSpec(a.shape, lambda i: (0, 0))
    return pl.pallas_call(
        functools.partial(_gdn_prep_kernel, seg=seg, conv_seg=conv_seg, chunks=chunks),
        grid=(n // rows,),
        in_specs=[row(cw), pl.BlockSpec((ncs, PREV_ROWS, cw), lambda i: (i, 0, 0)),
                  row(ROUTE_LANES), full(conv_w), full(alog_row), full(dtb_row)],
        out_specs=[row(bw)] * 6,
        out_shape=[jax.ShapeDtypeStruct((n, bw), F32)] * 6,
        scratch_shapes=[pltpu.VMEM((ncs * (conv_seg + PREV_ROWS), cw), F32), pltpu.VMEM((rows, cw), F32)],
        compiler_params=_cparams("parallel"),
        name=f"gdn_prep_{seg}",
    )(u_pre, prev, ab, conv_w, alog_row, dtb_row)


def _gdn_scan_kernel(w_ref, uv_ref, qg_ref, kt_ref, qk_ref, last_ref, s0_ref, o_ref, s_ref, *, seg, steps):
    dk = B_HEAD_DIM

    @pl.when(pl.program_id(1) == 0)
    def _():
        s_ref[...] = s0_ref[...]

    rows = max(seg, BF16_ROWS)
    pad = rows - seg
    padr = lambda t: jnp.concatenate([t, jnp.zeros((pad, t.shape[1]), F32)], axis=0) if pad else t
    ii = lax.broadcasted_iota(jnp.int32, (dk, dk), 0)
    jj = lax.broadcasted_iota(jnp.int32, (dk, dk), 1)
    eye = jnp.where(ii == jj, 1.0, 0.0).astype(BF16)
    chains = [(sq, h) for sq in range(s_ref.shape[0]) for h in range(B_HEADS)]
    hs = lambda h: slice(h * dk, (h + 1) * dk)
    s = [s_ref[sq, h] for sq, h in chains]
    for c in range(steps):
        rs = [slice((sq * steps + c) * seg, (sq * steps + c + 1) * seg) for sq, _ in chains]
        sb = [t.astype(BF16) for t in s]
        ws = [_bdot(padr(w_ref[rs[i], hs(h)]), sb[i]) for i, (_, h) in enumerate(chains)]
        ub = [(padr(uv_ref[rs[i], hs(h)]) - ws[i]).astype(BF16) for i, (_, h) in enumerate(chains)]
        ktt = [_bdot_nt(eye, padr(kt_ref[rs[i], hs(h)])) for i, (_, h) in enumerate(chains)]
        s_new = [last_ref[rs[i], hs(h)][0:1, :] * s[i] + _bdot(ktt[i], ub[i]) for i, (_, h) in enumerate(chains)]
        for i, (_, h) in enumerate(chains):
            o = _bdot(padr(qg_ref[rs[i], hs(h)]), sb[i]) + _bdot(padr(qk_ref[rs[i], hs(h)])[:, :rows], ub[i])
            o_ref[rs[i], hs(h)] = o[:seg, :]
        s = s_new
    for i, (sq, h) in enumerate(chains):
        s_ref[sq, h] = s[i]


def _gdn_scan(w, uv, qg, kt, qk, last, s0, *, seg, seqs, steps):
    n, bw = w.shape
    nseq = s0.shape[0]
    per_seq = n // (nseq * seg)
    assert nseq % seqs == 0 and per_seq % steps == 0 and (seqs == 1 or steps == per_seq)
    blocks = per_seq // steps
    row = pl.BlockSpec((seqs * steps * seg, bw), lambda s, i: (s * blocks + i, 0))
    st = pl.BlockSpec((seqs,) + s0.shape[1:], lambda s, i: (s, 0, 0, 0))
    return pl.pallas_call(
        functools.partial(_gdn_scan_kernel, seg=seg, steps=steps),
        grid=(nseq // seqs, blocks),
        in_specs=[row] * 6 + [st],
        out_specs=[row, st],
        out_shape=[jax.ShapeDtypeStruct((n, bw), F32), jax.ShapeDtypeStruct(s0.shape, F32)],
        compiler_params=_cparams("parallel", "arbitrary"),
        name=f"gdn_scan_{seg}",
    )(w, uv, qg, kt, qk, last, s0)


def _mem_kv_kernel(mem_ref, g_ref, wk_ref, wv_ref, gk_ref, k_ref, v_ref):
    mn = _rms(mem_ref[...], g_ref[...]).astype(BF16)
    k = jnp.dot(mn, wk_ref[...], preferred_element_type=F32)
    k_ref[...] = _lane_tile_rms(k) * gk_ref[...]
    v_ref[...] = jnp.dot(mn, wv_ref[...], preferred_element_type=F32)


def _mem_kv(mem, g_mem, w_xk_b, w_xv_b, gk_t):
    m = mem.shape[0]
    xw = w_xk_b.shape[1]
    return pl.pallas_call(
        _mem_kv_kernel,
        out_shape=[jax.ShapeDtypeStruct((m, xw), F32)] * 2,
        compiler_params=pltpu.CompilerParams(vmem_limit_bytes=VMEM_LIMIT),
        name="mem_kv",
    )(mem, g_mem, w_xk_b, w_xv_b, gk_t)


def _route(logits):
    lane = lax.broadcasted_iota(jnp.int32, logits.shape, 1)
    lane_f = lane.astype(F32)
    big = float(ROUTE_LANES)
    lg = jnp.where(lane < N_GROUPS, logits, NEG_INF)
    mg = jnp.max(lg, axis=-1, keepdims=True)
    zg = jnp.sum(jnp.exp(lg - mg), axis=-1, keepdims=True)
    pg_top = 1.0 / zg
    gidx = jnp.min(jnp.where(lg == mg, lane_f, big), axis=-1, keepdims=True)
    e_lo = N_GROUPS + GROUP_EXPERTS * gidx
    emask = (lane_f >= e_lo) & (lane_f < e_lo + GROUP_EXPERTS)
    le = jnp.where(emask, logits, NEG_INF)
    me = jnp.max(le, axis=-1, keepdims=True)
    ee = jnp.exp(le - me)
    pe = ee / jnp.sum(ee, axis=-1, keepdims=True)
    pe = jnp.where(emask, pe, -1.0)
    p1 = jnp.max(pe, axis=-1, keepdims=True)
    i1 = jnp.min(jnp.where(pe == p1, lane_f, big), axis=-1, keepdims=True)
    pe2 = jnp.where(lane_f == i1, -1.0, pe)
    p2 = jnp.max(pe2, axis=-1, keepdims=True)
    i2 = jnp.min(jnp.where(pe2 == p2, lane_f, big), axis=-1, keepdims=True)
    den = p1 + p2
    w1 = pg_top * p1 / den
    w2 = pg_top * p2 / den
    out = jnp.where(lane == 0, i1 - N_GROUPS, 0.0)
    out = jnp.where(lane == 1, i2 - N_GROUPS, out)
    out = jnp.where(lane == 2, w1, out)
    return jnp.where(lane == 3, w2, out)


def _post_common(oa, og_ref, z_ref, h_ref, wout_ref, ggdn_ref, gx_ref, wxq_ref, gxq_ref):
    og = og_ref[...]
    z = z_ref[...]
    ob = _lane_tile_rms(og) * ggdn_ref[...] * (z * _sigmoid(z))
    cat = jnp.concatenate([oa, ob], axis=-1).astype(BF16)
    h1 = h_ref[...] + jnp.dot(cat, wout_ref[...], preferred_element_type=F32)
    hx = _rms(h1, gx_ref[...]).astype(BF16)
    q = jnp.dot(hx, wxq_ref[...], preferred_element_type=F32)
    qn = _lane_tile_rms(q) * gxq_ref[...]
    return h1, qn


def _mem_attend_rows(qn, mk, mv):
    outs = []
    for h in range(qn.shape[1] // B_HEAD_DIM):
        hs = slice(h * B_HEAD_DIM, (h + 1) * B_HEAD_DIM)
        s = _bdot_nt(qn[:, hs], mk[:, hs])
        m = jnp.max(s, axis=-1, keepdims=True)
        e = jnp.exp(s - m)
        p = e / jnp.sum(e, axis=-1, keepdims=True)
        outs.append(_bdot(p, mv[:, hs]))
    return jnp.concatenate(outs, axis=-1)


def _post_tail(h1, ox, wxo_ref, gffn_ref, wr_ref, br_ref, h2_ref, hn_ref, route_ref):
    h2 = h1 + jnp.dot(ox.astype(BF16), wxo_ref[...], preferred_element_type=F32)
    h2_ref[...] = h2
    hn = _rms(h2, gffn_ref[...])
    tm = hn.shape[0]
    for j in range(hn.shape[1] // LANE):
        hn_ref[pl.ds(j, tm, stride=SUB), :] = hn[:, j * LANE:(j + 1) * LANE]
    hn_hi = hn.astype(BF16)
    hn_mid = (hn - hn_hi.astype(F32)).astype(BF16)
    d = lambda a, b: jnp.dot(a, b, preferred_element_type=F32)
    logits = d(hn_hi, wr_ref[0]) + d(hn_hi, wr_ref[1]) + d(hn_mid, wr_ref[0]) + br_ref[...]
    route_ref[...] = _route(logits)


def _post_prompt_kernel(oa_ref, og_ref, z_ref, h_ref,
                        wout_ref, ggdn_ref, gx_ref, wxq_ref, gxq_ref, mk_ref, mv_ref,
                        wxo_ref, gffn_ref, wr_ref, br_ref, h2_ref, hn_ref, route_ref):
    oa = jnp.concatenate([oa_ref[p] for p in range(oa_ref.shape[0])], axis=-1)
    h1, qn = _post_common(oa, og_ref, z_ref, h_ref, wout_ref, ggdn_ref, gx_ref, wxq_ref, gxq_ref)
    ox = _mem_attend_rows(qn, mk_ref[...], mv_ref[...])
    _post_tail(h1, ox, wxo_ref, gffn_ref, wr_ref, br_ref, h2_ref, hn_ref, route_ref)


def _post_sample_kernel(oa_ref, og_ref, z_ref, h_ref,
                        wout_ref, ggdn_ref, gx_ref, wxq_ref, gxq_ref, mk_ref, mv_ref,
                        wxo_ref, gffn_ref, wr_ref, br_ref, h2_ref, hn_ref, route_ref, *, t_new):
    h1, qn = _post_common(oa_ref[...], og_ref, z_ref, h_ref, wout_ref, ggdn_ref, gx_ref, wxq_ref, gxq_ref)
    dh = B_HEAD_DIM
    heads = qn.shape[1] // dh
    rows = heads * t_new
    ncol = mk_ref.shape[1]
    own = ((lax.broadcasted_iota(jnp.int32, (rows, ncol), 0) // t_new)
           == (lax.broadcasted_iota(jnp.int32, (rows, ncol), 1) % heads))
    outs = []
    for sq in range(mk_ref.shape[0]):
        qs = qn[sq * t_new:(sq + 1) * t_new, :]
        q_rows = jnp.concatenate([qs[:, h * dh:(h + 1) * dh] for h in range(heads)], axis=0)
        s = jnp.where(own, _bdot_nt(q_rows, mk_ref[sq]), NEG_INF)
        m = jnp.max(s, axis=-1, keepdims=True)
        e = jnp.exp(s - m)
        o = _bdot(e / jnp.sum(e, axis=-1, keepdims=True), mv_ref[sq])
        outs.append(jnp.concatenate([o[h * t_new:(h + 1) * t_new, :] for h in range(heads)], axis=1))
    ox = jnp.concatenate(outs, axis=0)
    _post_tail(h1, ox, wxo_ref, gffn_ref, wr_ref, br_ref, h2_ref, hn_ref, route_ref)


def _post_weights_specs(weights):
    return [pl.BlockSpec(a.shape, lambda i, nd=a.ndim: (0,) * nd) for a in weights]


def _post_outs(n, d, tm):
    assert d == SUB * LANE
    row = lambda w: pl.BlockSpec((tm, w), lambda i: (i, 0))
    specs = [row(d), pl.BlockSpec((tm * SUB, LANE), lambda i: (i, 0)), row(ROUTE_LANES)]
    shapes = [jax.ShapeDtypeStruct((n, d), F32), jax.ShapeDtypeStruct((n * SUB, LANE), F32),
              jax.ShapeDtypeStruct((n, ROUTE_LANES), F32)]
    return specs, shapes


def _post_prompt(oa_slabs, og, z, h, pw, mk, mv, *, tm):
    n, d = h.shape
    aw = og.shape[1]
    row = lambda w: pl.BlockSpec((tm, w), lambda i: (i, 0))
    full = lambda a: pl.BlockSpec(a.shape, lambda i: (0, 0))
    slab = pl.BlockSpec((oa_slabs.shape[0], tm, LANE), lambda i: (0, i, 0))
    w1 = [pw["w_out"], pw["g_gdn"], pw["g_xattn"], pw["w_xq"], pw["g_xq"]]
    w2 = [pw["w_xo"], pw["g_ffn"], pw["w_r"], pw["b_r"]]
    specs, shapes = _post_outs(n, d, tm)
    return pl.pallas_call(
        _post_prompt_kernel,
        grid=(n // tm,),
        in_specs=[slab, row(aw), row(aw), row(d)] + _post_weights_specs(w1) + [full(mk), full(mv)]
                 + _post_weights_specs(w2),
        out_specs=specs, out_shape=shapes,
        compiler_params=_cparams("parallel"),
        name="post_prompt",
    )(oa_slabs, og, z, h, *w1, mk, mv, *w2)


def _post_sample(oa, og, z, h, pw, mk, mv, *, t_new, seqs):
    n, d = h.shape
    aw = og.shape[1]
    tm = t_new * seqs
    row = lambda w: pl.BlockSpec((tm, w), lambda i: (i, 0))
    mem = pl.BlockSpec((seqs,) + mk.shape[1:], lambda i: (i, 0, 0))
    w1 = [pw["w_out"], pw["g_gdn"], pw["g_xattn"], pw["w_xq"], pw["g_xq"]]
    w2 = [pw["w_xo"], pw["g_ffn"], pw["w_r"], pw["b_r"]]
    specs, shapes = _post_outs(n, d, tm)
    return pl.pallas_call(
        functools.partial(_post_sample_kernel, t_new=t_new),
        grid=(n // tm,),
        in_specs=[row(aw)] * 3 + [row(d)] + _post_weights_specs(w1) + [mem, mem]
                 + _post_weights_specs(w2),
        out_specs=specs, out_shape=shapes,
        compiler_params=_cparams("parallel"),
        name="post_sample",
    )(oa, og, z, h, *w1, mk, mv, *w2)


def _moe_kernel(blk_e_ref, blk_start_ref, blk_cnt_ref, tok_ref, tw_ref, x_ref, res_ref,
                wg_ref, wu_ref, wd_ref, out_ref, acc_ref, xs_ref, ys_ref, *, nb):
    c = pl.program_id(0)
    b = pl.program_id(1)
    chunk, d = res_ref.shape
    nl = d // LANE
    cnt = blk_cnt_ref[c * nb + b]

    @pl.when(cnt > 0)
    def _():
        base = blk_start_ref[c * nb + b]
        for r in range(MOE_BLOCK):
            t = tok_ref[0, 0, base + r] >> 1
            xs_ref[r * SUB:(r + 1) * SUB, :] = x_ref[pl.ds(pl.multiple_of(t * SUB, SUB), SUB), :]
        xb = jnp.concatenate([xs_ref[pl.ds(j, MOE_BLOCK, stride=SUB), :] for j in range(nl)],
                             axis=1).astype(BF16)
        hg = jnp.dot(xb, wg_ref[0], preferred_element_type=F32)
        hu = jnp.dot(xb, wu_ref[0], preferred_element_type=F32)
        act = (hg * _sigmoid(hg) * hu).astype(BF16)
        y = jnp.dot(act, wd_ref[0], preferred_element_type=F32)
        for j in range(nl):
            ys_ref[pl.ds(j, MOE_BLOCK, stride=SUB), :] = y[:, j * LANE:(j + 1) * LANE]
        for r in range(MOE_BLOCK):
            e = tok_ref[0, 0, base + r]
            slot = jnp.where(r < cnt, (e & 1) * chunk + (e >> 1), 2 * chunk)
            off = pl.multiple_of(slot * SUB, SUB)
            acc_ref[pl.ds(off, SUB), :] = tw_ref[0, 0, base + r] * ys_ref[r * SUB:(r + 1) * SUB, :]

    @pl.when(b == nb - 1)
    def _():
        step = 256
        for r0 in range(0, chunk, step):
            for j in range(nl):
                out_ref[r0:r0 + step, j * LANE:(j + 1) * LANE] = (
                    res_ref[r0:r0 + step, j * LANE:(j + 1) * LANE]
                    + acc_ref[pl.ds(r0 * SUB + j, step, stride=SUB), :]
                    + acc_ref[pl.ds((chunk + r0) * SUB + j, step, stride=SUB), :])


def _dispatch(route, chunk, n_experts):
    n = route.shape[0]
    nch = n // chunk
    rows = 2 * chunk
    nb = rows // MOE_BLOCK + n_experts
    e = route[:, :2].astype(jnp.int32).reshape(nch, rows)
    w = route[:, 2:4].reshape(nch, rows)
    tok = jnp.argsort(e, axis=1, stable=True).astype(jnp.int32)
    tw = jnp.take_along_axis(w, tok, axis=1)
    ex = jnp.arange(n_experts, dtype=jnp.int32)
    counts = jnp.sum((e[:, :, None] == ex).astype(jnp.int32), axis=1)
    start = jnp.cumsum(counts, axis=1) - counts
    nblk_e = (counts + MOE_BLOCK - 1) // MOE_BLOCK
    bend = jnp.cumsum(nblk_e, axis=1)
    bstart = bend - nblk_e
    total = bend[:, -1:]
    b = jnp.arange(nb, dtype=jnp.int32)[None, :]
    bb = jnp.minimum(b, total - 1)
    eb = jnp.sum((bend[:, None, :] <= bb[:, :, None]).astype(jnp.int32), axis=2)
    sel = eb[:, :, None] == ex
    pick = lambda tbl: jnp.sum(jnp.where(sel, tbl[:, None, :], 0), axis=2)
    j = bb - pick(bstart)
    blk_start = pick(start) + j * MOE_BLOCK
    blk_cnt = jnp.where(b < total, jnp.clip(pick(counts) - j * MOE_BLOCK, 0, MOE_BLOCK), 0)
    flat = lambda t_: t_.astype(jnp.int32).reshape(-1)
    pad = lambda t_: jnp.pad(t_, ((0, 0), (0, MOE_BLOCK))).reshape(nch, 1, rows + MOE_BLOCK)
    return flat(eb), flat(blk_start), flat(blk_cnt), pad(tok), pad(tw), nb


def _moe(hn_t, h2, route, wg_b, wu_b, wd_b, *, chunk):
    n, d = h2.shape
    n_experts, _, ff = wg_b.shape
    chunk = min(chunk, n)
    assert n % chunk == 0 and d == SUB * LANE
    nch = n // chunk
    blk_e, blk_start, blk_cnt, tok, tw, nb = _dispatch(route, chunk, n_experts)
    tab = pl.BlockSpec((1, 1, tok.shape[2]), lambda c, b, *_: (c, 0, 0), memory_space=pltpu.SMEM)
    once = lambda shp: pl.BlockSpec(shp, lambda c, b, *_: (c, 0), pipeline_mode=pl.Buffered(1))
    wspec = lambda shp: pl.BlockSpec((1,) + shp, lambda c, b, be, *_: (be[c * nb + b], 0, 0))
    grid_spec = pltpu.PrefetchScalarGridSpec(
        num_scalar_prefetch=3,
        grid=(nch, nb),
        in_specs=[tab, tab, once((chunk * SUB, LANE)), once((chunk, d)),
                  wspec((d, ff)), wspec((d, ff)), wspec((ff, d))],
        out_specs=once((chunk, d)),
        scratch_shapes=[pltpu.VMEM(((2 * chunk + 1) * SUB, LANE), F32), pltpu.VMEM((MOE_BLOCK * SUB, LANE), F32),
                        pltpu.VMEM((MOE_BLOCK * SUB, LANE), F32)],
    )
    return pl.pallas_call(
        functools.partial(_moe_kernel, nb=nb),
        grid_spec=grid_spec,
        out_shape=jax.ShapeDtypeStruct((n, d), F32),
        compiler_params=_cparams("parallel", "arbitrary"),
        name="moe",
    )(blk_e, blk_start, blk_cnt, tok, tw, hn_t, h2, wg_b, wu_b, wd_b)


def _tile_row(g, reps, scale=1.0):
    return (jnp.tile(g.astype(F32), reps) * scale)[None, :]


def _layer_weights(p):
    d, in_cols = p["w_in"].shape
    aw = d // 2
    bw = d - aw
    pad = 3 * aw + 4 * bw + ROUTE_LANES - in_cols
    n_experts = p["w_re"].shape[1]
    w_r = jnp.concatenate([p["w_rg"], p["w_re"],
                           jnp.zeros((d, ROUTE_LANES - N_GROUPS - n_experts), F32)], axis=1)
    b_r = jnp.concatenate([p["b_rg"], p["b_re"], jnp.zeros((ROUTE_LANES - N_GROUPS - n_experts,), F32)])
    lane_pad = lambda v: jnp.concatenate([v.astype(F32), jnp.zeros((ROUTE_LANES - v.shape[0],), F32)])[None, :]
    return dict(
        aw=aw, bw=bw,
        g_mix=p["g_mix"][None, :],
        w_in=jnp.pad(p["w_in"], ((0, 0), (0, pad))).astype(BF16),
        gq=_tile_row(p["g_qa"], aw // A_HEAD_DIM, A_HEAD_DIM ** -0.5),
        gk=_tile_row(p["g_ka"], aw // A_HEAD_DIM),
        conv_w=p["conv_w"],
        alog=lane_pad(p["a_log"]),
        dtb=lane_pad(jnp.concatenate([p["dt_bias"], jnp.zeros_like(p["dt_bias"])])),
        w_out=p["w_out"].astype(BF16),
        g_gdn=_tile_row(p["g_gdn"], bw // B_HEAD_DIM),
        g_xattn=p["g_xattn"][None, :],
        w_xq=p["w_xq"].astype(BF16),
        g_xq=_tile_row(p["g_xq"], p["w_xq"].shape[1] // B_HEAD_DIM, B_HEAD_DIM ** -0.5),
        w_xo=p["w_xo"].astype(BF16),
        g_ffn=p["g_ffn"][None, :],
        w_r=jnp.stack([w_r.astype(BF16), (w_r - w_r.astype(BF16).astype(F32)).astype(BF16)]), b_r=b_r[None, :],
        w_gate=p["w_gate"].astype(BF16), w_up=p["w_up"].astype(BF16), w_down=p["w_down"].astype(BF16),
    )


def _gdn(u_pre, prev, ab, s0, pw, *, seg, conv_seg, chunks, scan_seqs, scan_steps):
    w, uv, qg, kt, qk, last = _gdn_prep(u_pre, prev, ab, pw["conv_w"], pw["alog"], pw["dtb"],
                                        seg=seg, conv_seg=conv_seg, chunks=chunks)
    return _gdn_scan(w, uv, qg, kt, qk, last, s0, seg=seg, seqs=scan_seqs, steps=scan_steps)


def _prompt_layer(h, mem, pw, praw):
    nb_, s, d = h.shape
    assert nb_ == 1
    x = h.reshape(s, d)
    aw, bw = pw["aw"], pw["bw"]
    q, k, v, u_pre, z, ab = _inproj(x, pw["g_mix"], pw["w_in"], pw["gq"], pw["gk"],
                                    aw=aw, bw=bw, tm=512)
    oa = _dil_attn(q, k, v, rows=DILATIONS[-1][0])
    gdn_chunks = 4
    blk_rows = gdn_chunks * GDN_CHUNK
    tails = u_pre.reshape(s // blk_rows, blk_rows, 3 * bw)[:, blk_rows - PREV_ROWS:, :]
    prev = jnp.concatenate([jnp.zeros((1, PREV_ROWS, 3 * bw), F32), tails[:-1]], axis=0)
    s0 = jnp.zeros((1, B_HEADS, B_HEAD_DIM, B_HEAD_DIM), F32)
    og, s_fin = _gdn(u_pre, prev, ab, s0, pw, seg=GDN_CHUNK, conv_seg=blk_rows, chunks=gdn_chunks,
                     scan_seqs=1, scan_steps=4)
    mk, mv = _mem_kv(mem.reshape(mem.shape[1], d), praw["g_mem"][None, :], praw["w_xk"].astype(BF16),
                     praw["w_xv"].astype(BF16), _tile_row(praw["g_xk"], bw // B_HEAD_DIM))
    h2, hn, route = _post_prompt(oa, og, z, x, pw, mk, mv, tm=256)
    y = _moe(hn, h2, route, pw["w_gate"], pw["w_up"], pw["w_down"], chunk=MOE_CHUNK)
    keep = min(DILATIONS[-1][0], s)
    heads = aw // A_HEAD_DIM
    tail = lambda t_: jnp.transpose(t_[:, s - keep:, :], (1, 0, 2)).reshape(1, keep, heads, A_HEAD_DIM)
    new_k, new_v = tail(k), tail(v)
    conv_new = u_pre[s - (CONV_WIDTH - 1):].reshape(1, CONV_WIDTH - 1, 3 * bw)
    xh = mk.shape[1] // B_HEAD_DIM
    return (y.reshape(1, s, d), new_k, new_v, conv_new, s_fin,
            mk.reshape(1, -1, xh, B_HEAD_DIM), mv.reshape(1, -1, xh, B_HEAD_DIM))


def _sample_layer(h, win_k, win_v, conv_prev, s0, mem_k, mem_v, pw):
    nseq, t_new, d = h.shape
    aw, bw = pw["aw"], pw["bw"]
    n = nseq * t_new
    assert t_new == 8 and GDN_CHUNK % t_new == 0
    x = h.reshape(n, d)
    q, k, v, u_pre, z, ab = _inproj(x, pw["g_mix"], pw["w_in"], pw["gq"], pw["gk"],
                                    aw=aw, bw=bw, tm=512)
    w_buf = win_k.shape[1]
    heads = aw // A_HEAD_DIM
    win_t = lambda c_: jnp.transpose(c_, (0, 2, 3, 1)).reshape(nseq, aw, w_buf)
    new_t = lambda t_: jnp.transpose(t_.reshape(aw // LANE, nseq, t_new, LANE), (1, 0, 3, 2)).reshape(nseq, aw, t_new)
    win_back = lambda c_: jnp.transpose(c_.reshape(nseq, heads, A_HEAD_DIM, w_buf), (0, 3, 1, 2))
    q_rows = jnp.transpose(q.reshape(aw // LANE, nseq, t_new, LANE), (1, 2, 0, 3)).reshape(nseq, t_new, aw)
    new_kt, new_vt, oa = _step_attn(q_rows, new_t(k), new_t(v), win_t(win_k), win_t(win_v))
    new_k, new_v = win_back(new_kt), win_back(new_vt)
    prev = jnp.concatenate([jnp.zeros((nseq, PREV_ROWS - (CONV_WIDTH - 1), 3 * bw), F32),
                            conv_prev.astype(F32)], axis=1)
    og, s_fin = _gdn(u_pre, prev, ab, s0, pw, seg=t_new, conv_seg=t_new, chunks=4, scan_seqs=8, scan_steps=1)
    mem_rows = lambda m_: m_.reshape(nseq, -1, m_.shape[3])
    h2, hn, route = _post_sample(oa.reshape(n, aw), og, z, x, pw, mem_rows(mem_k), mem_rows(mem_v),
                                 t_new=t_new, seqs=8)
    y = _moe(hn, h2, route, pw["w_gate"], pw["w_up"], pw["w_down"], chunk=MOE_CHUNK)
    conv_new = u_pre.reshape(nseq, t_new, 3 * bw)[:, t_new - (CONV_WIDTH - 1):, :]
    return (y.reshape(nseq, t_new, d), new_k, new_v, conv_new, s_fin)


def kernel(x_prompt, x_sample, mem_prompt, cache_win_k, cache_win_v, state_conv, state_delta, cache_mem_k, cache_mem_v, g_mix, w_in, g_qa, g_ka, conv_w, a_log, dt_bias, g_gdn, w_out, g_xattn, g_mem, w_xq, w_xk, w_xv, g_xq, g_xk, w_xo, g_ffn, w_rg, b_rg, w_re, b_re, w_gate, w_up, w_down):
    depth = w_in.shape[0]
    hp, hs = x_prompt, x_sample
    outs = [[] for _ in range(10)]
    for l in range(depth):
        praw = dict(g_mix=g_mix[l], w_in=w_in[l], g_qa=g_qa[l], g_ka=g_ka[l], conv_w=conv_w[l],
                    a_log=a_log[l], dt_bias=dt_bias[l], g_gdn=g_gdn[l], w_out=w_out[l],
                    g_xattn=g_xattn[l], g_mem=g_mem[l], w_xq=w_xq[l], w_xk=w_xk[l], w_xv=w_xv[l],
                    g_xq=g_xq[l], g_xk=g_xk[l], w_xo=w_xo[l], g_ffn=g_ffn[l], w_rg=w_rg[l],
                    b_rg=b_rg[l], w_re=w_re[l], b_re=b_re[l], w_gate=w_gate[l], w_up=w_up[l],
                    w_down=w_down[l])
        pw = _layer_weights(praw)
        hp, k_p, v_p, c_p, s_p, mk, mv = _prompt_layer(hp, mem_prompt, pw, praw)
        hs, k_s, v_s, c_s, s_s = _sample_layer(hs, cache_win_k[l], cache_win_v[l], state_conv[l],
                                               state_delta[l].astype(F32), cache_mem_k[l], cache_mem_v[l], pw)
        for lst, val in zip(outs, (k_p, v_p, c_p, s_p, mk, mv, k_s, v_s, c_s, s_s)):
            lst.append(val)
    st = [jnp.stack(o) for o in outs]
    st[3] = st[3].astype(state_delta.dtype)
    st[9] = st[9].astype(state_delta.dtype)
    return (hp, hs, *st)
```

```python
import functools

import jax
import jax.numpy as jnp
from jax import lax
from jax.experimental import pallas as pl
from jax.experimental.pallas import tpu as pltpu

F32 = jnp.float32
BF16 = jnp.bfloat16
EPS = 1e-6
NEG_INF = float("-inf")

A_HEAD_DIM = 64
B_HEAD_DIM = 128
B_HEADS = 4
BAND = 128
DILATIONS = ((128, 1), (512, 4), (2048, 16))
GDN_CHUNK = 64
CONV_WIDTH = 4
PREV_ROWS = 8
N_GROUPS = 4
GROUP_EXPERTS = 8
SUB, LANE = 8, 128
BF16_ROWS = 16
ROUTE_LANES = LANE
MOE_BLOCK = 160
MOE_CHUNK = 2048
WEIGHT_SLOTS = 3
VMEM_LIMIT = 56 * 1024 * 1024


def _cparams(*sem):
    return pltpu.CompilerParams(dimension_semantics=sem, vmem_limit_bytes=VMEM_LIMIT)


def _bdot(a, b):
    return jnp.dot(a.astype(BF16), b.astype(BF16), preferred_element_type=F32)


def _bdot_nt(a, b):
    return lax.dot_general(a.astype(BF16), b.astype(BF16), (((1,), (1,)), ((), ())),
                           preferred_element_type=F32)


def _split3(x):
    hi = x.astype(BF16)
    r1 = x - hi.astype(F32)
    mid = r1.astype(BF16)
    lo = (r1 - mid.astype(F32)).astype(BF16)
    return hi, mid, lo


def _dot_exact_lhs(a01, x):
    a = a01.astype(BF16)
    hi, mid, lo = _split3(x)
    d = lambda p: jnp.dot(a, p, preferred_element_type=F32)
    return d(hi) + d(mid) + d(lo)


def _lane_tile_rms(x):
    parts = []
    for c in range(x.shape[1] // LANE):
        xc = x[:, c * LANE:(c + 1) * LANE]
        parts.append(xc * lax.rsqrt(jnp.mean(xc * xc, axis=-1, keepdims=True) + EPS))
    return jnp.concatenate(parts, axis=1)


def _half_tile_rms(x):
    half = LANE // 2
    lo = lax.broadcasted_iota(jnp.int32, (x.shape[0], LANE), 1) < half
    parts = []
    for c in range(x.shape[1] // LANE):
        xc = x[:, c * LANE:(c + 1) * LANE]
        sq = xc * xc
        s_lo = jnp.sum(jnp.where(lo, sq, 0.0), axis=-1, keepdims=True)
        s_hi = jnp.sum(jnp.where(lo, 0.0, sq), axis=-1, keepdims=True)
        parts.append(xc * lax.rsqrt(jnp.where(lo, s_lo, s_hi) * (1.0 / half) + EPS))
    return jnp.concatenate(parts, axis=1)


def _rms(x, g):
    return x * lax.rsqrt(jnp.mean(x * x, axis=-1, keepdims=True) + EPS) * g


def _sigmoid(x):
    return 1.0 / (1.0 + jnp.exp(-x))


def _inproj_kernel(x_ref, g_ref, w_ref, gq_ref, gk_ref,
                   q_ref, k_ref, v_ref, u_ref, z_ref, ab_ref, *, aw, bw):
    nb = _rms(x_ref[...], g_ref[...]).astype(BF16)

    def proj(lo, hi):
        return jnp.dot(nb, w_ref[:, lo:hi], preferred_element_type=F32)

    def put_slabs(ref, val):
        for p in range(aw // LANE):
            ref[p] = val[:, p * LANE:(p + 1) * LANE]

    assert A_HEAD_DIM * 2 == LANE
    put_slabs(q_ref, _half_tile_rms(proj(0, aw)) * gq_ref[...])
    put_slabs(k_ref, _half_tile_rms(proj(aw, 2 * aw)) * gk_ref[...])
    put_slabs(v_ref, proj(2 * aw, 3 * aw))
    u_ref[...] = proj(3 * aw, 3 * aw + 3 * bw)
    z_ref[...] = proj(3 * aw + 3 * bw, 3 * aw + 4 * bw)
    ab_ref[...] = proj(3 * aw + 4 * bw, 3 * aw + 4 * bw + ROUTE_LANES)


def _inproj(x, g_mix, w_in_b, gq_t, gk_t, *, aw, bw, tm):
    n, d = x.shape
    cols = w_in_b.shape[1]
    row = lambda w: pl.BlockSpec((tm, w), lambda i: (i, 0))
    full = lambda a: pl.BlockSpec(a.shape, lambda i: (0, 0))
    out_w = (3 * bw, bw, ROUTE_LANES)
    slab = pl.BlockSpec((aw // LANE, tm, LANE), lambda i: (0, i, 0))
    return pl.pallas_call(
        functools.partial(_inproj_kernel, aw=aw, bw=bw),
        grid=(n // tm,),
        in_specs=[row(d), full(g_mix), pl.BlockSpec((d, cols), lambda i: (0, 0)),
                  full(gq_t), full(gk_t)],
        out_specs=[slab] * 3 + [row(w) for w in out_w],
        out_shape=[jax.ShapeDtypeStruct((aw // LANE, n, LANE), F32)] * 3
                  + [jax.ShapeDtypeStruct((n, w), F32) for w in out_w],
        compiler_params=_cparams("parallel"),
        name="inproj",
    )(x, g_mix, w_in_b, gq_t, gk_t)


def _dil_attn_kernel(q_ref, kp_ref, kc_ref, vp_ref, vc_ref, o_ref, od_ref, ld_ref):
    n = pl.program_id(1)
    rows = q_ref.shape[1]
    qi = lax.broadcasted_iota(jnp.int32, (BAND, 2 * BAND), 0)
    kj = lax.broadcasted_iota(jnp.int32, (BAND, 2 * BAND), 1)
    band = (kj >= qi) & (kj <= qi + BAND)
    bias = jnp.where(band, 0.0, NEG_INF)
    bias_first = jnp.where(band & ((kj >= BAND) | (n > 0)), 0.0, NEG_INF)
    lo_half = lax.broadcasted_iota(jnp.int32, (BAND, LANE), 1) < A_HEAD_DIM

    for di, (window, dil) in enumerate(DILATIONS):
        assert window // dil == BAND and rows % (BAND * dil) == 0
        span = BAND * dil
        take = lambda ref, st: ref[0, pl.ds(st, BAND, stride=dil), :]
        for r in range(dil):
            k_prev = take(kp_ref, r + rows - span).astype(BF16)
            v_prev = take(vp_ref, r + rows - span).astype(BF16)
            for j in range(rows // span):
                start = r + j * span
                q = take(q_ref, start)
                k_cur, v_cur = take(kc_ref, start).astype(BF16), take(vc_ref, start).astype(BF16)
                kcat = jnp.concatenate([k_prev, k_cur], axis=0)
                vcat = jnp.concatenate([v_prev, v_cur], axis=0)
                k_prev, v_prev = k_cur, v_cur
                res = []
                for half in range(2):
                    keep = lo_half if half == 0 else jnp.logical_not(lo_half)
                    s = _bdot_nt(jnp.where(keep, q, 0.0), kcat) + (bias if j else bias_first)
                    m = jnp.max(s, axis=-1, keepdims=True)
                    e = jnp.exp(s - m)
                    den = jnp.sum(e, axis=-1, keepdims=True)
                    o = jnp.dot(e.astype(BF16), vcat, preferred_element_type=F32) / den
                    res.append((o, m + jnp.log(den)))
                od_ref[di, pl.ds(start, BAND, stride=dil), :] = jnp.where(lo_half, res[0][0], res[1][0])
                ld_ref[di, pl.ds(start, BAND, stride=dil), :] = jnp.where(lo_half, res[0][1], res[1][1])

    step = 256
    for r0 in range(0, rows, step):
        ls = [ld_ref[di, r0:r0 + step, :] for di in range(len(DILATIONS))]
        mm = functools.reduce(jnp.maximum, ls)
        es = [jnp.exp(l - mm) for l in ls]
        num = functools.reduce(lambda a, b: a + b, [e * od_ref[di, r0:r0 + step, :] for di, e in enumerate(es)])
        o_ref[0, r0:r0 + step, :] = num / functools.reduce(lambda a, b: a + b, es)


def _dil_attn(q, k, v, *, rows):
    pairs, s, lanes = q.shape
    assert s % rows == 0 and lanes == LANE
    cur = pl.BlockSpec((1, rows, LANE), lambda p, n: (p, n, 0))
    prev = pl.BlockSpec((1, rows, LANE), lambda p, n: (p, jnp.maximum(n - 1, 0), 0))
    return pl.pallas_call(
        _dil_attn_kernel,
        grid=(pairs, s // rows),
        in_specs=[cur, prev, cur, prev, cur],
        out_specs=cur,
        out_shape=jax.ShapeDtypeStruct(q.shape, F32),
        scratch_shapes=[pltpu.VMEM((len(DILATIONS), rows, LANE), F32)] * 2,
        compiler_params=_cparams("parallel", "arbitrary"),
        name="dil_attn",
    )(q, k, k, v, v)


def _step_attn_kernel(q_ref, kn_ref, vn_ref, kc_ref, vc_ref, ok_ref, ov_ref, oa_ref,
                      kb_ref, vb_ref, *, w_buf, t_new):
    aw = q_ref.shape[2]
    heads = aw // A_HEAD_DIM
    lanes = LANE
    kt, vt = kc_ref[0], vc_ref[0]
    lead = jnp.zeros((aw, lanes - t_new), F32)
    kn = jnp.concatenate([lead, kn_ref[0]], axis=1)
    vn = jnp.concatenate([lead, vn_ref[0]], axis=1)
    is_new = lax.broadcasted_iota(jnp.int32, (aw, lanes), 1) >= lanes - t_new
    for src, new, dst in ((kt, kn, ok_ref), (vt, vn, ov_ref)):
        rolled = pltpu.roll(src, w_buf - t_new, axis=1)
        dst[0, :, 0:w_buf - lanes] = rolled[:, 0:w_buf - lanes]
        dst[0, :, w_buf - lanes:w_buf] = jnp.where(is_new, new, rolled[:, w_buf - lanes:w_buf])
    kb_ref[...] = kt.astype(BF16)
    vb_ref[...] = vt.astype(BF16)

    rows = heads * t_new
    ri = lax.broadcasted_iota(jnp.int32, (rows, aw), 0)
    li = lax.broadcasted_iota(jnp.int32, (rows, aw), 1)
    q_rep = jnp.concatenate([q_ref[0]] * heads, axis=0)
    q_blk = jnp.where((ri // t_new) == (li // A_HEAD_DIM), q_rep, 0.0).astype(BF16)
    ncol = w_buf + lanes
    s_all = jnp.concatenate([jnp.dot(q_blk, kb_ref[...], preferred_element_type=F32),
                             jnp.dot(q_blk, kn.astype(BF16), preferred_element_type=F32)], axis=1)

    def branch(window, dil, col0):
        s = s_all[:, col0:]
        t = lax.broadcasted_iota(jnp.int32, s.shape, 0) % t_new
        cidx = lax.broadcasted_iota(jnp.int32, s.shape, 1) + col0
        r = jnp.where(cidx < w_buf, cidx, cidx - (lanes - t_new))
        diff = w_buf + t - r
        valid = (diff >= 0) & (diff <= window) & ((diff & (dil - 1)) == 0)
        valid = valid & ((cidx < w_buf) | (cidx >= ncol - t_new))
        s = jnp.where(valid, s, NEG_INF)
        m = jnp.max(s, axis=-1, keepdims=True)
        e = jnp.exp(s - m)
        den = jnp.sum(e, axis=-1, keepdims=True)
        return e, den, m + jnp.log(den)

    parts = []
    for window, dil in DILATIONS:
        assert dil & (dil - 1) == 0
        col0 = max(0, (w_buf - window) // 128 * 128)
        parts.append((col0,) + branch(window, dil, col0))
    mm = functools.reduce(jnp.maximum, [p[3] for p in parts])
    wexp = [jnp.exp(p[3] - mm) for p in parts]
    wsum = functools.reduce(lambda a, b: a + b, wexp)
    p_tot = jnp.zeros((rows, ncol), F32)
    for (col0, e, den, _), we in zip(parts, wexp):
        pe = e * (we / (wsum * den))
        if col0:
            pe = jnp.concatenate([jnp.zeros((rows, col0), F32), pe], axis=1)
        p_tot = p_tot + pe
    o = _bdot_nt(p_tot[:, :w_buf], vb_ref[...]) + _bdot_nt(p_tot[:, w_buf:], vn)
    lh = lax.broadcasted_iota(jnp.int32, (t_new, aw), 1) // A_HEAD_DIM
    acc = jnp.zeros((t_new, aw), F32)
    for h in range(heads):
        acc = acc + jnp.where(lh == h, o[h * t_new:(h + 1) * t_new, :], 0.0)
    oa_ref[0] = acc


def _step_attn(q, kn_t, vn_t, cache_kt, cache_vt):
    nseq, aw, w_buf = cache_kt.shape
    t_new = q.shape[1]
    assert w_buf % LANE == 0 and t_new % SUB == 0 and t_new <= LANE and kn_t.shape[2] == t_new
    qs = pl.BlockSpec((1, t_new, aw), lambda i: (i, 0, 0))
    new = pl.BlockSpec((1, aw, t_new), lambda i: (i, 0, 0))
    win = pl.BlockSpec((1, aw, w_buf), lambda i: (i, 0, 0))
    return pl.pallas_call(
        functools.partial(_step_attn_kernel, w_buf=w_buf, t_new=t_new),
        grid=(nseq,),
        in_specs=[qs, new, new, win, win],
        out_specs=[win, win, qs],
        out_shape=[jax.ShapeDtypeStruct(cache_kt.shape, F32), jax.ShapeDtypeStruct(cache_vt.shape, F32),
                   jax.ShapeDtypeStruct(q.shape, F32)],
        scratch_shapes=[pltpu.VMEM((aw, w_buf), BF16), pltpu.VMEM((aw, w_buf), BF16)],
        compiler_params=_cparams("parallel"),
        name="step_attn",
    )(q, kn_t, vn_t, cache_kt, cache_vt)


def _gdn_prep_kernel(u_ref, prev_ref, ab_ref, cw_ref, alog_ref, dtb_ref,
                     w_ref, uv_ref, qg_ref, kt_ref, qk_ref, last_ref, full_ref, act_ref,
                     *, seg, conv_seg, chunks):
    c = GDN_CHUNK
    dk = B_HEAD_DIM
    bw = B_HEADS * dk
    ii = lax.broadcasted_iota(jnp.int32, (c, c), 0)
    jj = lax.broadcasted_iota(jnp.int32, (c, c), 1)
    same = (ii // seg) == (jj // seg)
    incl = same & (ii >= jj)
    strict = same & (ii > jj)
    eye = ii == jj
    ones_c = jnp.ones((c, c), BF16)
    seg_cols = jnp.where((lax.broadcasted_iota(jnp.int32, (c, dk), 0) % seg)
                         == lax.broadcasted_iota(jnp.int32, (c, dk), 1), 1.0, 0.0)
    cw = cw_ref[...]
    rows = chunks * c

    for sgi in range(rows // conv_seg):
        base = sgi * (conv_seg + PREV_ROWS)
        full_ref[base:base + PREV_ROWS, :] = prev_ref[sgi]
        full_ref[base + PREV_ROWS:base + PREV_ROWS + conv_seg, :] = u_ref[sgi * conv_seg:(sgi + 1) * conv_seg, :]
        y = jnp.zeros((conv_seg, 3 * bw), F32)
        for j in range(CONV_WIDTH):
            off = base + PREV_ROWS - (CONV_WIDTH - 1) + j
            y = y + full_ref[off:off + conv_seg, :] * cw[j:j + 1, :]
        act_ref[sgi * conv_seg:(sgi + 1) * conv_seg, :] = y * _sigmoid(y)

    ri = lax.broadcasted_iota(jnp.int32, (rows, rows), 0)
    rj = lax.broadcasted_iota(jnp.int32, (rows, rows), 1)
    rsame = (ri // seg) == (rj // seg)
    ab = ab_ref[...]
    sp = ab + dtb_ref[...]
    sp = jnp.maximum(sp, 0.0) + jnp.log(1.0 + jnp.exp(-jnp.abs(sp)))
    g_all = -jnp.exp(alog_ref[...]) * sp
    gcum_all = _dot_exact_lhs(jnp.where(rsame & (ri >= rj), 1.0, 0.0), g_all)
    gtot_all = _dot_exact_lhs(jnp.where(rsame, 1.0, 0.0), g_all)
    beta_all = _sigmoid(ab)
    eye_f = jnp.where(eye, 1.0, 0.0)

    grow = []
    for ci in range(chunks):
        rs = slice(ci * c, (ci + 1) * c)
        diag = jnp.concatenate([jnp.where(eye, gcum_all[rs, h:h + 1], 0.0) for h in range(B_HEADS)], axis=1)
        grow.append(_dot_exact_lhs(ones_c, diag))

    prob = [(ci, h) for ci in range(chunks) for h in range(B_HEADS)]
    col = lambda arr, ci, lane: arr[ci * c:(ci + 1) * c, lane:lane + 1]
    gc = [col(gcum_all, ci, h) for ci, h in prob]
    gt = [col(gtot_all, ci, h) for ci, h in prob]
    beta = [col(beta_all, ci, B_HEADS + h) for ci, h in prob]
    q, k, v = [], [], []
    for ci, h in prob:
        rs = slice(ci * c, (ci + 1) * c)
        qh = act_ref[rs, h * dk:(h + 1) * dk]
        kh = act_ref[rs, bw + h * dk:bw + (h + 1) * dk]
        q.append(qh * lax.rsqrt(jnp.sum(qh * qh, axis=-1, keepdims=True) + EPS) * (dk ** -0.5))
        k.append(kh * lax.rsqrt(jnp.sum(kh * kh, axis=-1, keepdims=True) + EPS))
        v.append(act_ref[rs, 2 * bw + h * dk:2 * bw + (h + 1) * dk])
    e_incl = [jnp.exp(jnp.where(incl, gc[i] - grow[ci][:, h * c:(h + 1) * c], NEG_INF))
              for i, (ci, h) in enumerate(prob)]
    qkk = [_bdot_nt(jnp.concatenate([q[i], k[i]], axis=0), k[i]) for i in range(len(prob))]
    a = [beta[i] * jnp.where(strict, e_incl[i], 0.0) * qkk[i][c:, :] for i in range(len(prob))]
    blk = 8
    inblk = (ii // blk) == (jj // blk)
    d0 = [jnp.where(inblk, t, 0.0) for t in a]
    d2 = [_bdot(t, t) for t in d0]
    d4 = [_bdot(t, t) for t in d2]
    x = [(eye_f - t) + _bdot(eye_f - t, t2) for t, t2 in zip(d0, d2)]
    x = [t + _bdot(t, t4) for t, t4 in zip(x, d4)]
    while blk < seg:
        off_blk = ((ii // (2 * blk)) == (jj // (2 * blk))) & ((ii // blk) != (jj // blk))
        xe = [_bdot(t, jnp.where(off_blk, ta, 0.0)) for t, ta in zip(x, a)]
        x = [t - _bdot(te, t) for t, te in zip(x, xe)]
        blk *= 2
    gamma = [jnp.exp(t) for t in gc]
    wuv = [_bdot(x[i], jnp.concatenate([(beta[i] * gamma[i]) * k[i], beta[i] * v[i]], axis=1))
           for i in range(len(prob))]
    for i, (ci, h) in enumerate(prob):
        rs = slice(ci * c, (ci + 1) * c)
        hs = slice(h * dk, (h + 1) * dk)
        w_ref[rs, hs] = wuv[i][:, :dk]
        uv_ref[rs, hs] = wuv[i][:, dk:]
        qg_ref[rs, hs] = gamma[i] * q[i]
        kt_ref[rs, hs] = k[i] * jnp.exp(gt[i] - gc[i])
        qk = qkk[i][:c, :] * e_incl[i]
        qk_ref[rs, hs] = (jnp.concatenate([qk, jnp.zeros((c, dk - c), F32)], axis=1) if seg == c
                          else _bdot(qk, seg_cols))
        last_ref[rs, hs] = jnp.broadcast_to(jnp.exp(gt[i]), (c, dk))


def _gdn_prep(u_pre, prev, ab, conv_w, alog_row, dtb_row, *, seg, conv_seg, chunks):
    n, cw = u_pre.shape
    bw = cw // 3
    c = GDN_CHUNK
    rows = chunks * c
    assert n % rows == 0 and c % seg == 0 and seg % 8 == 0 and rows % conv_seg == 0 and conv_seg % seg == 0
    ncs = rows // conv_seg
    row = lambda w: pl.BlockSpec((rows, w), lambda i: (i, 0))
    full = lambda a: pl.BlockSpec(a.shape, lambda i: (0, 0))
    return pl.pallas_call(
        functools.partial(_gdn_prep_kernel, seg=seg, conv_seg=conv_seg, chunks=chunks),
        grid=(n // rows,),
        in_specs=[row(cw), pl.BlockSpec((ncs, PREV_ROWS, cw), lambda i: (i, 0, 0)),
                  row(ROUTE_LANES), full(conv_w), full(alog_row), full(dtb_row)],
        out_specs=[row(bw)] * 6,
        out_shape=[jax.ShapeDtypeStruct((n, bw), F32)] * 6,
        scratch_shapes=[pltpu.VMEM((ncs * (conv_seg + PREV_ROWS), cw), F32), pltpu.VMEM((rows, cw), F32)],
        compiler_params=_cparams("parallel"),
        name=f"gdn_prep_{seg}",
    )(u_pre, prev, ab, conv_w, alog_row, dtb_row)


def _gdn_scan_kernel(w_ref, uv_ref, qg_ref, kt_ref, qk_ref, last_ref, s0_ref, o_ref, s_ref, *, seg, steps):
    dk = B_HEAD_DIM

    @pl.when(pl.program_id(1) == 0)
    def _():
        s_ref[...] = s0_ref[...]

    rows = max(seg, BF16_ROWS)
    pad = rows - seg
    padr = lambda t: jnp.concatenate([t, jnp.zeros((pad, t.shape[1]), F32)], axis=0) if pad else t
    ii = lax.broadcasted_iota(jnp.int32, (dk, dk), 0)
    jj = lax.broadcasted_iota(jnp.int32, (dk, dk), 1)
    eye = jnp.where(ii == jj, 1.0, 0.0).astype(BF16)
    chains = [(sq, h) for sq in range(s_ref.shape[0]) for h in range(B_HEADS)]
    hs = lambda h: slice(h * dk, (h + 1) * dk)
    s = [s_ref[sq, h] for sq, h in chains]
    for c in range(steps):
        rs = [slice((sq * steps + c) * seg, (sq * steps + c + 1) * seg) for sq, _ in chains]
        sb = [t.astype(BF16) for t in s]
        ws = [_bdot(padr(w_ref[rs[i], hs(h)]), sb[i]) for i, (_, h) in enumerate(chains)]
        ub = [(padr(uv_ref[rs[i], hs(h)]) - ws[i]).astype(BF16) for i, (_, h) in enumerate(chains)]
        ktt = [_bdot_nt(eye, padr(kt_ref[rs[i], hs(h)])) for i, (_, h) in enumerate(chains)]
        s_new = [last_ref[rs[i], hs(h)][0:1, :] * s[i] + _bdot(ktt[i], ub[i]) for i, (_, h) in enumerate(chains)]
        for i, (_, h) in enumerate(chains):
            o = _bdot(padr(qg_ref[rs[i], hs(h)]), sb[i]) + _bdot(padr(qk_ref[rs[i], hs(h)])[:, :rows], ub[i])
            o_ref[rs[i], hs(h)] = o[:seg, :]
        s = s_new
    for i, (sq, h) in enumerate(chains):
        s_ref[sq, h] = s[i]


def _gdn_scan(w, uv, qg, kt, qk, last, s0, *, seg, seqs, steps):
    n, bw = w.shape
    nseq = s0.shape[0]
    per_seq = n // (nseq * seg)
    assert nseq % seqs == 0 and per_seq % steps == 0 and (seqs == 1 or steps == per_seq)
    blocks = per_seq // steps
    row = pl.BlockSpec((seqs * steps * seg, bw), lambda s, i: (s * blocks + i, 0))
    st = pl.BlockSpec((seqs,) + s0.shape[1:], lambda s, i: (s, 0, 0, 0))
    return pl.pallas_call(
        functools.partial(_gdn_scan_kernel, seg=seg, steps=steps),
        grid=(nseq // seqs, blocks),
        in_specs=[row] * 6 + [st],
        out_specs=[row, st],
        out_shape=[jax.ShapeDtypeStruct((n, bw), F32), jax.ShapeDtypeStruct(s0.shape, F32)],
        compiler_params=_cparams("parallel", "arbitrary"),
        name=f"gdn_scan_{seg}",
    )(w, uv, qg, kt, qk, last, s0)


def _mem_kv_kernel(mem_ref, g_ref, wk_ref, wv_ref, gk_ref, k_ref, v_ref):
    mn = _rms(mem_ref[...], g_ref[...]).astype(BF16)
    k = jnp.dot(mn, wk_ref[...], preferred_element_type=F32)
    k_ref[...] = _lane_tile_rms(k) * gk_ref[...]
    v_ref[...] = jnp.dot(mn, wv_ref[...], preferred_element_type=F32)


def _mem_kv(mem, g_mem, w_xk_b, w_xv_b, gk_t):
    m = mem.shape[0]
    xw = w_xk_b.shape[1]
    return pl.pallas_call(
        _mem_kv_kernel,
        out_shape=[jax.ShapeDtypeStruct((m, xw), F32)] * 2,
        compiler_params=pltpu.CompilerParams(vmem_limit_bytes=VMEM_LIMIT),
        name="mem_kv",
    )(mem, g_mem, w_xk_b, w_xv_b, gk_t)


def _route(logits):
    lane = lax.broadcasted_iota(jnp.int32, logits.shape, 1)
    lane_f = lane.astype(F32)
    big = float(ROUTE_LANES)
    lg = jnp.where(lane < N_GROUPS, logits, NEG_INF)
    mg = jnp.max(lg, axis=-1, keepdims=True)
    zg = jnp.sum(jnp.exp(lg - mg), axis=-1, keepdims=True)
    pg_top = 1.0 / zg
    gidx = jnp.min(jnp.where(lg == mg, lane_f, big), axis=-1, keepdims=True)
    e_lo = N_GROUPS + GROUP_EXPERTS * gidx
    emask = (lane_f >= e_lo) & (lane_f < e_lo + GROUP_EXPERTS)
    le = jnp.where(emask, logits, NEG_INF)
    me = jnp.max(le, axis=-1, keepdims=True)
    ee = jnp.exp(le - me)
    pe = ee / jnp.sum(ee, axis=-1, keepdims=True)
    pe = jnp.where(emask, pe, -1.0)
    p1 = jnp.max(pe, axis=-1, keepdims=True)
    i1 = jnp.min(jnp.where(pe == p1, lane_f, big), axis=-1, keepdims=True)
    pe2 = jnp.where(lane_f == i1, -1.0, pe)
    p2 = jnp.max(pe2, axis=-1, keepdims=True)
    i2 = jnp.min(jnp.where(pe2 == p2, lane_f, big), axis=-1, keepdims=True)
    den = p1 + p2
    w1 = pg_top * p1 / den
    w2 = pg_top * p2 / den
    out = jnp.where(lane == 0, i1 - N_GROUPS, 0.0)
    out = jnp.where(lane == 1, i2 - N_GROUPS, out)
    out = jnp.where(lane == 2, w1, out)
    return jnp.where(lane == 3, w2, out)


def _post_common(oa, og_ref, z_ref, h_ref, wout_ref, ggdn_ref, gx_ref, wxq_ref, gxq_ref):
    og = og_ref[...]
    z = z_ref[...]
    ob = _lane_tile_rms(og) * ggdn_ref[...] * (z * _sigmoid(z))
    cat = jnp.concatenate([oa, ob], axis=-1).astype(BF16)
    h1 = h_ref[...] + jnp.dot(cat, wout_ref[...], preferred_element_type=F32)
    hx = _rms(h1, gx_ref[...]).astype(BF16)
    q = jnp.dot(hx, wxq_ref[...], preferred_element_type=F32)
    qn = _lane_tile_rms(q) * gxq_ref[...]
    return h1, qn


def _mem_attend_rows(qn, mk, mv):
    outs = []
    for h in range(qn.shape[1] // B_HEAD_DIM):
        hs = slice(h * B_HEAD_DIM, (h + 1) * B_HEAD_DIM)
        s = _bdot_nt(qn[:, hs], mk[:, hs])
        m = jnp.max(s, axis=-1, keepdims=True)
        e = jnp.exp(s - m)
        p = e / jnp.sum(e, axis=-1, keepdims=True)
        outs.append(_bdot(p, mv[:, hs]))
    return jnp.concatenate(outs, axis=-1)


def _post_tail(h1, ox, wxo_ref, gffn_ref, wr_ref, br_ref, h2_ref, hn_ref, route_ref):
    h2 = h1 + jnp.dot(ox.astype(BF16), wxo_ref[...], preferred_element_type=F32)
    h2_ref[...] = h2
    hn = _rms(h2, gffn_ref[...])
    tm = hn.shape[0]
    for j in range(hn.shape[1] // LANE):
        hn_ref[pl.ds(j, tm, stride=SUB), :] = hn[:, j * LANE:(j + 1) * LANE]
    hn_hi = hn.astype(BF16)
    hn_mid = (hn - hn_hi.astype(F32)).astype(BF16)
    d = lambda a, b: jnp.dot(a, b, preferred_element_type=F32)
    logits = d(hn_hi, wr_ref[0]) + d(hn_hi, wr_ref[1]) + d(hn_mid, wr_ref[0]) + br_ref[...]
    route_ref[...] = _route(logits)


def _post_prompt_kernel(oa_ref, og_ref, z_ref, h_ref,
                        wout_ref, ggdn_ref, gx_ref, wxq_ref, gxq_ref, mk_ref, mv_ref,
                        wxo_ref, gffn_ref, wr_ref, br_ref, h2_ref, hn_ref, route_ref):
    oa = jnp.concatenate([oa_ref[p] for p in range(oa_ref.shape[0])], axis=-1)
    h1, qn = _post_common(oa, og_ref, z_ref, h_ref, wout_ref, ggdn_ref, gx_ref, wxq_ref, gxq_ref)
    ox = _mem_attend_rows(qn, mk_ref[...], mv_ref[...])
    _post_tail(h1, ox, wxo_ref, gffn_ref, wr_ref, br_ref, h2_ref, hn_ref, route_ref)


def _post_sample_kernel(oa_ref, og_ref, z_ref, h_ref,
                        wout_ref, ggdn_ref, gx_ref, wxq_ref, gxq_ref, mk_ref, mv_ref,
                        wxo_ref, gffn_ref, wr_ref, br_ref, h2_ref, hn_ref, route_ref, *, t_new):
    h1, qn = _post_common(oa_ref[...], og_ref, z_ref, h_ref, wout_ref, ggdn_ref, gx_ref, wxq_ref, gxq_ref)
    dh = B_HEAD_DIM
    heads = qn.shape[1] // dh
    rows = heads * t_new
    ncol = mk_ref.shape[1]
    own = ((lax.broadcasted_iota(jnp.int32, (rows, ncol), 0) // t_new)
           == (lax.broadcasted_iota(jnp.int32, (rows, ncol), 1) % heads))
    outs = []
    for sq in range(mk_ref.shape[0]):
        qs = qn[sq * t_new:(sq + 1) * t_new, :]
        q_rows = jnp.concatenate([qs[:, h * dh:(h + 1) * dh] for h in range(heads)], axis=0)
        s = jnp.where(own, _bdot_nt(q_rows, mk_ref[sq]), NEG_INF)
        m = jnp.max(s, axis=-1, keepdims=True)
        e = jnp.exp(s - m)
        o = _bdot(e / jnp.sum(e, axis=-1, keepdims=True), mv_ref[sq])
        outs.append(jnp.concatenate([o[h * t_new:(h + 1) * t_new, :] for h in range(heads)], axis=1))
    ox = jnp.concatenate(outs, axis=0)
    _post_tail(h1, ox, wxo_ref, gffn_ref, wr_ref, br_ref, h2_ref, hn_ref, route_ref)


def _post_weights_specs(weights):
    return [pl.BlockSpec(a.shape, lambda i, nd=a.ndim: (0,) * nd) for a in weights]


def _post_outs(n, d, tm):
    assert d == SUB * LANE
    row = lambda w: pl.BlockSpec((tm, w), lambda i: (i, 0))
    specs = [row(d), pl.BlockSpec((tm * SUB, LANE), lambda i: (i, 0)), row(ROUTE_LANES)]
    shapes = [jax.ShapeDtypeStruct((n, d), F32), jax.ShapeDtypeStruct((n * SUB, LANE), F32),
              jax.ShapeDtypeStruct((n, ROUTE_LANES), F32)]
    return specs, shapes


def _post_prompt(oa_slabs, og, z, h, pw, mk, mv, *, tm):
    n, d = h.shape
    aw = og.shape[1]
    row = lambda w: pl.BlockSpec((tm, w), lambda i: (i, 0))
    full = lambda a: pl.BlockSpec(a.shape, lambda i: (0, 0))
    slab = pl.BlockSpec((oa_slabs.shape[0], tm, LANE), lambda i: (0, i, 0))
    w1 = [pw["w_out"], pw["g_gdn"], pw["g_xattn"], pw["w_xq"], pw["g_xq"]]
    w2 = [pw["w_xo"], pw["g_ffn"], pw["w_r"], pw["b_r"]]
    specs, shapes = _post_outs(n, d, tm)
    return pl.pallas_call(
        _post_prompt_kernel,
        grid=(n // tm,),
        in_specs=[slab, row(aw), row(aw), row(d)] + _post_weights_specs(w1) + [full(mk), full(mv)]
                 + _post_weights_specs(w2),
        out_specs=specs, out_shape=shapes,
        compiler_params=_cparams("parallel"),
        name="post_prompt",
    )(oa_slabs, og, z, h, *w1, mk, mv, *w2)


def _post_sample(oa, og, z, h, pw, mk, mv, *, t_new, seqs):
    n, d = h.shape
    aw = og.shape[1]
    tm = t_new * seqs
    row = lambda w: pl.BlockSpec((tm, w), lambda i: (i, 0))
    mem = pl.BlockSpec((seqs,) + mk.shape[1:], lambda i: (i, 0, 0))
    w1 = [pw["w_out"], pw["g_gdn"], pw["g_xattn"], pw["w_xq"], pw["g_xq"]]
    w2 = [pw["w_xo"], pw["g_ffn"], pw["w_r"], pw["b_r"]]
    specs, shapes = _post_outs(n, d, tm)
    return pl.pallas_call(
        functools.partial(_post_sample_kernel, t_new=t_new),
        grid=(n // tm,),
        in_specs=[row(aw)] * 3 + [row(d)] + _post_weights_specs(w1) + [mem, mem]
                 + _post_weights_specs(w2),
        out_specs=specs, out_shape=shapes,
        compiler_params=_cparams("parallel"),
        name="post_sample",
    )(oa, og, z, h, *w1, mk, mv, *w2)


def _moe_kernel(blk_e_ref, blk_start_ref, blk_cnt_ref, tok_ref, tw_ref, x_ref, res_ref,
                wg_hbm, wu_hbm, wd_hbm, out_ref, acc_ref, xs_ref, ys_ref, wg_buf, wu_buf, wd_buf, sem,
                *, nb, total):
    c = pl.program_id(0)
    b = pl.program_id(1)
    chunk, d = res_ref.shape
    nl = d // LANE
    step = c * nb + b
    cnt = blk_cnt_ref[step]
    wslot = step % WEIGHT_SLOTS

    def weight_copies(u):
        e, s_ = blk_e_ref[u], u % WEIGHT_SLOTS
        return [pltpu.make_async_copy(hbm.at[e], buf.at[s_], sem.at[s_, i])
                for i, (hbm, buf) in enumerate(((wg_hbm, wg_buf), (wu_hbm, wu_buf), (wd_hbm, wd_buf)))]

    def start_fetch(u):
        @pl.when(blk_cnt_ref[jnp.minimum(u, total - 1)] * (u < total) > 0)
        def _():
            for cp in weight_copies(u):
                cp.start()

    @pl.when(step == 0)
    def _():
        for u in range(WEIGHT_SLOTS - 1):
            start_fetch(jnp.int32(u))

    start_fetch(step + WEIGHT_SLOTS - 1)

    @pl.when(cnt > 0)
    def _():
        for cp in weight_copies(step):
            cp.wait()
        wg_ref, wu_ref, wd_ref = wg_buf.at[wslot], wu_buf.at[wslot], wd_buf.at[wslot]
        base = blk_start_ref[step]
        for r in range(MOE_BLOCK):
            t = tok_ref[0, 0, base + r] >> 1
            xs_ref[r * SUB:(r + 1) * SUB, :] = x_ref[pl.ds(pl.multiple_of(t * SUB, SUB), SUB), :]
        xb = jnp.concatenate([xs_ref[pl.ds(j, MOE_BLOCK, stride=SUB), :] for j in range(nl)],
                             axis=1).astype(BF16)
        hg = jnp.dot(xb, wg_ref[...], preferred_element_type=F32)
        hu = jnp.dot(xb, wu_ref[...], preferred_element_type=F32)
        act = (hg * _sigmoid(hg) * hu).astype(BF16)
        y = jnp.dot(act, wd_ref[...], preferred_element_type=F32)
        for j in range(nl):
            ys_ref[pl.ds(j, MOE_BLOCK, stride=SUB), :] = y[:, j * LANE:(j + 1) * LANE]
        for r in range(MOE_BLOCK):
            e = tok_ref[0, 0, base + r]
            slot = jnp.where(r < cnt, (e & 1) * chunk + (e >> 1), 2 * chunk)
            off = pl.multiple_of(slot * SUB, SUB)
            acc_ref[pl.ds(off, SUB), :] = tw_ref[0, 0, base + r] * ys_ref[r * SUB:(r + 1) * SUB, :]

    @pl.when(b == nb - 1)
    def _():
        step = 256
        for r0 in range(0, chunk, step):
            for j in range(nl):
                out_ref[r0:r0 + step, j * LANE:(j + 1) * LANE] = (
                    res_ref[r0:r0 + step, j * LANE:(j + 1) * LANE]
                    + acc_ref[pl.ds(r0 * SUB + j, step, stride=SUB), :]
                    + acc_ref[pl.ds((chunk + r0) * SUB + j, step, stride=SUB), :])


def _dispatch(route, chunk, n_experts):
    n = route.shape[0]
    nch = n // chunk
    rows = 2 * chunk
    nb = rows // MOE_BLOCK + n_experts
    e = route[:, :2].astype(jnp.int32).reshape(nch, rows)
    w = route[:, 2:4].reshape(nch, rows)
    tok = jnp.argsort(e, axis=1, stable=True).astype(jnp.int32)
    tw = jnp.take_along_axis(w, tok, axis=1)
    ex = jnp.arange(n_experts, dtype=jnp.int32)
    counts = jnp.sum((e[:, :, None] == ex).astype(jnp.int32), axis=1)
    start = jnp.cumsum(counts, axis=1) - counts
    nblk_e = (counts + MOE_BLOCK - 1) // MOE_BLOCK
    bend = jnp.cumsum(nblk_e, axis=1)
    bstart = bend - nblk_e
    total = bend[:, -1:]
    b = jnp.arange(nb, dtype=jnp.int32)[None, :]
    bb = jnp.minimum(b, total - 1)
    eb = jnp.sum((bend[:, None, :] <= bb[:, :, None]).astype(jnp.int32), axis=2)
    sel = eb[:, :, None] == ex
    pick = lambda tbl: jnp.sum(jnp.where(sel, tbl[:, None, :], 0), axis=2)
    j = bb - pick(bstart)
    blk_start = pick(start) + j * MOE_BLOCK
    blk_cnt = jnp.where(b < total, jnp.clip(pick(counts) - j * MOE_BLOCK, 0, MOE_BLOCK), 0)
    flat = lambda t_: t_.astype(jnp.int32).reshape(-1)
    pad = lambda t_: jnp.pad(t_, ((0, 0), (0, MOE_BLOCK))).reshape(nch, 1, rows + MOE_BLOCK)
    return flat(eb), flat(blk_start), flat(blk_cnt), pad(tok), pad(tw), nb


def _moe(hn_t, h2, route, wg_b, wu_b, wd_b, *, chunk):
    n, d = h2.shape
    n_experts, _, ff = wg_b.shape
    chunk = min(chunk, n)
    assert n % chunk == 0 and d == SUB * LANE
    nch = n // chunk
    blk_e, blk_start, blk_cnt, tok, tw, nb = _dispatch(route, chunk, n_experts)
    tab = pl.BlockSpec((1, 1, tok.shape[2]), lambda c, b, *_: (c, 0, 0), memory_space=pltpu.SMEM)
    once = lambda shp: pl.BlockSpec(shp, lambda c, b, *_: (c, 0), pipeline_mode=pl.Buffered(1))
    in_hbm = pl.BlockSpec(memory_space=pl.ANY)
    grid_spec = pltpu.PrefetchScalarGridSpec(
        num_scalar_prefetch=3,
        grid=(nch, nb),
        in_specs=[tab, tab, once((chunk * SUB, LANE)), once((chunk, d)), in_hbm, in_hbm, in_hbm],
        out_specs=once((chunk, d)),
        scratch_shapes=[pltpu.VMEM(((2 * chunk + 1) * SUB, LANE), F32), pltpu.VMEM((MOE_BLOCK * SUB, LANE), F32),
                        pltpu.VMEM((MOE_BLOCK * SUB, LANE), F32),
                        pltpu.VMEM((WEIGHT_SLOTS, d, ff), BF16), pltpu.VMEM((WEIGHT_SLOTS, d, ff), BF16),
                        pltpu.VMEM((WEIGHT_SLOTS, ff, d), BF16), pltpu.SemaphoreType.DMA((WEIGHT_SLOTS, 3))],
    )
    return pl.pallas_call(
        functools.partial(_moe_kernel, nb=nb, total=nch * nb),
        grid_spec=grid_spec,
        out_shape=jax.ShapeDtypeStruct((n, d), F32),
        compiler_params=_cparams("arbitrary", "arbitrary"),
        name="moe",
    )(blk_e, blk_start, blk_cnt, tok, tw, hn_t, h2, wg_b, wu_b, wd_b)


def _tile_row(g, reps, scale=1.0):
    return (jnp.tile(g.astype(F32), reps) * scale)[None, :]


def _layer_weights(p):
    d, in_cols = p["w_in"].shape
    aw = d // 2
    bw = d - aw
    pad = 3 * aw + 4 * bw + ROUTE_LANES - in_cols
    n_experts = p["w_re"].shape[1]
    w_r = jnp.concatenate([p["w_rg"], p["w_re"],
                           jnp.zeros((d, ROUTE_LANES - N_GROUPS - n_experts), F32)], axis=1)
    b_r = jnp.concatenate([p["b_rg"], p["b_re"], jnp.zeros((ROUTE_LANES - N_GROUPS - n_experts,), F32)])
    lane_pad = lambda v: jnp.concatenate([v.astype(F32), jnp.zeros((ROUTE_LANES - v.shape[0],), F32)])[None, :]
    return dict(
        aw=aw, bw=bw,
        g_mix=p["g_mix"][None, :],
        w_in=jnp.pad(p["w_in"], ((0, 0), (0, pad))).astype(BF16),
        gq=_tile_row(p["g_qa"], aw // A_HEAD_DIM, A_HEAD_DIM ** -0.5),
        gk=_tile_row(p["g_ka"], aw // A_HEAD_DIM),
        conv_w=p["conv_w"],
        alog=lane_pad(p["a_log"]),
        dtb=lane_pad(jnp.concatenate([p["dt_bias"], jnp.zeros_like(p["dt_bias"])])),
        w_out=p["w_out"].astype(BF16),
        g_gdn=_tile_row(p["g_gdn"], bw // B_HEAD_DIM),
        g_xattn=p["g_xattn"][None, :],
        w_xq=p["w_xq"].astype(BF16),
        g_xq=_tile_row(p["g_xq"], p["w_xq"].shape[1] // B_HEAD_DIM, B_HEAD_DIM ** -0.5),
        w_xo=p["w_xo"].astype(BF16),
        g_ffn=p["g_ffn"][None, :],
        w_r=jnp.stack([w_r.astype(BF16), (w_r - w_r.astype(BF16).astype(F32)).astype(BF16)]), b_r=b_r[None, :],
        w_gate=p["w_gate"].astype(BF16), w_up=p["w_up"].astype(BF16), w_down=p["w_down"].astype(BF16),
    )


def _gdn(u_pre, prev, ab, s0, pw, *, seg, conv_seg, chunks, scan_seqs, scan_steps):
    w, uv, qg, kt, qk, last = _gdn_prep(u_pre, prev, ab, pw["conv_w"], pw["alog"], pw["dtb"],
                                        seg=seg, conv_seg=conv_seg, chunks=chunks)
    return _gdn_scan(w, uv, qg, kt, qk, last, s0, seg=seg, seqs=scan_seqs, steps=scan_steps)


def _prompt_layer(h, mem, pw, praw):
    nb_, s, d = h.shape
    assert nb_ == 1
    x = h.reshape(s, d)
    aw, bw = pw["aw"], pw["bw"]
    q, k, v, u_pre, z, ab = _inproj(x, pw["g_mix"], pw["w_in"], pw["gq"], pw["gk"],
                                    aw=aw, bw=bw, tm=512)
    oa = _dil_attn(q, k, v, rows=DILATIONS[-1][0])
    gdn_chunks = 4
    blk_rows = gdn_chunks * GDN_CHUNK
    tails = u_pre.reshape(s // blk_rows, blk_rows, 3 * bw)[:, blk_rows - PREV_ROWS:, :]
    prev = jnp.concatenate([jnp.zeros((1, PREV_ROWS, 3 * bw), F32), tails[:-1]], axis=0)
    s0 = jnp.zeros((1, B_HEADS, B_HEAD_DIM, B_HEAD_DIM), F32)
    og, s_fin = _gdn(u_pre, prev, ab, s0, pw, seg=GDN_CHUNK, conv_seg=blk_rows, chunks=gdn_chunks,
                     scan_seqs=1, scan_steps=4)
    mk, mv = _mem_kv(mem.reshape(mem.shape[1], d), praw["g_mem"][None, :], praw["w_xk"].astype(BF16),
                     praw["w_xv"].astype(BF16), _tile_row(praw["g_xk"], bw // B_HEAD_DIM))
    h2, hn, route = _post_prompt(oa, og, z, x, pw, mk, mv, tm=256)
    y = _moe(hn, h2, route, pw["w_gate"], pw["w_up"], pw["w_down"], chunk=MOE_CHUNK)
    keep = min(DILATIONS[-1][0], s)
    heads = aw // A_HEAD_DIM
    tail = lambda t_: jnp.transpose(t_[:, s - keep:, :], (1, 0, 2)).reshape(1, keep, heads, A_HEAD_DIM)
    new_k, new_v = tail(k), tail(v)
    conv_new = u_pre[s - (CONV_WIDTH - 1):].reshape(1, CONV_WIDTH - 1, 3 * bw)
    xh = mk.shape[1] // B_HEAD_DIM
    return (y.reshape(1, s, d), new_k, new_v, conv_new, s_fin,
            mk.reshape(1, -1, xh, B_HEAD_DIM), mv.reshape(1, -1, xh, B_HEAD_DIM))


def _sample_layer(h, win_k, win_v, conv_prev, s0, mem_k, mem_v, pw):
    nseq, t_new, d = h.shape
    aw, bw = pw["aw"], pw["bw"]
    n = nseq * t_new
    assert t_new == 8 and GDN_CHUNK % t_new == 0
    x = h.reshape(n, d)
    q, k, v, u_pre, z, ab = _inproj(x, pw["g_mix"], pw["w_in"], pw["gq"], pw["gk"],
                                    aw=aw, bw=bw, tm=512)
    w_buf = win_k.shape[1]
    heads = aw // A_HEAD_DIM
    win_t = lambda c_: jnp.transpose(c_, (0, 2, 3, 1)).reshape(nseq, aw, w_buf)
    new_t = lambda t_: jnp.transpose(t_.reshape(aw // LANE, nseq, t_new, LANE), (1, 0, 3, 2)).reshape(nseq, aw, t_new)
    win_back = lambda c_: jnp.transpose(c_.reshape(nseq, heads, A_HEAD_DIM, w_buf), (0, 3, 1, 2))
    q_rows = jnp.transpose(q.reshape(aw // LANE, nseq, t_new, LANE), (1, 2, 0, 3)).reshape(nseq, t_new, aw)
    new_kt, new_vt, oa = _step_attn(q_rows, new_t(k), new_t(v), win_t(win_k), win_t(win_v))
    new_k, new_v = win_back(new_kt), win_back(new_vt)
    prev = jnp.concatenate([jnp.zeros((nseq, PREV_ROWS - (CONV_WIDTH - 1), 3 * bw), F32),
                            conv_prev.astype(F32)], axis=1)
    og, s_fin = _gdn(u_pre, prev, ab, s0, pw, seg=t_new, conv_seg=t_new, chunks=4, scan_seqs=8, scan_steps=1)
    mem_rows = lambda m_: m_.reshape(nseq, -1, m_.shape[3])
    h2, hn, route = _post_sample(oa.reshape(n, aw), og, z, x, pw, mem_rows(mem_k), mem_rows(mem_v),
                                 t_new=t_new, seqs=8)
    y = _moe(hn, h2, route, pw["w_gate"], pw["w_up"], pw["w_down"], chunk=MOE_CHUNK)
    conv_new = u_pre.reshape(nseq, t_new, 3 * bw)[:, t_new - (CONV_WIDTH - 1):, :]
    return (y.reshape(nseq, t_new, d), new_k, new_v, conv_new, s_fin)


def kernel(x_prompt, x_sample, mem_prompt, cache_win_k, cache_win_v, state_conv, state_delta, cache_mem_k, cache_mem_v, g_mix, w_in, g_qa, g_ka, conv_w, a_log, dt_bias, g_gdn, w_out, g_xattn, g_mem, w_xq, w_xk, w_xv, g_xq, g_xk, w_xo, g_ffn, w_rg, b_rg, w_re, b_re, w_gate, w_up, w_down):
    depth = w_in.shape[0]
    hp, hs = x_prompt, x_sample
    outs = [[] for _ in range(10)]
    for l in range(depth):
        praw = dict(g_mix=g_mix[l], w_in=w_in[l], g_qa=g_qa[l], g_ka=g_ka[l], conv_w=conv_w[l],
                    a_log=a_log[l], dt_bias=dt_bias[l], g_gdn=g_gdn[l], w_out=w_out[l],
                    g_xattn=g_xattn[l], g_mem=g_mem[l], w_xq=w_xq[l], w_xk=w_xk[l], w_xv=w_xv[l],
                    g_xq=g_xq[l], g_xk=g_xk[l], w_xo=w_xo[l], g_ffn=g_ffn[l], w_rg=w_rg[l],
                    b_rg=b_rg[l], w_re=w_re[l], b_re=b_re[l], w_gate=w_gate[l], w_up=w_up[l],
                    w_down=w_down[l])
        pw = _layer_weights(praw)
        hp, k_p, v_p, c_p, s_p, mk, mv = _prompt_layer(hp, mem_prompt, pw, praw)
        hs, k_s, v_s, c_s, s_s = _sample_layer(hs, cache_win_k[l], cache_win_v[l], state_conv[l],
                                               state_delta[l].astype(F32), cache_mem_k[l], cache_mem_v[l], pw)
        for lst, val in zip(outs, (k_p, v_p, c_p, s_p, mk, mv, k_s, v_s, c_s, s_s)):
            lst.append(val)
    st = [jnp.stack(o) for o in outs]
    st[3] = st[3].astype(state_delta.dtype)
    st[9] = st[9].astype(state_delta.dtype)
    return (hp, hs, *st)
```

```python
import functools

import jax
import jax.numpy as jnp
from jax import lax
from jax.experimental import pallas as pl
from jax.experimental.pallas import tpu as pltpu

F32 = jnp.float32
BF16 = jnp.bfloat16
EPS = 1e-6
NEG_INF = float("-inf")

A_HEAD_DIM = 64
B_HEAD_DIM = 128
B_HEADS = 4
BAND = 128
DILATIONS = ((128, 1), (512, 4), (2048, 16))
GDN_CHUNK = 64
CONV_WIDTH = 4
PREV_ROWS = 8
N_GROUPS = 4
GROUP_EXPERTS = 8
SUB, LANE = 8, 128
BF16_ROWS = 16
ROUTE_LANES = LANE
MOE_BLOCK = 160
MOE_CHUNK = 2048
WEIGHT_SLOTS = 3
VMEM_LIMIT = 56 * 1024 * 1024


def _cparams(*sem):
    return pltpu.CompilerParams(dimension_semantics=sem, vmem_limit_bytes=VMEM_LIMIT)


def _bdot(a, b):
    return jnp.dot(a.astype(BF16), b.astype(BF16), preferred_element_type=F32)


def _bdot_nt(a, b):
    return lax.dot_general(a.astype(BF16), b.astype(BF16), (((1,), (1,)), ((), ())),
                           preferred_element_type=F32)


def _split3(x):
    hi = x.astype(BF16)
    r1 = x - hi.astype(F32)
    mid = r1.astype(BF16)
    lo = (r1 - mid.astype(F32)).astype(BF16)
    return hi, mid, lo


def _dot_exact_lhs(a01, x):
    a = a01.astype(BF16)
    hi, mid, lo = _split3(x)
    d = lambda p: jnp.dot(a, p, preferred_element_type=F32)
    return d(hi) + d(mid) + d(lo)


def _lane_tile_rms(x):
    parts = []
    for c in range(x.shape[1] // LANE):
        xc = x[:, c * LANE:(c + 1) * LANE]
        parts.append(xc * lax.rsqrt(jnp.mean(xc * xc, axis=-1, keepdims=True) + EPS))
    return jnp.concatenate(parts, axis=1)


def _half_tile_rms(x):
    half = LANE // 2
    lo = lax.broadcasted_iota(jnp.int32, (x.shape[0], LANE), 1) < half
    parts = []
    for c in range(x.shape[1] // LANE):
        xc = x[:, c * LANE:(c + 1) * LANE]
        sq = xc * xc
        s_lo = jnp.sum(jnp.where(lo, sq, 0.0), axis=-1, keepdims=True)
        s_hi = jnp.sum(jnp.where(lo, 0.0, sq), axis=-1, keepdims=True)
        parts.append(xc * lax.rsqrt(jnp.where(lo, s_lo, s_hi) * (1.0 / half) + EPS))
    return jnp.concatenate(parts, axis=1)


def _rms(x, g):
    return x * lax.rsqrt(jnp.mean(x * x, axis=-1, keepdims=True) + EPS) * g


def _sigmoid(x):
    return 1.0 / (1.0 + jnp.exp(-x))


def _inproj_kernel(x_ref, g_ref, w_ref, gq_ref, gk_ref,
                   q_ref, k_ref, v_ref, u_ref, z_ref, ab_ref, *, aw, bw):
    nb = _rms(x_ref[...], g_ref[...]).astype(BF16)

    def proj(lo, hi):
        return jnp.dot(nb, w_ref[:, lo:hi], preferred_element_type=F32)

    def put_slabs(ref, val):
        for p in range(aw // LANE):
            ref[p] = val[:, p * LANE:(p + 1) * LANE]

    assert A_HEAD_DIM * 2 == LANE
    put_slabs(q_ref, _half_tile_rms(proj(0, aw)) * gq_ref[...])
    put_slabs(k_ref, _half_tile_rms(proj(aw, 2 * aw)) * gk_ref[...])
    put_slabs(v_ref, proj(2 * aw, 3 * aw))
    u_ref[...] = proj(3 * aw, 3 * aw + 3 * bw)
    z_ref[...] = proj(3 * aw + 3 * bw, 3 * aw + 4 * bw)
    ab_ref[...] = proj(3 * aw + 4 * bw, 3 * aw + 4 * bw + ROUTE_LANES)


def _inproj(x, g_mix, w_in_b, gq_t, gk_t, *, aw, bw, tm):
    n, d = x.shape
    cols = w_in_b.shape[1]
    row = lambda w: pl.BlockSpec((tm, w), lambda i: (i, 0))
    full = lambda a: pl.BlockSpec(a.shape, lambda i: (0, 0))
    out_w = (3 * bw, bw, ROUTE_LANES)
    slab = pl.BlockSpec((aw // LANE, tm, LANE), lambda i: (0, i, 0))
    return pl.pallas_call(
        functools.partial(_inproj_kernel, aw=aw, bw=bw),
        grid=(n // tm,),
        in_specs=[row(d), full(g_mix), pl.BlockSpec((d, cols), lambda i: (0, 0)),
                  full(gq_t), full(gk_t)],
        out_specs=[slab] * 3 + [row(w) for w in out_w],
        out_shape=[jax.ShapeDtypeStruct((aw // LANE, n, LANE), F32)] * 3
                  + [jax.ShapeDtypeStruct((n, w), F32) for w in out_w],
        compiler_params=_cparams("parallel"),
        name="inproj",
    )(x, g_mix, w_in_b, gq_t, gk_t)


def _dil_attn_kernel(q_ref, kp_ref, kc_ref, vp_ref, vc_ref, o_ref, od_ref, ld_ref):
    n = pl.program_id(1)
    rows = q_ref.shape[1]
    qi = lax.broadcasted_iota(jnp.int32, (BAND, 2 * BAND), 0)
    kj = lax.broadcasted_iota(jnp.int32, (BAND, 2 * BAND), 1)
    band = (kj >= qi) & (kj <= qi + BAND)
    bias = jnp.where(band, 0.0, NEG_INF)
    bias_first = jnp.where(band & ((kj >= BAND) | (n > 0)), 0.0, NEG_INF)
    lo_half = lax.broadcasted_iota(jnp.int32, (BAND, LANE), 1) < A_HEAD_DIM

    for di, (window, dil) in enumerate(DILATIONS):
        assert window // dil == BAND and rows % (BAND * dil) == 0
        span = BAND * dil
        take = lambda ref, st: ref[0, pl.ds(st, BAND, stride=dil), :]
        for r in range(dil):
            k_prev = take(kp_ref, r + rows - span).astype(BF16)
            v_prev = take(vp_ref, r + rows - span).astype(BF16)
            for j in range(rows // span):
                start = r + j * span
                q = take(q_ref, start)
                k_cur, v_cur = take(kc_ref, start).astype(BF16), take(vc_ref, start).astype(BF16)
                kcat = jnp.concatenate([k_prev, k_cur], axis=0)
                vcat = jnp.concatenate([v_prev, v_cur], axis=0)
                k_prev, v_prev = k_cur, v_cur
                res = []
                for half in range(2):
                    keep = lo_half if half == 0 else jnp.logical_not(lo_half)
                    s = _bdot_nt(jnp.where(keep, q, 0.0), kcat) + (bias if j else bias_first)
                    m = jnp.max(s, axis=-1, keepdims=True)
                    e = jnp.exp(s - m)
                    den = jnp.sum(e, axis=-1, keepdims=True)
                    o = jnp.dot(e.astype(BF16), vcat, preferred_element_type=F32) / den
                    res.append((o, m + jnp.log(den)))
                od_ref[di, pl.ds(start, BAND, stride=dil), :] = jnp.where(lo_half, res[0][0], res[1][0])
                ld_ref[di, pl.ds(start, BAND, stride=dil), :] = jnp.where(lo_half, res[0][1], res[1][1])

    step = 256
    for r0 in range(0, rows, step):
        ls = [ld_ref[di, r0:r0 + step, :] for di in range(len(DILATIONS))]
        mm = functools.reduce(jnp.maximum, ls)
        es = [jnp.exp(l - mm) for l in ls]
        num = functools.reduce(lambda a, b: a + b, [e * od_ref[di, r0:r0 + step, :] for di, e in enumerate(es)])
        o_ref[0, r0:r0 + step, :] = num / functools.reduce(lambda a, b: a + b, es)


def _dil_attn(q, k, v, *, rows):
    pairs, s, lanes = q.shape
    assert s % rows == 0 and lanes == LANE
    cur = pl.BlockSpec((1, rows, LANE), lambda p, n: (p, n, 0))
    prev = pl.BlockSpec((1, rows, LANE), lambda p, n: (p, jnp.maximum(n - 1, 0), 0))
    return pl.pallas_call(
        _dil_attn_kernel,
        grid=(pairs, s // rows),
        in_specs=[cur, prev, cur, prev, cur],
        out_specs=cur,
        out_shape=jax.ShapeDtypeStruct(q.shape, F32),
        scratch_shapes=[pltpu.VMEM((len(DILATIONS), rows, LANE), F32)] * 2,
        compiler_params=_cparams("parallel", "arbitrary"),
        name="dil_attn",
    )(q, k, k, v, v)


def _step_attn_kernel(q_ref, kn_ref, vn_ref, kc_ref, vc_ref, ok_ref, ov_ref, oa_ref,
                      kb_ref, vb_ref, *, w_buf, t_new):
    aw = q_ref.shape[2]
    heads = aw // A_HEAD_DIM
    lanes = LANE
    kt, vt = kc_ref[0], vc_ref[0]
    lead = jnp.zeros((aw, lanes - t_new), F32)
    kn = jnp.concatenate([lead, kn_ref[0]], axis=1)
    vn = jnp.concatenate([lead, vn_ref[0]], axis=1)
    is_new = lax.broadcasted_iota(jnp.int32, (aw, lanes), 1) >= lanes - t_new
    for src, new, dst in ((kt, kn, ok_ref), (vt, vn, ov_ref)):
        rolled = pltpu.roll(src, w_buf - t_new, axis=1)
        dst[0, :, 0:w_buf - lanes] = rolled[:, 0:w_buf - lanes]
        dst[0, :, w_buf - lanes:w_buf] = jnp.where(is_new, new, rolled[:, w_buf - lanes:w_buf])
    kb_ref[...] = kt.astype(BF16)
    vb_ref[...] = vt.astype(BF16)

    rows = heads * t_new
    ri = lax.broadcasted_iota(jnp.int32, (rows, aw), 0)
    li = lax.broadcasted_iota(jnp.int32, (rows, aw), 1)
    q_rep = jnp.concatenate([q_ref[0]] * heads, axis=0)
    q_blk = jnp.where((ri // t_new) == (li // A_HEAD_DIM), q_rep, 0.0).astype(BF16)
    ncol = w_buf + lanes
    s_all = jnp.concatenate([jnp.dot(q_blk, kb_ref[...], preferred_element_type=F32),
                             jnp.dot(q_blk, kn.astype(BF16), preferred_element_type=F32)], axis=1)

    def branch(window, dil, col0):
        s = s_all[:, col0:]
        t = lax.broadcasted_iota(jnp.int32, s.shape, 0) % t_new
        cidx = lax.broadcasted_iota(jnp.int32, s.shape, 1) + col0
        r = jnp.where(cidx < w_buf, cidx, cidx - (lanes - t_new))
        diff = w_buf + t - r
        valid = (diff >= 0) & (diff <= window) & ((diff & (dil - 1)) == 0)
        valid = valid & ((cidx < w_buf) | (cidx >= ncol - t_new))
        s = jnp.where(valid, s, NEG_INF)
        m = jnp.max(s, axis=-1, keepdims=True)
        e = jnp.exp(s - m)
        den = jnp.sum(e, axis=-1, keepdims=True)
        return e, den, m + jnp.log(den)

    parts = []
    for window, dil in DILATIONS:
        assert dil & (dil - 1) == 0
        col0 = max(0, (w_buf - window) // 128 * 128)
        parts.append((col0,) + branch(window, dil, col0))
    mm = functools.reduce(jnp.maximum, [p[3] for p in parts])
    wexp = [jnp.exp(p[3] - mm) for p in parts]
    wsum = functools.reduce(lambda a, b: a + b, wexp)
    p_tot = jnp.zeros((rows, ncol), F32)
    for (col0, e, den, _), we in zip(parts, wexp):
        pe = e * (we / (wsum * den))
        if col0:
            pe = jnp.concatenate([jnp.zeros((rows, col0), F32), pe], axis=1)
        p_tot = p_tot + pe
    o = _bdot_nt(p_tot[:, :w_buf], vb_ref[...]) + _bdot_nt(p_tot[:, w_buf:], vn)
    lh = lax.broadcasted_iota(jnp.int32, (t_new, aw), 1) // A_HEAD_DIM
    acc = jnp.zeros((t_new, aw), F32)
    for h in range(heads):
        acc = acc + jnp.where(lh == h, o[h * t_new:(h + 1) * t_new, :], 0.0)
    oa_ref[0] = acc


def _step_attn(q, kn_t, vn_t, cache_kt, cache_vt):
    nseq, aw, w_buf = cache_kt.shape
    t_new = q.shape[1]
    assert w_buf % LANE == 0 and t_new % SUB == 0 and t_new <= LANE and kn_t.shape[2] == t_new
    qs = pl.BlockSpec((1, t_new, aw), lambda i: (i, 0, 0))
    new = pl.BlockSpec((1, aw, t_new), lambda i: (i, 0, 0))
    win = pl.BlockSpec((1, aw, w_buf), lambda i: (i, 0, 0))
    return pl.pallas_call(
        functools.partial(_step_attn_kernel, w_buf=w_buf, t_new=t_new),
        grid=(nseq,),
        in_specs=[qs, new, new, win, win],
        out_specs=[win, win, qs],
        out_shape=[jax.ShapeDtypeStruct(cache_kt.shape, F32), jax.ShapeDtypeStruct(cache_vt.shape, F32),
                   jax.ShapeDtypeStruct(q.shape, F32)],
        scratch_shapes=[pltpu.VMEM((aw, w_buf), BF16), pltpu.VMEM((aw, w_buf), BF16)],
        compiler_params=_cparams("parallel"),
        name="step_attn",
    )(q, kn_t, vn_t, cache_kt, cache_vt)


def _gdn_prep_kernel(u_ref, prev_ref, ab_ref, cw_ref, alog_ref, dtb_ref,
                     w_ref, uv_ref, qg_ref, kt_ref, qk_ref, last_ref, full_ref, act_ref,
                     *, seg, conv_seg, chunks):
    c = GDN_CHUNK
    dk = B_HEAD_DIM
    bw = B_HEADS * dk
    ii = lax.broadcasted_iota(jnp.int32, (c, c), 0)
    jj = lax.broadcasted_iota(jnp.int32, (c, c), 1)
    same = (ii // seg) == (jj // seg)
    incl = same & (ii >= jj)
    strict = same & (ii > jj)
    eye = ii == jj
    ones_c = jnp.ones((c, c), BF16)
    seg_cols = jnp.where((lax.broadcasted_iota(jnp.int32, (c, dk), 0) % seg)
                         == lax.broadcasted_iota(jnp.int32, (c, dk), 1), 1.0, 0.0)
    cw = cw_ref[...]
    rows = chunks * c

    for sgi in range(rows // conv_seg):
        base = sgi * (conv_seg + PREV_ROWS)
        full_ref[base:base + PREV_ROWS, :] = prev_ref[sgi]
        full_ref[base + PREV_ROWS:base + PREV_ROWS + conv_seg, :] = u_ref[sgi * conv_seg:(sgi + 1) * conv_seg, :]
        y = jnp.zeros((conv_seg, 3 * bw), F32)
        for j in range(CONV_WIDTH):
            off = base + PREV_ROWS - (CONV_WIDTH - 1) + j
            y = y + full_ref[off:off + conv_seg, :] * cw[j:j + 1, :]
        act_ref[sgi * conv_seg:(sgi + 1) * conv_seg, :] = y * _sigmoid(y)

    ri = lax.broadcasted_iota(jnp.int32, (rows, rows), 0)
    rj = lax.broadcasted_iota(jnp.int32, (rows, rows), 1)
    rsame = (ri // seg) == (rj // seg)
    ab = ab_ref[...]
    sp = ab + dtb_ref[...]
    sp = jnp.maximum(sp, 0.0) + jnp.log(1.0 + jnp.exp(-jnp.abs(sp)))
    g_all = -jnp.exp(alog_ref[...]) * sp
    gcum_all = _dot_exact_lhs(jnp.where(rsame & (ri >= rj), 1.0, 0.0), g_all)
    gtot_all = _dot_exact_lhs(jnp.where(rsame, 1.0, 0.0), g_all)
    beta_all = _sigmoid(ab)
    eye_f = jnp.where(eye, 1.0, 0.0)

    grow = []
    for ci in range(chunks):
        rs = slice(ci * c, (ci + 1) * c)
        diag = jnp.concatenate([jnp.where(eye, gcum_all[rs, h:h + 1], 0.0) for h in range(B_HEADS)], axis=1)
        grow.append(_dot_exact_lhs(ones_c, diag))

    prob = [(ci, h) for ci in range(chunks) for h in range(B_HEADS)]
    col = lambda arr, ci, lane: arr[ci * c:(ci + 1) * c, lane:lane + 1]
    gc = [col(gcum_all, ci, h) for ci, h in prob]
    gt = [col(gtot_all, ci, h) for ci, h in prob]
    beta = [col(beta_all, ci, B_HEADS + h) for ci, h in prob]
    q, k, v = [], [], []
    for ci, h in prob:
        rs = slice(ci * c, (ci + 1) * c)
        qh = act_ref[rs, h * dk:(h + 1) * dk]
        kh = act_ref[rs, bw + h * dk:bw + (h + 1) * dk]
        q.append(qh * lax.rsqrt(jnp.sum(qh * qh, axis=-1, keepdims=True) + EPS) * (dk ** -0.5))
        k.append(kh * lax.rsqrt(jnp.sum(kh * kh, axis=-1, keepdims=True) + EPS))
        v.append(act_ref[rs, 2 * bw + h * dk:2 * bw + (h + 1) * dk])
    e_incl = [jnp.exp(jnp.where(incl, gc[i] - grow[ci][:, h * c:(h + 1) * c], NEG_INF))
              for i, (ci, h) in enumerate(prob)]
    qkk = [_bdot_nt(jnp.concatenate([q[i], k[i]], axis=0), k[i]) for i in range(len(prob))]
    a = [beta[i] * jnp.where(strict, e_incl[i], 0.0) * qkk[i][c:, :] for i in range(len(prob))]
    blk = 8
    inblk = (ii // blk) == (jj // blk)
    d0 = [jnp.where(inblk, t, 0.0) for t in a]
    d2 = [_bdot(t, t) for t in d0]
    d4 = [_bdot(t, t) for t in d2]
    x = [(eye_f - t) + _bdot(eye_f - t, t2) for t, t2 in zip(d0, d2)]
    x = [t + _bdot(t, t4) for t, t4 in zip(x, d4)]
    while blk < seg:
        off_blk = ((ii // (2 * blk)) == (jj // (2 * blk))) & ((ii // blk) != (jj // blk))
        xe = [_bdot(t, jnp.where(off_blk, ta, 0.0)) for t, ta in zip(x, a)]
        x = [t - _bdot(te, t) for t, te in zip(x, xe)]
        blk *= 2
    gamma = [jnp.exp(t) for t in gc]
    wuv = [_bdot(x[i], jnp.concatenate([(beta[i] * gamma[i]) * k[i], beta[i] * v[i]], axis=1))
           for i in range(len(prob))]
    for i, (ci, h) in enumerate(prob):
        rs = slice(ci * c, (ci + 1) * c)
        hs = slice(h * dk, (h + 1) * dk)
        w_ref[rs, hs] = wuv[i][:, :dk]
        uv_ref[rs, hs] = wuv[i][:, dk:]
        qg_ref[rs, hs] = gamma[i] * q[i]
        kt_ref[rs, hs] = k[i] * jnp.exp(gt[i] - gc[i])
        qk = qkk[i][:c, :] * e_incl[i]
        qk_ref[rs, hs] = (jnp.concatenate([qk, jnp.zeros((c, dk - c), F32)], axis=1) if seg == c
                          else _bdot(qk, seg_cols))
        last_ref[rs, hs] = jnp.broadcast_to(jnp.exp(gt[i]), (c, dk))


def _gdn_prep(u_pre, prev, ab, conv_w, alog_row, dtb_row, *, seg, conv_seg, chunks):
    n, cw = u_pre.shape
    bw = cw // 3
    c = GDN_CHUNK
    rows = chunks * c
    assert n % rows == 0 and c % seg == 0 and seg % 8 == 0 and rows % conv_seg == 0 and conv_seg % seg == 0
    ncs = rows // conv_seg
    row = lambda w: pl.BlockSpec((rows, w), lambda i: (i, 0))
    full = lambda a: pl.BlockSpec(a.shape, lambda i: (0, 0))
    return pl.pallas_call(
        functools.partial(_gdn_prep_kernel, seg=seg, conv_seg=conv_seg, chunks=chunks),
        grid=(n // rows,),
        in_specs=[row(cw), pl.BlockSpec((ncs, PREV_ROWS, cw), lambda i: (i, 0, 0)),
                  row(ROUTE_LANES), full(conv_w), full(alog_row), full(dtb_row)],
        out_specs=[row(bw)] * 6,
        out_shape=[jax.ShapeDtypeStruct((n, bw), F32)] * 6,
        scratch_shapes=[pltpu.VMEM((ncs * (conv_seg + PREV_ROWS), cw), F32), pltpu.VMEM((rows, cw), F32)],
        compiler_params=_cparams("parallel"),
        name=f"gdn_prep_{seg}",
    )(u_pre, prev, ab, conv_w, alog_row, dtb_row)


def _gdn_scan_kernel(w_ref, uv_ref, qg_ref, kt_ref, qk_ref, last_ref, s0_ref, o_ref, s_ref, *, seg, steps):
    dk = B_HEAD_DIM

    @pl.when(pl.program_id(1) == 0)
    def _():
        s_ref[...] = s0_ref[...]

    rows = max(seg, BF16_ROWS)
    pad = rows - seg
    padr = lambda t: jnp.concatenate([t, jnp.zeros((pad, t.shape[1]), F32)], axis=0) if pad else t
    ii = lax.broadcasted_iota(jnp.int32, (dk, dk), 0)
    jj = lax.broadcasted_iota(jnp.int32, (dk, dk), 1)
    eye = jnp.where(ii == jj, 1.0, 0.0).astype(BF16)
    chains = [(sq, h) for sq in range(s_ref.shape[0]) for h in range(B_HEADS)]
    hs = lambda h: slice(h * dk, (h + 1) * dk)
    s = [s_ref[sq, h] for sq, h in chains]
    for c in range(steps):
        rs = [slice((sq * steps + c) * seg, (sq * steps + c + 1) * seg) for sq, _ in chains]
        sb = [t.astype(BF16) for t in s]
        ws = [_bdot(padr(w_ref[rs[i], hs(h)]), sb[i]) for i, (_, h) in enumerate(chains)]
        ub = [(padr(uv_ref[rs[i], hs(h)]) - ws[i]).astype(BF16) for i, (_, h) in enumerate(chains)]
        ktt = [_bdot_nt(eye, padr(kt_ref[rs[i], hs(h)])) for i, (_, h) in enumerate(chains)]
        s_new = [last_ref[rs[i], hs(h)][0:1, :] * s[i] + _bdot(ktt[i], ub[i]) for i, (_, h) in enumerate(chains)]
        for i, (_, h) in enumerate(chains):
            o = _bdot(padr(qg_ref[rs[i], hs(h)]), sb[i]) + _bdot(padr(qk_ref[rs[i], hs(h)])[:, :rows], ub[i])
            o_ref[rs[i], hs(h)] = o[:seg, :]
        s = s_new
    for i, (sq, h) in enumerate(chains):
        s_ref[sq, h] = s[i]


def _gdn_scan(w, uv, qg, kt, qk, last, s0, *, seg, seqs, steps):
    n, bw = w.shape
    nseq = s0.shape[0]
    per_seq = n // (nseq * seg)
    assert nseq % seqs == 0 and per_seq % steps == 0 and (seqs == 1 or steps == per_seq)
    blocks = per_seq // steps
    row = pl.BlockSpec((seqs * steps * seg, bw), lambda s, i: (s * blocks + i, 0))
    st = pl.BlockSpec((seqs,) + s0.shape[1:], lambda s, i: (s, 0, 0, 0))
    return pl.pallas_call(
        functools.partial(_gdn_scan_kernel, seg=seg, steps=steps),
        grid=(nseq // seqs, blocks),
        in_specs=[row] * 6 + [st],
        out_specs=[row, st],
        out_shape=[jax.ShapeDtypeStruct((n, bw), F32), jax.ShapeDtypeStruct(s0.shape, F32)],
        compiler_params=_cparams("parallel", "arbitrary"),
        name=f"gdn_scan_{seg}",
    )(w, uv, qg, kt, qk, last, s0)


def _mem_kv_kernel(mem_ref, g_ref, wk_ref, wv_ref, gk_ref, k_ref, v_ref):
    mn = _rms(mem_ref[...], g_ref[...]).astype(BF16)
    k = jnp.dot(mn, wk_ref[...], preferred_element_type=F32)
    k_ref[...] = _lane_tile_rms(k) * gk_ref[...]
    v_ref[...] = jnp.dot(mn, wv_ref[...], preferred_element_type=F32)


def _mem_kv(mem, g_mem, w_xk_b, w_xv_b, gk_t):
    m = mem.shape[0]
    xw = w_xk_b.shape[1]
    return pl.pallas_call(
        _mem_kv_kernel,
        out_shape=[jax.ShapeDtypeStruct((m, xw), F32)] * 2,
        compiler_params=pltpu.CompilerParams(vmem_limit_bytes=VMEM_LIMIT),
        name="mem_kv",
    )(mem, g_mem, w_xk_b, w_xv_b, gk_t)


def _route(logits):
    lane = lax.broadcasted_iota(jnp.int32, logits.shape, 1)
    lane_f = lane.astype(F32)
    big = float(ROUTE_LANES)
    lg = jnp.where(lane < N_GROUPS, logits, NEG_INF)
    mg = jnp.max(lg, axis=-1, keepdims=True)
    zg = jnp.sum(jnp.exp(lg - mg), axis=-1, keepdims=True)
    pg_top = 1.0 / zg
    gidx = jnp.min(jnp.where(lg == mg, lane_f, big), axis=-1, keepdims=True)
    e_lo = N_GROUPS + GROUP_EXPERTS * gidx
    emask = (lane_f >= e_lo) & (lane_f < e_lo + GROUP_EXPERTS)
    le = jnp.where(emask, logits, NEG_INF)
    me = jnp.max(le, axis=-1, keepdims=True)
    ee = jnp.exp(le - me)
    pe = ee / jnp.sum(ee, axis=-1, keepdims=True)
    pe = jnp.where(emask, pe, -1.0)
    p1 = jnp.max(pe, axis=-1, keepdims=True)
    i1 = jnp.min(jnp.where(pe == p1, lane_f, big), axis=-1, keepdims=True)
    pe2 = jnp.where(lane_f == i1, -1.0, pe)
    p2 = jnp.max(pe2, axis=-1, keepdims=True)
    i2 = jnp.min(jnp.where(pe2 == p2, lane_f, big), axis=-1, keepdims=True)
    den = p1 + p2
    w1 = pg_top * p1 / den
    w2 = pg_top * p2 / den
    out = jnp.where(lane == 0, i1 - N_GROUPS, 0.0)
    out = jnp.where(lane == 1, i2 - N_GROUPS, out)
    out = jnp.where(lane == 2, w1, out)
    return jnp.where(lane == 3, w2, out)


def _post_common(oa, og_ref, z_ref, h_ref, wout_ref, ggdn_ref, gx_ref, wxq_ref, gxq_ref):
    og = og_ref[...]
    z = z_ref[...]
    ob = _lane_tile_rms(og) * ggdn_ref[...] * (z * _sigmoid(z))
    cat = jnp.concatenate([oa, ob], axis=-1).astype(BF16)
    h1 = h_ref[...] + jnp.dot(cat, wout_ref[...], preferred_element_type=F32)
    hx = _rms(h1, gx_ref[...]).astype(BF16)
    q = jnp.dot(hx, wxq_ref[...], preferred_element_type=F32)
    qn = _lane_tile_rms(q) * gxq_ref[...]
    return h1, qn


def _mem_attend_rows(qn, mk, mv):
    outs = []
    for h in range(qn.shape[1] // B_HEAD_DIM):
        hs = slice(h * B_HEAD_DIM, (h + 1) * B_HEAD_DIM)
        s = _bdot_nt(qn[:, hs], mk[:, hs])
        m = jnp.max(s, axis=-1, keepdims=True)
        e = jnp.exp(s - m)
        p = e / jnp.sum(e, axis=-1, keepdims=True)
        outs.append(_bdot(p, mv[:, hs]))
    return jnp.concatenate(outs, axis=-1)


def _post_tail(h1, ox, wxo_ref, gffn_ref, wr_ref, br_ref, h2_ref, hn_ref, route_ref):
    h2 = h1 + jnp.dot(ox.astype(BF16), wxo_ref[...], preferred_element_type=F32)
    h2_ref[...] = h2
    hn = _rms(h2, gffn_ref[...])
    tm = hn.shape[0]
    for j in range(hn.shape[1] // LANE):
        hn_ref[pl.ds(j, tm, stride=SUB), :] = hn[:, j * LANE:(j + 1) * LANE]
    hn_hi = hn.astype(BF16)
    hn_mid = (hn - hn_hi.astype(F32)).astype(BF16)
    d = lambda a, b: jnp.dot(a, b, preferred_element_type=F32)
    logits = d(hn_hi, wr_ref[0]) + d(hn_hi, wr_ref[1]) + d(hn_mid, wr_ref[0]) + br_ref[...]
    route_ref[...] = _route(logits)


def _post_prompt_kernel(oa_ref, og_ref, z_ref, h_ref,
                        wout_ref, ggdn_ref, gx_ref, wxq_ref, gxq_ref, mk_ref, mv_ref,
                        wxo_ref, gffn_ref, wr_ref, br_ref, h2_ref, hn_ref, route_ref):
    oa = jnp.concatenate([oa_ref[p] for p in range(oa_ref.shape[0])], axis=-1)
    h1, qn = _post_common(oa, og_ref, z_ref, h_ref, wout_ref, ggdn_ref, gx_ref, wxq_ref, gxq_ref)
    ox = _mem_attend_rows(qn, mk_ref[...], mv_ref[...])
    _post_tail(h1, ox, wxo_ref, gffn_ref, wr_ref, br_ref, h2_ref, hn_ref, route_ref)


def _post_sample_kernel(oa_ref, og_ref, z_ref, h_ref,
                        wout_ref, ggdn_ref, gx_ref, wxq_ref, gxq_ref, mk_ref, mv_ref,
                        wxo_ref, gffn_ref, wr_ref, br_ref, h2_ref, hn_ref, route_ref, *, t_new):
    h1, qn = _post_common(oa_ref[...], og_ref, z_ref, h_ref, wout_ref, ggdn_ref, gx_ref, wxq_ref, gxq_ref)
    dh = B_HEAD_DIM
    heads = qn.shape[1] // dh
    rows = heads * t_new
    ncol = mk_ref.shape[1]
    own = ((lax.broadcasted_iota(jnp.int32, (rows, ncol), 0) // t_new)
           == (lax.broadcasted_iota(jnp.int32, (rows, ncol), 1) % heads))
    outs = []
    for sq in range(mk_ref.shape[0]):
        qs = qn[sq * t_new:(sq + 1) * t_new, :]
        q_rows = jnp.concatenate([qs[:, h * dh:(h + 1) * dh] for h in range(heads)], axis=0)
        s = jnp.where(own, _bdot_nt(q_rows, mk_ref[sq]), NEG_INF)
        m = jnp.max(s, axis=-1, keepdims=True)
        e = jnp.exp(s - m)
        o = _bdot(e / jnp.sum(e, axis=-1, keepdims=True), mv_ref[sq])
        outs.append(jnp.concatenate([o[h * t_new:(h + 1) * t_new, :] for h in range(heads)], axis=1))
    ox = jnp.concatenate(outs, axis=0)
    _post_tail(h1, ox, wxo_ref, gffn_ref, wr_ref, br_ref, h2_ref, hn_ref, route_ref)


def _post_weights_specs(weights):
    return [pl.BlockSpec(a.shape, lambda i, nd=a.ndim: (0,) * nd) for a in weights]


def _post_outs(n, d, tm):
    assert d == SUB * LANE
    row = lambda w: pl.BlockSpec((tm, w), lambda i: (i, 0))
    specs = [row(d), pl.BlockSpec((tm * SUB, LANE), lambda i: (i, 0)), row(ROUTE_LANES)]
    shapes = [jax.ShapeDtypeStruct((n, d), F32), jax.ShapeDtypeStruct((n * SUB, LANE), F32),
              jax.ShapeDtypeStruct((n, ROUTE_LANES), F32)]
    return specs, shapes


def _post_prompt(oa_slabs, og, z, h, pw, mk, mv, *, tm):
    n, d = h.shape
    aw = og.shape[1]
    row = lambda w: pl.BlockSpec((tm, w), lambda i: (i, 0))
    full = lambda a: pl.BlockSpec(a.shape, lambda i: (0, 0))
    slab = pl.BlockSpec((oa_slabs.shape[0], tm, LANE), lambda i: (0, i, 0))
    w1 = [pw["w_out"], pw["g_gdn"], pw["g_xattn"], pw["w_xq"], pw["g_xq"]]
    w2 = [pw["w_xo"], pw["g_ffn"], pw["w_r"], pw["b_r"]]
    specs, shapes = _post_outs(n, d, tm)
    return pl.pallas_call(
        _post_prompt_kernel,
        grid=(n // tm,),
        in_specs=[slab, row(aw), row(aw), row(d)] + _post_weights_specs(w1) + [full(mk), full(mv)]
                 + _post_weights_specs(w2),
        out_specs=specs, out_shape=shapes,
        compiler_params=_cparams("parallel"),
        name="post_prompt",
    )(oa_slabs, og, z, h, *w1, mk, mv, *w2)


def _post_sample(oa, og, z, h, pw, mk, mv, *, t_new, seqs):
    n, d = h.shape
    aw = og.shape[1]
    tm = t_new * seqs
    row = lambda w: pl.BlockSpec((tm, w), lambda i: (i, 0))
    mem = pl.BlockSpec((seqs,) + mk.shape[1:], lambda i: (i, 0, 0))
    w1 = [pw["w_out"], pw["g_gdn"], pw["g_xattn"], pw["w_xq"], pw["g_xq"]]
    w2 = [pw["w_xo"], pw["g_ffn"], pw["w_r"], pw["b_r"]]
    specs, shapes = _post_outs(n, d, tm)
    return pl.pallas_call(
        functools.partial(_post_sample_kernel, t_new=t_new),
        grid=(n // tm,),
        in_specs=[row(aw)] * 3 + [row(d)] + _post_weights_specs(w1) + [mem, mem]
                 + _post_weights_specs(w2),
        out_specs=specs, out_shape=shapes,
        compiler_params=_cparams("parallel"),
        name="post_sample",
    )(oa, og, z, h, *w1, mk, mv, *w2)


def _moe_kernel(blk_e_ref, blk_start_ref, blk_cnt_ref, tok_ref, tw_ref, x_ref, res_hbm,
                wg_hbm, wu_hbm, wd_hbm, out_ref, acc_ref, xs_ref, ys_ref, wg_buf, wu_buf, wd_buf, sem, res_sem,
                *, nb, total):
    c = pl.program_id(0)
    b = pl.program_id(1)
    chunk, d = out_ref.shape
    nl = d // LANE
    step = c * nb + b
    cnt = blk_cnt_ref[step]
    wslot = step % WEIGHT_SLOTS

    def weight_copies(u):
        e, s_ = blk_e_ref[u], u % WEIGHT_SLOTS
        return [pltpu.make_async_copy(hbm.at[e], buf.at[s_], sem.at[s_, i])
                for i, (hbm, buf) in enumerate(((wg_hbm, wg_buf), (wu_hbm, wu_buf), (wd_hbm, wd_buf)))]

    def start_fetch(u):
        @pl.when(blk_cnt_ref[jnp.minimum(u, total - 1)] * (u < total) > 0)
        def _():
            for cp in weight_copies(u):
                cp.start()

    res_copy = pltpu.make_async_copy(res_hbm.at[pl.ds(pl.multiple_of(c * chunk, chunk), chunk), :],
                                     out_ref, res_sem.at[0])

    @pl.when(b == 0)
    def _():
        res_copy.start()

    @pl.when(step == 0)
    def _():
        for u in range(WEIGHT_SLOTS - 1):
            start_fetch(jnp.int32(u))

    start_fetch(step + WEIGHT_SLOTS - 1)

    @pl.when(cnt > 0)
    def _():
        for cp in weight_copies(step):
            cp.wait()
        wg_ref, wu_ref, wd_ref = wg_buf.at[wslot], wu_buf.at[wslot], wd_buf.at[wslot]
        base = blk_start_ref[step]
        for r in range(MOE_BLOCK):
            t = tok_ref[0, 0, base + r] >> 1
            xs_ref[r * SUB:(r + 1) * SUB, :] = x_ref[pl.ds(pl.multiple_of(t * SUB, SUB), SUB), :]
        xb = jnp.concatenate([xs_ref[pl.ds(j, MOE_BLOCK, stride=SUB), :] for j in range(nl)],
                             axis=1).astype(BF16)
        hg = jnp.dot(xb, wg_ref[...], preferred_element_type=F32)
        hu = jnp.dot(xb, wu_ref[...], preferred_element_type=F32)
        act = (hg * _sigmoid(hg) * hu).astype(BF16)
        y = jnp.dot(act, wd_ref[...], preferred_element_type=F32)
        for j in range(nl):
            ys_ref[pl.ds(j, MOE_BLOCK, stride=SUB), :] = y[:, j * LANE:(j + 1) * LANE]
        for r in range(MOE_BLOCK):
            e = tok_ref[0, 0, base + r]
            slot = jnp.where(r < cnt, (e & 1) * chunk + (e >> 1), 2 * chunk)
            off = pl.multiple_of(slot * SUB, SUB)
            acc_ref[pl.ds(off, SUB), :] = tw_ref[0, 0, base + r] * ys_ref[r * SUB:(r + 1) * SUB, :]

    @pl.when(b == nb - 1)
    def _():
        res_copy.wait()
        rows_per = 256
        for r0 in range(0, chunk, rows_per):
            for j in range(nl):
                out_ref[r0:r0 + rows_per, j * LANE:(j + 1) * LANE] = (
                    out_ref[r0:r0 + rows_per, j * LANE:(j + 1) * LANE]
                    + acc_ref[pl.ds(r0 * SUB + j, rows_per, stride=SUB), :]
                    + acc_ref[pl.ds((chunk + r0) * SUB + j, rows_per, stride=SUB), :])


def _dispatch(route, chunk, n_experts):
    n = route.shape[0]
    nch = n // chunk
    rows = 2 * chunk
    nb = rows // MOE_BLOCK + n_experts
    e = route[:, :2].astype(jnp.int32).reshape(nch, rows)
    w = route[:, 2:4].reshape(nch, rows)
    tok = jnp.argsort(e, axis=1, stable=True).astype(jnp.int32)
    tw = jnp.take_along_axis(w, tok, axis=1)
    ex = jnp.arange(n_experts, dtype=jnp.int32)
    counts = jnp.sum((e[:, :, None] == ex).astype(jnp.int32), axis=1)
    start = jnp.cumsum(counts, axis=1) - counts
    nblk_e = (counts + MOE_BLOCK - 1) // MOE_BLOCK
    bend = jnp.cumsum(nblk_e, axis=1)
    bstart = bend - nblk_e
    total = bend[:, -1:]
    b = jnp.arange(nb, dtype=jnp.int32)[None, :]
    bb = jnp.minimum(b, total - 1)
    eb = jnp.sum((bend[:, None, :] <= bb[:, :, None]).astype(jnp.int32), axis=2)
    sel = eb[:, :, None] == ex
    pick = lambda tbl: jnp.sum(jnp.where(sel, tbl[:, None, :], 0), axis=2)
    j = bb - pick(bstart)
    blk_start = pick(start) + j * MOE_BLOCK
    blk_cnt = jnp.where(b < total, jnp.clip(pick(counts) - j * MOE_BLOCK, 0, MOE_BLOCK), 0)
    flat = lambda t_: t_.astype(jnp.int32).reshape(-1)
    pad = lambda t_: jnp.pad(t_, ((0, 0), (0, MOE_BLOCK))).reshape(nch, 1, rows + MOE_BLOCK)
    return flat(eb), flat(blk_start), flat(blk_cnt), pad(tok), pad(tw), nb


def _moe(hn_t, h2, route, wg_b, wu_b, wd_b, *, chunk):
    n, d = h2.shape
    n_experts, _, ff = wg_b.shape
    chunk = min(chunk, n)
    assert n % chunk == 0 and d == SUB * LANE
    nch = n // chunk
    blk_e, blk_start, blk_cnt, tok, tw, nb = _dispatch(route, chunk, n_experts)
    tab = pl.BlockSpec((1, 1, tok.shape[2]), lambda c, b, *_: (c, 0, 0), memory_space=pltpu.SMEM)
    once = lambda shp: pl.BlockSpec(shp, lambda c, b, *_: (c, 0), pipeline_mode=pl.Buffered(1))
    in_hbm = pl.BlockSpec(memory_space=pl.ANY)
    grid_spec = pltpu.PrefetchScalarGridSpec(
        num_scalar_prefetch=3,
        grid=(nch, nb),
        in_specs=[tab, tab, pl.BlockSpec((chunk * SUB, LANE), lambda c, b, *_: (c, 0)),
                  in_hbm, in_hbm, in_hbm, in_hbm],
        out_specs=once((chunk, d)),
        scratch_shapes=[pltpu.VMEM(((2 * chunk + 1) * SUB, LANE), F32), pltpu.VMEM((MOE_BLOCK * SUB, LANE), F32),
                        pltpu.VMEM((MOE_BLOCK * SUB, LANE), F32),
                        pltpu.VMEM((WEIGHT_SLOTS, d, ff), BF16), pltpu.VMEM((WEIGHT_SLOTS, d, ff), BF16),
                        pltpu.VMEM((WEIGHT_SLOTS, ff, d), BF16), pltpu.SemaphoreType.DMA((WEIGHT_SLOTS, 3)),
                        pltpu.SemaphoreType.DMA((1,))],
    )
    return pl.pallas_call(
        functools.partial(_moe_kernel, nb=nb, total=nch * nb),
        grid_spec=grid_spec,
        out_shape=jax.ShapeDtypeStruct((n, d), F32),
        compiler_params=_cparams("arbitrary", "arbitrary"),
        name="moe",
    )(blk_e, blk_start, blk_cnt, tok, tw, hn_t, h2, wg_b, wu_b, wd_b)


def _tile_row(g, reps, scale=1.0):
    return (jnp.tile(g.astype(F32), reps) * scale)[None, :]


def _layer_weights(p):
    d, in_cols = p["w_in"].shape
    aw = d // 2
    bw = d - aw
    pad = 3 * aw + 4 * bw + ROUTE_LANES - in_cols
    n_experts = p["w_re"].shape[1]
    w_r = jnp.concatenate([p["w_rg"], p["w_re"],
                           jnp.zeros((d, ROUTE_LANES - N_GROUPS - n_experts), F32)], axis=1)
    b_r = jnp.concatenate([p["b_rg"], p["b_re"], jnp.zeros((ROUTE_LANES - N_GROUPS - n_experts,), F32)])
    lane_pad = lambda v: jnp.concatenate([v.astype(F32), jnp.zeros((ROUTE_LANES - v.shape[0],), F32)])[None, :]
    return dict(
        aw=aw, bw=bw,
        g_mix=p["g_mix"][None, :],
        w_in=jnp.pad(p["w_in"], ((0, 0), (0, pad))).astype(BF16),
        gq=_tile_row(p["g_qa"], aw // A_HEAD_DIM, A_HEAD_DIM ** -0.5),
        gk=_tile_row(p["g_ka"], aw // A_HEAD_DIM),
        conv_w=p["conv_w"],
        alog=lane_pad(p["a_log"]),
        dtb=lane_pad(jnp.concatenate([p["dt_bias"], jnp.zeros_like(p["dt_bias"])])),
        w_out=p["w_out"].astype(BF16),
        g_gdn=_tile_row(p["g_gdn"], bw // B_HEAD_DIM),
        g_xattn=p["g_xattn"][None, :],
        w_xq=p["w_xq"].astype(BF16),
        g_xq=_tile_row(p["g_xq"], p["w_xq"].shape[1] // B_HEAD_DIM, B_HEAD_DIM ** -0.5),
        w_xo=p["w_xo"].astype(BF16),
        g_ffn=p["g_ffn"][None, :],
        w_r=jnp.stack([w_r.astype(BF16), (w_r - w_r.astype(BF16).astype(F32)).astype(BF16)]), b_r=b_r[None, :],
        w_gate=p["w_gate"].astype(BF16), w_up=p["w_up"].astype(BF16), w_down=p["w_down"].astype(BF16),
    )


def _gdn(u_pre, prev, ab, s0, pw, *, seg, conv_seg, chunks, scan_seqs, scan_steps):
    w, uv, qg, kt, qk, last = _gdn_prep(u_pre, prev, ab, pw["conv_w"], pw["alog"], pw["dtb"],
                                        seg=seg, conv_seg=conv_seg, chunks=chunks)
    return _gdn_scan(w, uv, qg, kt, qk, last, s0, seg=seg, seqs=scan_seqs, steps=scan_steps)


def _prompt_layer(h, mem, pw, praw):
    nb_, s, d = h.shape
    assert nb_ == 1
    x = h.reshape(s, d)
    aw, bw = pw["aw"], pw["bw"]
    q, k, v, u_pre, z, ab = _inproj(x, pw["g_mix"], pw["w_in"], pw["gq"], pw["gk"],
                                    aw=aw, bw=bw, tm=512)
    oa = _dil_attn(q, k, v, rows=DILATIONS[-1][0])
    gdn_chunks = 4
    blk_rows = gdn_chunks * GDN_CHUNK
    tails = u_pre.reshape(s // blk_rows, blk_rows, 3 * bw)[:, blk_rows - PREV_ROWS:, :]
    prev = jnp.concatenate([jnp.zeros((1, PREV_ROWS, 3 * bw), F32), tails[:-1]], axis=0)
    s0 = jnp.zeros((1, B_HEADS, B_HEAD_DIM, B_HEAD_DIM), F32)
    og, s_fin = _gdn(u_pre, prev, ab, s0, pw, seg=GDN_CHUNK, conv_seg=blk_rows, chunks=gdn_chunks,
                     scan_seqs=1, scan_steps=4)
    mk, mv = _mem_kv(mem.reshape(mem.shape[1], d), praw["g_mem"][None, :], praw["w_xk"].astype(BF16),
                     praw["w_xv"].astype(BF16), _tile_row(praw["g_xk"], bw // B_HEAD_DIM))
    h2, hn, route = _post_prompt(oa, og, z, x, pw, mk, mv, tm=256)
    y = _moe(hn, h2, route, pw["w_gate"], pw["w_up"], pw["w_down"], chunk=MOE_CHUNK)
    keep = min(DILATIONS[-1][0], s)
    heads = aw // A_HEAD_DIM
    tail = lambda t_: jnp.transpose(t_[:, s - keep:, :], (1, 0, 2)).reshape(1, keep, heads, A_HEAD_DIM)
    new_k, new_v = tail(k), tail(v)
    conv_new = u_pre[s - (CONV_WIDTH - 1):].reshape(1, CONV_WIDTH - 1, 3 * bw)
    xh = mk.shape[1] // B_HEAD_DIM
    return (y.reshape(1, s, d), new_k, new_v, conv_new, s_fin,
            mk.reshape(1, -1, xh, B_HEAD_DIM), mv.reshape(1, -1, xh, B_HEAD_DIM))


def _sample_layer(h, win_k, win_v, conv_prev, s0, mem_k, mem_v, pw):
    nseq, t_new, d = h.shape
    aw, bw = pw["aw"], pw["bw"]
    n = nseq * t_new
    assert t_new == 8 and GDN_CHUNK % t_new == 0
    x = h.reshape(n, d)
    q, k, v, u_pre, z, ab = _inproj(x, pw["g_mix"], pw["w_in"], pw["gq"], pw["gk"],
                                    aw=aw, bw=bw, tm=512)
    w_buf = win_k.shape[1]
    heads = aw // A_HEAD_DIM
    win_t = lambda c_: jnp.transpose(c_, (0, 2, 3, 1)).reshape(nseq, aw, w_buf)
    new_t = lambda t_: jnp.transpose(t_.reshape(aw // LANE, nseq, t_new, LANE), (1, 0, 3, 2)).reshape(nseq, aw, t_new)
    win_back = lambda c_: jnp.transpose(c_.reshape(nseq, heads, A_HEAD_DIM, w_buf), (0, 3, 1, 2))
    q_rows = jnp.transpose(q.reshape(aw // LANE, nseq, t_new, LANE), (1, 2, 0, 3)).reshape(nseq, t_new, aw)
    new_kt, new_vt, oa = _step_attn(q_rows, new_t(k), new_t(v), win_t(win_k), win_t(win_v))
    new_k, new_v = win_back(new_kt), win_back(new_vt)
    prev = jnp.concatenate([jnp.zeros((nseq, PREV_ROWS - (CONV_WIDTH - 1), 3 * bw), F32),
                            conv_prev.astype(F32)], axis=1)
    og, s_fin = _gdn(u_pre, prev, ab, s0, pw, seg=t_new, conv_seg=t_new, chunks=4, scan_seqs=8, scan_steps=1)
    mem_rows = lambda m_: m_.reshape(nseq, -1, m_.shape[3])
    h2, hn, route = _post_sample(oa.reshape(n, aw), og, z, x, pw, mem_rows(mem_k), mem_rows(mem_v),
                                 t_new=t_new, seqs=8)
    y = _moe(hn, h2, route, pw["w_gate"], pw["w_up"], pw["w_down"], chunk=MOE_CHUNK)
    conv_new = u_pre.reshape(nseq, t_new, 3 * bw)[:, t_new - (CONV_WIDTH - 1):, :]
    return (y.reshape(nseq, t_new, d), new_k, new_v, conv_new, s_fin)


def kernel(x_prompt, x_sample, mem_prompt, cache_win_k, cache_win_v, state_conv, state_delta, cache_mem_k, cache_mem_v, g_mix, w_in, g_qa, g_ka, conv_w, a_log, dt_bias, g_gdn, w_out, g_xattn, g_mem, w_xq, w_xk, w_xv, g_xq, g_xk, w_xo, g_ffn, w_rg, b_rg, w_re, b_re, w_gate, w_up, w_down):
    depth = w_in.shape[0]
    hp, hs = x_prompt, x_sample
    outs = [[] for _ in range(10)]
    for l in range(depth):
        praw = dict(g_mix=g_mix[l], w_in=w_in[l], g_qa=g_qa[l], g_ka=g_ka[l], conv_w=conv_w[l],
                    a_log=a_log[l], dt_bias=dt_bias[l], g_gdn=g_gdn[l], w_out=w_out[l],
                    g_xattn=g_xattn[l], g_mem=g_mem[l], w_xq=w_xq[l], w_xk=w_xk[l], w_xv=w_xv[l],
                    g_xq=g_xq[l], g_xk=g_xk[l], w_xo=w_xo[l], g_ffn=g_ffn[l], w_rg=w_rg[l],
                    b_rg=b_rg[l], w_re=w_re[l], b_re=b_re[l], w_gate=w_gate[l], w_up=w_up[l],
                    w_down=w_down[l])
        pw = _layer_weights(praw)
        hp, k_p, v_p, c_p, s_p, mk, mv = _prompt_layer(hp, mem_prompt, pw, praw)
        hs, k_s, v_s, c_s, s_s = _sample_layer(hs, cache_win_k[l], cache_win_v[l], state_conv[l],
                                               state_delta[l].astype(F32), cache_mem_k[l], cache_mem_v[l], pw)
        for lst, val in zip(outs, (k_p, v_p, c_p, s_p, mk, mv, k_s, v_s, c_s, s_s)):
            lst.append(val)
    st = [jnp.stack(o) for o in outs]
    st[3] = st[3].astype(state_delta.dtype)
    st[9] = st[9].astype(state_delta.dtype)
    return (hp, hs, *st)
```

```python
import functools

import jax
import jax.numpy as jnp
from jax import lax
from jax.experimental import pallas as pl
from jax.experimental.pallas import tpu as pltpu

F32 = jnp.float32
BF16 = jnp.bfloat16
EPS = 1e-6
NEG_INF = float("-inf")

A_HEAD_DIM = 64
B_HEAD_DIM = 128
B_HEADS = 4
BAND = 128
DILATIONS = ((128, 1), (512, 4), (2048, 16))
GDN_CHUNK = 64
CONV_WIDTH = 4
PREV_ROWS = 8
N_GROUPS = 4
GROUP_EXPERTS = 8
SUB, LANE = 8, 128
BF16_ROWS = 16
FAST_STRIDE = 4
ROUTE_LANES = LANE
MOE_BLOCK = 160
MOE_CHUNK = 2048
WEIGHT_SLOTS = 3
VMEM_LIMIT = 56 * 1024 * 1024


def _cparams(*sem):
    return pltpu.CompilerParams(dimension_semantics=sem, vmem_limit_bytes=VMEM_LIMIT)


def _bdot(a, b):
    return jnp.dot(a.astype(BF16), b.astype(BF16), preferred_element_type=F32)


def _bdot_nt(a, b):
    return lax.dot_general(a.astype(BF16), b.astype(BF16), (((1,), (1,)), ((), ())),
                           preferred_element_type=F32)


def _split3(x):
    hi = x.astype(BF16)
    r1 = x - hi.astype(F32)
    mid = r1.astype(BF16)
    lo = (r1 - mid.astype(F32)).astype(BF16)
    return hi, mid, lo


def _dot_exact_lhs(a01, x):
    a = a01.astype(BF16)
    hi, mid, lo = _split3(x)
    d = lambda p: jnp.dot(a, p, preferred_element_type=F32)
    return d(hi) + d(mid) + d(lo)


def _lane_tile_rms(x):
    parts = []
    for c in range(x.shape[1] // LANE):
        xc = x[:, c * LANE:(c + 1) * LANE]
        parts.append(xc * lax.rsqrt(jnp.mean(xc * xc, axis=-1, keepdims=True) + EPS))
    return jnp.concatenate(parts, axis=1)


def _half_tile_rms(x):
    half = LANE // 2
    lo = lax.broadcasted_iota(jnp.int32, (x.shape[0], LANE), 1) < half
    parts = []
    for c in range(x.shape[1] // LANE):
        xc = x[:, c * LANE:(c + 1) * LANE]
        sq = xc * xc
        s_lo = jnp.sum(jnp.where(lo, sq, 0.0), axis=-1, keepdims=True)
        s_hi = jnp.sum(jnp.where(lo, 0.0, sq), axis=-1, keepdims=True)
        parts.append(xc * lax.rsqrt(jnp.where(lo, s_lo, s_hi) * (1.0 / half) + EPS))
    return jnp.concatenate(parts, axis=1)


def _rms(x, g):
    return x * lax.rsqrt(jnp.mean(x * x, axis=-1, keepdims=True) + EPS) * g


def _sigmoid(x):
    return 1.0 / (1.0 + jnp.exp(-x))


def _inproj_kernel(x_ref, g_ref, w_ref, gq_ref, gk_ref,
                   q_ref, k_ref, v_ref, u_ref, z_ref, ab_ref, *, aw, bw):
    nb = _rms(x_ref[...], g_ref[...]).astype(BF16)

    def proj(lo, hi):
        return jnp.dot(nb, w_ref[:, lo:hi], preferred_element_type=F32)

    def put_slabs(ref, val):
        for p in range(aw // LANE):
            ref[p] = val[:, p * LANE:(p + 1) * LANE]

    assert A_HEAD_DIM * 2 == LANE
    put_slabs(q_ref, _half_tile_rms(proj(0, aw)) * gq_ref[...])
    put_slabs(k_ref, _half_tile_rms(proj(aw, 2 * aw)) * gk_ref[...])
    put_slabs(v_ref, proj(2 * aw, 3 * aw))
    u_ref[...] = proj(3 * aw, 3 * aw + 3 * bw)
    z_ref[...] = proj(3 * aw + 3 * bw, 3 * aw + 4 * bw)
    ab_ref[...] = proj(3 * aw + 4 * bw, 3 * aw + 4 * bw + ROUTE_LANES)


def _inproj(x, g_mix, w_in_b, gq_t, gk_t, *, aw, bw, tm):
    n, d = x.shape
    cols = w_in_b.shape[1]
    row = lambda w: pl.BlockSpec((tm, w), lambda i: (i, 0))
    full = lambda a: pl.BlockSpec(a.shape, lambda i: (0, 0))
    out_w = (3 * bw, bw, ROUTE_LANES)
    slab = pl.BlockSpec((aw // LANE, tm, LANE), lambda i: (0, i, 0))
    return pl.pallas_call(
        functools.partial(_inproj_kernel, aw=aw, bw=bw),
        grid=(n // tm,),
        in_specs=[row(d), full(g_mix), pl.BlockSpec((d, cols), lambda i: (0, 0)),
                  full(gq_t), full(gk_t)],
        out_specs=[slab] * 3 + [row(w) for w in out_w],
        out_shape=[jax.ShapeDtypeStruct((aw // LANE, n, LANE), F32)] * 3
                  + [jax.ShapeDtypeStruct((n, w), F32) for w in out_w],
        compiler_params=_cparams("parallel"),
        name="inproj",
    )(x, g_mix, w_in_b, gq_t, gk_t)


def _dil_attn_kernel(q_ref, kp_ref, kc_ref, vp_ref, vc_ref, o_ref, od_ref, ld_ref, stage_ref):
    n = pl.program_id(1)
    rows = q_ref.shape[1]
    qi = lax.broadcasted_iota(jnp.int32, (BAND, 2 * BAND), 0)
    kj = lax.broadcasted_iota(jnp.int32, (BAND, 2 * BAND), 1)
    band = (kj >= qi) & (kj <= qi + BAND)
    bias = jnp.where(band, 0.0, NEG_INF)
    bias_first = jnp.where(band & ((kj >= BAND) | (n > 0)), 0.0, NEG_INF)
    lo_half = lax.broadcasted_iota(jnp.int32, (BAND, LANE), 1) < A_HEAD_DIM

    for di, (window, dil) in enumerate(DILATIONS):
        assert window // dil == BAND and rows % (BAND * dil) == 0
        span = BAND * dil
        srcs = (q_ref, kc_ref, vc_ref, kp_ref, vp_ref)
        if dil > FAST_STRIDE:
            inner = dil // FAST_STRIDE
            assert inner <= FAST_STRIDE and dil % FAST_STRIDE == 0
            for ti, ref in enumerate(srcs):
                for a in range(FAST_STRIDE):
                    stage_ref[ti, a] = ref[0, pl.ds(a, rows // FAST_STRIDE, stride=FAST_STRIDE), :]
            take = lambda ti, r, blk: stage_ref[ti, r % FAST_STRIDE,
                                                pl.ds(r // FAST_STRIDE + blk * BAND * inner, BAND, stride=inner), :]
        else:
            take = lambda ti, r, blk: srcs[ti][0, pl.ds(r + blk * span, BAND, stride=dil), :]
        last = rows // span - 1
        for r in range(dil):
            k_prev = take(3, r, last).astype(BF16)
            v_prev = take(4, r, last).astype(BF16)
            for j in range(rows // span):
                start = r + j * span
                q = take(0, r, j)
                k_cur, v_cur = take(1, r, j).astype(BF16), take(2, r, j).astype(BF16)
                kcat = jnp.concatenate([k_prev, k_cur], axis=0)
                vcat = jnp.concatenate([v_prev, v_cur], axis=0)
                k_prev, v_prev = k_cur, v_cur
                res = []
                for half in range(2):
                    keep = lo_half if half == 0 else jnp.logical_not(lo_half)
                    s = _bdot_nt(jnp.where(keep, q, 0.0), kcat) + (bias if j else bias_first)
                    m = jnp.max(s, axis=-1, keepdims=True)
                    e = jnp.exp(s - m)
                    den = jnp.sum(e, axis=-1, keepdims=True)
                    o = jnp.dot(e.astype(BF16), vcat, preferred_element_type=F32) / den
                    res.append((o, m + jnp.log(den)))
                od_ref[di, pl.ds(start, BAND, stride=dil), :] = jnp.where(lo_half, res[0][0], res[1][0])
                ld_ref[di, pl.ds(start, BAND, stride=dil), :] = jnp.where(lo_half, res[0][1], res[1][1])

    step = 256
    for r0 in range(0, rows, step):
        ls = [ld_ref[di, r0:r0 + step, :] for di in range(len(DILATIONS))]
        mm = functools.reduce(jnp.maximum, ls)
        es = [jnp.exp(l - mm) for l in ls]
        num = functools.reduce(lambda a, b: a + b, [e * od_ref[di, r0:r0 + step, :] for di, e in enumerate(es)])
        o_ref[0, r0:r0 + step, :] = num / functools.reduce(lambda a, b: a + b, es)


def _dil_attn(q, k, v, *, rows):
    pairs, s, lanes = q.shape
    assert s % rows == 0 and lanes == LANE
    cur = pl.BlockSpec((1, rows, LANE), lambda p, n: (p, n, 0))
    prev = pl.BlockSpec((1, rows, LANE), lambda p, n: (p, jnp.maximum(n - 1, 0), 0))
    return pl.pallas_call(
        _dil_attn_kernel,
        grid=(pairs, s // rows),
        in_specs=[cur, prev, cur, prev, cur],
        out_specs=cur,
        out_shape=jax.ShapeDtypeStruct(q.shape, F32),
        scratch_shapes=[pltpu.VMEM((len(DILATIONS), rows, LANE), F32)] * 2
                       + [pltpu.VMEM((5, FAST_STRIDE, rows // FAST_STRIDE, LANE), F32)],
        compiler_params=_cparams("parallel", "arbitrary"),
        name="dil_attn",
    )(q, k, k, v, v)


def _step_attn_kernel(q_ref, kn_ref, vn_ref, kc_ref, vc_ref, ok_ref, ov_ref, oa_ref,
                      kb_ref, vb_ref, *, w_buf, t_new):
    aw = q_ref.shape[2]
    heads = aw // A_HEAD_DIM
    lanes = LANE
    kt, vt = kc_ref[0], vc_ref[0]
    lead = jnp.zeros((aw, lanes - t_new), F32)
    kn = jnp.concatenate([lead, kn_ref[0]], axis=1)
    vn = jnp.concatenate([lead, vn_ref[0]], axis=1)
    is_new = lax.broadcasted_iota(jnp.int32, (aw, lanes), 1) >= lanes - t_new
    for src, new, dst in ((kt, kn, ok_ref), (vt, vn, ov_ref)):
        rolled = pltpu.roll(src, w_buf - t_new, axis=1)
        dst[0, :, 0:w_buf - lanes] = rolled[:, 0:w_buf - lanes]
        dst[0, :, w_buf - lanes:w_buf] = jnp.where(is_new, new, rolled[:, w_buf - lanes:w_buf])
    kb_ref[...] = kt.astype(BF16)
    vb_ref[...] = vt.astype(BF16)

    rows = heads * t_new
    ri = lax.broadcasted_iota(jnp.int32, (rows, aw), 0)
    li = lax.broadcasted_iota(jnp.int32, (rows, aw), 1)
    q_rep = jnp.concatenate([q_ref[0]] * heads, axis=0)
    q_blk = jnp.where((ri // t_new) == (li // A_HEAD_DIM), q_rep, 0.0).astype(BF16)
    ncol = w_buf + lanes
    s_all = jnp.concatenate([jnp.dot(q_blk, kb_ref[...], preferred_element_type=F32),
                             jnp.dot(q_blk, kn.astype(BF16), preferred_element_type=F32)], axis=1)

    def branch(window, dil, col0):
        s = s_all[:, col0:]
        t = lax.broadcasted_iota(jnp.int32, s.shape, 0) % t_new
        cidx = lax.broadcasted_iota(jnp.int32, s.shape, 1) + col0
        r = jnp.where(cidx < w_buf, cidx, cidx - (lanes - t_new))
        diff = w_buf + t - r
        valid = (diff >= 0) & (diff <= window) & ((diff & (dil - 1)) == 0)
        valid = valid & ((cidx < w_buf) | (cidx >= ncol - t_new))
        s = jnp.where(valid, s, NEG_INF)
        m = jnp.max(s, axis=-1, keepdims=True)
        e = jnp.exp(s - m)
        den = jnp.sum(e, axis=-1, keepdims=True)
        return e, den, m + jnp.log(den)

    parts = []
    for window, dil in DILATIONS:
        assert dil & (dil - 1) == 0
        col0 = max(0, (w_buf - window) // 128 * 128)
        parts.append((col0,) + branch(window, dil, col0))
    mm = functools.reduce(jnp.maximum, [p[3] for p in parts])
    wexp = [jnp.exp(p[3] - mm) for p in parts]
    wsum = functools.reduce(lambda a, b: a + b, wexp)
    p_tot = jnp.zeros((rows, ncol), F32)
    for (col0, e, den, _), we in zip(parts, wexp):
        pe = e * (we / (wsum * den))
        if col0:
            pe = jnp.concatenate([jnp.zeros((rows, col0), F32), pe], axis=1)
        p_tot = p_tot + pe
    o = _bdot_nt(p_tot[:, :w_buf], vb_ref[...]) + _bdot_nt(p_tot[:, w_buf:], vn)
    lh = lax.broadcasted_iota(jnp.int32, (t_new, aw), 1) // A_HEAD_DIM
    acc = jnp.zeros((t_new, aw), F32)
    for h in range(heads):
        acc = acc + jnp.where(lh == h, o[h * t_new:(h + 1) * t_new, :], 0.0)
    oa_ref[0] = acc


def _step_attn(q, kn_t, vn_t, cache_kt, cache_vt):
    nseq, aw, w_buf = cache_kt.shape
    t_new = q.shape[1]
    assert w_buf % LANE == 0 and t_new % SUB == 0 and t_new <= LANE and kn_t.shape[2] == t_new
    qs = pl.BlockSpec((1, t_new, aw), lambda i: (i, 0, 0))
    new = pl.BlockSpec((1, aw, t_new), lambda i: (i, 0, 0))
    win = pl.BlockSpec((1, aw, w_buf), lambda i: (i, 0, 0))
    return pl.pallas_call(
        functools.partial(_step_attn_kernel, w_buf=w_buf, t_new=t_new),
        grid=(nseq,),
        in_specs=[qs, new, new, win, win],
        out_specs=[win, win, qs],
        out_shape=[jax.ShapeDtypeStruct(cache_kt.shape, F32), jax.ShapeDtypeStruct(cache_vt.shape, F32),
                   jax.ShapeDtypeStruct(q.shape, F32)],
        scratch_shapes=[pltpu.VMEM((aw, w_buf), BF16), pltpu.VMEM((aw, w_buf), BF16)],
        compiler_params=_cparams("parallel"),
        name="step_attn",
    )(q, kn_t, vn_t, cache_kt, cache_vt)


def _gdn_prep_kernel(u_ref, prev_ref, ab_ref, cw_ref, alog_ref, dtb_ref,
                     w_ref, uv_ref, qg_ref, kt_ref, qk_ref, last_ref, full_ref, act_ref,
                     *, seg, conv_seg, chunks):
    c = GDN_CHUNK
    dk = B_HEAD_DIM
    bw = B_HEADS * dk
    ii = lax.broadcasted_iota(jnp.int32, (c, c), 0)
    jj = lax.broadcasted_iota(jnp.int32, (c, c), 1)
    same = (ii // seg) == (jj // seg)
    incl = same & (ii >= jj)
    strict = same & (ii > jj)
    eye = ii == jj
    ones_c = jnp.ones((c, c), BF16)
    seg_cols = jnp.where((lax.broadcasted_iota(jnp.int32, (c, dk), 0) % seg)
                         == lax.broadcasted_iota(jnp.int32, (c, dk), 1), 1.0, 0.0)
    cw = cw_ref[...]
    rows = chunks * c

    for sgi in range(rows // conv_seg):
        base = sgi * (conv_seg + PREV_ROWS)
        full_ref[base:base + PREV_ROWS, :] = prev_ref[sgi]
        full_ref[base + PREV_ROWS:base + PREV_ROWS + conv_seg, :] = u_ref[sgi * conv_seg:(sgi + 1) * conv_seg, :]
        y = jnp.zeros((conv_seg, 3 * bw), F32)
        for j in range(CONV_WIDTH):
            off = base + PREV_ROWS - (CONV_WIDTH - 1) + j
            y = y + full_ref[off:off + conv_seg, :] * cw[j:j + 1, :]
        act_ref[sgi * conv_seg:(sgi + 1) * conv_seg, :] = y * _sigmoid(y)

    ri = lax.broadcasted_iota(jnp.int32, (rows, rows), 0)
    rj = lax.broadcasted_iota(jnp.int32, (rows, rows), 1)
    rsame = (ri // seg) == (rj // seg)
    ab = ab_ref[...]
    sp = ab + dtb_ref[...]
    sp = jnp.maximum(sp, 0.0) + jnp.log(1.0 + jnp.exp(-jnp.abs(sp)))
    g_all = -jnp.exp(alog_ref[...]) * sp
    gcum_all = _dot_exact_lhs(jnp.where(rsame & (ri >= rj), 1.0, 0.0), g_all)
    gtot_all = _dot_exact_lhs(jnp.where(rsame, 1.0, 0.0), g_all)
    beta_all = _sigmoid(ab)
    eye_f = jnp.where(eye, 1.0, 0.0)

    grow = []
    for ci in range(chunks):
        rs = slice(ci * c, (ci + 1) * c)
        diag = jnp.concatenate([jnp.where(eye, gcum_all[rs, h:h + 1], 0.0) for h in range(B_HEADS)], axis=1)
        grow.append(_dot_exact_lhs(ones_c, diag))

    prob = [(ci, h) for ci in range(chunks) for h in range(B_HEADS)]
    col = lambda arr, ci, lane: arr[ci * c:(ci + 1) * c, lane:lane + 1]
    gc = [col(gcum_all, ci, h) for ci, h in prob]
    gt = [col(gtot_all, ci, h) for ci, h in prob]
    beta = [col(beta_all, ci, B_HEADS + h) for ci, h in prob]
    q, k, v = [], [], []
    for ci, h in prob:
        rs = slice(ci * c, (ci + 1) * c)
        qh = act_ref[rs, h * dk:(h + 1) * dk]
        kh = act_ref[rs, bw + h * dk:bw + (h + 1) * dk]
        q.append(qh * lax.rsqrt(jnp.sum(qh * qh, axis=-1, keepdims=True) + EPS) * (dk ** -0.5))
        k.append(kh * lax.rsqrt(jnp.sum(kh * kh, axis=-1, keepdims=True) + EPS))
        v.append(act_ref[rs, 2 * bw + h * dk:2 * bw + (h + 1) * dk])
    e_incl = [jnp.exp(jnp.where(incl, gc[i] - grow[ci][:, h * c:(h + 1) * c], NEG_INF))
              for i, (ci, h) in enumerate(prob)]
    qkk = [_bdot_nt(jnp.concatenate([q[i], k[i]], axis=0), k[i]) for i in range(len(prob))]
    a = [beta[i] * jnp.where(strict, e_incl[i], 0.0) * qkk[i][c:, :] for i in range(len(prob))]
    blk = 8
    inblk = (ii // blk) == (jj // blk)
    d0 = [jnp.where(inblk, t, 0.0) for t in a]
    d2 = [_bdot(t, t) for t in d0]
    d4 = [_bdot(t, t) for t in d2]
    x = [(eye_f - t) + _bdot(eye_f - t, t2) for t, t2 in zip(d0, d2)]
    x = [t + _bdot(t, t4) for t, t4 in zip(x, d4)]
    while blk < seg:
        off_blk = ((ii // (2 * blk)) == (jj // (2 * blk))) & ((ii // blk) != (jj // blk))
        xe = [_bdot(t, jnp.where(off_blk, ta, 0.0)) for t, ta in zip(x, a)]
        x = [t - _bdot(te, t) for t, te in zip(x, xe)]
        blk *= 2
    gamma = [jnp.exp(t) for t in gc]
    wuv = [_bdot(x[i], jnp.concatenate([(beta[i] * gamma[i]) * k[i], beta[i] * v[i]], axis=1))
           for i in range(len(prob))]
    for i, (ci, h) in enumerate(prob):
        rs = slice(ci * c, (ci + 1) * c)
        hs = slice(h * dk, (h + 1) * dk)
        w_ref[rs, hs] = wuv[i][:, :dk]
        uv_ref[rs, hs] = wuv[i][:, dk:]
        qg_ref[rs, hs] = gamma[i] * q[i]
        kt_ref[rs, hs] = k[i] * jnp.exp(gt[i] - gc[i])
        qk = qkk[i][:c, :] * e_incl[i]
        qk_ref[rs, hs] = (jnp.concatenate([qk, jnp.zeros((c, dk - c), F32)], axis=1) if seg == c
                          else _bdot(qk, seg_cols))
        last_ref[rs, hs] = jnp.broadcast_to(jnp.exp(gt[i]), (c, dk))


def _gdn_prep(u_pre, prev, ab, conv_w, alog_row, dtb_row, *, seg, conv_seg, chunks):
    n, cw = u_pre.shape
    bw = cw // 3
    c = GDN_CHUNK
    rows = chunks * c
    assert n % rows == 0 and c % seg == 0 and seg % 8 == 0 and rows % conv_seg == 0 and conv_seg % seg == 0
    ncs = rows // conv_seg
    row = lambda w: pl.BlockSpec((rows, w), lambda i: (i, 0))
    full = lambda a: pl.BlockSpec(a.shape, lambda i: (0, 0))
    return pl.pallas_call(
        functools.partial(_gdn_prep_kernel, seg=seg, conv_seg=conv_seg, chunks=chunks),
        grid=(n // rows,),
        in_specs=[row(cw), pl.BlockSpec((ncs, PREV_ROWS, cw), lambda i: (i, 0, 0)),
                  row(ROUTE_LANES), full(conv_w), full(alog_row), full(dtb_row)],
        out_specs=[row(bw)] * 6,
        out_shape=[jax.ShapeDtypeStruct((n, bw), F32)] * 6,
        scratch_shapes=[pltpu.VMEM((ncs * (conv_seg + PREV_ROWS), cw), F32), pltpu.VMEM((rows, cw), F32)],
        compiler_params=_cparams("parallel"),
        name=f"gdn_prep_{seg}",
    )(u_pre, prev, ab, conv_w, alog_row, dtb_row)


def _gdn_scan_kernel(w_ref, uv_ref, qg_ref, kt_ref, qk_ref, last_ref, s0_ref, o_ref, s_ref, *, seg, steps):
    dk = B_HEAD_DIM

    @pl.when(pl.program_id(1) == 0)
    def _():
        s_ref[...] = s0_ref[...]

    rows = max(seg, BF16_ROWS)
    pad = rows - seg
    padr = lambda t: jnp.concatenate([t, jnp.zeros((pad, t.shape[1]), F32)], axis=0) if pad else t
    ii = lax.broadcasted_iota(jnp.int32, (dk, dk), 0)
    jj = lax.broadcasted_iota(jnp.int32, (dk, dk), 1)
    eye = jnp.where(ii == jj, 1.0, 0.0).astype(BF16)
    chains = [(sq, h) for sq in range(s_ref.shape[0]) for h in range(B_HEADS)]
    hs = lambda h: slice(h * dk, (h + 1) * dk)
    s = [s_ref[sq, h] for sq, h in chains]
    for c in range(steps):
        rs = [slice((sq * steps + c) * seg, (sq * steps + c + 1) * seg) for sq, _ in chains]
        sb = [t.astype(BF16) for t in s]
        ws = [_bdot(padr(w_ref[rs[i], hs(h)]), sb[i]) for i, (_, h) in enumerate(chains)]
        ub = [(padr(uv_ref[rs[i], hs(h)]) - ws[i]).astype(BF16) for i, (_, h) in enumerate(chains)]
        ktt = [_bdot_nt(eye, padr(kt_ref[rs[i], hs(h)])) for i, (_, h) in enumerate(chains)]
        s_new = [last_ref[rs[i], hs(h)][0:1, :] * s[i] + _bdot(ktt[i], ub[i]) for i, (_, h) in enumerate(chains)]
        for i, (_, h) in enumerate(chains):
            o = _bdot(padr(qg_ref[rs[i], hs(h)]), sb[i]) + _bdot(padr(qk_ref[rs[i], hs(h)])[:, :rows], ub[i])
            o_ref[rs[i], hs(h)] = o[:seg, :]
        s = s_new
    for i, (sq, h) in enumerate(chains):
        s_ref[sq, h] = s[i]


def _gdn_scan(w, uv, qg, kt, qk, last, s0, *, seg, seqs, steps):
    n, bw = w.shape
    nseq = s0.shape[0]
    per_seq = n // (nseq * seg)
    assert nseq % seqs == 0 and per_seq % steps == 0 and (seqs == 1 or steps == per_seq)
    blocks = per_seq // steps
    row = pl.BlockSpec((seqs * steps * seg, bw), lambda s, i: (s * blocks + i, 0))
    st = pl.BlockSpec((seqs,) + s0.shape[1:], lambda s, i: (s, 0, 0, 0))
    return pl.pallas_call(
        functools.partial(_gdn_scan_kernel, seg=seg, steps=steps),
        grid=(nseq // seqs, blocks),
        in_specs=[row] * 6 + [st],
        out_specs=[row, st],
        out_shape=[jax.ShapeDtypeStruct((n, bw), F32), jax.ShapeDtypeStruct(s0.shape, F32)],
        compiler_params=_cparams("parallel", "arbitrary"),
        name=f"gdn_scan_{seg}",
    )(w, uv, qg, kt, qk, last, s0)


def _mem_kv_kernel(mem_ref, g_ref, wk_ref, wv_ref, gk_ref, k_ref, v_ref):
    mn = _rms(mem_ref[...], g_ref[...]).astype(BF16)
    k = jnp.dot(mn, wk_ref[...], preferred_element_type=F32)
    k_ref[...] = _lane_tile_rms(k) * gk_ref[...]
    v_ref[...] = jnp.dot(mn, wv_ref[...], preferred_element_type=F32)


def _mem_kv(mem, g_mem, w_xk_b, w_xv_b, gk_t):
    m = mem.shape[0]
    xw = w_xk_b.shape[1]
    return pl.pallas_call(
        _mem_kv_kernel,
        out_shape=[jax.ShapeDtypeStruct((m, xw), F32)] * 2,
        compiler_params=pltpu.CompilerParams(vmem_limit_bytes=VMEM_LIMIT),
        name="mem_kv",
    )(mem, g_mem, w_xk_b, w_xv_b, gk_t)


def _route(logits):
    lane = lax.broadcasted_iota(jnp.int32, logits.shape, 1)
    lane_f = lane.astype(F32)
    big = float(ROUTE_LANES)
    lg = jnp.where(lane < N_GROUPS, logits, NEG_INF)
    mg = jnp.max(lg, axis=-1, keepdims=True)
    zg = jnp.sum(jnp.exp(lg - mg), axis=-1, keepdims=True)
    pg_top = 1.0 / zg
    gidx = jnp.min(jnp.where(lg == mg, lane_f, big), axis=-1, keepdims=True)
    e_lo = N_GROUPS + GROUP_EXPERTS * gidx
    emask = (lane_f >= e_lo) & (lane_f < e_lo + GROUP_EXPERTS)
    le = jnp.where(emask, logits, NEG_INF)
    me = jnp.max(le, axis=-1, keepdims=True)
    ee = jnp.exp(le - me)
    pe = ee / jnp.sum(ee, axis=-1, keepdims=True)
    pe = jnp.where(emask, pe, -1.0)
    p1 = jnp.max(pe, axis=-1, keepdims=True)
    i1 = jnp.min(jnp.where(pe == p1, lane_f, big), axis=-1, keepdims=True)
    pe2 = jnp.where(lane_f == i1, -1.0, pe)
    p2 = jnp.max(pe2, axis=-1, keepdims=True)
    i2 = jnp.min(jnp.where(pe2 == p2, lane_f, big), axis=-1, keepdims=True)
    den = p1 + p2
    w1 = pg_top * p1 / den
    w2 = pg_top * p2 / den
    out = jnp.where(lane == 0, i1 - N_GROUPS, 0.0)
    out = jnp.where(lane == 1, i2 - N_GROUPS, out)
    out = jnp.where(lane == 2, w1, out)
    return jnp.where(lane == 3, w2, out)


def _post_common(oa, og_ref, z_ref, h_ref, wout_ref, ggdn_ref, gx_ref, wxq_ref, gxq_ref):
    og = og_ref[...]
    z = z_ref[...]
    ob = _lane_tile_rms(og) * ggdn_ref[...] * (z * _sigmoid(z))
    cat = jnp.concatenate([oa, ob], axis=-1).astype(BF16)
    h1 = h_ref[...] + jnp.dot(cat, wout_ref[...], preferred_element_type=F32)
    hx = _rms(h1, gx_ref[...]).astype(BF16)
    q = jnp.dot(hx, wxq_ref[...], preferred_element_type=F32)
    qn = _lane_tile_rms(q) * gxq_ref[...]
    return h1, qn


def _mem_attend_rows(qn, mk, mv):
    outs = []
    for h in range(qn.shape[1] // B_HEAD_DIM):
        hs = slice(h * B_HEAD_DIM, (h + 1) * B_HEAD_DIM)
        s = _bdot_nt(qn[:, hs], mk[:, hs])
        m = jnp.max(s, axis=-1, keepdims=True)
        e = jnp.exp(s - m)
        p = e / jnp.sum(e, axis=-1, keepdims=True)
        outs.append(_bdot(p, mv[:, hs]))
    return jnp.concatenate(outs, axis=-1)


def _post_tail(h1, ox, wxo_ref, gffn_ref, wr_ref, br_ref, h2_ref, hn_ref, route_ref):
    h2 = h1 + jnp.dot(ox.astype(BF16), wxo_ref[...], preferred_element_type=F32)
    h2_ref[...] = h2
    hn = _rms(h2, gffn_ref[...])
    tm = hn.shape[0]
    for j in range(hn.shape[1] // LANE):
        hn_ref[pl.ds(j, tm, stride=SUB), :] = hn[:, j * LANE:(j + 1) * LANE]
    hn_hi = hn.astype(BF16)
    hn_mid = (hn - hn_hi.astype(F32)).astype(BF16)
    d = lambda a, b: jnp.dot(a, b, preferred_element_type=F32)
    logits = d(hn_hi, wr_ref[0]) + d(hn_hi, wr_ref[1]) + d(hn_mid, wr_ref[0]) + br_ref[...]
    route_ref[...] = _route(logits)


def _post_prompt_kernel(oa_ref, og_ref, z_ref, h_ref,
                        wout_ref, ggdn_ref, gx_ref, wxq_ref, gxq_ref, mk_ref, mv_ref,
                        wxo_ref, gffn_ref, wr_ref, br_ref, h2_ref, hn_ref, route_ref):
    oa = jnp.concatenate([oa_ref[p] for p in range(oa_ref.shape[0])], axis=-1)
    h1, qn = _post_common(oa, og_ref, z_ref, h_ref, wout_ref, ggdn_ref, gx_ref, wxq_ref, gxq_ref)
    ox = _mem_attend_rows(qn, mk_ref[...], mv_ref[...])
    _post_tail(h1, ox, wxo_ref, gffn_ref, wr_ref, br_ref, h2_ref, hn_ref, route_ref)


def _post_sample_kernel(oa_ref, og_ref, z_ref, h_ref,
                        wout_ref, ggdn_ref, gx_ref, wxq_ref, gxq_ref, mk_ref, mv_ref,
                        wxo_ref, gffn_ref, wr_ref, br_ref, h2_ref, hn_ref, route_ref, *, t_new):
    h1, qn = _post_common(oa_ref[...], og_ref, z_ref, h_ref, wout_ref, ggdn_ref, gx_ref, wxq_ref, gxq_ref)
    dh = B_HEAD_DIM
    heads = qn.shape[1] // dh
    rows = heads * t_new
    ncol = mk_ref.shape[1]
    own = ((lax.broadcasted_iota(jnp.int32, (rows, ncol), 0) // t_new)
           == (lax.broadcasted_iota(jnp.int32, (rows, ncol), 1) % heads))
    outs = []
    for sq in range(mk_ref.shape[0]):
        qs = qn[sq * t_new:(sq + 1) * t_new, :]
        q_rows = jnp.concatenate([qs[:, h * dh:(h + 1) * dh] for h in range(heads)], axis=0)
        s = jnp.where(own, _bdot_nt(q_rows, mk_ref[sq]), NEG_INF)
        m = jnp.max(s, axis=-1, keepdims=True)
        e = jnp.exp(s - m)
        o = _bdot(e / jnp.sum(e, axis=-1, keepdims=True), mv_ref[sq])
        outs.append(jnp.concatenate([o[h * t_new:(h + 1) * t_new, :] for h in range(heads)], axis=1))
    ox = jnp.concatenate(outs, axis=0)
    _post_tail(h1, ox, wxo_ref, gffn_ref, wr_ref, br_ref, h2_ref, hn_ref, route_ref)


def _post_weights_specs(weights):
    return [pl.BlockSpec(a.shape, lambda i, nd=a.ndim: (0,) * nd) for a in weights]


def _post_outs(n, d, tm):
    assert d == SUB * LANE
    row = lambda w: pl.BlockSpec((tm, w), lambda i: (i, 0))
    specs = [row(d), pl.BlockSpec((tm * SUB, LANE), lambda i: (i, 0)), row(ROUTE_LANES)]
    shapes = [jax.ShapeDtypeStruct((n, d), F32), jax.ShapeDtypeStruct((n * SUB, LANE), F32),
              jax.ShapeDtypeStruct((n, ROUTE_LANES), F32)]
    return specs, shapes


def _post_prompt(oa_slabs, og, z, h, pw, mk, mv, *, tm):
    n, d = h.shape
    aw = og.shape[1]
    row = lambda w: pl.BlockSpec((tm, w), lambda i: (i, 0))
    full = lambda a: pl.BlockSpec(a.shape, lambda i: (0, 0))
    slab = pl.BlockSpec((oa_slabs.shape[0], tm, LANE), lambda i: (0, i, 0))
    w1 = [pw["w_out"], pw["g_gdn"], pw["g_xattn"], pw["w_xq"], pw["g_xq"]]
    w2 = [pw["w_xo"], pw["g_ffn"], pw["w_r"], pw["b_r"]]
    specs, shapes = _post_outs(n, d, tm)
    return pl.pallas_call(
        _post_prompt_kernel,
        grid=(n // tm,),
        in_specs=[slab, row(aw), row(aw), row(d)] + _post_weights_specs(w1) + [full(mk), full(mv)]
                 + _post_weights_specs(w2),
        out_specs=specs, out_shape=shapes,
        compiler_params=_cparams("parallel"),
        name="post_prompt",
    )(oa_slabs, og, z, h, *w1, mk, mv, *w2)


def _post_sample(oa, og, z, h, pw, mk, mv, *, t_new, seqs):
    n, d = h.shape
    aw = og.shape[1]
    tm = t_new * seqs
    row = lambda w: pl.BlockSpec((tm, w), lambda i: (i, 0))
    mem = pl.BlockSpec((seqs,) + mk.shape[1:], lambda i: (i, 0, 0))
    w1 = [pw["w_out"], pw["g_gdn"], pw["g_xattn"], pw["w_xq"], pw["g_xq"]]
    w2 = [pw["w_xo"], pw["g_ffn"], pw["w_r"], pw["b_r"]]
    specs, shapes = _post_outs(n, d, tm)
    return pl.pallas_call(
        functools.partial(_post_sample_kernel, t_new=t_new),
        grid=(n // tm,),
        in_specs=[row(aw)] * 3 + [row(d)] + _post_weights_specs(w1) + [mem, mem]
                 + _post_weights_specs(w2),
        out_specs=specs, out_shape=shapes,
        compiler_params=_cparams("parallel"),
        name="post_sample",
    )(oa, og, z, h, *w1, mk, mv, *w2)


def _moe_kernel(blk_e_ref, blk_start_ref, blk_cnt_ref, tok_ref, tw_ref, x_ref, res_hbm,
                wg_hbm, wu_hbm, wd_hbm, out_ref, acc_ref, xs_ref, ys_ref, wg_buf, wu_buf, wd_buf, sem, res_sem,
                *, nb, total):
    c = pl.program_id(0)
    b = pl.program_id(1)
    chunk, d = out_ref.shape
    nl = d // LANE
    step = c * nb + b
    cnt = blk_cnt_ref[step]
    wslot = step % WEIGHT_SLOTS

    def weight_copies(u):
        e, s_ = blk_e_ref[u], u % WEIGHT_SLOTS
        return [pltpu.make_async_copy(hbm.at[e], buf.at[s_], sem.at[s_, i])
                for i, (hbm, buf) in enumerate(((wg_hbm, wg_buf), (wu_hbm, wu_buf), (wd_hbm, wd_buf)))]

    def start_fetch(u):
        @pl.when(blk_cnt_ref[jnp.minimum(u, total - 1)] * (u < total) > 0)
        def _():
            for cp in weight_copies(u):
                cp.start()

    res_copy = pltpu.make_async_copy(res_hbm.at[pl.ds(pl.multiple_of(c * chunk, chunk), chunk), :],
                                     out_ref, res_sem.at[0])

    @pl.when(b == 0)
    def _():
        res_copy.start()

    @pl.when(step == 0)
    def _():
        for u in range(WEIGHT_SLOTS - 1):
            start_fetch(jnp.int32(u))

    start_fetch(step + WEIGHT_SLOTS - 1)

    @pl.when(cnt > 0)
    def _():
        for cp in weight_copies(step):
            cp.wait()
        wg_ref, wu_ref, wd_ref = wg_buf.at[wslot], wu_buf.at[wslot], wd_buf.at[wslot]
        base = blk_start_ref[step]
        for r in range(MOE_BLOCK):
            t = tok_ref[0, 0, base + r] >> 1
            xs_ref[r * SUB:(r + 1) * SUB, :] = x_ref[pl.ds(pl.multiple_of(t * SUB, SUB), SUB), :]
        xb = jnp.concatenate([xs_ref[pl.ds(j, MOE_BLOCK, stride=SUB), :] for j in range(nl)],
                             axis=1).astype(BF16)
        hg = jnp.dot(xb, wg_ref[...], preferred_element_type=F32)
        hu = jnp.dot(xb, wu_ref[...], preferred_element_type=F32)
        act = (hg * _sigmoid(hg) * hu).astype(BF16)
        y = jnp.dot(act, wd_ref[...], preferred_element_type=F32)
        for j in range(nl):
            ys_ref[pl.ds(j, MOE_BLOCK, stride=SUB), :] = y[:, j * LANE:(j + 1) * LANE]
        for r in range(MOE_BLOCK):
            e = tok_ref[0, 0, base + r]
            slot = jnp.where(r < cnt, (e & 1) * chunk + (e >> 1), 2 * chunk)
            off = pl.multiple_of(slot * SUB, SUB)
            acc_ref[pl.ds(off, SUB), :] = tw_ref[0, 0, base + r] * ys_ref[r * SUB:(r + 1) * SUB, :]

    @pl.when(b == nb - 1)
    def _():
        res_copy.wait()
        rows_per = 256
        for r0 in range(0, chunk, rows_per):
            for j in range(nl):
                out_ref[r0:r0 + rows_per, j * LANE:(j + 1) * LANE] = (
                    out_ref[r0:r0 + rows_per, j * LANE:(j + 1) * LANE]
                    + acc_ref[pl.ds(r0 * SUB + j, rows_per, stride=SUB), :]
                    + acc_ref[pl.ds((chunk + r0) * SUB + j, rows_per, stride=SUB), :])


def _dispatch(route, chunk, n_experts):
    n = route.shape[0]
    nch = n // chunk
    rows = 2 * chunk
    nb = rows // MOE_BLOCK + n_experts
    e = route[:, :2].astype(jnp.int32).reshape(nch, rows)
    w = route[:, 2:4].reshape(nch, rows)
    tok = jnp.argsort(e, axis=1, stable=True).astype(jnp.int32)
    tw = jnp.take_along_axis(w, tok, axis=1)
    ex = jnp.arange(n_experts, dtype=jnp.int32)
    counts = jnp.sum((e[:, :, None] == ex).astype(jnp.int32), axis=1)
    start = jnp.cumsum(counts, axis=1) - counts
    nblk_e = (counts + MOE_BLOCK - 1) // MOE_BLOCK
    bend = jnp.cumsum(nblk_e, axis=1)
    bstart = bend - nblk_e
    total = bend[:, -1:]
    b = jnp.arange(nb, dtype=jnp.int32)[None, :]
    bb = jnp.minimum(b, total - 1)
    eb = jnp.sum((bend[:, None, :] <= bb[:, :, None]).astype(jnp.int32), axis=2)
    sel = eb[:, :, None] == ex
    pick = lambda tbl: jnp.sum(jnp.where(sel, tbl[:, None, :], 0), axis=2)
    j = bb - pick(bstart)
    blk_start = pick(start) + j * MOE_BLOCK
    blk_cnt = jnp.where(b < total, jnp.clip(pick(counts) - j * MOE_BLOCK, 0, MOE_BLOCK), 0)
    flat = lambda t_: t_.astype(jnp.int32).reshape(-1)
    pad = lambda t_: jnp.pad(t_, ((0, 0), (0, MOE_BLOCK))).reshape(nch, 1, rows + MOE_BLOCK)
    return flat(eb), flat(blk_start), flat(blk_cnt), pad(tok), pad(tw), nb


def _moe(hn_t, h2, route, wg_b, wu_b, wd_b, *, chunk):
    n, d = h2.shape
    n_experts, _, ff = wg_b.shape
    chunk = min(chunk, n)
    assert n % chunk == 0 and d == SUB * LANE
    nch = n // chunk
    blk_e, blk_start, blk_cnt, tok, tw, nb = _dispatch(route, chunk, n_experts)
    tab = pl.BlockSpec((1, 1, tok.shape[2]), lambda c, b, *_: (c, 0, 0), memory_space=pltpu.SMEM)
    once = lambda shp: pl.BlockSpec(shp, lambda c, b, *_: (c, 0), pipeline_mode=pl.Buffered(1))
    in_hbm = pl.BlockSpec(memory_space=pl.ANY)
    grid_spec = pltpu.PrefetchScalarGridSpec(
        num_scalar_prefetch=3,
        grid=(nch, nb),
        in_specs=[tab, tab, pl.BlockSpec((chunk * SUB, LANE), lambda c, b, *_: (c, 0)),
                  in_hbm, in_hbm, in_hbm, in_hbm],
        out_specs=once((chunk, d)),
        scratch_shapes=[pltpu.VMEM(((2 * chunk + 1) * SUB, LANE), F32), pltpu.VMEM((MOE_BLOCK * SUB, LANE), F32),
                        pltpu.VMEM((MOE_BLOCK * SUB, LANE), F32),
                        pltpu.VMEM((WEIGHT_SLOTS, d, ff), BF16), pltpu.VMEM((WEIGHT_SLOTS, d, ff), BF16),
                        pltpu.VMEM((WEIGHT_SLOTS, ff, d), BF16), pltpu.SemaphoreType.DMA((WEIGHT_SLOTS, 3)),
                        pltpu.SemaphoreType.DMA((1,))],
    )
    return pl.pallas_call(
        functools.partial(_moe_kernel, nb=nb, total=nch * nb),
        grid_spec=grid_spec,
        out_shape=jax.ShapeDtypeStruct((n, d), F32),
        compiler_params=_cparams("arbitrary", "arbitrary"),
        name="moe",
    )(blk_e, blk_start, blk_cnt, tok, tw, hn_t, h2, wg_b, wu_b, wd_b)


def _tile_row(g, reps, scale=1.0):
    return (jnp.tile(g.astype(F32), reps) * scale)[None, :]


def _layer_weights(p):
    d, in_cols = p["w_in"].shape
    aw = d // 2
    bw = d - aw
    pad = 3 * aw + 4 * bw + ROUTE_LANES - in_cols
    n_experts = p["w_re"].shape[1]
    w_r = jnp.concatenate([p["w_rg"], p["w_re"],
                           jnp.zeros((d, ROUTE_LANES - N_GROUPS - n_experts), F32)], axis=1)
    b_r = jnp.concatenate([p["b_rg"], p["b_re"], jnp.zeros((ROUTE_LANES - N_GROUPS - n_experts,), F32)])
    lane_pad = lambda v: jnp.concatenate([v.astype(F32), jnp.zeros((ROUTE_LANES - v.shape[0],), F32)])[None, :]
    return dict(
        aw=aw, bw=bw,
        g_mix=p["g_mix"][None, :],
        w_in=jnp.pad(p["w_in"], ((0, 0), (0, pad))).astype(BF16),
        gq=_tile_row(p["g_qa"], aw // A_HEAD_DIM, A_HEAD_DIM ** -0.5),
        gk=_tile_row(p["g_ka"], aw // A_HEAD_DIM),
        conv_w=p["conv_w"],
        alog=lane_pad(p["a_log"]),
        dtb=lane_pad(jnp.concatenate([p["dt_bias"], jnp.zeros_like(p["dt_bias"])])),
        w_out=p["w_out"].astype(BF16),
        g_gdn=_tile_row(p["g_gdn"], bw // B_HEAD_DIM),
        g_xattn=p["g_xattn"][None, :],
        w_xq=p["w_xq"].astype(BF16),
        g_xq=_tile_row(p["g_xq"], p["w_xq"].shape[1] // B_HEAD_DIM, B_HEAD_DIM ** -0.5),
        w_xo=p["w_xo"].astype(BF16),
        g_ffn=p["g_ffn"][None, :],
        w_r=jnp.stack([w_r.astype(BF16), (w_r - w_r.astype(BF16).astype(F32)).astype(BF16)]), b_r=b_r[None, :],
        w_gate=p["w_gate"].astype(BF16), w_up=p["w_up"].astype(BF16), w_down=p["w_down"].astype(BF16),
    )


def _gdn(u_pre, prev, ab, s0, pw, *, seg, conv_seg, chunks, scan_seqs, scan_steps):
    w, uv, qg, kt, qk, last = _gdn_prep(u_pre, prev, ab, pw["conv_w"], pw["alog"], pw["dtb"],
                                        seg=seg, conv_seg=conv_seg, chunks=chunks)
    return _gdn_scan(w, uv, qg, kt, qk, last, s0, seg=seg, seqs=scan_seqs, steps=scan_steps)


def _prompt_layer(h, mem, pw, praw):
    nb_, s, d = h.shape
    assert nb_ == 1
    x = h.reshape(s, d)
    aw, bw = pw["aw"], pw["bw"]
    q, k, v, u_pre, z, ab = _inproj(x, pw["g_mix"], pw["w_in"], pw["gq"], pw["gk"],
                                    aw=aw, bw=bw, tm=512)
    oa = _dil_attn(q, k, v, rows=DILATIONS[-1][0])
    gdn_chunks = 4
    blk_rows = gdn_chunks * GDN_CHUNK
    tails = u_pre.reshape(s // blk_rows, blk_rows, 3 * bw)[:, blk_rows - PREV_ROWS:, :]
    prev = jnp.concatenate([jnp.zeros((1, PREV_ROWS, 3 * bw), F32), tails[:-1]], axis=0)
    s0 = jnp.zeros((1, B_HEADS, B_HEAD_DIM, B_HEAD_DIM), F32)
    og, s_fin = _gdn(u_pre, prev, ab, s0, pw, seg=GDN_CHUNK, conv_seg=blk_rows, chunks=gdn_chunks,
                     scan_seqs=1, scan_steps=8)
    mk, mv = _mem_kv(mem.reshape(mem.shape[1], d), praw["g_mem"][None, :], praw["w_xk"].astype(BF16),
                     praw["w_xv"].astype(BF16), _tile_row(praw["g_xk"], bw // B_HEAD_DIM))
    h2, hn, route = _post_prompt(oa, og, z, x, pw, mk, mv, tm=256)
    y = _moe(hn, h2, route, pw["w_gate"], pw["w_up"], pw["w_down"], chunk=MOE_CHUNK)
    keep = min(DILATIONS[-1][0], s)
    heads = aw // A_HEAD_DIM
    tail = lambda t_: jnp.transpose(t_[:, s - keep:, :], (1, 0, 2)).reshape(1, keep, heads, A_HEAD_DIM)
    new_k, new_v = tail(k), tail(v)
    conv_new = u_pre[s - (CONV_WIDTH - 1):].reshape(1, CONV_WIDTH - 1, 3 * bw)
    xh = mk.shape[1] // B_HEAD_DIM
    return (y.reshape(1, s, d), new_k, new_v, conv_new, s_fin,
            mk.reshape(1, -1, xh, B_HEAD_DIM), mv.reshape(1, -1, xh, B_HEAD_DIM))


def _sample_layer(h, win_k, win_v, conv_prev, s0, mem_k, mem_v, pw):
    nseq, t_new, d = h.shape
    aw, bw = pw["aw"], pw["bw"]
    n = nseq * t_new
    assert t_new == 8 and GDN_CHUNK % t_new == 0
    x = h.reshape(n, d)
    q, k, v, u_pre, z, ab = _inproj(x, pw["g_mix"], pw["w_in"], pw["gq"], pw["gk"],
                                    aw=aw, bw=bw, tm=512)
    w_buf = win_k.shape[1]
    heads = aw // A_HEAD_DIM
    win_t = lambda c_: jnp.transpose(c_, (0, 2, 3, 1)).reshape(nseq, aw, w_buf)
    new_t = lambda t_: jnp.transpose(t_.reshape(aw // LANE, nseq, t_new, LANE), (1, 0, 3, 2)).reshape(nseq, aw, t_new)
    win_back = lambda c_: jnp.transpose(c_.reshape(nseq, heads, A_HEAD_DIM, w_buf), (0, 3, 1, 2))
    q_rows = jnp.transpose(q.reshape(aw // LANE, nseq, t_new, LANE), (1, 2, 0, 3)).reshape(nseq, t_new, aw)
    new_kt, new_vt, oa = _step_attn(q_rows, new_t(k), new_t(v), win_t(win_k), win_t(win_v))
    new_k, new_v = win_back(new_kt), win_back(new_vt)
    prev = jnp.concatenate([jnp.zeros((nseq, PREV_ROWS - (CONV_WIDTH - 1), 3 * bw), F32),
                            conv_prev.astype(F32)], axis=1)
    og, s_fin = _gdn(u_pre, prev, ab, s0, pw, seg=t_new, conv_seg=t_new, chunks=4, scan_seqs=8, scan_steps=1)
    mem_rows = lambda m_: m_.reshape(nseq, -1, m_.shape[3])
    h2, hn, route = _post_sample(oa.reshape(n, aw), og, z, x, pw, mem_rows(mem_k), mem_rows(mem_v),
                                 t_new=t_new, seqs=8)
    y = _moe(hn, h2, route, pw["w_gate"], pw["w_up"], pw["w_down"], chunk=MOE_CHUNK)
    conv_new = u_pre.reshape(nseq, t_new, 3 * bw)[:, t_new - (CONV_WIDTH - 1):, :]
    return (y.reshape(nseq, t_new, d), new_k, new_v, conv_new, s_fin)


def kernel(x_prompt, x_sample, mem_prompt, cache_win_k, cache_win_v, state_conv, state_delta, cache_mem_k, cache_mem_v, g_mix, w_in, g_qa, g_ka, conv_w, a_log, dt_bias, g_gdn, w_out, g_xattn, g_mem, w_xq, w_xk, w_xv, g_xq, g_xk, w_xo, g_ffn, w_rg, b_rg, w_re, b_re, w_gate, w_up, w_down):
    depth = w_in.shape[0]
    hp, hs = x_prompt, x_sample
    outs = [[] for _ in range(10)]
    for l in range(depth):
        praw = dict(g_mix=g_mix[l], w_in=w_in[l], g_qa=g_qa[l], g_ka=g_ka[l], conv_w=conv_w[l],
                    a_log=a_log[l], dt_bias=dt_bias[l], g_gdn=g_gdn[l], w_out=w_out[l],
                    g_xattn=g_xattn[l], g_mem=g_mem[l], w_xq=w_xq[l], w_xk=w_xk[l], w_xv=w_xv[l],
                    g_xq=g_xq[l], g_xk=g_xk[l], w_xo=w_xo[l], g_ffn=g_ffn[l], w_rg=w_rg[l],
                    b_rg=b_rg[l], w_re=w_re[l], b_re=b_re[l], w_gate=w_gate[l], w_up=w_up[l],
                    w_down=w_down[l])
        pw = _layer_weights(praw)
        hp, k_p, v_p, c_p, s_p, mk, mv = _prompt_layer(hp, mem_prompt, pw, praw)
        hs, k_s, v_s, c_s, s_s = _sample_layer(hs, cache_win_k[l], cache_win_v[l], state_conv[l],
                                               state_delta[l].astype(F32), cache_mem_k[l], cache_mem_v[l], pw)
        for lst, val in zip(outs, (k_p, v_p, c_p, s_p, mk, mv, k_s, v_s, c_s, s_s)):
            lst.append(val)
    st = [jnp.stack(o) for o in outs]
    st[3] = st[3].astype(state_delta.dtype)
    st[9] = st[9].astype(state_delta.dtype)
    return (hp, hs, *st)
```

```python
import functools

import jax
import jax.numpy as jnp
from jax import lax
from jax.experimental import pallas as pl
from jax.experimental.pallas import tpu as pltpu

F32 = jnp.float32
BF16 = jnp.bfloat16
EPS = 1e-6
NEG_INF = float("-inf")

A_HEAD_DIM = 64
B_HEAD_DIM = 128
B_HEADS = 4
BAND = 128
DILATIONS = ((128, 1), (512, 4), (2048, 16))
GDN_CHUNK = 64
CONV_WIDTH = 4
PREV_ROWS = 8
N_GROUPS = 4
GROUP_EXPERTS = 8
SUB, LANE = 8, 128
BF16_ROWS = 16
FAST_STRIDE = 4
ROUTE_LANES = LANE
MOE_BLOCK = 160
MOE_CHUNK = 2048
WEIGHT_SLOTS = 3
VMEM_LIMIT = 56 * 1024 * 1024


def _cparams(*sem):
    return pltpu.CompilerParams(dimension_semantics=sem, vmem_limit_bytes=VMEM_LIMIT)


def _bdot(a, b):
    return jnp.dot(a.astype(BF16), b.astype(BF16), preferred_element_type=F32)


def _bdot_nt(a, b):
    return lax.dot_general(a.astype(BF16), b.astype(BF16), (((1,), (1,)), ((), ())),
                           preferred_element_type=F32)


def _split3(x):
    hi = x.astype(BF16)
    r1 = x - hi.astype(F32)
    mid = r1.astype(BF16)
    lo = (r1 - mid.astype(F32)).astype(BF16)
    return hi, mid, lo


def _dot_exact_lhs(a01, x):
    a = a01.astype(BF16)
    hi, mid, lo = _split3(x)
    d = lambda p: jnp.dot(a, p, preferred_element_type=F32)
    return d(hi) + d(mid) + d(lo)


def _lane_tile_rms(x):
    parts = []
    for c in range(x.shape[1] // LANE):
        xc = x[:, c * LANE:(c + 1) * LANE]
        parts.append(xc * lax.rsqrt(jnp.mean(xc * xc, axis=-1, keepdims=True) + EPS))
    return jnp.concatenate(parts, axis=1)


def _half_tile_rms(x):
    half = LANE // 2
    lo = lax.broadcasted_iota(jnp.int32, (x.shape[0], LANE), 1) < half
    parts = []
    for c in range(x.shape[1] // LANE):
        xc = x[:, c * LANE:(c + 1) * LANE]
        sq = xc * xc
        s_lo = jnp.sum(jnp.where(lo, sq, 0.0), axis=-1, keepdims=True)
        s_hi = jnp.sum(jnp.where(lo, 0.0, sq), axis=-1, keepdims=True)
        parts.append(xc * lax.rsqrt(jnp.where(lo, s_lo, s_hi) * (1.0 / half) + EPS))
    return jnp.concatenate(parts, axis=1)


def _rms(x, g):
    return x * lax.rsqrt(jnp.mean(x * x, axis=-1, keepdims=True) + EPS) * g


def _sigmoid(x):
    return 1.0 / (1.0 + jnp.exp(-x))


def _inproj_kernel(x_ref, g_ref, w_ref, gq_ref, gk_ref,
                   q_ref, k_ref, v_ref, u_ref, z_ref, ab_ref, *, aw, bw):
    nb = _rms(x_ref[...], g_ref[...]).astype(BF16)

    def proj(lo, hi):
        return jnp.dot(nb, w_ref[:, lo:hi], preferred_element_type=F32)

    def put_slabs(ref, val):
        for p in range(aw // LANE):
            ref[p] = val[:, p * LANE:(p + 1) * LANE]

    assert A_HEAD_DIM * 2 == LANE
    put_slabs(q_ref, _half_tile_rms(proj(0, aw)) * gq_ref[...])
    put_slabs(k_ref, _half_tile_rms(proj(aw, 2 * aw)) * gk_ref[...])
    put_slabs(v_ref, proj(2 * aw, 3 * aw))
    u_ref[...] = proj(3 * aw, 3 * aw + 3 * bw)
    z_ref[...] = proj(3 * aw + 3 * bw, 3 * aw + 4 * bw)
    ab_ref[...] = proj(3 * aw + 4 * bw, 3 * aw + 4 * bw + ROUTE_LANES)


def _inproj(x, g_mix, w_in_b, gq_t, gk_t, *, aw, bw, tm):
    n, d = x.shape
    cols = w_in_b.shape[1]
    row = lambda w: pl.BlockSpec((tm, w), lambda i: (i, 0))
    full = lambda a: pl.BlockSpec(a.shape, lambda i: (0, 0))
    out_w = (3 * bw, bw, ROUTE_LANES)
    slab = pl.BlockSpec((aw // LANE, tm, LANE), lambda i: (0, i, 0))
    return pl.pallas_call(
        functools.partial(_inproj_kernel, aw=aw, bw=bw),
        grid=(n // tm,),
        in_specs=[row(d), full(g_mix), pl.BlockSpec((d, cols), lambda i: (0, 0)),
                  full(gq_t), full(gk_t)],
        out_specs=[slab] * 3 + [row(w) for w in out_w],
        out_shape=[jax.ShapeDtypeStruct((aw // LANE, n, LANE), F32)] * 3
                  + [jax.ShapeDtypeStruct((n, w), F32) for w in out_w],
        compiler_params=_cparams("parallel"),
        name="inproj",
    )(x, g_mix, w_in_b, gq_t, gk_t)


def _dil_attn_kernel(q_ref, kp_ref, kc_ref, vp_ref, vc_ref, o_ref, od_ref, ld_ref, stage_ref):
    n = pl.program_id(1)
    rows = q_ref.shape[1]
    qi = lax.broadcasted_iota(jnp.int32, (BAND, 2 * BAND), 0)
    kj = lax.broadcasted_iota(jnp.int32, (BAND, 2 * BAND), 1)
    band = (kj >= qi) & (kj <= qi + BAND)
    bias = jnp.where(band, 0.0, NEG_INF)
    bias_first = jnp.where(band & ((kj >= BAND) | (n > 0)), 0.0, NEG_INF)
    lo_half = lax.broadcasted_iota(jnp.int32, (BAND, LANE), 1) < A_HEAD_DIM

    for di, (window, dil) in enumerate(DILATIONS):
        assert window // dil == BAND and rows % (BAND * dil) == 0
        span = BAND * dil
        srcs = (q_ref, kc_ref, vc_ref, kp_ref, vp_ref)
        if dil > FAST_STRIDE:
            inner = dil // FAST_STRIDE
            assert inner <= FAST_STRIDE and dil % FAST_STRIDE == 0
            for ti, ref in enumerate(srcs):
                for a in range(FAST_STRIDE):
                    stage_ref[ti, a] = ref[0, pl.ds(a, rows // FAST_STRIDE, stride=FAST_STRIDE), :]
            take = lambda ti, r, blk: stage_ref[ti, r % FAST_STRIDE,
                                                pl.ds(r // FAST_STRIDE + blk * BAND * inner, BAND, stride=inner), :]
        else:
            take = lambda ti, r, blk: srcs[ti][0, pl.ds(r + blk * span, BAND, stride=dil), :]
        last = rows // span - 1
        for r in range(dil):
            k_prev = take(3, r, last).astype(BF16)
            v_prev = take(4, r, last).astype(BF16)
            for j in range(rows // span):
                start = r + j * span
                q = take(0, r, j)
                k_cur, v_cur = take(1, r, j).astype(BF16), take(2, r, j).astype(BF16)
                kcat = jnp.concatenate([k_prev, k_cur], axis=0)
                vcat = jnp.concatenate([v_prev, v_cur], axis=0)
                k_prev, v_prev = k_cur, v_cur
                res = []
                for half in range(2):
                    keep = lo_half if half == 0 else jnp.logical_not(lo_half)
                    s = _bdot_nt(jnp.where(keep, q, 0.0), kcat) + (bias if j else bias_first)
                    m = jnp.max(s, axis=-1, keepdims=True)
                    e = jnp.exp(s - m)
                    den = jnp.sum(e, axis=-1, keepdims=True)
                    o = jnp.dot(e.astype(BF16), vcat, preferred_element_type=F32) / den
                    res.append((o, m + jnp.log(den)))
                od_ref[di, pl.ds(start, BAND, stride=dil), :] = jnp.where(lo_half, res[0][0], res[1][0])
                ld_ref[di, pl.ds(start, BAND, stride=dil), :] = jnp.where(lo_half, res[0][1], res[1][1])

    step = 256
    for r0 in range(0, rows, step):
        ls = [ld_ref[di, r0:r0 + step, :] for di in range(len(DILATIONS))]
        mm = functools.reduce(jnp.maximum, ls)
        es = [jnp.exp(l - mm) for l in ls]
        num = functools.reduce(lambda a, b: a + b, [e * od_ref[di, r0:r0 + step, :] for di, e in enumerate(es)])
        o_ref[0, r0:r0 + step, :] = num / functools.reduce(lambda a, b: a + b, es)


def _dil_attn(q, k, v, *, rows):
    pairs, s, lanes = q.shape
    assert s % rows == 0 and lanes == LANE
    cur = pl.BlockSpec((1, rows, LANE), lambda p, n: (p, n, 0))
    prev = pl.BlockSpec((1, rows, LANE), lambda p, n: (p, jnp.maximum(n - 1, 0), 0))
    return pl.pallas_call(
        _dil_attn_kernel,
        grid=(pairs, s // rows),
        in_specs=[cur, prev, cur, prev, cur],
        out_specs=cur,
        out_shape=jax.ShapeDtypeStruct(q.shape, F32),
        scratch_shapes=[pltpu.VMEM((len(DILATIONS), rows, LANE), F32)] * 2
                       + [pltpu.VMEM((5, FAST_STRIDE, rows // FAST_STRIDE, LANE), F32)],
        compiler_params=_cparams("parallel", "arbitrary"),
        name="dil_attn",
    )(q, k, k, v, v)


def _step_attn_kernel(q_ref, kn_ref, vn_ref, kc_ref, vc_ref, ok_ref, ov_ref, oa_ref,
                      kb_ref, vb_ref, *, w_buf, t_new):
    aw = q_ref.shape[2]
    heads = aw // A_HEAD_DIM
    lanes = LANE
    kt, vt = kc_ref[0], vc_ref[0]
    lead = jnp.zeros((aw, lanes - t_new), F32)
    kn = jnp.concatenate([lead, kn_ref[0]], axis=1)
    vn = jnp.concatenate([lead, vn_ref[0]], axis=1)
    is_new = lax.broadcasted_iota(jnp.int32, (aw, lanes), 1) >= lanes - t_new
    for src, new, dst in ((kt, kn, ok_ref), (vt, vn, ov_ref)):
        rolled = pltpu.roll(src, w_buf - t_new, axis=1)
        dst[0, :, 0:w_buf - lanes] = rolled[:, 0:w_buf - lanes]
        dst[0, :, w_buf - lanes:w_buf] = jnp.where(is_new, new, rolled[:, w_buf - lanes:w_buf])
    kb_ref[...] = kt.astype(BF16)
    vb_ref[...] = vt.astype(BF16)

    rows = heads * t_new
    ri = lax.broadcasted_iota(jnp.int32, (rows, aw), 0)
    li = lax.broadcasted_iota(jnp.int32, (rows, aw), 1)
    q_rep = jnp.concatenate([q_ref[0]] * heads, axis=0)
    q_blk = jnp.where((ri // t_new) == (li // A_HEAD_DIM), q_rep, 0.0).astype(BF16)
    ncol = w_buf + lanes
    s_all = jnp.concatenate([jnp.dot(q_blk, kb_ref[...], preferred_element_type=F32),
                             jnp.dot(q_blk, kn.astype(BF16), preferred_element_type=F32)], axis=1)

    def branch(window, dil, col0):
        s = s_all[:, col0:]
        t = lax.broadcasted_iota(jnp.int32, s.shape, 0) % t_new
        cidx = lax.broadcasted_iota(jnp.int32, s.shape, 1) + col0
        r = jnp.where(cidx < w_buf, cidx, cidx - (lanes - t_new))
        diff = w_buf + t - r
        valid = (diff >= 0) & (diff <= window) & ((diff & (dil - 1)) == 0)
        valid = valid & ((cidx < w_buf) | (cidx >= ncol - t_new))
        s = jnp.where(valid, s, NEG_INF)
        m = jnp.max(s, axis=-1, keepdims=True)
        e = jnp.exp(s - m)
        den = jnp.sum(e, axis=-1, keepdims=True)
        return e, den, m + jnp.log(den)

    parts = []
    for window, dil in DILATIONS:
        assert dil & (dil - 1) == 0
        col0 = max(0, (w_buf - window) // 128 * 128)
        parts.append((col0,) + branch(window, dil, col0))
    mm = functools.reduce(jnp.maximum, [p[3] for p in parts])
    wexp = [jnp.exp(p[3] - mm) for p in parts]
    wsum = functools.reduce(lambda a, b: a + b, wexp)
    p_tot = jnp.zeros((rows, ncol), F32)
    for (col0, e, den, _), we in zip(parts, wexp):
        pe = e * (we / (wsum * den))
        if col0:
            pe = jnp.concatenate([jnp.zeros((rows, col0), F32), pe], axis=1)
        p_tot = p_tot + pe
    o = _bdot_nt(p_tot[:, :w_buf], vb_ref[...]) + _bdot_nt(p_tot[:, w_buf:], vn)
    lh = lax.broadcasted_iota(jnp.int32, (t_new, aw), 1) // A_HEAD_DIM
    acc = jnp.zeros((t_new, aw), F32)
    for h in range(heads):
        acc = acc + jnp.where(lh == h, o[h * t_new:(h + 1) * t_new, :], 0.0)
    oa_ref[0] = acc


def _step_attn(q, kn_t, vn_t, cache_kt, cache_vt):
    nseq, aw, w_buf = cache_kt.shape
    t_new = q.shape[1]
    assert w_buf % LANE == 0 and t_new % SUB == 0 and t_new <= LANE and kn_t.shape[2] == t_new
    qs = pl.BlockSpec((1, t_new, aw), lambda i: (i, 0, 0))
    new = pl.BlockSpec((1, aw, t_new), lambda i: (i, 0, 0))
    win = pl.BlockSpec((1, aw, w_buf), lambda i: (i, 0, 0))
    return pl.pallas_call(
        functools.partial(_step_attn_kernel, w_buf=w_buf, t_new=t_new),
        grid=(nseq,),
        in_specs=[qs, new, new, win, win],
        out_specs=[win, win, qs],
        out_shape=[jax.ShapeDtypeStruct(cache_kt.shape, F32), jax.ShapeDtypeStruct(cache_vt.shape, F32),
                   jax.ShapeDtypeStruct(q.shape, F32)],
        scratch_shapes=[pltpu.VMEM((aw, w_buf), BF16), pltpu.VMEM((aw, w_buf), BF16)],
        compiler_params=_cparams("parallel"),
        name="step_attn",
    )(q, kn_t, vn_t, cache_kt, cache_vt)


def _gdn_prep_kernel(u_ref, prev_ref, ab_ref, cw_ref, alog_ref, dtb_ref,
                     w_ref, uv_ref, qg_ref, kt_ref, qk_ref, last_ref, full_ref, act_ref,
                     *, seg, conv_seg, chunks):
    c = GDN_CHUNK
    dk = B_HEAD_DIM
    bw = B_HEADS * dk
    ii = lax.broadcasted_iota(jnp.int32, (c, c), 0)
    jj = lax.broadcasted_iota(jnp.int32, (c, c), 1)
    same = (ii // seg) == (jj // seg)
    incl = same & (ii >= jj)
    strict = same & (ii > jj)
    eye = ii == jj
    ones_c = jnp.ones((c, c), BF16)
    seg_cols = jnp.where((lax.broadcasted_iota(jnp.int32, (c, dk), 0) % seg)
                         == lax.broadcasted_iota(jnp.int32, (c, dk), 1), 1.0, 0.0)
    cw = cw_ref[...]
    rows = chunks * c

    for sgi in range(rows // conv_seg):
        base = sgi * (conv_seg + PREV_ROWS)
        full_ref[base:base + PREV_ROWS, :] = prev_ref[sgi]
        full_ref[base + PREV_ROWS:base + PREV_ROWS + conv_seg, :] = u_ref[sgi * conv_seg:(sgi + 1) * conv_seg, :]
        y = jnp.zeros((conv_seg, 3 * bw), F32)
        for j in range(CONV_WIDTH):
            off = base + PREV_ROWS - (CONV_WIDTH - 1) + j
            y = y + full_ref[off:off + conv_seg, :] * cw[j:j + 1, :]
        act_ref[sgi * conv_seg:(sgi + 1) * conv_seg, :] = y * _sigmoid(y)

    ri = lax.broadcasted_iota(jnp.int32, (rows, rows), 0)
    rj = lax.broadcasted_iota(jnp.int32, (rows, rows), 1)
    rsame = (ri // seg) == (rj // seg)
    ab = ab_ref[...]
    sp = ab + dtb_ref[...]
    sp = jnp.maximum(sp, 0.0) + jnp.log(1.0 + jnp.exp(-jnp.abs(sp)))
    g_all = -jnp.exp(alog_ref[...]) * sp
    gcum_all = _dot_exact_lhs(jnp.where(rsame & (ri >= rj), 1.0, 0.0), g_all)
    gtot_all = _dot_exact_lhs(jnp.where(rsame, 1.0, 0.0), g_all)
    beta_all = _sigmoid(ab)
    eye_f = jnp.where(eye, 1.0, 0.0)

    grow = []
    for ci in range(chunks):
        rs = slice(ci * c, (ci + 1) * c)
        diag = jnp.concatenate([jnp.where(eye, gcum_all[rs, h:h + 1], 0.0) for h in range(B_HEADS)], axis=1)
        grow.append(_dot_exact_lhs(ones_c, diag))

    prob = [(ci, h) for ci in range(chunks) for h in range(B_HEADS)]
    col = lambda arr, ci, lane: arr[ci * c:(ci + 1) * c, lane:lane + 1]
    gc = [col(gcum_all, ci, h) for ci, h in prob]
    gt = [col(gtot_all, ci, h) for ci, h in prob]
    beta = [col(beta_all, ci, B_HEADS + h) for ci, h in prob]
    q, k, v = [], [], []
    for ci, h in prob:
        rs = slice(ci * c, (ci + 1) * c)
        qh = act_ref[rs, h * dk:(h + 1) * dk]
        kh = act_ref[rs, bw + h * dk:bw + (h + 1) * dk]
        q.append(qh * lax.rsqrt(jnp.sum(qh * qh, axis=-1, keepdims=True) + EPS) * (dk ** -0.5))
        k.append(kh * lax.rsqrt(jnp.sum(kh * kh, axis=-1, keepdims=True) + EPS))
        v.append(act_ref[rs, 2 * bw + h * dk:2 * bw + (h + 1) * dk])
    e_incl = [jnp.exp(jnp.where(incl, gc[i] - grow[ci][:, h * c:(h + 1) * c], NEG_INF))
              for i, (ci, h) in enumerate(prob)]
    qkk = [_bdot_nt(jnp.concatenate([q[i], k[i]], axis=0), k[i]) for i in range(len(prob))]
    a = [beta[i] * jnp.where(strict, e_incl[i], 0.0) * qkk[i][c:, :] for i in range(len(prob))]
    blk = 8
    inblk = (ii // blk) == (jj // blk)
    d0 = [jnp.where(inblk, t, 0.0) for t in a]
    d2 = [_bdot(t, t) for t in d0]
    d4 = [_bdot(t, t) for t in d2]
    x = [(eye_f - t) + _bdot(eye_f - t, t2) for t, t2 in zip(d0, d2)]
    x = [t + _bdot(t, t4) for t, t4 in zip(x, d4)]
    while blk < seg:
        off_blk = ((ii // (2 * blk)) == (jj // (2 * blk))) & ((ii // blk) != (jj // blk))
        xe = [_bdot(t, jnp.where(off_blk, ta, 0.0)) for t, ta in zip(x, a)]
        x = [t - _bdot(te, t) for t, te in zip(x, xe)]
        blk *= 2
    gamma = [jnp.exp(t) for t in gc]
    wuv = [_bdot(x[i], jnp.concatenate([(beta[i] * gamma[i]) * k[i], beta[i] * v[i]], axis=1))
           for i in range(len(prob))]
    for i, (ci, h) in enumerate(prob):
        rs = slice(ci * c, (ci + 1) * c)
        hs = slice(h * dk, (h + 1) * dk)
        w_ref[rs, hs] = wuv[i][:, :dk]
        uv_ref[rs, hs] = wuv[i][:, dk:]
        qg_ref[rs, hs] = gamma[i] * q[i]
        kt_ref[rs, hs] = k[i] * jnp.exp(gt[i] - gc[i])
        qk = qkk[i][:c, :] * e_incl[i]
        qk_ref[rs, hs] = (jnp.concatenate([qk, jnp.zeros((c, dk - c), F32)], axis=1) if seg == c
                          else _bdot(qk, seg_cols))
        last_ref[rs, hs] = jnp.broadcast_to(jnp.exp(gt[i]), (c, dk))


def _gdn_prep(u_pre, prev, ab, conv_w, alog_row, dtb_row, *, seg, conv_seg, chunks):
    n, cw = u_pre.shape
    bw = cw // 3
    c = GDN_CHUNK
    rows = chunks * c
    assert n % rows == 0 and c % seg == 0 and seg % 8 == 0 and rows % conv_seg == 0 and conv_seg % seg == 0
    ncs = rows // conv_seg
    row = lambda w: pl.BlockSpec((rows, w), lambda i: (i, 0))
    full = lambda a: pl.BlockSpec(a.shape, lambda i: (0, 0))
    return pl.pallas_call(
        functools.partial(_gdn_prep_kernel, seg=seg, conv_seg=conv_seg, chunks=chunks),
        grid=(n // rows,),
        in_specs=[row(cw), pl.BlockSpec((ncs, PREV_ROWS, cw), lambda i: (i, 0, 0)),
                  row(ROUTE_LANES), full(conv_w), full(alog_row), full(dtb_row)],
        out_specs=[row(bw)] * 6,
        out_shape=[jax.ShapeDtypeStruct((n, bw), F32)] * 6,
        scratch_shapes=[pltpu.VMEM((ncs * (conv_seg + PREV_ROWS), cw), F32), pltpu.VMEM((rows, cw), F32)],
        compiler_params=_cparams("parallel"),
        name=f"gdn_prep_{seg}",
    )(u_pre, prev, ab, conv_w, alog_row, dtb_row)


def _gdn_scan_kernel(w_ref, uv_ref, qg_ref, kt_ref, qk_ref, last_ref, s0_ref, o_ref, s_ref, *, seg, steps):
    dk = B_HEAD_DIM

    @pl.when(pl.program_id(1) == 0)
    def _():
        s_ref[...] = s0_ref[...]

    rows = max(seg, BF16_ROWS)
    pad = rows - seg
    padr = lambda t: jnp.concatenate([t, jnp.zeros((pad, t.shape[1]), F32)], axis=0) if pad else t
    ii = lax.broadcasted_iota(jnp.int32, (dk, dk), 0)
    jj = lax.broadcasted_iota(jnp.int32, (dk, dk), 1)
    eye = jnp.where(ii == jj, 1.0, 0.0).astype(BF16)
    chains = [(sq, h) for sq in range(s_ref.shape[0]) for h in range(B_HEADS)]
    hs = lambda h: slice(h * dk, (h + 1) * dk)
    s = [s_ref[sq, h] for sq, h in chains]
    for c in range(steps):
        rs = [slice((sq * steps + c) * seg, (sq * steps + c + 1) * seg) for sq, _ in chains]
        sb = [t.astype(BF16) for t in s]
        ws = [_bdot(padr(w_ref[rs[i], hs(h)]), sb[i]) for i, (_, h) in enumerate(chains)]
        ub = [(padr(uv_ref[rs[i], hs(h)]) - ws[i]).astype(BF16) for i, (_, h) in enumerate(chains)]
        ktt = [_bdot_nt(eye, padr(kt_ref[rs[i], hs(h)])) for i, (_, h) in enumerate(chains)]
        s_new = [last_ref[rs[i], hs(h)][0:1, :] * s[i] + _bdot(ktt[i], ub[i]) for i, (_, h) in enumerate(chains)]
        for i, (_, h) in enumerate(chains):
            o = _bdot(padr(qg_ref[rs[i], hs(h)]), sb[i]) + _bdot(padr(qk_ref[rs[i], hs(h)])[:, :rows], ub[i])
            o_ref[rs[i], hs(h)] = o[:seg, :]
        s = s_new
    for i, (sq, h) in enumerate(chains):
        s_ref[sq, h] = s[i]


def _gdn_scan(w, uv, qg, kt, qk, last, s0, *, seg, seqs, steps):
    n, bw = w.shape
    nseq = s0.shape[0]
    per_seq = n // (nseq * seg)
    assert nseq % seqs == 0 and per_seq % steps == 0 and (seqs == 1 or steps == per_seq)
    blocks = per_seq // steps
    row = pl.BlockSpec((seqs * steps * seg, bw), lambda s, i: (s * blocks + i, 0))
    st = pl.BlockSpec((seqs,) + s0.shape[1:], lambda s, i: (s, 0, 0, 0))
    return pl.pallas_call(
        functools.partial(_gdn_scan_kernel, seg=seg, steps=steps),
        grid=(nseq // seqs, blocks),
        in_specs=[row] * 6 + [st],
        out_specs=[row, st],
        out_shape=[jax.ShapeDtypeStruct((n, bw), F32), jax.ShapeDtypeStruct(s0.shape, F32)],
        compiler_params=_cparams("parallel", "arbitrary"),
        name=f"gdn_scan_{seg}",
    )(w, uv, qg, kt, qk, last, s0)


def _mem_kv_kernel(mem_ref, g_ref, wk_ref, wv_ref, gk_ref, k_ref, v_ref):
    mn = _rms(mem_ref[...], g_ref[...]).astype(BF16)
    k = jnp.dot(mn, wk_ref[...], preferred_element_type=F32)
    k_ref[...] = _lane_tile_rms(k) * gk_ref[...]
    v_ref[...] = jnp.dot(mn, wv_ref[...], preferred_element_type=F32)


def _mem_kv(mem, g_mem, w_xk_b, w_xv_b, gk_t):
    m = mem.shape[0]
    xw = w_xk_b.shape[1]
    return pl.pallas_call(
        _mem_kv_kernel,
        out_shape=[jax.ShapeDtypeStruct((m, xw), F32)] * 2,
        compiler_params=pltpu.CompilerParams(vmem_limit_bytes=VMEM_LIMIT),
        name="mem_kv",
    )(mem, g_mem, w_xk_b, w_xv_b, gk_t)


def _route(logits):
    lane = lax.broadcasted_iota(jnp.int32, logits.shape, 1)
    lane_f = lane.astype(F32)
    big = float(ROUTE_LANES)
    lg = jnp.where(lane < N_GROUPS, logits, NEG_INF)
    mg = jnp.max(lg, axis=-1, keepdims=True)
    zg = jnp.sum(jnp.exp(lg - mg), axis=-1, keepdims=True)
    pg_top = 1.0 / zg
    gidx = jnp.min(jnp.where(lg == mg, lane_f, big), axis=-1, keepdims=True)
    e_lo = N_GROUPS + GROUP_EXPERTS * gidx
    emask = (lane_f >= e_lo) & (lane_f < e_lo + GROUP_EXPERTS)
    le = jnp.where(emask, logits, NEG_INF)
    me = jnp.max(le, axis=-1, keepdims=True)
    ee = jnp.exp(le - me)
    pe = ee / jnp.sum(ee, axis=-1, keepdims=True)
    pe = jnp.where(emask, pe, -1.0)
    p1 = jnp.max(pe, axis=-1, keepdims=True)
    i1 = jnp.min(jnp.where(pe == p1, lane_f, big), axis=-1, keepdims=True)
    pe2 = jnp.where(lane_f == i1, -1.0, pe)
    p2 = jnp.max(pe2, axis=-1, keepdims=True)
    i2 = jnp.min(jnp.where(pe2 == p2, lane_f, big), axis=-1, keepdims=True)
    den = p1 + p2
    w1 = pg_top * p1 / den
    w2 = pg_top * p2 / den
    out = jnp.where(lane == 0, i1 - N_GROUPS, 0.0)
    out = jnp.where(lane == 1, i2 - N_GROUPS, out)
    out = jnp.where(lane == 2, w1, out)
    return jnp.where(lane == 3, w2, out)


def _post_common(oa, og_ref, z_ref, h_ref, wout_ref, ggdn_ref, gx_ref, wxq_ref, gxq_ref):
    og = og_ref[...]
    z = z_ref[...]
    ob = _lane_tile_rms(og) * ggdn_ref[...] * (z * _sigmoid(z))
    cat = jnp.concatenate([oa, ob], axis=-1).astype(BF16)
    h1 = h_ref[...] + jnp.dot(cat, wout_ref[...], preferred_element_type=F32)
    hx = _rms(h1, gx_ref[...]).astype(BF16)
    q = jnp.dot(hx, wxq_ref[...], preferred_element_type=F32)
    qn = _lane_tile_rms(q) * gxq_ref[...]
    return h1, qn


def _mem_attend_rows(qn, mk, mv):
    outs = []
    for h in range(qn.shape[1] // B_HEAD_DIM):
        hs = slice(h * B_HEAD_DIM, (h + 1) * B_HEAD_DIM)
        s = _bdot_nt(qn[:, hs], mk[:, hs])
        m = jnp.max(s, axis=-1, keepdims=True)
        e = jnp.exp(s - m)
        p = e / jnp.sum(e, axis=-1, keepdims=True)
        outs.append(_bdot(p, mv[:, hs]))
    return jnp.concatenate(outs, axis=-1)


def _post_tail(h1, ox, wxo_ref, gffn_ref, wr_ref, br_ref, h2_ref, hn_ref, route_ref):
    h2 = h1 + jnp.dot(ox.astype(BF16), wxo_ref[...], preferred_element_type=F32)
    h2_ref[...] = h2
    hn = _rms(h2, gffn_ref[...])
    tm = hn.shape[0]
    for j in range(hn.shape[1] // LANE):
        hn_ref[pl.ds(j, tm, stride=SUB), :] = hn[:, j * LANE:(j + 1) * LANE]
    hn_hi = hn.astype(BF16)
    hn_mid = (hn - hn_hi.astype(F32)).astype(BF16)
    d = lambda a, b: jnp.dot(a, b, preferred_element_type=F32)
    logits = d(hn_hi, wr_ref[0]) + d(hn_hi, wr_ref[1]) + d(hn_mid, wr_ref[0]) + br_ref[...]
    route_ref[...] = _route(logits)


def _post_prompt_kernel(oa_ref, og_ref, z_ref, h_ref,
                        wout_ref, ggdn_ref, gx_ref, wxq_ref, gxq_ref, mk_ref, mv_ref,
                        wxo_ref, gffn_ref, wr_ref, br_ref, h2_ref, hn_ref, route_ref):
    oa = jnp.concatenate([oa_ref[p] for p in range(oa_ref.shape[0])], axis=-1)
    h1, qn = _post_common(oa, og_ref, z_ref, h_ref, wout_ref, ggdn_ref, gx_ref, wxq_ref, gxq_ref)
    ox = _mem_attend_rows(qn, mk_ref[...], mv_ref[...])
    _post_tail(h1, ox, wxo_ref, gffn_ref, wr_ref, br_ref, h2_ref, hn_ref, route_ref)


def _post_sample_kernel(oa_ref, og_ref, z_ref, h_ref,
                        wout_ref, ggdn_ref, gx_ref, wxq_ref, gxq_ref, mk_ref, mv_ref,
                        wxo_ref, gffn_ref, wr_ref, br_ref, h2_ref, hn_ref, route_ref, *, t_new):
    h1, qn = _post_common(oa_ref[...], og_ref, z_ref, h_ref, wout_ref, ggdn_ref, gx_ref, wxq_ref, gxq_ref)
    dh = B_HEAD_DIM
    heads = qn.shape[1] // dh
    rows = heads * t_new
    ncol = mk_ref.shape[1]
    own = ((lax.broadcasted_iota(jnp.int32, (rows, ncol), 0) // t_new)
           == (lax.broadcasted_iota(jnp.int32, (rows, ncol), 1) % heads))
    outs = []
    for sq in range(mk_ref.shape[0]):
        qs = qn[sq * t_new:(sq + 1) * t_new, :]
        q_rows = jnp.concatenate([qs[:, h * dh:(h + 1) * dh] for h in range(heads)], axis=0)
        s = jnp.where(own, _bdot_nt(q_rows, mk_ref[sq]), NEG_INF)
        m = jnp.max(s, axis=-1, keepdims=True)
        e = jnp.exp(s - m)
        o = _bdot(e / jnp.sum(e, axis=-1, keepdims=True), mv_ref[sq])
        outs.append(jnp.concatenate([o[h * t_new:(h + 1) * t_new, :] for h in range(heads)], axis=1))
    ox = jnp.concatenate(outs, axis=0)
    _post_tail(h1, ox, wxo_ref, gffn_ref, wr_ref, br_ref, h2_ref, hn_ref, route_ref)


def _post_weights_specs(weights):
    return [pl.BlockSpec(a.shape, lambda i, nd=a.ndim: (0,) * nd) for a in weights]


def _post_outs(n, d, tm):
    assert d == SUB * LANE
    row = lambda w: pl.BlockSpec((tm, w), lambda i: (i, 0))
    specs = [row(d), pl.BlockSpec((tm * SUB, LANE), lambda i: (i, 0)), row(ROUTE_LANES)]
    shapes = [jax.ShapeDtypeStruct((n, d), F32), jax.ShapeDtypeStruct((n * SUB, LANE), F32),
              jax.ShapeDtypeStruct((n, ROUTE_LANES), F32)]
    return specs, shapes


def _post_prompt(oa_slabs, og, z, h, pw, mk, mv, *, tm):
    n, d = h.shape
    aw = og.shape[1]
    row = lambda w: pl.BlockSpec((tm, w), lambda i: (i, 0))
    full = lambda a: pl.BlockSpec(a.shape, lambda i: (0, 0))
    slab = pl.BlockSpec((oa_slabs.shape[0], tm, LANE), lambda i: (0, i, 0))
    w1 = [pw["w_out"], pw["g_gdn"], pw["g_xattn"], pw["w_xq"], pw["g_xq"]]
    w2 = [pw["w_xo"], pw["g_ffn"], pw["w_r"], pw["b_r"]]
    specs, shapes = _post_outs(n, d, tm)
    return pl.pallas_call(
        _post_prompt_kernel,
        grid=(n // tm,),
        in_specs=[slab, row(aw), row(aw), row(d)] + _post_weights_specs(w1) + [full(mk), full(mv)]
                 + _post_weights_specs(w2),
        out_specs=specs, out_shape=shapes,
        compiler_params=_cparams("parallel"),
        name="post_prompt",
    )(oa_slabs, og, z, h, *w1, mk, mv, *w2)


def _post_sample(oa, og, z, h, pw, mk, mv, *, t_new, seqs):
    n, d = h.shape
    aw = og.shape[1]
    tm = t_new * seqs
    row = lambda w: pl.BlockSpec((tm, w), lambda i: (i, 0))
    mem = pl.BlockSpec((seqs,) + mk.shape[1:], lambda i: (i, 0, 0))
    w1 = [pw["w_out"], pw["g_gdn"], pw["g_xattn"], pw["w_xq"], pw["g_xq"]]
    w2 = [pw["w_xo"], pw["g_ffn"], pw["w_r"], pw["b_r"]]
    specs, shapes = _post_outs(n, d, tm)
    return pl.pallas_call(
        functools.partial(_post_sample_kernel, t_new=t_new),
        grid=(n // tm,),
        in_specs=[row(aw)] * 3 + [row(d)] + _post_weights_specs(w1) + [mem, mem]
                 + _post_weights_specs(w2),
        out_specs=specs, out_shape=shapes,
        compiler_params=_cparams("parallel"),
        name="post_sample",
    )(oa, og, z, h, *w1, mk, mv, *w2)


def _moe_kernel(blk_e_ref, blk_start_ref, blk_cnt_ref, tok_ref, tw_ref, x_ref, res_hbm,
                wg_hbm, wu_hbm, wd_hbm, out_ref, acc_ref, xs_ref, ys_ref, wg_buf, wu_buf, wd_buf, sem, res_sem,
                *, nb, total):
    c = pl.program_id(0)
    b = pl.program_id(1)
    chunk, d = out_ref.shape
    nl = d // LANE
    step = c * nb + b
    cnt = blk_cnt_ref[step]
    wslot = step % WEIGHT_SLOTS

    def weight_copies(u):
        e, s_ = blk_e_ref[u], u % WEIGHT_SLOTS
        return [pltpu.make_async_copy(hbm.at[e], buf.at[s_], sem.at[s_, i])
                for i, (hbm, buf) in enumerate(((wg_hbm, wg_buf), (wu_hbm, wu_buf), (wd_hbm, wd_buf)))]

    def start_fetch(u):
        @pl.when(blk_cnt_ref[jnp.minimum(u, total - 1)] * (u < total) > 0)
        def _():
            for cp in weight_copies(u):
                cp.start()

    res_copy = pltpu.make_async_copy(res_hbm.at[pl.ds(pl.multiple_of(c * chunk, chunk), chunk), :],
                                     out_ref, res_sem.at[0])

    @pl.when(b == 0)
    def _():
        res_copy.start()

    @pl.when(step == 0)
    def _():
        for u in range(WEIGHT_SLOTS - 1):
            start_fetch(jnp.int32(u))

    start_fetch(step + WEIGHT_SLOTS - 1)

    @pl.when(cnt > 0)
    def _():
        for cp in weight_copies(step):
            cp.wait()
        wg_ref, wu_ref, wd_ref = wg_buf.at[wslot], wu_buf.at[wslot], wd_buf.at[wslot]
        base = blk_start_ref[step]
        for r in range(MOE_BLOCK):
            t = tok_ref[0, 0, base + r] >> 1
            xs_ref[r * SUB:(r + 1) * SUB, :] = x_ref[pl.ds(pl.multiple_of(t * SUB, SUB), SUB), :]
        xb = jnp.concatenate([xs_ref[pl.ds(j, MOE_BLOCK, stride=SUB), :] for j in range(nl)],
                             axis=1).astype(BF16)
        hg = jnp.dot(xb, wg_ref[...], preferred_element_type=F32)
        hu = jnp.dot(xb, wu_ref[...], preferred_element_type=F32)
        act = (hg * _sigmoid(hg) * hu).astype(BF16)
        y = jnp.dot(act, wd_ref[...], preferred_element_type=F32)
        for j in range(nl):
            ys_ref[pl.ds(j, MOE_BLOCK, stride=SUB), :] = y[:, j * LANE:(j + 1) * LANE]
        for r in range(MOE_BLOCK):
            e = tok_ref[0, 0, base + r]
            slot = jnp.where(r < cnt, (e & 1) * chunk + (e >> 1), 2 * chunk)
            off = pl.multiple_of(slot * SUB, SUB)
            acc_ref[pl.ds(off, SUB), :] = tw_ref[0, 0, base + r] * ys_ref[r * SUB:(r + 1) * SUB, :]

    @pl.when(b == nb - 1)
    def _():
        res_copy.wait()
        rows_per = 256
        for r0 in range(0, chunk, rows_per):
            for j in range(nl):
                out_ref[r0:r0 + rows_per, j * LANE:(j + 1) * LANE] = (
                    out_ref[r0:r0 + rows_per, j * LANE:(j + 1) * LANE]
                    + acc_ref[pl.ds(r0 * SUB + j, rows_per, stride=SUB), :]
                    + acc_ref[pl.ds((chunk + r0) * SUB + j, rows_per, stride=SUB), :])


def _dispatch(route, chunk, n_experts):
    n = route.shape[0]
    nch = n // chunk
    rows = 2 * chunk
    nb = rows // MOE_BLOCK + n_experts
    e = route[:, :2].astype(jnp.int32).reshape(nch, rows)
    w = route[:, 2:4].reshape(nch, rows)
    tok = jnp.argsort(e, axis=1, stable=True).astype(jnp.int32)
    tw = jnp.take_along_axis(w, tok, axis=1)
    ex = jnp.arange(n_experts, dtype=jnp.int32)
    counts = jnp.sum((e[:, :, None] == ex).astype(jnp.int32), axis=1)
    start = jnp.cumsum(counts, axis=1) - counts
    nblk_e = (counts + MOE_BLOCK - 1) // MOE_BLOCK
    bend = jnp.cumsum(nblk_e, axis=1)
    bstart = bend - nblk_e
    total = bend[:, -1:]
    b = jnp.arange(nb, dtype=jnp.int32)[None, :]
    bb = jnp.minimum(b, total - 1)
    eb = jnp.sum((bend[:, None, :] <= bb[:, :, None]).astype(jnp.int32), axis=2)
    sel = eb[:, :, None] == ex
    pick = lambda tbl: jnp.sum(jnp.where(sel, tbl[:, None, :], 0), axis=2)
    j = bb - pick(bstart)
    blk_start = pick(start) + j * MOE_BLOCK
    blk_cnt = jnp.where(b < total, jnp.clip(pick(counts) - j * MOE_BLOCK, 0, MOE_BLOCK), 0)
    flat = lambda t_: t_.astype(jnp.int32).reshape(-1)
    pad = lambda t_: jnp.pad(t_, ((0, 0), (0, MOE_BLOCK))).reshape(nch, 1, rows + MOE_BLOCK)
    return flat(eb), flat(blk_start), flat(blk_cnt), pad(tok), pad(tw), nb


def _moe(hn_t, h2, route, wg_b, wu_b, wd_b, *, chunk):
    n, d = h2.shape
    n_experts, _, ff = wg_b.shape
    chunk = min(chunk, n)
    assert n % chunk == 0 and d == SUB * LANE
    nch = n // chunk
    blk_e, blk_start, blk_cnt, tok, tw, nb = _dispatch(route, chunk, n_experts)
    tab = pl.BlockSpec((1, 1, tok.shape[2]), lambda c, b, *_: (c, 0, 0), memory_space=pltpu.SMEM)
    once = lambda shp: pl.BlockSpec(shp, lambda c, b, *_: (c, 0), pipeline_mode=pl.Buffered(1))
    in_hbm = pl.BlockSpec(memory_space=pl.ANY)
    grid_spec = pltpu.PrefetchScalarGridSpec(
        num_scalar_prefetch=3,
        grid=(nch, nb),
        in_specs=[tab, tab, pl.BlockSpec((chunk * SUB, LANE), lambda c, b, *_: (c, 0)),
                  in_hbm, in_hbm, in_hbm, in_hbm],
        out_specs=once((chunk, d)),
        scratch_shapes=[pltpu.VMEM(((2 * chunk + 1) * SUB, LANE), F32), pltpu.VMEM((MOE_BLOCK * SUB, LANE), F32),
                        pltpu.VMEM((MOE_BLOCK * SUB, LANE), F32),
                        pltpu.VMEM((WEIGHT_SLOTS, d, ff), BF16), pltpu.VMEM((WEIGHT_SLOTS, d, ff), BF16),
                        pltpu.VMEM((WEIGHT_SLOTS, ff, d), BF16), pltpu.SemaphoreType.DMA((WEIGHT_SLOTS, 3)),
                        pltpu.SemaphoreType.DMA((1,))],
    )
    return pl.pallas_call(
        functools.partial(_moe_kernel, nb=nb, total=nch * nb),
        grid_spec=grid_spec,
        out_shape=jax.ShapeDtypeStruct((n, d), F32),
        compiler_params=_cparams("arbitrary", "arbitrary"),
        name="moe",
    )(blk_e, blk_start, blk_cnt, tok, tw, hn_t, h2, wg_b, wu_b, wd_b)


def _tile_row(g, reps, scale=1.0):
    return (jnp.tile(g.astype(F32), reps) * scale)[None, :]


def _layer_weights(p):
    d, in_cols = p["w_in"].shape
    aw = d // 2
    bw = d - aw
    pad = 3 * aw + 4 * bw + ROUTE_LANES - in_cols
    n_experts = p["w_re"].shape[1]
    w_r = jnp.concatenate([p["w_rg"], p["w_re"],
                           jnp.zeros((d, ROUTE_LANES - N_GROUPS - n_experts), F32)], axis=1)
    b_r = jnp.concatenate([p["b_rg"], p["b_re"], jnp.zeros((ROUTE_LANES - N_GROUPS - n_experts,), F32)])
    lane_pad = lambda v: jnp.concatenate([v.astype(F32), jnp.zeros((ROUTE_LANES - v.shape[0],), F32)])[None, :]
    return dict(
        aw=aw, bw=bw,
        g_mix=p["g_mix"][None, :],
        w_in=jnp.pad(p["w_in"], ((0, 0), (0, pad))).astype(BF16),
        gq=_tile_row(p["g_qa"], aw // A_HEAD_DIM, A_HEAD_DIM ** -0.5),
        gk=_tile_row(p["g_ka"], aw // A_HEAD_DIM),
        conv_w=p["conv_w"],
        alog=lane_pad(p["a_log"]),
        dtb=lane_pad(jnp.concatenate([p["dt_bias"], jnp.zeros_like(p["dt_bias"])])),
        w_out=p["w_out"].astype(BF16),
        g_gdn=_tile_row(p["g_gdn"], bw // B_HEAD_DIM),
        g_xattn=p["g_xattn"][None, :],
        w_xq=p["w_xq"].astype(BF16),
        g_xq=_tile_row(p["g_xq"], p["w_xq"].shape[1] // B_HEAD_DIM, B_HEAD_DIM ** -0.5),
        w_xo=p["w_xo"].astype(BF16),
        g_ffn=p["g_ffn"][None, :],
        w_r=jnp.stack([w_r.astype(BF16), (w_r - w_r.astype(BF16).astype(F32)).astype(BF16)]), b_r=b_r[None, :],
        w_gate=p["w_gate"].astype(BF16), w_up=p["w_up"].astype(BF16), w_down=p["w_down"].astype(BF16),
    )


def _gdn(u_pre, prev, ab, s0, pw, *, seg, conv_seg, chunks, scan_seqs, scan_steps):
    w, uv, qg, kt, qk, last = _gdn_prep(u_pre, prev, ab, pw["conv_w"], pw["alog"], pw["dtb"],
                                        seg=seg, conv_seg=conv_seg, chunks=chunks)
    return _gdn_scan(w, uv, qg, kt, qk, last, s0, seg=seg, seqs=scan_seqs, steps=scan_steps)


def _prompt_layer(h, mem, pw, praw):
    nb_, s, d = h.shape
    assert nb_ == 1
    x = h.reshape(s, d)
    aw, bw = pw["aw"], pw["bw"]
    q, k, v, u_pre, z, ab = _inproj(x, pw["g_mix"], pw["w_in"], pw["gq"], pw["gk"],
                                    aw=aw, bw=bw, tm=512)
    oa = _dil_attn(q, k, v, rows=DILATIONS[-1][0])
    gdn_chunks = 8
    blk_rows = gdn_chunks * GDN_CHUNK
    tails = u_pre.reshape(s // blk_rows, blk_rows, 3 * bw)[:, blk_rows - PREV_ROWS:, :]
    prev = jnp.concatenate([jnp.zeros((1, PREV_ROWS, 3 * bw), F32), tails[:-1]], axis=0)
    s0 = jnp.zeros((1, B_HEADS, B_HEAD_DIM, B_HEAD_DIM), F32)
    og, s_fin = _gdn(u_pre, prev, ab, s0, pw, seg=GDN_CHUNK, conv_seg=blk_rows, chunks=gdn_chunks,
                     scan_seqs=1, scan_steps=8)
    mk, mv = _mem_kv(mem.reshape(mem.shape[1], d), praw["g_mem"][None, :], praw["w_xk"].astype(BF16),
                     praw["w_xv"].astype(BF16), _tile_row(praw["g_xk"], bw // B_HEAD_DIM))
    h2, hn, route = _post_prompt(oa, og, z, x, pw, mk, mv, tm=512)
    y = _moe(hn, h2, route, pw["w_gate"], pw["w_up"], pw["w_down"], chunk=MOE_CHUNK)
    keep = min(DILATIONS[-1][0], s)
    heads = aw // A_HEAD_DIM
    tail = lambda t_: jnp.transpose(t_[:, s - keep:, :], (1, 0, 2)).reshape(1, keep, heads, A_HEAD_DIM)
    new_k, new_v = tail(k), tail(v)
    conv_new = u_pre[s - (CONV_WIDTH - 1):].reshape(1, CONV_WIDTH - 1, 3 * bw)
    xh = mk.shape[1] // B_HEAD_DIM
    return (y.reshape(1, s, d), new_k, new_v, conv_new, s_fin,
            mk.reshape(1, -1, xh, B_HEAD_DIM), mv.reshape(1, -1, xh, B_HEAD_DIM))


def _sample_layer(h, win_k, win_v, conv_prev, s0, mem_k, mem_v, pw):
    nseq, t_new, d = h.shape
    aw, bw = pw["aw"], pw["bw"]
    n = nseq * t_new
    assert t_new == 8 and GDN_CHUNK % t_new == 0
    x = h.reshape(n, d)
    q, k, v, u_pre, z, ab = _inproj(x, pw["g_mix"], pw["w_in"], pw["gq"], pw["gk"],
                                    aw=aw, bw=bw, tm=512)
    w_buf = win_k.shape[1]
    heads = aw // A_HEAD_DIM
    win_t = lambda c_: jnp.transpose(c_, (0, 2, 3, 1)).reshape(nseq, aw, w_buf)
    new_t = lambda t_: jnp.transpose(t_.reshape(aw // LANE, nseq, t_new, LANE), (1, 0, 3, 2)).reshape(nseq, aw, t_new)
    win_back = lambda c_: jnp.transpose(c_.reshape(nseq, heads, A_HEAD_DIM, w_buf), (0, 3, 1, 2))
    q_rows = jnp.transpose(q.reshape(aw // LANE, nseq, t_new, LANE), (1, 2, 0, 3)).reshape(nseq, t_new, aw)
    new_kt, new_vt, oa = _step_attn(q_rows, new_t(k), new_t(v), win_t(win_k), win_t(win_v))
    new_k, new_v = win_back(new_kt), win_back(new_vt)
    prev = jnp.concatenate([jnp.zeros((nseq, PREV_ROWS - (CONV_WIDTH - 1), 3 * bw), F32),
                            conv_prev.astype(F32)], axis=1)
    og, s_fin = _gdn(u_pre, prev, ab, s0, pw, seg=t_new, conv_seg=t_new, chunks=4, scan_seqs=8, scan_steps=1)
    mem_rows = lambda m_: m_.reshape(nseq, -1, m_.shape[3])
    h2, hn, route = _post_sample(oa.reshape(n, aw), og, z, x, pw, mem_rows(mem_k), mem_rows(mem_v),
                                 t_new=t_new, seqs=8)
    y = _moe(hn, h2, route, pw["w_gate"], pw["w_up"], pw["w_down"], chunk=MOE_CHUNK)
    conv_new = u_pre.reshape(nseq, t_new, 3 * bw)[:, t_new - (CONV_WIDTH - 1):, :]
    return (y.reshape(nseq, t_new, d), new_k, new_v, conv_new, s_fin)


def kernel(x_prompt, x_sample, mem_prompt, cache_win_k, cache_win_v, state_conv, state_delta, cache_mem_k, cache_mem_v, g_mix, w_in, g_qa, g_ka, conv_w, a_log, dt_bias, g_gdn, w_out, g_xattn, g_mem, w_xq, w_xk, w_xv, g_xq, g_xk, w_xo, g_ffn, w_rg, b_rg, w_re, b_re, w_gate, w_up, w_down):
    depth = w_in.shape[0]
    hp, hs = x_prompt, x_sample
    outs = [[] for _ in range(10)]
    for l in range(depth):
        praw = dict(g_mix=g_mix[l], w_in=w_in[l], g_qa=g_qa[l], g_ka=g_ka[l], conv_w=conv_w[l],
                    a_log=a_log[l], dt_bias=dt_bias[l], g_gdn=g_gdn[l], w_out=w_out[l],
                    g_xattn=g_xattn[l], g_mem=g_mem[l], w_xq=w_xq[l], w_xk=w_xk[l], w_xv=w_xv[l],
                    g_xq=g_xq[l], g_xk=g_xk[l], w_xo=w_xo[l], g_ffn=g_ffn[l], w_rg=w_rg[l],
                    b_rg=b_rg[l], w_re=w_re[l], b_re=b_re[l], w_gate=w_gate[l], w_up=w_up[l],
                    w_down=w_down[l])
        pw = _layer_weights(praw)
        hp, k_p, v_p, c_p, s_p, mk, mv = _prompt_layer(hp, mem_prompt, pw, praw)
        hs, k_s, v_s, c_s, s_s = _sample_layer(hs, cache_win_k[l], cache_win_v[l], state_conv[l],
                                               state_delta[l].astype(F32), cache_mem_k[l], cache_mem_v[l], pw)
        for lst, val in zip(outs, (k_p, v_p, c_p, s_p, mk, mv, k_s, v_s, c_s, s_s)):
            lst.append(val)
    st = [jnp.stack(o) for o in outs]
    st[3] = st[3].astype(state_delta.dtype)
    st[9] = st[9].astype(state_delta.dtype)
    return (hp, hs, *st)
```

```python
import functools

import jax
import jax.numpy as jnp
from jax import lax
from jax.experimental import pallas as pl
from jax.experimental.pallas import tpu as pltpu

F32 = jnp.float32
BF16 = jnp.bfloat16
EPS = 1e-6
NEG_INF = float("-inf")

A_HEAD_DIM = 64
B_HEAD_DIM = 128
B_HEADS = 4
BAND = 128
DILATIONS = ((128, 1), (512, 4), (2048, 16))
GDN_CHUNK = 64
CONV_WIDTH = 4
PREV_ROWS = 8
N_GROUPS = 4
GROUP_EXPERTS = 8
SUB, LANE = 8, 128
BF16_ROWS = 16
FAST_STRIDE = 4
ROUTE_LANES = LANE
MOE_BLOCK = 160
MOE_CHUNK = 2048
WEIGHT_SLOTS = 3
VMEM_LIMIT = 56 * 1024 * 1024


def _cparams(*sem):
    return pltpu.CompilerParams(dimension_semantics=sem, vmem_limit_bytes=VMEM_LIMIT)


def _bdot(a, b):
    return jnp.dot(a.astype(BF16), b.astype(BF16), preferred_element_type=F32)


def _bdot_nt(a, b):
    return lax.dot_general(a.astype(BF16), b.astype(BF16), (((1,), (1,)), ((), ())),
                           preferred_element_type=F32)


def _split3(x):
    hi = x.astype(BF16)
    r1 = x - hi.astype(F32)
    mid = r1.astype(BF16)
    lo = (r1 - mid.astype(F32)).astype(BF16)
    return hi, mid, lo


def _dot_exact_lhs(a01, x):
    a = a01.astype(BF16)
    hi, mid, lo = _split3(x)
    d = lambda p: jnp.dot(a, p, preferred_element_type=F32)
    return d(hi) + d(mid) + d(lo)


def _lane_tile_rms(x):
    parts = []
    for c in range(x.shape[1] // LANE):
        xc = x[:, c * LANE:(c + 1) * LANE]
        parts.append(xc * lax.rsqrt(jnp.mean(xc * xc, axis=-1, keepdims=True) + EPS))
    return jnp.concatenate(parts, axis=1)


def _half_tile_rms(x):
    half = LANE // 2
    lo = lax.broadcasted_iota(jnp.int32, (x.shape[0], LANE), 1) < half
    parts = []
    for c in range(x.shape[1] // LANE):
        xc = x[:, c * LANE:(c + 1) * LANE]
        sq = xc * xc
        s_lo = jnp.sum(jnp.where(lo, sq, 0.0), axis=-1, keepdims=True)
        s_hi = jnp.sum(jnp.where(lo, 0.0, sq), axis=-1, keepdims=True)
        parts.append(xc * lax.rsqrt(jnp.where(lo, s_lo, s_hi) * (1.0 / half) + EPS))
    return jnp.concatenate(parts, axis=1)


def _rms(x, g):
    return x * lax.rsqrt(jnp.mean(x * x, axis=-1, keepdims=True) + EPS) * g


def _sigmoid(x):
    return 1.0 / (1.0 + jnp.exp(-x))


def _inproj_kernel(x_ref, g_ref, w_ref, gq_ref, gk_ref,
                   q_ref, k_ref, v_ref, u_ref, z_ref, ab_ref, *, aw, bw):
    nb = _rms(x_ref[...], g_ref[...]).astype(BF16)

    def proj(lo, hi):
        return jnp.dot(nb, w_ref[:, lo:hi], preferred_element_type=F32)

    def put_slabs(ref, val):
        for p in range(aw // LANE):
            ref[p] = val[:, p * LANE:(p + 1) * LANE]

    assert A_HEAD_DIM * 2 == LANE
    put_slabs(q_ref, _half_tile_rms(proj(0, aw)) * gq_ref[...])
    put_slabs(k_ref, _half_tile_rms(proj(aw, 2 * aw)) * gk_ref[...])
    put_slabs(v_ref, proj(2 * aw, 3 * aw))
    u_ref[...] = proj(3 * aw, 3 * aw + 3 * bw)
    z_ref[...] = proj(3 * aw + 3 * bw, 3 * aw + 4 * bw)
    ab_ref[...] = proj(3 * aw + 4 * bw, 3 * aw + 4 * bw + ROUTE_LANES)


def _inproj(x, g_mix, w_in_b, gq_t, gk_t, *, aw, bw, tm):
    n, d = x.shape
    cols = w_in_b.shape[1]
    row = lambda w: pl.BlockSpec((tm, w), lambda i: (i, 0))
    full = lambda a: pl.BlockSpec(a.shape, lambda i: (0, 0))
    out_w = (3 * bw, bw, ROUTE_LANES)
    slab = pl.BlockSpec((aw // LANE, tm, LANE), lambda i: (0, i, 0))
    return pl.pallas_call(
        functools.partial(_inproj_kernel, aw=aw, bw=bw),
        grid=(n // tm,),
        in_specs=[row(d), full(g_mix), pl.BlockSpec((d, cols), lambda i: (0, 0)),
                  full(gq_t), full(gk_t)],
        out_specs=[slab] * 3 + [row(w) for w in out_w],
        out_shape=[jax.ShapeDtypeStruct((aw // LANE, n, LANE), F32)] * 3
                  + [jax.ShapeDtypeStruct((n, w), F32) for w in out_w],
        compiler_params=_cparams("parallel"),
        name="inproj",
    )(x, g_mix, w_in_b, gq_t, gk_t)


def _dil_attn_kernel(q_ref, kp_ref, kc_ref, vp_ref, vc_ref, o_ref, od_ref, ld_ref, stage_ref):
    n = pl.program_id(1)
    rows = q_ref.shape[1]
    qi = lax.broadcasted_iota(jnp.int32, (BAND, 2 * BAND), 0)
    kj = lax.broadcasted_iota(jnp.int32, (BAND, 2 * BAND), 1)
    band = (kj >= qi) & (kj <= qi + BAND)
    bias = jnp.where(band, 0.0, NEG_INF)
    bias_first = jnp.where(band & ((kj >= BAND) | (n > 0)), 0.0, NEG_INF)
    lo_half = lax.broadcasted_iota(jnp.int32, (BAND, LANE), 1) < A_HEAD_DIM

    for di, (window, dil) in enumerate(DILATIONS):
        assert window // dil == BAND and rows % (BAND * dil) == 0
        span = BAND * dil
        srcs = (q_ref, kc_ref, vc_ref, kp_ref, vp_ref)
        if dil > FAST_STRIDE:
            inner = dil // FAST_STRIDE
            assert inner <= FAST_STRIDE and dil % FAST_STRIDE == 0
            for ti, ref in enumerate(srcs):
                for a in range(FAST_STRIDE):
                    stage_ref[ti, a] = ref[0, pl.ds(a, rows // FAST_STRIDE, stride=FAST_STRIDE), :]
            take = lambda ti, r, blk: stage_ref[ti, r % FAST_STRIDE,
                                                pl.ds(r // FAST_STRIDE + blk * BAND * inner, BAND, stride=inner), :]
        else:
            take = lambda ti, r, blk: srcs[ti][0, pl.ds(r + blk * span, BAND, stride=dil), :]
        last = rows // span - 1
        for r in range(dil):
            k_prev = take(3, r, last).astype(BF16)
            v_prev = take(4, r, last).astype(BF16)
            for j in range(rows // span):
                start = r + j * span
                q = take(0, r, j)
                k_cur, v_cur = take(1, r, j).astype(BF16), take(2, r, j).astype(BF16)
                kcat = jnp.concatenate([k_prev, k_cur], axis=0)
                vcat = jnp.concatenate([v_prev, v_cur], axis=0)
                k_prev, v_prev = k_cur, v_cur
                res = []
                for half in range(2):
                    keep = lo_half if half == 0 else jnp.logical_not(lo_half)
                    s = _bdot_nt(jnp.where(keep, q, 0.0), kcat) + (bias if j else bias_first)
                    m = jnp.max(s, axis=-1, keepdims=True)
                    e = jnp.exp(s - m)
                    den = jnp.sum(e, axis=-1, keepdims=True)
                    o = jnp.dot(e.astype(BF16), vcat, preferred_element_type=F32) / den
                    res.append((o, m + jnp.log(den)))
                od_ref[di, pl.ds(start, BAND, stride=dil), :] = jnp.where(lo_half, res[0][0], res[1][0])
                ld_ref[di, pl.ds(start, BAND, stride=dil), :] = jnp.where(lo_half, res[0][1], res[1][1])

    step = 256
    for r0 in range(0, rows, step):
        ls = [ld_ref[di, r0:r0 + step, :] for di in range(len(DILATIONS))]
        mm = functools.reduce(jnp.maximum, ls)
        es = [jnp.exp(l - mm) for l in ls]
        num = functools.reduce(lambda a, b: a + b, [e * od_ref[di, r0:r0 + step, :] for di, e in enumerate(es)])
        o_ref[0, r0:r0 + step, :] = num / functools.reduce(lambda a, b: a + b, es)


def _dil_attn(q, k, v, *, rows):
    pairs, s, lanes = q.shape
    assert s % rows == 0 and lanes == LANE
    cur = pl.BlockSpec((1, rows, LANE), lambda p, n: (p, n, 0))
    prev = pl.BlockSpec((1, rows, LANE), lambda p, n: (p, jnp.maximum(n - 1, 0), 0))
    return pl.pallas_call(
        _dil_attn_kernel,
        grid=(pairs, s // rows),
        in_specs=[cur, prev, cur, prev, cur],
        out_specs=cur,
        out_shape=jax.ShapeDtypeStruct(q.shape, F32),
        scratch_shapes=[pltpu.VMEM((len(DILATIONS), rows, LANE), F32)] * 2
                       + [pltpu.VMEM((5, FAST_STRIDE, rows // FAST_STRIDE, LANE), F32)],
        compiler_params=_cparams("parallel", "arbitrary"),
        name="dil_attn",
    )(q, k, k, v, v)


def _step_attn_kernel(q_ref, kn_ref, vn_ref, kc_ref, vc_ref, ok_ref, ov_ref, oa_ref,
                      kb_ref, vb_ref, *, w_buf, t_new):
    aw = q_ref.shape[2]
    heads = aw // A_HEAD_DIM
    lanes = LANE
    kt, vt = kc_ref[0], vc_ref[0]
    lead = jnp.zeros((aw, lanes - t_new), F32)
    kn = jnp.concatenate([lead, kn_ref[0]], axis=1)
    vn = jnp.concatenate([lead, vn_ref[0]], axis=1)
    is_new = lax.broadcasted_iota(jnp.int32, (aw, lanes), 1) >= lanes - t_new
    for src, new, dst in ((kt, kn, ok_ref), (vt, vn, ov_ref)):
        rolled = pltpu.roll(src, w_buf - t_new, axis=1)
        dst[0, :, 0:w_buf - lanes] = rolled[:, 0:w_buf - lanes]
        dst[0, :, w_buf - lanes:w_buf] = jnp.where(is_new, new, rolled[:, w_buf - lanes:w_buf])
    kb_ref[...] = kt.astype(BF16)
    vb_ref[...] = vt.astype(BF16)

    rows = heads * t_new
    ri = lax.broadcasted_iota(jnp.int32, (rows, aw), 0)
    li = lax.broadcasted_iota(jnp.int32, (rows, aw), 1)
    q_rep = jnp.concatenate([q_ref[0]] * heads, axis=0)
    q_blk = jnp.where((ri // t_new) == (li // A_HEAD_DIM), q_rep, 0.0).astype(BF16)
    ncol = w_buf + lanes
    s_all = jnp.concatenate([jnp.dot(q_blk, kb_ref[...], preferred_element_type=F32),
                             jnp.dot(q_blk, kn.astype(BF16), preferred_element_type=F32)], axis=1)

    def branch(window, dil, col0):
        s = s_all[:, col0:]
        t = lax.broadcasted_iota(jnp.int32, s.shape, 0) % t_new
        cidx = lax.broadcasted_iota(jnp.int32, s.shape, 1) + col0
        r = jnp.where(cidx < w_buf, cidx, cidx - (lanes - t_new))
        diff = w_buf + t - r
        valid = (diff >= 0) & (diff <= window) & ((diff & (dil - 1)) == 0)
        valid = valid & ((cidx < w_buf) | (cidx >= ncol - t_new))
        s = jnp.where(valid, s, NEG_INF)
        m = jnp.max(s, axis=-1, keepdims=True)
        e = jnp.exp(s - m)
        den = jnp.sum(e, axis=-1, keepdims=True)
        return e, den, m + jnp.log(den)

    parts = []
    for window, dil in DILATIONS:
        assert dil & (dil - 1) == 0
        col0 = max(0, (w_buf - window) // 128 * 128)
        parts.append((col0,) + branch(window, dil, col0))
    mm = functools.reduce(jnp.maximum, [p[3] for p in parts])
    wexp = [jnp.exp(p[3] - mm) for p in parts]
    wsum = functools.reduce(lambda a, b: a + b, wexp)
    p_tot = jnp.zeros((rows, ncol), F32)
    for (col0, e, den, _), we in zip(parts, wexp):
        pe = e * (we / (wsum * den))
        if col0:
            pe = jnp.concatenate([jnp.zeros((rows, col0), F32), pe], axis=1)
        p_tot = p_tot + pe
    o = _bdot_nt(p_tot[:, :w_buf], vb_ref[...]) + _bdot_nt(p_tot[:, w_buf:], vn)
    lh = lax.broadcasted_iota(jnp.int32, (t_new, aw), 1) // A_HEAD_DIM
    acc = jnp.zeros((t_new, aw), F32)
    for h in range(heads):
        acc = acc + jnp.where(lh == h, o[h * t_new:(h + 1) * t_new, :], 0.0)
    oa_ref[0] = acc


def _step_attn(q, kn_t, vn_t, cache_kt, cache_vt):
    nseq, aw, w_buf = cache_kt.shape
    t_new = q.shape[1]
    assert w_buf % LANE == 0 and t_new % SUB == 0 and t_new <= LANE and kn_t.shape[2] == t_new
    qs = pl.BlockSpec((1, t_new, aw), lambda i: (i, 0, 0))
    new = pl.BlockSpec((1, aw, t_new), lambda i: (i, 0, 0))
    win = pl.BlockSpec((1, aw, w_buf), lambda i: (i, 0, 0))
    return pl.pallas_call(
        functools.partial(_step_attn_kernel, w_buf=w_buf, t_new=t_new),
        grid=(nseq,),
        in_specs=[qs, new, new, win, win],
        out_specs=[win, win, qs],
        out_shape=[jax.ShapeDtypeStruct(cache_kt.shape, F32), jax.ShapeDtypeStruct(cache_vt.shape, F32),
                   jax.ShapeDtypeStruct(q.shape, F32)],
        scratch_shapes=[pltpu.VMEM((aw, w_buf), BF16), pltpu.VMEM((aw, w_buf), BF16)],
        compiler_params=_cparams("parallel"),
        name="step_attn",
    )(q, kn_t, vn_t, cache_kt, cache_vt)


def _gdn_prep_kernel(u_ref, prev_ref, ab_ref, cw_ref, alog_ref, dtb_ref,
                     w_ref, uv_ref, qg_ref, kt_ref, qk_ref, last_ref, full_ref, act_ref,
                     *, seg, conv_seg, chunks):
    c = GDN_CHUNK
    dk = B_HEAD_DIM
    bw = B_HEADS * dk
    ii = lax.broadcasted_iota(jnp.int32, (c, c), 0)
    jj = lax.broadcasted_iota(jnp.int32, (c, c), 1)
    same = (ii // seg) == (jj // seg)
    incl = same & (ii >= jj)
    strict = same & (ii > jj)
    eye = ii == jj
    ones_c = jnp.ones((c, c), BF16)
    seg_cols = jnp.where((lax.broadcasted_iota(jnp.int32, (c, dk), 0) % seg)
                         == lax.broadcasted_iota(jnp.int32, (c, dk), 1), 1.0, 0.0)
    cw = cw_ref[...]
    rows = chunks * c

    for sgi in range(rows // conv_seg):
        base = sgi * (conv_seg + PREV_ROWS)
        full_ref[base:base + PREV_ROWS, :] = prev_ref[sgi]
        full_ref[base + PREV_ROWS:base + PREV_ROWS + conv_seg, :] = u_ref[sgi * conv_seg:(sgi + 1) * conv_seg, :]
        y = jnp.zeros((conv_seg, 3 * bw), F32)
        for j in range(CONV_WIDTH):
            off = base + PREV_ROWS - (CONV_WIDTH - 1) + j
            y = y + full_ref[off:off + conv_seg, :] * cw[j:j + 1, :]
        act_ref[sgi * conv_seg:(sgi + 1) * conv_seg, :] = y * _sigmoid(y)

    ri = lax.broadcasted_iota(jnp.int32, (rows, rows), 0)
    rj = lax.broadcasted_iota(jnp.int32, (rows, rows), 1)
    rsame = (ri // seg) == (rj // seg)
    ab = ab_ref[...]
    sp = ab + dtb_ref[...]
    sp = jnp.maximum(sp, 0.0) + jnp.log(1.0 + jnp.exp(-jnp.abs(sp)))
    g_all = -jnp.exp(alog_ref[...]) * sp
    gcum_all = _dot_exact_lhs(jnp.where(rsame & (ri >= rj), 1.0, 0.0), g_all)
    gtot_all = _dot_exact_lhs(jnp.where(rsame, 1.0, 0.0), g_all)
    beta_all = _sigmoid(ab)
    eye_f = jnp.where(eye, 1.0, 0.0)

    grow = []
    for ci in range(chunks):
        rs = slice(ci * c, (ci + 1) * c)
        diag = jnp.concatenate([jnp.where(eye, gcum_all[rs, h:h + 1], 0.0) for h in range(B_HEADS)], axis=1)
        grow.append(_dot_exact_lhs(ones_c, diag))

    prob = [(ci, h) for ci in range(chunks) for h in range(B_HEADS)]
    col = lambda arr, ci, lane: arr[ci * c:(ci + 1) * c, lane:lane + 1]
    gc = [col(gcum_all, ci, h) for ci, h in prob]
    gt = [col(gtot_all, ci, h) for ci, h in prob]
    beta = [col(beta_all, ci, B_HEADS + h) for ci, h in prob]
    q, k, v = [], [], []
    for ci, h in prob:
        rs = slice(ci * c, (ci + 1) * c)
        qh = act_ref[rs, h * dk:(h + 1) * dk]
        kh = act_ref[rs, bw + h * dk:bw + (h + 1) * dk]
        q.append(qh * lax.rsqrt(jnp.sum(qh * qh, axis=-1, keepdims=True) + EPS) * (dk ** -0.5))
        k.append(kh * lax.rsqrt(jnp.sum(kh * kh, axis=-1, keepdims=True) + EPS))
        v.append(act_ref[rs, 2 * bw + h * dk:2 * bw + (h + 1) * dk])
    e_incl = [jnp.exp(jnp.where(incl, gc[i] - grow[ci][:, h * c:(h + 1) * c], NEG_INF))
              for i, (ci, h) in enumerate(prob)]
    qkk = [_bdot_nt(jnp.concatenate([q[i], k[i]], axis=0), k[i]) for i in range(len(prob))]
    a = [beta[i] * jnp.where(strict, e_incl[i], 0.0) * qkk[i][c:, :] for i in range(len(prob))]
    blk = 8
    inblk = (ii // blk) == (jj // blk)
    d0 = [jnp.where(inblk, t, 0.0) for t in a]
    d2 = [_bdot(t, t) for t in d0]
    d4 = [_bdot(t, t) for t in d2]
    x = [(eye_f - t) + _bdot(eye_f - t, t2) for t, t2 in zip(d0, d2)]
    x = [t + _bdot(t, t4) for t, t4 in zip(x, d4)]
    while blk < seg:
        off_blk = ((ii // (2 * blk)) == (jj // (2 * blk))) & ((ii // blk) != (jj // blk))
        xe = [_bdot(t, jnp.where(off_blk, ta, 0.0)) for t, ta in zip(x, a)]
        x = [t - _bdot(te, t) for t, te in zip(x, xe)]
        blk *= 2
    gamma = [jnp.exp(t) for t in gc]
    wuv = [_bdot(x[i], jnp.concatenate([(beta[i] * gamma[i]) * k[i], beta[i] * v[i]], axis=1))
           for i in range(len(prob))]
    for i, (ci, h) in enumerate(prob):
        rs = slice(ci * c, (ci + 1) * c)
        hs = slice(h * dk, (h + 1) * dk)
        w_ref[rs, hs] = wuv[i][:, :dk]
        uv_ref[rs, hs] = wuv[i][:, dk:]
        qg_ref[rs, hs] = gamma[i] * q[i]
        kt_ref[rs, hs] = k[i] * jnp.exp(gt[i] - gc[i])
        qk = qkk[i][:c, :] * e_incl[i]
        qk_ref[rs, hs] = (jnp.concatenate([qk, jnp.zeros((c, dk - c), F32)], axis=1) if seg == c
                          else _bdot(qk, seg_cols))
        last_ref[rs, hs] = jnp.broadcast_to(jnp.exp(gt[i]), (c, dk))


def _gdn_prep(u_pre, prev, ab, conv_w, alog_row, dtb_row, *, seg, conv_seg, chunks):
    n, cw = u_pre.shape
    bw = cw // 3
    c = GDN_CHUNK
    rows = chunks * c
    assert n % rows == 0 and c % seg == 0 and seg % 8 == 0 and rows % conv_seg == 0 and conv_seg % seg == 0
    ncs = rows // conv_seg
    row = lambda w: pl.BlockSpec((rows, w), lambda i: (i, 0))
    full = lambda a: pl.BlockSpec(a.shape, lambda i: (0, 0))
    return pl.pallas_call(
        functools.partial(_gdn_prep_kernel, seg=seg, conv_seg=conv_seg, chunks=chunks),
        grid=(n // rows,),
        in_specs=[row(cw), pl.BlockSpec((ncs, PREV_ROWS, cw), lambda i: (i, 0, 0)),
                  row(ROUTE_LANES), full(conv_w), full(alog_row), full(dtb_row)],
        out_specs=[row(bw)] * 6,
        out_shape=[jax.ShapeDtypeStruct((n, bw), F32)] * 6,
        scratch_shapes=[pltpu.VMEM((ncs * (conv_seg + PREV_ROWS), cw), F32), pltpu.VMEM((rows, cw), F32)],
        compiler_params=_cparams("parallel"),
        name=f"gdn_prep_{seg}",
    )(u_pre, prev, ab, conv_w, alog_row, dtb_row)


def _gdn_scan_kernel(w_ref, uv_ref, qg_ref, kt_ref, qk_ref, last_ref, s0_ref, o_ref, s_ref, *, seg, steps):
    dk = B_HEAD_DIM

    @pl.when(pl.program_id(1) == 0)
    def _():
        s_ref[...] = s0_ref[...]

    rows = max(seg, BF16_ROWS)
    pad = rows - seg
    padr = lambda t: jnp.concatenate([t, jnp.zeros((pad, t.shape[1]), F32)], axis=0) if pad else t
    ii = lax.broadcasted_iota(jnp.int32, (dk, dk), 0)
    jj = lax.broadcasted_iota(jnp.int32, (dk, dk), 1)
    eye = jnp.where(ii == jj, 1.0, 0.0).astype(BF16)
    chains = [(sq, h) for sq in range(s_ref.shape[0]) for h in range(B_HEADS)]
    hs = lambda h: slice(h * dk, (h + 1) * dk)
    s = [s_ref[sq, h] for sq, h in chains]
    for c in range(steps):
        rs = [slice((sq * steps + c) * seg, (sq * steps + c + 1) * seg) for sq, _ in chains]
        sb = [t.astype(BF16) for t in s]
        ws = [_bdot(padr(w_ref[rs[i], hs(h)]), sb[i]) for i, (_, h) in enumerate(chains)]
        ub = [(padr(uv_ref[rs[i], hs(h)]) - ws[i]).astype(BF16) for i, (_, h) in enumerate(chains)]
        ktt = [_bdot_nt(eye, padr(kt_ref[rs[i], hs(h)])) for i, (_, h) in enumerate(chains)]
        s_new = [last_ref[rs[i], hs(h)][0:1, :] * s[i] + _bdot(ktt[i], ub[i]) for i, (_, h) in enumerate(chains)]
        for i, (_, h) in enumerate(chains):
            o = _bdot(padr(qg_ref[rs[i], hs(h)]), sb[i]) + _bdot(padr(qk_ref[rs[i], hs(h)])[:, :rows], ub[i])
            o_ref[rs[i], hs(h)] = o[:seg, :]
        s = s_new
    for i, (sq, h) in enumerate(chains):
        s_ref[sq, h] = s[i]


def _gdn_scan(w, uv, qg, kt, qk, last, s0, *, seg, seqs, steps):
    n, bw = w.shape
    nseq = s0.shape[0]
    per_seq = n // (nseq * seg)
    assert nseq % seqs == 0 and per_seq % steps == 0 and (seqs == 1 or steps == per_seq)
    blocks = per_seq // steps
    row = pl.BlockSpec((seqs * steps * seg, bw), lambda s, i: (s * blocks + i, 0))
    st = pl.BlockSpec((seqs,) + s0.shape[1:], lambda s, i: (s, 0, 0, 0))
    return pl.pallas_call(
        functools.partial(_gdn_scan_kernel, seg=seg, steps=steps),
        grid=(nseq // seqs, blocks),
        in_specs=[row] * 6 + [st],
        out_specs=[row, st],
        out_shape=[jax.ShapeDtypeStruct((n, bw), F32), jax.ShapeDtypeStruct(s0.shape, F32)],
        compiler_params=_cparams("parallel", "arbitrary"),
        name=f"gdn_scan_{seg}",
    )(w, uv, qg, kt, qk, last, s0)


def _mem_kv_kernel(mem_ref, g_ref, wk_ref, wv_ref, gk_ref, k_ref, v_ref):
    mn = _rms(mem_ref[...], g_ref[...]).astype(BF16)
    k = jnp.dot(mn, wk_ref[...], preferred_element_type=F32)
    k_ref[...] = _lane_tile_rms(k) * gk_ref[...]
    v_ref[...] = jnp.dot(mn, wv_ref[...], preferred_element_type=F32)


def _mem_kv(mem, g_mem, w_xk_b, w_xv_b, gk_t):
    m = mem.shape[0]
    xw = w_xk_b.shape[1]
    return pl.pallas_call(
        _mem_kv_kernel,
        out_shape=[jax.ShapeDtypeStruct((m, xw), F32)] * 2,
        compiler_params=pltpu.CompilerParams(vmem_limit_bytes=VMEM_LIMIT),
        name="mem_kv",
    )(mem, g_mem, w_xk_b, w_xv_b, gk_t)


def _route(logits):
    lane = lax.broadcasted_iota(jnp.int32, logits.shape, 1)
    lane_f = lane.astype(F32)
    big = float(ROUTE_LANES)
    lg = jnp.where(lane < N_GROUPS, logits, NEG_INF)
    mg = jnp.max(lg, axis=-1, keepdims=True)
    zg = jnp.sum(jnp.exp(lg - mg), axis=-1, keepdims=True)
    pg_top = 1.0 / zg
    gidx = jnp.min(jnp.where(lg == mg, lane_f, big), axis=-1, keepdims=True)
    e_lo = N_GROUPS + GROUP_EXPERTS * gidx
    emask = (lane_f >= e_lo) & (lane_f < e_lo + GROUP_EXPERTS)
    le = jnp.where(emask, logits, NEG_INF)
    me = jnp.max(le, axis=-1, keepdims=True)
    ee = jnp.exp(le - me)
    pe = ee / jnp.sum(ee, axis=-1, keepdims=True)
    pe = jnp.where(emask, pe, -1.0)
    p1 = jnp.max(pe, axis=-1, keepdims=True)
    i1 = jnp.min(jnp.where(pe == p1, lane_f, big), axis=-1, keepdims=True)
    pe2 = jnp.where(lane_f == i1, -1.0, pe)
    p2 = jnp.max(pe2, axis=-1, keepdims=True)
    i2 = jnp.min(jnp.where(pe2 == p2, lane_f, big), axis=-1, keepdims=True)
    den = p1 + p2
    w1 = pg_top * p1 / den
    w2 = pg_top * p2 / den
    out = jnp.where(lane == 0, i1 - N_GROUPS, 0.0)
    out = jnp.where(lane == 1, i2 - N_GROUPS, out)
    out = jnp.where(lane == 2, w1, out)
    return jnp.where(lane == 3, w2, out)


def _post_common(oa, og_ref, z_ref, h_ref, wout_ref, ggdn_ref, gx_ref, wxq_ref, gxq_ref):
    og = og_ref[...]
    z = z_ref[...]
    ob = _lane_tile_rms(og) * ggdn_ref[...] * (z * _sigmoid(z))
    cat = jnp.concatenate([oa, ob], axis=-1).astype(BF16)
    h1 = h_ref[...] + jnp.dot(cat, wout_ref[...], preferred_element_type=F32)
    hx = _rms(h1, gx_ref[...]).astype(BF16)
    q = jnp.dot(hx, wxq_ref[...], preferred_element_type=F32)
    qn = _lane_tile_rms(q) * gxq_ref[...]
    return h1, qn


def _mem_attend_rows(qn, mk, mv):
    outs = []
    for h in range(qn.shape[1] // B_HEAD_DIM):
        hs = slice(h * B_HEAD_DIM, (h + 1) * B_HEAD_DIM)
        s = _bdot_nt(qn[:, hs], mk[:, hs])
        m = jnp.max(s, axis=-1, keepdims=True)
        e = jnp.exp(s - m)
        p = e / jnp.sum(e, axis=-1, keepdims=True)
        outs.append(_bdot(p, mv[:, hs]))
    return jnp.concatenate(outs, axis=-1)


def _post_tail(h1, ox, wxo_ref, gffn_ref, wr_ref, br_ref, h2_ref, hn_ref, route_ref):
    h2 = h1 + jnp.dot(ox.astype(BF16), wxo_ref[...], preferred_element_type=F32)
    h2_ref[...] = h2
    hn = _rms(h2, gffn_ref[...])
    tm = hn.shape[0]
    for j in range(hn.shape[1] // LANE):
        hn_ref[pl.ds(j, tm, stride=SUB), :] = hn[:, j * LANE:(j + 1) * LANE]
    hn_hi = hn.astype(BF16)
    hn_mid = (hn - hn_hi.astype(F32)).astype(BF16)
    d = lambda a, b: jnp.dot(a, b, preferred_element_type=F32)
    logits = d(hn_hi, wr_ref[0]) + d(hn_hi, wr_ref[1]) + d(hn_mid, wr_ref[0]) + br_ref[...]
    route_ref[...] = _route(logits)


def _post_prompt_kernel(oa_ref, og_ref, z_ref, h_ref,
                        wout_ref, ggdn_ref, gx_ref, wxq_ref, gxq_ref, mk_ref, mv_ref,
                        wxo_ref, gffn_ref, wr_ref, br_ref, h2_ref, hn_ref, route_ref):
    oa = jnp.concatenate([oa_ref[p] for p in range(oa_ref.shape[0])], axis=-1)
    h1, qn = _post_common(oa, og_ref, z_ref, h_ref, wout_ref, ggdn_ref, gx_ref, wxq_ref, gxq_ref)
    ox = _mem_attend_rows(qn, mk_ref[...], mv_ref[...])
    _post_tail(h1, ox, wxo_ref, gffn_ref, wr_ref, br_ref, h2_ref, hn_ref, route_ref)


def _post_sample_kernel(oa_ref, og_ref, z_ref, h_ref,
                        wout_ref, ggdn_ref, gx_ref, wxq_ref, gxq_ref, mk_ref, mv_ref,
                        wxo_ref, gffn_ref, wr_ref, br_ref, h2_ref, hn_ref, route_ref, *, t_new):
    h1, qn = _post_common(oa_ref[...], og_ref, z_ref, h_ref, wout_ref, ggdn_ref, gx_ref, wxq_ref, gxq_ref)
    dh = B_HEAD_DIM
    heads = qn.shape[1] // dh
    rows = heads * t_new
    ncol = mk_ref.shape[1]
    own = ((lax.broadcasted_iota(jnp.int32, (rows, ncol), 0) // t_new)
           == (lax.broadcasted_iota(jnp.int32, (rows, ncol), 1) % heads))
    outs = []
    for sq in range(mk_ref.shape[0]):
        qs = qn[sq * t_new:(sq + 1) * t_new, :]
        q_rows = jnp.concatenate([qs[:, h * dh:(h + 1) * dh] for h in range(heads)], axis=0)
        s = jnp.where(own, _bdot_nt(q_rows, mk_ref[sq]), NEG_INF)
        m = jnp.max(s, axis=-1, keepdims=True)
        e = jnp.exp(s - m)
        o = _bdot(e / jnp.sum(e, axis=-1, keepdims=True), mv_ref[sq])
        outs.append(jnp.concatenate([o[h * t_new:(h + 1) * t_new, :] for h in range(heads)], axis=1))
    ox = jnp.concatenate(outs, axis=0)
    _post_tail(h1, ox, wxo_ref, gffn_ref, wr_ref, br_ref, h2_ref, hn_ref, route_ref)


def _post_weights_specs(weights):
    return [pl.BlockSpec(a.shape, lambda i, nd=a.ndim: (0,) * nd) for a in weights]


def _post_outs(n, d, tm):
    assert d == SUB * LANE
    row = lambda w: pl.BlockSpec((tm, w), lambda i: (i, 0))
    specs = [row(d), pl.BlockSpec((tm * SUB, LANE), lambda i: (i, 0)), row(ROUTE_LANES)]
    shapes = [jax.ShapeDtypeStruct((n, d), F32), jax.ShapeDtypeStruct((n * SUB, LANE), F32),
              jax.ShapeDtypeStruct((n, ROUTE_LANES), F32)]
    return specs, shapes


def _post_prompt(oa_slabs, og, z, h, pw, mk, mv, *, tm):
    n, d = h.shape
    aw = og.shape[1]
    row = lambda w: pl.BlockSpec((tm, w), lambda i: (i, 0))
    full = lambda a: pl.BlockSpec(a.shape, lambda i: (0, 0))
    slab = pl.BlockSpec((oa_slabs.shape[0], tm, LANE), lambda i: (0, i, 0))
    w1 = [pw["w_out"], pw["g_gdn"], pw["g_xattn"], pw["w_xq"], pw["g_xq"]]
    w2 = [pw["w_xo"], pw["g_ffn"], pw["w_r"], pw["b_r"]]
    specs, shapes = _post_outs(n, d, tm)
    return pl.pallas_call(
        _post_prompt_kernel,
        grid=(n // tm,),
        in_specs=[slab, row(aw), row(aw), row(d)] + _post_weights_specs(w1) + [full(mk), full(mv)]
                 + _post_weights_specs(w2),
        out_specs=specs, out_shape=shapes,
        compiler_params=_cparams("parallel"),
        name="post_prompt",
    )(oa_slabs, og, z, h, *w1, mk, mv, *w2)


def _post_sample(oa, og, z, h, pw, mk, mv, *, t_new, seqs):
    n, d = h.shape
    aw = og.shape[1]
    tm = t_new * seqs
    row = lambda w: pl.BlockSpec((tm, w), lambda i: (i, 0))
    mem = pl.BlockSpec((seqs,) + mk.shape[1:], lambda i: (i, 0, 0))
    w1 = [pw["w_out"], pw["g_gdn"], pw["g_xattn"], pw["w_xq"], pw["g_xq"]]
    w2 = [pw["w_xo"], pw["g_ffn"], pw["w_r"], pw["b_r"]]
    specs, shapes = _post_outs(n, d, tm)
    return pl.pallas_call(
        functools.partial(_post_sample_kernel, t_new=t_new),
        grid=(n // tm,),
        in_specs=[row(aw)] * 3 + [row(d)] + _post_weights_specs(w1) + [mem, mem]
                 + _post_weights_specs(w2),
        out_specs=specs, out_shape=shapes,
        compiler_params=_cparams("parallel"),
        name="post_sample",
    )(oa, og, z, h, *w1, mk, mv, *w2)


def _moe_kernel(blk_e_ref, blk_start_ref, blk_cnt_ref, tok_ref, tw_ref, x_ref, res_hbm,
                wg_hbm, wu_hbm, wd_hbm, out_ref, acc_ref, xs_ref, ys_ref, wg_buf, wu_buf, wd_buf, sem, res_sem,
                *, nb, total):
    c = pl.program_id(0)
    b = pl.program_id(1)
    chunk, d = out_ref.shape
    nl = d // LANE
    step = c * nb + b
    cnt = blk_cnt_ref[step]
    wslot = step % WEIGHT_SLOTS

    def weight_copies(u):
        e, s_ = blk_e_ref[u], u % WEIGHT_SLOTS
        return [pltpu.make_async_copy(hbm.at[e], buf.at[s_], sem.at[s_, i])
                for i, (hbm, buf) in enumerate(((wg_hbm, wg_buf), (wu_hbm, wu_buf), (wd_hbm, wd_buf)))]

    def start_fetch(u):
        @pl.when(blk_cnt_ref[jnp.minimum(u, total - 1)] * (u < total) > 0)
        def _():
            for cp in weight_copies(u):
                cp.start()

    res_copy = pltpu.make_async_copy(res_hbm.at[pl.ds(pl.multiple_of(c * chunk, chunk), chunk), :],
                                     out_ref, res_sem.at[0])

    @pl.when(b == 0)
    def _():
        res_copy.start()

    @pl.when(step == 0)
    def _():
        for u in range(WEIGHT_SLOTS - 1):
            start_fetch(jnp.int32(u))

    start_fetch(step + WEIGHT_SLOTS - 1)

    @pl.when(cnt > 0)
    def _():
        for cp in weight_copies(step):
            cp.wait()
        wg_ref, wu_ref, wd_ref = wg_buf.at[wslot], wu_buf.at[wslot], wd_buf.at[wslot]
        base = blk_start_ref[step]
        for r in range(MOE_BLOCK):
            t = tok_ref[0, 0, base + r] >> 1
            xs_ref[r * SUB:(r + 1) * SUB, :] = x_ref[pl.ds(pl.multiple_of(t * SUB, SUB), SUB), :]
        xb = jnp.concatenate([xs_ref[pl.ds(j, MOE_BLOCK, stride=SUB), :] for j in range(nl)],
                             axis=1).astype(BF16)
        hg = jnp.dot(xb, wg_ref[...], preferred_element_type=F32)
        hu = jnp.dot(xb, wu_ref[...], preferred_element_type=F32)
        act = (hg * _sigmoid(hg) * hu).astype(BF16)
        y = jnp.dot(act, wd_ref[...], preferred_element_type=F32)
        for j in range(nl):
            ys_ref[pl.ds(j, MOE_BLOCK, stride=SUB), :] = y[:, j * LANE:(j + 1) * LANE]
        for r in range(MOE_BLOCK):
            e = tok_ref[0, 0, base + r]
            slot = jnp.where(r < cnt, (e & 1) * chunk + (e >> 1), 2 * chunk)
            off = pl.multiple_of(slot * SUB, SUB)
            acc_ref[pl.ds(off, SUB), :] = tw_ref[0, 0, base + r] * ys_ref[r * SUB:(r + 1) * SUB, :]

    @pl.when(b == nb - 1)
    def _():
        res_copy.wait()
        rows_per = 256
        for r0 in range(0, chunk, rows_per):
            for j in range(nl):
                out_ref[r0:r0 + rows_per, j * LANE:(j + 1) * LANE] = (
                    out_ref[r0:r0 + rows_per, j * LANE:(j + 1) * LANE]
                    + acc_ref[pl.ds(r0 * SUB + j, rows_per, stride=SUB), :]
                    + acc_ref[pl.ds((chunk + r0) * SUB + j, rows_per, stride=SUB), :])


def _dispatch(route, chunk, n_experts):
    n = route.shape[0]
    nch = n // chunk
    rows = 2 * chunk
    nb = rows // MOE_BLOCK + n_experts
    e = route[:, :2].astype(jnp.int32).reshape(nch, rows)
    w = route[:, 2:4].reshape(nch, rows)
    tok = jnp.argsort(e, axis=1, stable=True).astype(jnp.int32)
    tw = jnp.take_along_axis(w, tok, axis=1)
    ex = jnp.arange(n_experts, dtype=jnp.int32)
    counts = jnp.sum((e[:, :, None] == ex).astype(jnp.int32), axis=1)
    start = jnp.cumsum(counts, axis=1) - counts
    nblk_e = (counts + MOE_BLOCK - 1) // MOE_BLOCK
    bend = jnp.cumsum(nblk_e, axis=1)
    bstart = bend - nblk_e
    total = bend[:, -1:]
    b = jnp.arange(nb, dtype=jnp.int32)[None, :]
    bb = jnp.minimum(b, total - 1)
    eb = jnp.sum((bend[:, None, :] <= bb[:, :, None]).astype(jnp.int32), axis=2)
    sel = eb[:, :, None] == ex
    pick = lambda tbl: jnp.sum(jnp.where(sel, tbl[:, None, :], 0), axis=2)
    j = bb - pick(bstart)
    blk_start = pick(start) + j * MOE_BLOCK
    blk_cnt = jnp.where(b < total, jnp.clip(pick(counts) - j * MOE_BLOCK, 0, MOE_BLOCK), 0)
    flat = lambda t_: t_.astype(jnp.int32).reshape(-1)
    pad = lambda t_: jnp.pad(t_, ((0, 0), (0, MOE_BLOCK))).reshape(nch, 1, rows + MOE_BLOCK)
    return flat(eb), flat(blk_start), flat(blk_cnt), pad(tok), pad(tw), nb


def _moe(hn_t, h2, route, wg_b, wu_b, wd_b, *, chunk):
    n, d = h2.shape
    n_experts, _, ff = wg_b.shape
    chunk = min(chunk, n)
    assert n % chunk == 0 and d == SUB * LANE
    nch = n // chunk
    blk_e, blk_start, blk_cnt, tok, tw, nb = _dispatch(route, chunk, n_experts)
    tab = pl.BlockSpec((1, 1, tok.shape[2]), lambda c, b, *_: (c, 0, 0), memory_space=pltpu.SMEM)
    once = lambda shp: pl.BlockSpec(shp, lambda c, b, *_: (c, 0), pipeline_mode=pl.Buffered(1))
    in_hbm = pl.BlockSpec(memory_space=pl.ANY)
    grid_spec = pltpu.PrefetchScalarGridSpec(
        num_scalar_prefetch=3,
        grid=(nch, nb),
        in_specs=[tab, tab, pl.BlockSpec((chunk * SUB, LANE), lambda c, b, *_: (c, 0)),
                  in_hbm, in_hbm, in_hbm, in_hbm],
        out_specs=once((chunk, d)),
        scratch_shapes=[pltpu.VMEM(((2 * chunk + 1) * SUB, LANE), F32), pltpu.VMEM((MOE_BLOCK * SUB, LANE), F32),
                        pltpu.VMEM((MOE_BLOCK * SUB, LANE), F32),
                        pltpu.VMEM((WEIGHT_SLOTS, d, ff), BF16), pltpu.VMEM((WEIGHT_SLOTS, d, ff), BF16),
                        pltpu.VMEM((WEIGHT_SLOTS, ff, d), BF16), pltpu.SemaphoreType.DMA((WEIGHT_SLOTS, 3)),
                        pltpu.SemaphoreType.DMA((1,))],
    )
    return pl.pallas_call(
        functools.partial(_moe_kernel, nb=nb, total=nch * nb),
        grid_spec=grid_spec,
        out_shape=jax.ShapeDtypeStruct((n, d), F32),
        compiler_params=_cparams("arbitrary", "arbitrary"),
        name="moe",
    )(blk_e, blk_start, blk_cnt, tok, tw, hn_t, h2, wg_b, wu_b, wd_b)


def _tile_row(g, reps, scale=1.0):
    return (jnp.tile(g.astype(F32), reps) * scale)[None, :]


def _layer_weights(p):
    d, in_cols = p["w_in"].shape
    aw = d // 2
    bw = d - aw
    pad = 3 * aw + 4 * bw + ROUTE_LANES - in_cols
    n_experts = p["w_re"].shape[1]
    w_r = jnp.concatenate([p["w_rg"], p["w_re"],
                           jnp.zeros((d, ROUTE_LANES - N_GROUPS - n_experts), F32)], axis=1)
    b_r = jnp.concatenate([p["b_rg"], p["b_re"], jnp.zeros((ROUTE_LANES - N_GROUPS - n_experts,), F32)])
    lane_pad = lambda v: jnp.concatenate([v.astype(F32), jnp.zeros((ROUTE_LANES - v.shape[0],), F32)])[None, :]
    return dict(
        aw=aw, bw=bw,
        g_mix=p["g_mix"][None, :],
        w_in=jnp.pad(p["w_in"], ((0, 0), (0, pad))).astype(BF16),
        gq=_tile_row(p["g_qa"], aw // A_HEAD_DIM, A_HEAD_DIM ** -0.5),
        gk=_tile_row(p["g_ka"], aw // A_HEAD_DIM),
        conv_w=p["conv_w"],
        alog=lane_pad(p["a_log"]),
        dtb=lane_pad(jnp.concatenate([p["dt_bias"], jnp.zeros_like(p["dt_bias"])])),
        w_out=p["w_out"].astype(BF16),
        g_gdn=_tile_row(p["g_gdn"], bw // B_HEAD_DIM),
        g_xattn=p["g_xattn"][None, :],
        w_xq=p["w_xq"].astype(BF16),
        g_xq=_tile_row(p["g_xq"], p["w_xq"].shape[1] // B_HEAD_DIM, B_HEAD_DIM ** -0.5),
        w_xo=p["w_xo"].astype(BF16),
        g_ffn=p["g_ffn"][None, :],
        w_r=jnp.stack([w_r.astype(BF16), (w_r - w_r.astype(BF16).astype(F32)).astype(BF16)]), b_r=b_r[None, :],
        w_gate=p["w_gate"].astype(BF16), w_up=p["w_up"].astype(BF16), w_down=p["w_down"].astype(BF16),
    )


def _gdn(u_pre, prev, ab, s0, pw, *, seg, conv_seg, chunks, scan_seqs, scan_steps):
    w, uv, qg, kt, qk, last = _gdn_prep(u_pre, prev, ab, pw["conv_w"], pw["alog"], pw["dtb"],
                                        seg=seg, conv_seg=conv_seg, chunks=chunks)
    return _gdn_scan(w, uv, qg, kt, qk, last, s0, seg=seg, seqs=scan_seqs, steps=scan_steps)


def _prompt_layer(h, mem, pw, praw):
    nb_, s, d = h.shape
    assert nb_ == 1
    x = h.reshape(s, d)
    aw, bw = pw["aw"], pw["bw"]
    q, k, v, u_pre, z, ab = _inproj(x, pw["g_mix"], pw["w_in"], pw["gq"], pw["gk"],
                                    aw=aw, bw=bw, tm=512)
    oa = _dil_attn(q, k, v, rows=DILATIONS[-1][0])
    gdn_chunks = 8
    blk_rows = gdn_chunks * GDN_CHUNK
    tails = u_pre.reshape(s // blk_rows, blk_rows, 3 * bw)[:, blk_rows - PREV_ROWS:, :]
    prev = jnp.concatenate([jnp.zeros((1, PREV_ROWS, 3 * bw), F32), tails[:-1]], axis=0)
    s0 = jnp.zeros((1, B_HEADS, B_HEAD_DIM, B_HEAD_DIM), F32)
    og, s_fin = _gdn(u_pre, prev, ab, s0, pw, seg=GDN_CHUNK, conv_seg=blk_rows, chunks=gdn_chunks,
                     scan_seqs=1, scan_steps=16)
    mk, mv = _mem_kv(mem.reshape(mem.shape[1], d), praw["g_mem"][None, :], praw["w_xk"].astype(BF16),
                     praw["w_xv"].astype(BF16), _tile_row(praw["g_xk"], bw // B_HEAD_DIM))
    h2, hn, route = _post_prompt(oa, og, z, x, pw, mk, mv, tm=512)
    y = _moe(hn, h2, route, pw["w_gate"], pw["w_up"], pw["w_down"], chunk=MOE_CHUNK)
    keep = min(DILATIONS[-1][0], s)
    heads = aw // A_HEAD_DIM
    tail = lambda t_: jnp.transpose(t_[:, s - keep:, :], (1, 0, 2)).reshape(1, keep, heads, A_HEAD_DIM)
    new_k, new_v = tail(k), tail(v)
    conv_new = u_pre[s - (CONV_WIDTH - 1):].reshape(1, CONV_WIDTH - 1, 3 * bw)
    xh = mk.shape[1] // B_HEAD_DIM
    return (y.reshape(1, s, d), new_k, new_v, conv_new, s_fin,
            mk.reshape(1, -1, xh, B_HEAD_DIM), mv.reshape(1, -1, xh, B_HEAD_DIM))


def _sample_layer(h, win_k, win_v, conv_prev, s0, mem_k, mem_v, pw):
    nseq, t_new, d = h.shape
    aw, bw = pw["aw"], pw["bw"]
    n = nseq * t_new
    assert t_new == 8 and GDN_CHUNK % t_new == 0
    x = h.reshape(n, d)
    q, k, v, u_pre, z, ab = _inproj(x, pw["g_mix"], pw["w_in"], pw["gq"], pw["gk"],
                                    aw=aw, bw=bw, tm=512)
    w_buf = win_k.shape[1]
    heads = aw // A_HEAD_DIM
    win_t = lambda c_: jnp.transpose(c_, (0, 2, 3, 1)).reshape(nseq, aw, w_buf)
    new_t = lambda t_: jnp.transpose(t_.reshape(aw // LANE, nseq, t_new, LANE), (1, 0, 3, 2)).reshape(nseq, aw, t_new)
    win_back = lambda c_: jnp.transpose(c_.reshape(nseq, heads, A_HEAD_DIM, w_buf), (0, 3, 1, 2))
    q_rows = jnp.transpose(q.reshape(aw // LANE, nseq, t_new, LANE), (1, 2, 0, 3)).reshape(nseq, t_new, aw)
    new_kt, new_vt, oa = _step_attn(q_rows, new_t(k), new_t(v), win_t(win_k), win_t(win_v))
    new_k, new_v = win_back(new_kt), win_back(new_vt)
    prev = jnp.concatenate([jnp.zeros((nseq, PREV_ROWS - (CONV_WIDTH - 1), 3 * bw), F32),
                            conv_prev.astype(F32)], axis=1)
    og, s_fin = _gdn(u_pre, prev, ab, s0, pw, seg=t_new, conv_seg=t_new, chunks=4, scan_seqs=8, scan_steps=1)
    mem_rows = lambda m_: m_.reshape(nseq, -1, m_.shape[3])
    h2, hn, route = _post_sample(oa.reshape(n, aw), og, z, x, pw, mem_rows(mem_k), mem_rows(mem_v),
                                 t_new=t_new, seqs=16)
    y = _moe(hn, h2, route, pw["w_gate"], pw["w_up"], pw["w_down"], chunk=MOE_CHUNK)
    conv_new = u_pre.reshape(nseq, t_new, 3 * bw)[:, t_new - (CONV_WIDTH - 1):, :]
    return (y.reshape(nseq, t_new, d), new_k, new_v, conv_new, s_fin)


def kernel(x_prompt, x_sample, mem_prompt, cache_win_k, cache_win_v, state_conv, state_delta, cache_mem_k, cache_mem_v, g_mix, w_in, g_qa, g_ka, conv_w, a_log, dt_bias, g_gdn, w_out, g_xattn, g_mem, w_xq, w_xk, w_xv, g_xq, g_xk, w_xo, g_ffn, w_rg, b_rg, w_re, b_re, w_gate, w_up, w_down):
    depth = w_in.shape[0]
    hp, hs = x_prompt, x_sample
    outs = [[] for _ in range(10)]
    for l in range(depth):
        praw = dict(g_mix=g_mix[l], w_in=w_in[l], g_qa=g_qa[l], g_ka=g_ka[l], conv_w=conv_w[l],
                    a_log=a_log[l], dt_bias=dt_bias[l], g_gdn=g_gdn[l], w_out=w_out[l],
                    g_xattn=g_xattn[l], g_mem=g_mem[l], w_xq=w_xq[l], w_xk=w_xk[l], w_xv=w_xv[l],
                    g_xq=g_xq[l], g_xk=g_xk[l], w_xo=w_xo[l], g_ffn=g_ffn[l], w_rg=w_rg[l],
                    b_rg=b_rg[l], w_re=w_re[l], b_re=b_re[l], w_gate=w_gate[l], w_up=w_up[l],
                    w_down=w_down[l])
        pw = _layer_weights(praw)
        hp, k_p, v_p, c_p, s_p, mk, mv = _prompt_layer(hp, mem_prompt, pw, praw)
        hs, k_s, v_s, c_s, s_s = _sample_layer(hs, cache_win_k[l], cache_win_v[l], state_conv[l],
                                               state_delta[l].astype(F32), cache_mem_k[l], cache_mem_v[l], pw)
        for lst, val in zip(outs, (k_p, v_p, c_p, s_p, mk, mv, k_s, v_s, c_s, s_s)):
            lst.append(val)
    st = [jnp.stack(o) for o in outs]
    st[3] = st[3].astype(state_delta.dtype)
    st[9] = st[9].astype(state_delta.dtype)
    return (hp, hs, *st)
```

```python
import functools

import jax
import jax.numpy as jnp
from jax import lax
from jax.experimental import pallas as pl
from jax.experimental.pallas import tpu as pltpu

F32 = jnp.float32
BF16 = jnp.bfloat16
EPS = 1e-6
NEG_INF = float("-inf")

A_HEAD_DIM = 64
B_HEAD_DIM = 128
B_HEADS = 4
BAND = 128
DILATIONS = ((128, 1), (512, 4), (2048, 16))
GDN_CHUNK = 64
CONV_WIDTH = 4
PREV_ROWS = 8
N_GROUPS = 4
GROUP_EXPERTS = 8
SUB, LANE = 8, 128
BF16_ROWS = 16
FAST_STRIDE = 4
ROUTE_LANES = LANE
MOE_BLOCK = 160
MOE_CHUNK = 2048
WEIGHT_SLOTS = 3
VMEM_LIMIT = 56 * 1024 * 1024


def _cparams(*sem):
    return pltpu.CompilerParams(dimension_semantics=sem, vmem_limit_bytes=VMEM_LIMIT)


def _bdot(a, b):
    return jnp.dot(a.astype(BF16), b.astype(BF16), preferred_element_type=F32)


def _bdot_nt(a, b):
    return lax.dot_general(a.astype(BF16), b.astype(BF16), (((1,), (1,)), ((), ())),
                           preferred_element_type=F32)


def _split3(x):
    hi = x.astype(BF16)
    r1 = x - hi.astype(F32)
    mid = r1.astype(BF16)
    lo = (r1 - mid.astype(F32)).astype(BF16)
    return hi, mid, lo


def _dot_exact_lhs(a01, x):
    a = a01.astype(BF16)
    hi, mid, lo = _split3(x)
    d = lambda p: jnp.dot(a, p, preferred_element_type=F32)
    return d(hi) + d(mid) + d(lo)


def _lane_tile_rms(x):
    parts = []
    for c in range(x.shape[1] // LANE):
        xc = x[:, c * LANE:(c + 1) * LANE]
        parts.append(xc * lax.rsqrt(jnp.mean(xc * xc, axis=-1, keepdims=True) + EPS))
    return jnp.concatenate(parts, axis=1)


def _half_tile_rms(x):
    half = LANE // 2
    lo = lax.broadcasted_iota(jnp.int32, (x.shape[0], LANE), 1) < half
    parts = []
    for c in range(x.shape[1] // LANE):
        xc = x[:, c * LANE:(c + 1) * LANE]
        sq = xc * xc
        s_lo = jnp.sum(jnp.where(lo, sq, 0.0), axis=-1, keepdims=True)
        s_hi = jnp.sum(jnp.where(lo, 0.0, sq), axis=-1, keepdims=True)
        parts.append(xc * lax.rsqrt(jnp.where(lo, s_lo, s_hi) * (1.0 / half) + EPS))
    return jnp.concatenate(parts, axis=1)


def _rms(x, g):
    return x * lax.rsqrt(jnp.mean(x * x, axis=-1, keepdims=True) + EPS) * g


def _sigmoid(x):
    return 1.0 / (1.0 + jnp.exp(-x))


def _inproj_kernel(x_ref, g_ref, w_ref, gq_ref, gk_ref,
                   q_ref, k_ref, v_ref, u_ref, z_ref, ab_ref, *, aw, bw):
    nb = _rms(x_ref[...], g_ref[...]).astype(BF16)

    def proj(lo, hi):
        return jnp.dot(nb, w_ref[:, lo:hi], preferred_element_type=F32)

    def put_slabs(ref, val):
        for p in range(aw // LANE):
            ref[p] = val[:, p * LANE:(p + 1) * LANE]

    assert A_HEAD_DIM * 2 == LANE
    put_slabs(q_ref, _half_tile_rms(proj(0, aw)) * gq_ref[...])
    put_slabs(k_ref, _half_tile_rms(proj(aw, 2 * aw)) * gk_ref[...])
    put_slabs(v_ref, proj(2 * aw, 3 * aw))
    u_ref[...] = proj(3 * aw, 3 * aw + 3 * bw)
    z_ref[...] = proj(3 * aw + 3 * bw, 3 * aw + 4 * bw)
    ab_ref[...] = proj(3 * aw + 4 * bw, 3 * aw + 4 * bw + ROUTE_LANES)


def _inproj(x, g_mix, w_in_b, gq_t, gk_t, *, aw, bw, tm):
    n, d = x.shape
    cols = w_in_b.shape[1]
    row = lambda w: pl.BlockSpec((tm, w), lambda i: (i, 0))
    full = lambda a: pl.BlockSpec(a.shape, lambda i: (0, 0))
    out_w = (3 * bw, bw, ROUTE_LANES)
    slab = pl.BlockSpec((aw // LANE, tm, LANE), lambda i: (0, i, 0))
    return pl.pallas_call(
        functools.partial(_inproj_kernel, aw=aw, bw=bw),
        grid=(n // tm,),
        in_specs=[row(d), full(g_mix), pl.BlockSpec((d, cols), lambda i: (0, 0)),
                  full(gq_t), full(gk_t)],
        out_specs=[slab] * 3 + [row(w) for w in out_w],
        out_shape=[jax.ShapeDtypeStruct((aw // LANE, n, LANE), F32)] * 3
                  + [jax.ShapeDtypeStruct((n, w), F32) for w in out_w],
        compiler_params=_cparams("parallel"),
        name="inproj",
    )(x, g_mix, w_in_b, gq_t, gk_t)


def _dil_attn_kernel(q_ref, kp_ref, kc_ref, vp_ref, vc_ref, o_ref, od_ref, ld_ref, stage_ref):
    n = pl.program_id(1)
    rows = q_ref.shape[1]
    qi = lax.broadcasted_iota(jnp.int32, (BAND, 2 * BAND), 0)
    kj = lax.broadcasted_iota(jnp.int32, (BAND, 2 * BAND), 1)
    band = (kj >= qi) & (kj <= qi + BAND)
    bias = jnp.where(band, 0.0, NEG_INF)
    bias_first = jnp.where(band & ((kj >= BAND) | (n > 0)), 0.0, NEG_INF)
    lo_half = lax.broadcasted_iota(jnp.int32, (BAND, LANE), 1) < A_HEAD_DIM

    for di, (window, dil) in enumerate(DILATIONS):
        assert window // dil == BAND and rows % (BAND * dil) == 0
        span = BAND * dil
        srcs = (q_ref, kc_ref, vc_ref, kp_ref, vp_ref)
        if dil > FAST_STRIDE:
            inner = dil // FAST_STRIDE
            assert inner <= FAST_STRIDE and dil % FAST_STRIDE == 0
            for ti, ref in enumerate(srcs):
                for a in range(FAST_STRIDE):
                    stage_ref[ti, a] = ref[0, pl.ds(a, rows // FAST_STRIDE, stride=FAST_STRIDE), :]
            take = lambda ti, r, blk: stage_ref[ti, r % FAST_STRIDE,
                                                pl.ds(r // FAST_STRIDE + blk * BAND * inner, BAND, stride=inner), :]
        else:
            take = lambda ti, r, blk: srcs[ti][0, pl.ds(r + blk * span, BAND, stride=dil), :]
        last = rows // span - 1
        for r in range(dil):
            k_prev = take(3, r, last).astype(BF16)
            v_prev = take(4, r, last).astype(BF16)
            for j in range(rows // span):
                start = r + j * span
                q = take(0, r, j)
                k_cur, v_cur = take(1, r, j).astype(BF16), take(2, r, j).astype(BF16)
                kcat = jnp.concatenate([k_prev, k_cur], axis=0)
                vcat = jnp.concatenate([v_prev, v_cur], axis=0)
                k_prev, v_prev = k_cur, v_cur
                res = []
                for half in range(2):
                    keep = lo_half if half == 0 else jnp.logical_not(lo_half)
                    s = _bdot_nt(jnp.where(keep, q, 0.0), kcat) + (bias if j else bias_first)
                    m = jnp.max(s, axis=-1, keepdims=True)
                    e = jnp.exp(s - m)
                    den = jnp.sum(e, axis=-1, keepdims=True)
                    o = jnp.dot(e.astype(BF16), vcat, preferred_element_type=F32) / den
                    res.append((o, m + jnp.log(den)))
                od_ref[di, pl.ds(start, BAND, stride=dil), :] = jnp.where(lo_half, res[0][0], res[1][0])
                ld_ref[di, pl.ds(start, BAND, stride=dil), :] = jnp.where(lo_half, res[0][1], res[1][1])

    step = 256
    for r0 in range(0, rows, step):
        ls = [ld_ref[di, r0:r0 + step, :] for di in range(len(DILATIONS))]
        mm = functools.reduce(jnp.maximum, ls)
        es = [jnp.exp(l - mm) for l in ls]
        num = functools.reduce(lambda a, b: a + b, [e * od_ref[di, r0:r0 + step, :] for di, e in enumerate(es)])
        o_ref[0, r0:r0 + step, :] = num / functools.reduce(lambda a, b: a + b, es)


def _dil_attn(q, k, v, *, rows):
    pairs, s, lanes = q.shape
    assert s % rows == 0 and lanes == LANE
    cur = pl.BlockSpec((1, rows, LANE), lambda p, n: (p, n, 0))
    prev = pl.BlockSpec((1, rows, LANE), lambda p, n: (p, jnp.maximum(n - 1, 0), 0))
    return pl.pallas_call(
        _dil_attn_kernel,
        grid=(pairs, s // rows),
        in_specs=[cur, prev, cur, prev, cur],
        out_specs=cur,
        out_shape=jax.ShapeDtypeStruct(q.shape, F32),
        scratch_shapes=[pltpu.VMEM((len(DILATIONS), rows, LANE), F32)] * 2
                       + [pltpu.VMEM((5, FAST_STRIDE, rows // FAST_STRIDE, LANE), F32)],
        compiler_params=_cparams("parallel", "arbitrary"),
        name="dil_attn",
    )(q, k, k, v, v)


def _step_attn_kernel(q_ref, kn_ref, vn_ref, kc_ref, vc_ref, ok_ref, ov_ref, oa_ref,
                      kb_ref, vb_ref, *, w_buf, t_new):
    aw = q_ref.shape[2]
    heads = aw // A_HEAD_DIM
    lanes = LANE
    kt, vt = kc_ref[0], vc_ref[0]
    lead = jnp.zeros((aw, lanes - t_new), F32)
    kn = jnp.concatenate([lead, kn_ref[0]], axis=1)
    vn = jnp.concatenate([lead, vn_ref[0]], axis=1)
    is_new = lax.broadcasted_iota(jnp.int32, (aw, lanes), 1) >= lanes - t_new
    for src, new, dst in ((kt, kn, ok_ref), (vt, vn, ov_ref)):
        rolled = pltpu.roll(src, w_buf - t_new, axis=1)
        dst[0, :, 0:w_buf - lanes] = rolled[:, 0:w_buf - lanes]
        dst[0, :, w_buf - lanes:w_buf] = jnp.where(is_new, new, rolled[:, w_buf - lanes:w_buf])
    kb_ref[...] = kt.astype(BF16)
    vb_ref[...] = vt.astype(BF16)

    rows = heads * t_new
    ri = lax.broadcasted_iota(jnp.int32, (rows, aw), 0)
    li = lax.broadcasted_iota(jnp.int32, (rows, aw), 1)
    q_rep = jnp.concatenate([q_ref[0]] * heads, axis=0)
    q_blk = jnp.where((ri // t_new) == (li // A_HEAD_DIM), q_rep, 0.0).astype(BF16)
    ncol = w_buf + lanes
    s_all = jnp.concatenate([jnp.dot(q_blk, kb_ref[...], preferred_element_type=F32),
                             jnp.dot(q_blk, kn.astype(BF16), preferred_element_type=F32)], axis=1)

    def branch(window, dil, col0):
        s = s_all[:, col0:]
        t = lax.broadcasted_iota(jnp.int32, s.shape, 0) % t_new
        cidx = lax.broadcasted_iota(jnp.int32, s.shape, 1) + col0
        r = jnp.where(cidx < w_buf, cidx, cidx - (lanes - t_new))
        diff = w_buf + t - r
        valid = (diff >= 0) & (diff <= window) & ((diff & (dil - 1)) == 0)
        valid = valid & ((cidx < w_buf) | (cidx >= ncol - t_new))
        s = jnp.where(valid, s, NEG_INF)
        m = jnp.max(s, axis=-1, keepdims=True)
        e = jnp.exp(s - m)
        den = jnp.sum(e, axis=-1, keepdims=True)
        return e, den, m + jnp.log(den)

    parts = []
    for window, dil in DILATIONS:
        assert dil & (dil - 1) == 0
        col0 = max(0, (w_buf - window) // 128 * 128)
        parts.append((col0,) + branch(window, dil, col0))
    mm = functools.reduce(jnp.maximum, [p[3] for p in parts])
    wexp = [jnp.exp(p[3] - mm) for p in parts]
    wsum = functools.reduce(lambda a, b: a + b, wexp)
    p_tot = jnp.zeros((rows, ncol), F32)
    for (col0, e, den, _), we in zip(parts, wexp):
        pe = e * (we / (wsum * den))
        if col0:
            pe = jnp.concatenate([jnp.zeros((rows, col0), F32), pe], axis=1)
        p_tot = p_tot + pe
    o = _bdot_nt(p_tot[:, :w_buf], vb_ref[...]) + _bdot_nt(p_tot[:, w_buf:], vn)
    lh = lax.broadcasted_iota(jnp.int32, (t_new, aw), 1) // A_HEAD_DIM
    acc = jnp.zeros((t_new, aw), F32)
    for h in range(heads):
        acc = acc + jnp.where(lh == h, o[h * t_new:(h + 1) * t_new, :], 0.0)
    oa_ref[0] = acc


def _step_attn(q, kn_t, vn_t, cache_kt, cache_vt):
    nseq, aw, w_buf = cache_kt.shape
    t_new = q.shape[1]
    assert w_buf % LANE == 0 and t_new % SUB == 0 and t_new <= LANE and kn_t.shape[2] == t_new
    qs = pl.BlockSpec((1, t_new, aw), lambda i: (i, 0, 0))
    new = pl.BlockSpec((1, aw, t_new), lambda i: (i, 0, 0))
    win = pl.BlockSpec((1, aw, w_buf), lambda i: (i, 0, 0))
    return pl.pallas_call(
        functools.partial(_step_attn_kernel, w_buf=w_buf, t_new=t_new),
        grid=(nseq,),
        in_specs=[qs, new, new, win, win],
        out_specs=[win, win, qs],
        out_shape=[jax.ShapeDtypeStruct(cache_kt.shape, F32), jax.ShapeDtypeStruct(cache_vt.shape, F32),
                   jax.ShapeDtypeStruct(q.shape, F32)],
        scratch_shapes=[pltpu.VMEM((aw, w_buf), BF16), pltpu.VMEM((aw, w_buf), BF16)],
        compiler_params=_cparams("parallel"),
        name="step_attn",
    )(q, kn_t, vn_t, cache_kt, cache_vt)


def _gdn_prep_kernel(u_ref, prev_ref, ab_ref, cw_ref, alog_ref, dtb_ref,
                     w_ref, uv_ref, qg_ref, kt_ref, qk_ref, last_ref, full_ref, act_ref,
                     *, seg, conv_seg, chunks):
    c = GDN_CHUNK
    dk = B_HEAD_DIM
    bw = B_HEADS * dk
    ii = lax.broadcasted_iota(jnp.int32, (c, c), 0)
    jj = lax.broadcasted_iota(jnp.int32, (c, c), 1)
    same = (ii // seg) == (jj // seg)
    incl = same & (ii >= jj)
    strict = same & (ii > jj)
    eye = ii == jj
    ones_c = jnp.ones((c, c), BF16)
    seg_cols = jnp.where((lax.broadcasted_iota(jnp.int32, (c, dk), 0) % seg)
                         == lax.broadcasted_iota(jnp.int32, (c, dk), 1), 1.0, 0.0)
    cw = cw_ref[...]
    rows = chunks * c

    for sgi in range(rows // conv_seg):
        base = sgi * (conv_seg + PREV_ROWS)
        full_ref[base:base + PREV_ROWS, :] = prev_ref[sgi]
        full_ref[base + PREV_ROWS:base + PREV_ROWS + conv_seg, :] = u_ref[sgi * conv_seg:(sgi + 1) * conv_seg, :]
        y = jnp.zeros((conv_seg, 3 * bw), F32)
        for j in range(CONV_WIDTH):
            off = base + PREV_ROWS - (CONV_WIDTH - 1) + j
            y = y + full_ref[off:off + conv_seg, :] * cw[j:j + 1, :]
        act_ref[sgi * conv_seg:(sgi + 1) * conv_seg, :] = y * _sigmoid(y)

    ri = lax.broadcasted_iota(jnp.int32, (rows, rows), 0)
    rj = lax.broadcasted_iota(jnp.int32, (rows, rows), 1)
    rsame = (ri // seg) == (rj // seg)
    ab = ab_ref[...]
    sp = ab + dtb_ref[...]
    sp = jnp.maximum(sp, 0.0) + jnp.log(1.0 + jnp.exp(-jnp.abs(sp)))
    g_all = -jnp.exp(alog_ref[...]) * sp
    gcum_all = _dot_exact_lhs(jnp.where(rsame & (ri >= rj), 1.0, 0.0), g_all)
    gtot_all = _dot_exact_lhs(jnp.where(rsame, 1.0, 0.0), g_all)
    beta_all = _sigmoid(ab)
    eye_f = jnp.where(eye, 1.0, 0.0)

    grow = []
    for ci in range(chunks):
        rs = slice(ci * c, (ci + 1) * c)
        diag = jnp.concatenate([jnp.where(eye, gcum_all[rs, h:h + 1], 0.0) for h in range(B_HEADS)], axis=1)
        grow.append(_dot_exact_lhs(ones_c, diag))

    prob = [(ci, h) for ci in range(chunks) for h in range(B_HEADS)]
    col = lambda arr, ci, lane: arr[ci * c:(ci + 1) * c, lane:lane + 1]
    gc = [col(gcum_all, ci, h) for ci, h in prob]
    gt = [col(gtot_all, ci, h) for ci, h in prob]
    beta = [col(beta_all, ci, B_HEADS + h) for ci, h in prob]
    q, k, v = [], [], []
    for ci, h in prob:
        rs = slice(ci * c, (ci + 1) * c)
        qh = act_ref[rs, h * dk:(h + 1) * dk]
        kh = act_ref[rs, bw + h * dk:bw + (h + 1) * dk]
        q.append(qh * lax.rsqrt(jnp.sum(qh * qh, axis=-1, keepdims=True) + EPS) * (dk ** -0.5))
        k.append(kh * lax.rsqrt(jnp.sum(kh * kh, axis=-1, keepdims=True) + EPS))
        v.append(act_ref[rs, 2 * bw + h * dk:2 * bw + (h + 1) * dk])
    e_incl = [jnp.exp(jnp.where(incl, gc[i] - grow[ci][:, h * c:(h + 1) * c], NEG_INF))
              for i, (ci, h) in enumerate(prob)]
    qkk = [_bdot_nt(jnp.concatenate([q[i], k[i]], axis=0), k[i]) for i in range(len(prob))]
    a = [beta[i] * jnp.where(strict, e_incl[i], 0.0) * qkk[i][c:, :] for i in range(len(prob))]
    blk = 8
    inblk = (ii // blk) == (jj // blk)
    d0 = [jnp.where(inblk, t, 0.0) for t in a]
    d2 = [_bdot(t, t) for t in d0]
    d4 = [_bdot(t, t) for t in d2]
    x = [(eye_f - t) + _bdot(eye_f - t, t2) for t, t2 in zip(d0, d2)]
    x = [t + _bdot(t, t4) for t, t4 in zip(x, d4)]
    while blk < seg:
        off_blk = ((ii // (2 * blk)) == (jj // (2 * blk))) & ((ii // blk) != (jj // blk))
        xe = [_bdot(t, jnp.where(off_blk, ta, 0.0)) for t, ta in zip(x, a)]
        x = [t - _bdot(te, t) for t, te in zip(x, xe)]
        blk *= 2
    gamma = [jnp.exp(t) for t in gc]
    wuv = [_bdot(x[i], jnp.concatenate([(beta[i] * gamma[i]) * k[i], beta[i] * v[i]], axis=1))
           for i in range(len(prob))]
    for i, (ci, h) in enumerate(prob):
        rs = slice(ci * c, (ci + 1) * c)
        hs = slice(h * dk, (h + 1) * dk)
        w_ref[rs, hs] = wuv[i][:, :dk]
        uv_ref[rs, hs] = wuv[i][:, dk:]
        qg_ref[rs, hs] = gamma[i] * q[i]
        kt_ref[rs, hs] = k[i] * jnp.exp(gt[i] - gc[i])
        qk = qkk[i][:c, :] * e_incl[i]
        qk_ref[rs, hs] = (jnp.concatenate([qk, jnp.zeros((c, dk - c), F32)], axis=1) if seg == c
                          else _bdot(qk, seg_cols))
        last_ref[rs, hs] = jnp.broadcast_to(jnp.exp(gt[i]), (c, dk))


def _gdn_prep(u_pre, prev, ab, conv_w, alog_row, dtb_row, *, seg, conv_seg, chunks):
    n, cw = u_pre.shape
    bw = cw // 3
    c = GDN_CHUNK
    rows = chunks * c
    assert n % rows == 0 and c % seg == 0 and seg % 8 == 0 and rows % conv_seg == 0 and conv_seg % seg == 0
    ncs = rows // conv_seg
    row = lambda w: pl.BlockSpec((rows, w), lambda i: (i, 0))
    full = lambda a: pl.BlockSpec(a.shape, lambda i: (0, 0))
    return pl.pallas_call(
        functools.partial(_gdn_prep_kernel, seg=seg, conv_seg=conv_seg, chunks=chunks),
        grid=(n // rows,),
        in_specs=[row(cw), pl.BlockSpec((ncs, PREV_ROWS, cw), lambda i: (i, 0, 0)),
                  row(ROUTE_LANES), full(conv_w), full(alog_row), full(dtb_row)],
        out_specs=[row(bw)] * 6,
        out_shape=[jax.ShapeDtypeStruct((n, bw), F32)] * 6,
        scratch_shapes=[pltpu.VMEM((ncs * (conv_seg + PREV_ROWS), cw), F32), pltpu.VMEM((rows, cw), F32)],
        compiler_params=_cparams("parallel"),
        name=f"gdn_prep_{seg}",
    )(u_pre, prev, ab, conv_w, alog_row, dtb_row)


def _gdn_scan_kernel(w_ref, uv_ref, qg_ref, kt_ref, qk_ref, last_ref, s0_ref, o_ref, s_ref, *, seg, steps):
    dk = B_HEAD_DIM

    @pl.when(pl.program_id(1) == 0)
    def _():
        s_ref[...] = s0_ref[...]

    rows = max(seg, BF16_ROWS)
    pad = rows - seg
    padr = lambda t: jnp.concatenate([t, jnp.zeros((pad, t.shape[1]), F32)], axis=0) if pad else t
    ii = lax.broadcasted_iota(jnp.int32, (dk, dk), 0)
    jj = lax.broadcasted_iota(jnp.int32, (dk, dk), 1)
    eye = jnp.where(ii == jj, 1.0, 0.0).astype(BF16)
    chains = [(sq, h) for sq in range(s_ref.shape[0]) for h in range(B_HEADS)]
    hs = lambda h: slice(h * dk, (h + 1) * dk)
    s = [s_ref[sq, h] for sq, h in chains]
    for c in range(steps):
        rs = [slice((sq * steps + c) * seg, (sq * steps + c + 1) * seg) for sq, _ in chains]
        sb = [t.astype(BF16) for t in s]
        ws = [_bdot(padr(w_ref[rs[i], hs(h)]), sb[i]) for i, (_, h) in enumerate(chains)]
        ub = [(padr(uv_ref[rs[i], hs(h)]) - ws[i]).astype(BF16) for i, (_, h) in enumerate(chains)]
        ktt = [_bdot_nt(eye, padr(kt_ref[rs[i], hs(h)])) for i, (_, h) in enumerate(chains)]
        s_new = [last_ref[rs[i], hs(h)][0:1, :] * s[i] + _bdot(ktt[i], ub[i]) for i, (_, h) in enumerate(chains)]
        for i, (_, h) in enumerate(chains):
            o = _bdot(padr(qg_ref[rs[i], hs(h)]), sb[i]) + _bdot(padr(qk_ref[rs[i], hs(h)])[:, :rows], ub[i])
            o_ref[rs[i], hs(h)] = o[:seg, :]
        s = s_new
    for i, (sq, h) in enumerate(chains):
        s_ref[sq, h] = s[i]


def _gdn_scan(w, uv, qg, kt, qk, last, s0, *, seg, seqs, steps):
    n, bw = w.shape
    nseq = s0.shape[0]
    per_seq = n // (nseq * seg)
    assert nseq % seqs == 0 and per_seq % steps == 0 and (seqs == 1 or steps == per_seq)
    blocks = per_seq // steps
    row = pl.BlockSpec((seqs * steps * seg, bw), lambda s, i: (s * blocks + i, 0))
    st = pl.BlockSpec((seqs,) + s0.shape[1:], lambda s, i: (s, 0, 0, 0))
    return pl.pallas_call(
        functools.partial(_gdn_scan_kernel, seg=seg, steps=steps),
        grid=(nseq // seqs, blocks),
        in_specs=[row] * 6 + [st],
        out_specs=[row, st],
        out_shape=[jax.ShapeDtypeStruct((n, bw), F32), jax.ShapeDtypeStruct(s0.shape, F32)],
        compiler_params=_cparams("parallel", "arbitrary"),
        name=f"gdn_scan_{seg}",
    )(w, uv, qg, kt, qk, last, s0)


def _mem_kv_kernel(mem_ref, g_ref, wk_ref, wv_ref, gk_ref, k_ref, v_ref):
    mn = _rms(mem_ref[...], g_ref[...]).astype(BF16)
    k = jnp.dot(mn, wk_ref[...], preferred_element_type=F32)
    k_ref[...] = _lane_tile_rms(k) * gk_ref[...]
    v_ref[...] = jnp.dot(mn, wv_ref[...], preferred_element_type=F32)


def _mem_kv(mem, g_mem, w_xk_b, w_xv_b, gk_t):
    m = mem.shape[0]
    xw = w_xk_b.shape[1]
    return pl.pallas_call(
        _mem_kv_kernel,
        out_shape=[jax.ShapeDtypeStruct((m, xw), F32)] * 2,
        compiler_params=pltpu.CompilerParams(vmem_limit_bytes=VMEM_LIMIT),
        name="mem_kv",
    )(mem, g_mem, w_xk_b, w_xv_b, gk_t)


def _route(logits):
    lane = lax.broadcasted_iota(jnp.int32, logits.shape, 1)
    lane_f = lane.astype(F32)
    big = float(ROUTE_LANES)
    lg = jnp.where(lane < N_GROUPS, logits, NEG_INF)
    mg = jnp.max(lg, axis=-1, keepdims=True)
    zg = jnp.sum(jnp.exp(lg - mg), axis=-1, keepdims=True)
    pg_top = 1.0 / zg
    gidx = jnp.min(jnp.where(lg == mg, lane_f, big), axis=-1, keepdims=True)
    e_lo = N_GROUPS + GROUP_EXPERTS * gidx
    emask = (lane_f >= e_lo) & (lane_f < e_lo + GROUP_EXPERTS)
    le = jnp.where(emask, logits, NEG_INF)
    me = jnp.max(le, axis=-1, keepdims=True)
    ee = jnp.exp(le - me)
    pe = ee / jnp.sum(ee, axis=-1, keepdims=True)
    pe = jnp.where(emask, pe, -1.0)
    p1 = jnp.max(pe, axis=-1, keepdims=True)
    i1 = jnp.min(jnp.where(pe == p1, lane_f, big), axis=-1, keepdims=True)
    pe2 = jnp.where(lane_f == i1, -1.0, pe)
    p2 = jnp.max(pe2, axis=-1, keepdims=True)
    i2 = jnp.min(jnp.where(pe2 == p2, lane_f, big), axis=-1, keepdims=True)
    den = p1 + p2
    w1 = pg_top * p1 / den
    w2 = pg_top * p2 / den
    out = jnp.where(lane == 0, i1 - N_GROUPS, 0.0)
    out = jnp.where(lane == 1, i2 - N_GROUPS, out)
    out = jnp.where(lane == 2, w1, out)
    return jnp.where(lane == 3, w2, out)


def _post_common(oa, og_ref, z_ref, h_ref, wout_ref, ggdn_ref, gx_ref, wxq_ref, gxq_ref):
    og = og_ref[...]
    z = z_ref[...]
    ob = _lane_tile_rms(og) * ggdn_ref[...] * (z * _sigmoid(z))
    cat = jnp.concatenate([oa, ob], axis=-1).astype(BF16)
    h1 = h_ref[...] + jnp.dot(cat, wout_ref[...], preferred_element_type=F32)
    hx = _rms(h1, gx_ref[...]).astype(BF16)
    q = jnp.dot(hx, wxq_ref[...], preferred_element_type=F32)
    qn = _lane_tile_rms(q) * gxq_ref[...]
    return h1, qn


def _mem_attend_rows(qn, mk, mv):
    outs = []
    for h in range(qn.shape[1] // B_HEAD_DIM):
        hs = slice(h * B_HEAD_DIM, (h + 1) * B_HEAD_DIM)
        s = _bdot_nt(qn[:, hs], mk[:, hs])
        m = jnp.max(s, axis=-1, keepdims=True)
        e = jnp.exp(s - m)
        p = e / jnp.sum(e, axis=-1, keepdims=True)
        outs.append(_bdot(p, mv[:, hs]))
    return jnp.concatenate(outs, axis=-1)


def _post_tail(h1, ox, wxo_ref, gffn_ref, wr_ref, br_ref, h2_ref, hn_ref, route_ref):
    h2 = h1 + jnp.dot(ox.astype(BF16), wxo_ref[...], preferred_element_type=F32)
    h2_ref[...] = h2
    hn = _rms(h2, gffn_ref[...])
    tm = hn.shape[0]
    for j in range(hn.shape[1] // LANE):
        hn_ref[pl.ds(j, tm, stride=SUB), :] = hn[:, j * LANE:(j + 1) * LANE]
    hn_hi = hn.astype(BF16)
    hn_mid = (hn - hn_hi.astype(F32)).astype(BF16)
    d = lambda a, b: jnp.dot(a, b, preferred_element_type=F32)
    logits = d(hn_hi, wr_ref[0]) + d(hn_hi, wr_ref[1]) + d(hn_mid, wr_ref[0]) + br_ref[...]
    route_ref[...] = _route(logits)


def _post_prompt_kernel(oa_ref, og_ref, z_ref, h_ref,
                        wout_ref, ggdn_ref, gx_ref, wxq_ref, gxq_ref, mk_ref, mv_ref,
                        wxo_ref, gffn_ref, wr_ref, br_ref, h2_ref, hn_ref, route_ref):
    oa = jnp.concatenate([oa_ref[p] for p in range(oa_ref.shape[0])], axis=-1)
    h1, qn = _post_common(oa, og_ref, z_ref, h_ref, wout_ref, ggdn_ref, gx_ref, wxq_ref, gxq_ref)
    ox = _mem_attend_rows(qn, mk_ref[...], mv_ref[...])
    _post_tail(h1, ox, wxo_ref, gffn_ref, wr_ref, br_ref, h2_ref, hn_ref, route_ref)


def _post_sample_kernel(oa_ref, og_ref, z_ref, h_ref,
                        wout_ref, ggdn_ref, gx_ref, wxq_ref, gxq_ref, mk_ref, mv_ref,
                        wxo_ref, gffn_ref, wr_ref, br_ref, h2_ref, hn_ref, route_ref, *, t_new):
    h1, qn = _post_common(oa_ref[...], og_ref, z_ref, h_ref, wout_ref, ggdn_ref, gx_ref, wxq_ref, gxq_ref)
    dh = B_HEAD_DIM
    heads = qn.shape[1] // dh
    rows = heads * t_new
    ncol = mk_ref.shape[1]
    own = ((lax.broadcasted_iota(jnp.int32, (rows, ncol), 0) // t_new)
           == (lax.broadcasted_iota(jnp.int32, (rows, ncol), 1) % heads))
    outs = []
    for sq in range(mk_ref.shape[0]):
        qs = qn[sq * t_new:(sq + 1) * t_new, :]
        q_rows = jnp.concatenate([qs[:, h * dh:(h + 1) * dh] for h in range(heads)], axis=0)
        s = jnp.where(own, _bdot_nt(q_rows, mk_ref[sq]), NEG_INF)
        m = jnp.max(s, axis=-1, keepdims=True)
        e = jnp.exp(s - m)
        o = _bdot(e / jnp.sum(e, axis=-1, keepdims=True), mv_ref[sq])
        outs.append(jnp.concatenate([o[h * t_new:(h + 1) * t_new, :] for h in range(heads)], axis=1))
    ox = jnp.concatenate(outs, axis=0)
    _post_tail(h1, ox, wxo_ref, gffn_ref, wr_ref, br_ref, h2_ref, hn_ref, route_ref)


def _post_weights_specs(weights):
    return [pl.BlockSpec(a.shape, lambda i, nd=a.ndim: (0,) * nd) for a in weights]


def _post_outs(n, d, tm):
    assert d == SUB * LANE
    row = lambda w: pl.BlockSpec((tm, w), lambda i: (i, 0))
    specs = [row(d), pl.BlockSpec((tm * SUB, LANE), lambda i: (i, 0)), row(ROUTE_LANES)]
    shapes = [jax.ShapeDtypeStruct((n, d), F32), jax.ShapeDtypeStruct((n * SUB, LANE), F32),
              jax.ShapeDtypeStruct((n, ROUTE_LANES), F32)]
    return specs, shapes


def _post_prompt(oa_slabs, og, z, h, pw, mk, mv, *, tm):
    n, d = h.shape
    aw = og.shape[1]
    row = lambda w: pl.BlockSpec((tm, w), lambda i: (i, 0))
    full = lambda a: pl.BlockSpec(a.shape, lambda i: (0, 0))
    slab = pl.BlockSpec((oa_slabs.shape[0], tm, LANE), lambda i: (0, i, 0))
    w1 = [pw["w_out"], pw["g_gdn"], pw["g_xattn"], pw["w_xq"], pw["g_xq"]]
    w2 = [pw["w_xo"], pw["g_ffn"], pw["w_r"], pw["b_r"]]
    specs, shapes = _post_outs(n, d, tm)
    return pl.pallas_call(
        _post_prompt_kernel,
        grid=(n // tm,),
        in_specs=[slab, row(aw), row(aw), row(d)] + _post_weights_specs(w1) + [full(mk), full(mv)]
                 + _post_weights_specs(w2),
        out_specs=specs, out_shape=shapes,
        compiler_params=_cparams("parallel"),
        name="post_prompt",
    )(oa_slabs, og, z, h, *w1, mk, mv, *w2)


def _post_sample(oa, og, z, h, pw, mk, mv, *, t_new, seqs):
    n, d = h.shape
    aw = og.shape[1]
    tm = t_new * seqs
    row = lambda w: pl.BlockSpec((tm, w), lambda i: (i, 0))
    mem = pl.BlockSpec((seqs,) + mk.shape[1:], lambda i: (i, 0, 0))
    w1 = [pw["w_out"], pw["g_gdn"], pw["g_xattn"], pw["w_xq"], pw["g_xq"]]
    w2 = [pw["w_xo"], pw["g_ffn"], pw["w_r"], pw["b_r"]]
    specs, shapes = _post_outs(n, d, tm)
    return pl.pallas_call(
        functools.partial(_post_sample_kernel, t_new=t_new),
        grid=(n // tm,),
        in_specs=[row(aw)] * 3 + [row(d)] + _post_weights_specs(w1) + [mem, mem]
                 + _post_weights_specs(w2),
        out_specs=specs, out_shape=shapes,
        compiler_params=_cparams("parallel"),
        name="post_sample",
    )(oa, og, z, h, *w1, mk, mv, *w2)


def _moe_kernel(blk_e_ref, blk_start_ref, blk_cnt_ref, tok_ref, tw_ref, x_ref, res_hbm,
                wg_hbm, wu_hbm, wd_hbm, out_ref, acc_ref, xs0_ref, ys_ref, wg_buf, wu_buf, wd_buf, sem, res_sem,
                xs1_ref, *, nb, total):
    c = pl.program_id(0)
    b = pl.program_id(1)
    chunk, d = out_ref.shape
    nl = d // LANE
    step = c * nb + b
    cnt = blk_cnt_ref[step]
    wslot = step % WEIGHT_SLOTS

    def weight_copies(u):
        e, s_ = blk_e_ref[u], u % WEIGHT_SLOTS
        return [pltpu.make_async_copy(hbm.at[e], buf.at[s_], sem.at[s_, i])
                for i, (hbm, buf) in enumerate(((wg_hbm, wg_buf), (wu_hbm, wu_buf), (wd_hbm, wd_buf)))]

    def start_fetch(u):
        @pl.when(blk_cnt_ref[jnp.minimum(u, total - 1)] * (u < total) > 0)
        def _():
            for cp in weight_copies(u):
                cp.start()

    res_copy = pltpu.make_async_copy(res_hbm.at[pl.ds(pl.multiple_of(c * chunk, chunk), chunk), :],
                                     out_ref, res_sem.at[0])

    @pl.when(b == 0)
    def _():
        res_copy.start()

    @pl.when(step == 0)
    def _():
        for u in range(WEIGHT_SLOTS - 1):
            start_fetch(jnp.int32(u))

    start_fetch(step + WEIGHT_SLOTS - 1)

    def gather(u, dst):
        base_u = blk_start_ref[u]
        for r in range(MOE_BLOCK):
            t = tok_ref[0, 0, base_u + r] >> 1
            dst[r * SUB:(r + 1) * SUB, :] = x_ref[pl.ds(pl.multiple_of(t * SUB, SUB), SUB), :]

    def run_block(xs_ref, nxt_ref):
        for cp in weight_copies(step):
            cp.wait()
        wg_ref, wu_ref, wd_ref = wg_buf.at[wslot], wu_buf.at[wslot], wd_buf.at[wslot]
        base = blk_start_ref[step]
        gather(jnp.minimum(step + 1, total - 1), nxt_ref)
        xb = jnp.concatenate([xs_ref[pl.ds(j, MOE_BLOCK, stride=SUB), :] for j in range(nl)],
                             axis=1).astype(BF16)
        hg = jnp.dot(xb, wg_ref[...], preferred_element_type=F32)
        hu = jnp.dot(xb, wu_ref[...], preferred_element_type=F32)
        act = (hg * _sigmoid(hg) * hu).astype(BF16)
        y = jnp.dot(act, wd_ref[...], preferred_element_type=F32)
        for j in range(nl):
            ys_ref[pl.ds(j, MOE_BLOCK, stride=SUB), :] = y[:, j * LANE:(j + 1) * LANE]
        for r in range(MOE_BLOCK):
            e = tok_ref[0, 0, base + r]
            slot = jnp.where(r < cnt, (e & 1) * chunk + (e >> 1), 2 * chunk)
            off = pl.multiple_of(slot * SUB, SUB)
            acc_ref[pl.ds(off, SUB), :] = tw_ref[0, 0, base + r] * ys_ref[r * SUB:(r + 1) * SUB, :]

    @pl.when((cnt > 0) & (b % 2 == 0))
    def _():
        @pl.when(b == 0)
        def _():
            gather(step, xs0_ref)
        run_block(xs0_ref, xs1_ref)

    @pl.when((cnt > 0) & (b % 2 == 1))
    def _():
        run_block(xs1_ref, xs0_ref)

    @pl.when(b == nb - 1)
    def _():
        res_copy.wait()
        rows_per = 256
        for r0 in range(0, chunk, rows_per):
            for j in range(nl):
                out_ref[r0:r0 + rows_per, j * LANE:(j + 1) * LANE] = (
                    out_ref[r0:r0 + rows_per, j * LANE:(j + 1) * LANE]
                    + acc_ref[pl.ds(r0 * SUB + j, rows_per, stride=SUB), :]
                    + acc_ref[pl.ds((chunk + r0) * SUB + j, rows_per, stride=SUB), :])


def _dispatch(route, chunk, n_experts):
    n = route.shape[0]
    nch = n // chunk
    rows = 2 * chunk
    nb = rows // MOE_BLOCK + n_experts
    e = route[:, :2].astype(jnp.int32).reshape(nch, rows)
    w = route[:, 2:4].reshape(nch, rows)
    tok = jnp.argsort(e, axis=1, stable=True).astype(jnp.int32)
    tw = jnp.take_along_axis(w, tok, axis=1)
    ex = jnp.arange(n_experts, dtype=jnp.int32)
    counts = jnp.sum((e[:, :, None] == ex).astype(jnp.int32), axis=1)
    start = jnp.cumsum(counts, axis=1) - counts
    nblk_e = (counts + MOE_BLOCK - 1) // MOE_BLOCK
    bend = jnp.cumsum(nblk_e, axis=1)
    bstart = bend - nblk_e
    total = bend[:, -1:]
    b = jnp.arange(nb, dtype=jnp.int32)[None, :]
    bb = jnp.minimum(b, total - 1)
    eb = jnp.sum((bend[:, None, :] <= bb[:, :, None]).astype(jnp.int32), axis=2)
    sel = eb[:, :, None] == ex
    pick = lambda tbl: jnp.sum(jnp.where(sel, tbl[:, None, :], 0), axis=2)
    j = bb - pick(bstart)
    blk_start = pick(start) + j * MOE_BLOCK
    blk_cnt = jnp.where(b < total, jnp.clip(pick(counts) - j * MOE_BLOCK, 0, MOE_BLOCK), 0)
    flat = lambda t_: t_.astype(jnp.int32).reshape(-1)
    pad = lambda t_: jnp.pad(t_, ((0, 0), (0, MOE_BLOCK))).reshape(nch, 1, rows + MOE_BLOCK)
    return flat(eb), flat(blk_start), flat(blk_cnt), pad(tok), pad(tw), nb


def _moe(hn_t, h2, route, wg_b, wu_b, wd_b, *, chunk):
    n, d = h2.shape
    n_experts, _, ff = wg_b.shape
    chunk = min(chunk, n)
    assert n % chunk == 0 and d == SUB * LANE
    nch = n // chunk
    blk_e, blk_start, blk_cnt, tok, tw, nb = _dispatch(route, chunk, n_experts)
    tab = pl.BlockSpec((1, 1, tok.shape[2]), lambda c, b, *_: (c, 0, 0), memory_space=pltpu.SMEM)
    once = lambda shp: pl.BlockSpec(shp, lambda c, b, *_: (c, 0), pipeline_mode=pl.Buffered(1))
    in_hbm = pl.BlockSpec(memory_space=pl.ANY)
    grid_spec = pltpu.PrefetchScalarGridSpec(
        num_scalar_prefetch=3,
        grid=(nch, nb),
        in_specs=[tab, tab, pl.BlockSpec((chunk * SUB, LANE), lambda c, b, *_: (c, 0)),
                  in_hbm, in_hbm, in_hbm, in_hbm],
        out_specs=once((chunk, d)),
        scratch_shapes=[pltpu.VMEM(((2 * chunk + 1) * SUB, LANE), F32), pltpu.VMEM((MOE_BLOCK * SUB, LANE), F32),
                        pltpu.VMEM((MOE_BLOCK * SUB, LANE), F32),
                        pltpu.VMEM((WEIGHT_SLOTS, d, ff), BF16), pltpu.VMEM((WEIGHT_SLOTS, d, ff), BF16),
                        pltpu.VMEM((WEIGHT_SLOTS, ff, d), BF16), pltpu.SemaphoreType.DMA((WEIGHT_SLOTS, 3)),
                        pltpu.SemaphoreType.DMA((1,)), pltpu.VMEM((MOE_BLOCK * SUB, LANE), F32)],
    )
    return pl.pallas_call(
        functools.partial(_moe_kernel, nb=nb, total=nch * nb),
        grid_spec=grid_spec,
        out_shape=jax.ShapeDtypeStruct((n, d), F32),
        compiler_params=_cparams("arbitrary", "arbitrary"),
        name="moe",
    )(blk_e, blk_start, blk_cnt, tok, tw, hn_t, h2, wg_b, wu_b, wd_b)


def _tile_row(g, reps, scale=1.0):
    return (jnp.tile(g.astype(F32), reps) * scale)[None, :]


def _layer_weights(p):
    d, in_cols = p["w_in"].shape
    aw = d // 2
    bw = d - aw
    pad = 3 * aw + 4 * bw + ROUTE_LANES - in_cols
    n_experts = p["w_re"].shape[1]
    w_r = jnp.concatenate([p["w_rg"], p["w_re"],
                           jnp.zeros((d, ROUTE_LANES - N_GROUPS - n_experts), F32)], axis=1)
    b_r = jnp.concatenate([p["b_rg"], p["b_re"], jnp.zeros((ROUTE_LANES - N_GROUPS - n_experts,), F32)])
    lane_pad = lambda v: jnp.concatenate([v.astype(F32), jnp.zeros((ROUTE_LANES - v.shape[0],), F32)])[None, :]
    return dict(
        aw=aw, bw=bw,
        g_mix=p["g_mix"][None, :],
        w_in=jnp.pad(p["w_in"], ((0, 0), (0, pad))).astype(BF16),
        gq=_tile_row(p["g_qa"], aw // A_HEAD_DIM, A_HEAD_DIM ** -0.5),
        gk=_tile_row(p["g_ka"], aw // A_HEAD_DIM),
        conv_w=p["conv_w"],
        alog=lane_pad(p["a_log"]),
        dtb=lane_pad(jnp.concatenate([p["dt_bias"], jnp.zeros_like(p["dt_bias"])])),
        w_out=p["w_out"].astype(BF16),
        g_gdn=_tile_row(p["g_gdn"], bw // B_HEAD_DIM),
        g_xattn=p["g_xattn"][None, :],
        w_xq=p["w_xq"].astype(BF16),
        g_xq=_tile_row(p["g_xq"], p["w_xq"].shape[1] // B_HEAD_DIM, B_HEAD_DIM ** -0.5),
        w_xo=p["w_xo"].astype(BF16),
        g_ffn=p["g_ffn"][None, :],
        w_r=jnp.stack([w_r.astype(BF16), (w_r - w_r.astype(BF16).astype(F32)).astype(BF16)]), b_r=b_r[None, :],
        w_gate=p["w_gate"].astype(BF16), w_up=p["w_up"].astype(BF16), w_down=p["w_down"].astype(BF16),
    )


def _gdn(u_pre, prev, ab, s0, pw, *, seg, conv_seg, chunks, scan_seqs, scan_steps):
    w, uv, qg, kt, qk, last = _gdn_prep(u_pre, prev, ab, pw["conv_w"], pw["alog"], pw["dtb"],
                                        seg=seg, conv_seg=conv_seg, chunks=chunks)
    return _gdn_scan(w, uv, qg, kt, qk, last, s0, seg=seg, seqs=scan_seqs, steps=scan_steps)


def _prompt_layer(h, mem, pw, praw):
    nb_, s, d = h.shape
    assert nb_ == 1
    x = h.reshape(s, d)
    aw, bw = pw["aw"], pw["bw"]
    q, k, v, u_pre, z, ab = _inproj(x, pw["g_mix"], pw["w_in"], pw["gq"], pw["gk"],
                                    aw=aw, bw=bw, tm=512)
    oa = _dil_attn(q, k, v, rows=DILATIONS[-1][0])
    gdn_chunks = 8
    blk_rows = gdn_chunks * GDN_CHUNK
    tails = u_pre.reshape(s // blk_rows, blk_rows, 3 * bw)[:, blk_rows - PREV_ROWS:, :]
    prev = jnp.concatenate([jnp.zeros((1, PREV_ROWS, 3 * bw), F32), tails[:-1]], axis=0)
    s0 = jnp.zeros((1, B_HEADS, B_HEAD_DIM, B_HEAD_DIM), F32)
    og, s_fin = _gdn(u_pre, prev, ab, s0, pw, seg=GDN_CHUNK, conv_seg=blk_rows, chunks=gdn_chunks,
                     scan_seqs=1, scan_steps=16)
    mk, mv = _mem_kv(mem.reshape(mem.shape[1], d), praw["g_mem"][None, :], praw["w_xk"].astype(BF16),
                     praw["w_xv"].astype(BF16), _tile_row(praw["g_xk"], bw // B_HEAD_DIM))
    h2, hn, route = _post_prompt(oa, og, z, x, pw, mk, mv, tm=512)
    y = _moe(hn, h2, route, pw["w_gate"], pw["w_up"], pw["w_down"], chunk=MOE_CHUNK)
    keep = min(DILATIONS[-1][0], s)
    heads = aw // A_HEAD_DIM
    tail = lambda t_: jnp.transpose(t_[:, s - keep:, :], (1, 0, 2)).reshape(1, keep, heads, A_HEAD_DIM)
    new_k, new_v = tail(k), tail(v)
    conv_new = u_pre[s - (CONV_WIDTH - 1):].reshape(1, CONV_WIDTH - 1, 3 * bw)
    xh = mk.shape[1] // B_HEAD_DIM
    return (y.reshape(1, s, d), new_k, new_v, conv_new, s_fin,
            mk.reshape(1, -1, xh, B_HEAD_DIM), mv.reshape(1, -1, xh, B_HEAD_DIM))


def _sample_layer(h, win_k, win_v, conv_prev, s0, mem_k, mem_v, pw):
    nseq, t_new, d = h.shape
    aw, bw = pw["aw"], pw["bw"]
    n = nseq * t_new
    assert t_new == 8 and GDN_CHUNK % t_new == 0
    x = h.reshape(n, d)
    q, k, v, u_pre, z, ab = _inproj(x, pw["g_mix"], pw["w_in"], pw["gq"], pw["gk"],
                                    aw=aw, bw=bw, tm=512)
    w_buf = win_k.shape[1]
    heads = aw // A_HEAD_DIM
    win_t = lambda c_: jnp.transpose(c_, (0, 2, 3, 1)).reshape(nseq, aw, w_buf)
    new_t = lambda t_: jnp.transpose(t_.reshape(aw // LANE, nseq, t_new, LANE), (1, 0, 3, 2)).reshape(nseq, aw, t_new)
    win_back = lambda c_: jnp.transpose(c_.reshape(nseq, heads, A_HEAD_DIM, w_buf), (0, 3, 1, 2))
    q_rows = jnp.transpose(q.reshape(aw // LANE, nseq, t_new, LANE), (1, 2, 0, 3)).reshape(nseq, t_new, aw)
    new_kt, new_vt, oa = _step_attn(q_rows, new_t(k), new_t(v), win_t(win_k), win_t(win_v))
    new_k, new_v = win_back(new_kt), win_back(new_vt)
    prev = jnp.concatenate([jnp.zeros((nseq, PREV_ROWS - (CONV_WIDTH - 1), 3 * bw), F32),
                            conv_prev.astype(F32)], axis=1)
    og, s_fin = _gdn(u_pre, prev, ab, s0, pw, seg=t_new, conv_seg=t_new, chunks=4, scan_seqs=8, scan_steps=1)
    mem_rows = lambda m_: m_.reshape(nseq, -1, m_.shape[3])
    h2, hn, route = _post_sample(oa.reshape(n, aw), og, z, x, pw, mem_rows(mem_k), mem_rows(mem_v),
                                 t_new=t_new, seqs=16)
    y = _moe(hn, h2, route, pw["w_gate"], pw["w_up"], pw["w_down"], chunk=MOE_CHUNK)
    conv_new = u_pre.reshape(nseq, t_new, 3 * bw)[:, t_new - (CONV_WIDTH - 1):, :]
    return (y.reshape(nseq, t_new, d), new_k, new_v, conv_new, s_fin)


def kernel(x_prompt, x_sample, mem_prompt, cache_win_k, cache_win_v, state_conv, state_delta, cache_mem_k, cache_mem_v, g_mix, w_in, g_qa, g_ka, conv_w, a_log, dt_bias, g_gdn, w_out, g_xattn, g_mem, w_xq, w_xk, w_xv, g_xq, g_xk, w_xo, g_ffn, w_rg, b_rg, w_re, b_re, w_gate, w_up, w_down):
    depth = w_in.shape[0]
    hp, hs = x_prompt, x_sample
    outs = [[] for _ in range(10)]
    for l in range(depth):
        praw = dict(g_mix=g_mix[l], w_in=w_in[l], g_qa=g_qa[l], g_ka=g_ka[l], conv_w=conv_w[l],
                    a_log=a_log[l], dt_bias=dt_bias[l], g_gdn=g_gdn[l], w_out=w_out[l],
                    g_xattn=g_xattn[l], g_mem=g_mem[l], w_xq=w_xq[l], w_xk=w_xk[l], w_xv=w_xv[l],
                    g_xq=g_xq[l], g_xk=g_xk[l], w_xo=w_xo[l], g_ffn=g_ffn[l], w_rg=w_rg[l],
                    b_rg=b_rg[l], w_re=w_re[l], b_re=b_re[l], w_gate=w_gate[l], w_up=w_up[l],
                    w_down=w_down[l])
        pw = _layer_weights(praw)
        hp, k_p, v_p, c_p, s_p, mk, mv = _prompt_layer(hp, mem_prompt, pw, praw)
        hs, k_s, v_s, c_s, s_s = _sample_layer(hs, cache_win_k[l], cache_win_v[l], state_conv[l],
                                               state_delta[l].astype(F32), cache_mem_k[l], cache_mem_v[l], pw)
        for lst, val in zip(outs, (k_p, v_p, c_p, s_p, mk, mv, k_s, v_s, c_s, s_s)):
            lst.append(val)
    st = [jnp.stack(o) for o in outs]
    st[3] = st[3].astype(state_delta.dtype)
    st[9] = st[9].astype(state_delta.dtype)
    return (hp, hs, *st)
```
